```python
import math
import jax, jax.numpy as jnp
from jax import lax
import numpy as np

D_MODEL = 1024
BATCH = 8
SEQ = 4096
DEPTH = 1

N_HEADS_ATTN = 8
HEAD_DIM = 64
ATTN_WIDTH = N_HEADS_ATTN * HEAD_DIM
DILATION_PATTERNS = ((128, 1), (512, 4), (2048, 16))
NUM_BUCKETS = 32
MAX_DISTANCE = 2048
SSM_WIDTH = D_MODEL // 2
SSM_GROUP = 16
SSM_GROUPS = SSM_WIDTH // SSM_GROUP
SSM_STATE = 64
DT_MIN = 1e-3
DT_MAX = 1e-1
IN_WIDTH = 3 * ATTN_WIDTH + SSM_WIDTH + 2 * D_MODEL
N_EXPERT_GROUPS = 4
EXPERTS_PER_GROUP = 8
N_EXPERTS = N_EXPERT_GROUPS * EXPERTS_PER_GROUP
TOP_K = 2
D_EXPERT = D_MODEL // 2
MOE_BLOCK = 128
RMS_EPS = 1e-6
NEG_INF = -1e30

kernel_name = 'hybrid_dilated_attn_s5_hmoe_block'


def rms_norm(x, gain):
    xf = x.astype(jnp.float32)
    y = xf * lax.rsqrt(jnp.mean(xf * xf, axis=-1, keepdims=True) + RMS_EPS)
    return (y * gain.astype(jnp.float32)).astype(x.dtype)


def t5_bucket(dist):
    exact = NUM_BUCKETS // 2
    d_f = jnp.maximum(dist, exact).astype(jnp.float32)
    large = exact + (jnp.log(d_f / exact) / math.log(MAX_DISTANCE / exact)
                     * (NUM_BUCKETS - exact)).astype(jnp.int32)
    return jnp.where(dist < exact, dist, jnp.minimum(large, NUM_BUCKETS - 1))


def dilated_band_attention(q, k, v, rel_bias, window, dil):
    bsz, seq, n_heads, head_dim = q.shape
    span = window // dil
    blk = span
    sub_len = -(-seq // (dil * blk)) * blk
    seq_pad = sub_len * dil
    n_blk = sub_len // blk
    pad = ((0, 0), (0, seq_pad - seq), (0, 0), (0, 0))

    def to_blocks(t):
        return jnp.pad(t, pad).reshape(bsz, n_blk, blk, dil, n_heads, head_dim)

    def with_prev(t):
        prev = jnp.pad(t[:, :-1], ((0, 0), (1, 0), (0, 0), (0, 0), (0, 0), (0, 0)))
        return jnp.concatenate([prev, t], axis=2)

    qb = to_blocks(q)
    kc = with_prev(to_blocks(k))
    vc = with_prev(to_blocks(v))
    qi = jnp.arange(blk)[:, None]
    ki = jnp.arange(2 * blk)[None, :]
    sub_dist = blk + qi - ki
    band = (sub_dist >= 0) & (sub_dist <= span)
    in_range = (jnp.arange(n_blk)[:, None, None] * blk + ki[None] - blk) >= 0
    mask = (band[None] & in_range)[None, :, None, None]
    bias = rel_bias.astype(jnp.float32)[t5_bucket(jnp.maximum(sub_dist, 0) * dil)]
    bias = jnp.transpose(bias, (2, 0, 1))
    s = jnp.einsum('bnqrhd,bnkrhd->bnrhqk', qb, kc) * head_dim ** -0.5 + bias
    s = jnp.where(mask, s, NEG_INF)
    m = jnp.max(s, axis=-1, keepdims=True)
    p = jnp.exp(s - m)
    l = jnp.sum(p, axis=-1)
    o = jnp.einsum('bnrhqk,bnkrhd->bnqrhd', p, vc) / jnp.transpose(l, (0, 1, 4, 2, 3))[..., None]
    lse = jnp.transpose(m[..., 0] + jnp.log(l), (0, 1, 4, 2, 3))
    o = o.reshape(bsz, seq_pad, n_heads, head_dim)[:, :seq]
    lse = lse.reshape(bsz, seq_pad, n_heads)[:, :seq]
    return o, lse


def dilated_mixture_attention(q, k, v, rel_bias):
    q, k, v = (t.astype(jnp.float32) for t in (q, k, v))
    outs, lses = [], []
    for window, dil in DILATION_PATTERNS:
        o, lse = dilated_band_attention(q, k, v, rel_bias, window, dil)
        outs.append(o)
        lses.append(lse)
    w = jax.nn.softmax(jnp.stack(lses), axis=0)
    return jnp.einsum('pbsh,pbshd->bshd', w, jnp.stack(outs))


def s5_mixer(u, a_re, a_im, log_dt, b_re, b_im, c_re, c_im, d_skip, w_glu, b_glu):
    bsz, seq, _ = u.shape
    f32 = jnp.float32
    ug = u.astype(f32).reshape(bsz, seq, SSM_GROUPS, SSM_GROUP)
    lam = lax.complex(a_re.astype(f32), a_im.astype(f32))
    dt = jnp.exp(log_dt.astype(f32))[:, None]
    a_bar = jnp.exp(lam * dt)
    b_bar = ((a_bar - 1.0) / lam)[..., None] * lax.complex(b_re.astype(f32), b_im.astype(f32))
    bu = jnp.einsum('bsgh,gph->bsgp', ug.astype(jnp.complex64), b_bar)
    a_seq = jnp.broadcast_to(a_bar, (1, seq) + a_bar.shape)

    def combine(e1, e2):
        a1, h1 = e1
        a2, h2 = e2
        return a1 * a2, a2 * h1 + h2

    _, states = lax.associative_scan(combine, (a_seq, bu), axis=1)
    c_mat = lax.complex(c_re.astype(f32), c_im.astype(f32))
    y = jnp.real(jnp.einsum('ghp,bsgp->bsgh', c_mat, states)) \
        + d_skip.astype(f32).reshape(SSM_GROUPS, SSM_GROUP) * ug
    y = jax.nn.gelu(y.reshape(bsz, seq, SSM_WIDTH))
    y = y * jax.nn.sigmoid(y @ w_glu.astype(f32) + b_glu.astype(f32))
    return y.astype(u.dtype)


def hierarchical_moe(h, w_rg, b_rg, w_re, b_re, w1, w3, w2):
    bsz, seq, d = h.shape
    n_tok = bsz * seq
    hf = h.reshape(n_tok, d)
    g_logits = (hf @ w_rg + b_rg).astype(jnp.float32)
    g_prob = jax.nn.softmax(g_logits, axis=-1)
    _, g_sel = lax.top_k(g_logits, 1)
    g_gate = jnp.take_along_axis(g_prob, g_sel, axis=-1)
    e_logits = (hf @ w_re + b_re).astype(jnp.float32).reshape(n_tok, N_EXPERT_GROUPS, EXPERTS_PER_GROUP)
    e_in = jnp.take_along_axis(e_logits, g_sel[:, :, None], axis=1)[:, 0]
    top_v, top_i = lax.top_k(e_in, TOP_K)
    weights = g_gate * jax.nn.softmax(top_v, axis=-1)
    expert_ids = g_sel * EXPERTS_PER_GROUP + top_i
    n_assign = n_tok * TOP_K
    e_flat = expert_ids.reshape(n_assign)
    tok_flat = jnp.repeat(jnp.arange(n_tok, dtype=jnp.int32), TOP_K)
    w_flat = weights.reshape(n_assign)
    order = jnp.argsort(e_flat)
    e_sorted = e_flat[order]
    counts = jnp.bincount(e_flat, length=N_EXPERTS)
    starts = jnp.cumsum(counts) - counts
    padded = (counts + MOE_BLOCK - 1) // MOE_BLOCK * MOE_BLOCK
    pad_ends = jnp.cumsum(padded)
    pad_starts = pad_ends - padded
    dest = pad_starts[e_sorted] + jnp.arange(n_assign) - starts[e_sorted]
    n_blocks = (n_assign + N_EXPERTS * (MOE_BLOCK - 1) + MOE_BLOCK - 1) // MOE_BLOCK
    cap = n_blocks * MOE_BLOCK
    row_tok = jnp.zeros((cap,), jnp.int32).at[dest].set(tok_flat[order])
    row_w = jnp.zeros((cap,), jnp.float32).at[dest].set(w_flat[order])
    block_expert = jnp.minimum(
        jnp.searchsorted(pad_ends, jnp.arange(n_blocks) * MOE_BLOCK, side='right'), N_EXPERTS - 1)
    xb = hf[row_tok].reshape(n_blocks, MOE_BLOCK, d)

    def expert_block(args):
        xblk, e = args
        return (jax.nn.silu(xblk @ w1[e]) * (xblk @ w3[e])) @ w2[e]

    yb = lax.map(expert_block, (xb, block_expert)).reshape(cap, d)
    out = jnp.zeros((n_tok, d), jnp.float32).at[row_tok].add(yb.astype(jnp.float32) * row_w[:, None])
    return out.reshape(bsz, seq, d).astype(h.dtype)


def setup_inputs(seed: int = 0) -> dict:
    key = jax.random.key(seed)
    ks = jax.random.split(key, 32)
    f32 = jnp.float32
    L = DEPTH

    def nrm(k, shape, fan_in):
        return jax.random.normal(k, shape, f32) * fan_in ** -0.5

    def gain(k):
        return 1.0 + 0.05 * jax.random.normal(k, (L, D_MODEL), f32)

    return {
        'x': jax.random.normal(ks[0], (BATCH, SEQ, D_MODEL), f32),
        'c': jax.random.normal(ks[1], (BATCH, D_MODEL), f32),
        'w_mod': 0.5 * nrm(ks[2], (L, D_MODEL, 6 * D_MODEL), D_MODEL),
        'b_mod': 0.01 * jax.random.normal(ks[3], (L, 6 * D_MODEL), f32),
        'g_pre_mix': gain(ks[4]),
        'g_post_mix': gain(ks[5]),
        'g_pre_ffn': gain(ks[6]),
        'g_post_ffn': gain(ks[7]),
        'w_in': nrm(ks[8], (L, D_MODEL, IN_WIDTH), D_MODEL),
        'rel_bias': 0.5 * jax.random.normal(ks[9], (NUM_BUCKETS, N_HEADS_ATTN), f32),
        'a_re': -0.5 + 0.01 * jax.random.normal(ks[10], (L, SSM_GROUPS, SSM_STATE), f32),
        'a_im': jnp.pi * jnp.arange(SSM_STATE, dtype=f32) + 0.01 * jax.random.normal(ks[11], (L, SSM_GROUPS, SSM_STATE), f32),
        'log_dt': jax.random.uniform(ks[12], (L, SSM_GROUPS), f32, math.log(DT_MIN), math.log(DT_MAX)),
        'ssm_b_re': nrm(ks[13], (L, SSM_GROUPS, SSM_STATE, SSM_GROUP), 2 * SSM_GROUP),
        'ssm_b_im': nrm(ks[14], (L, SSM_GROUPS, SSM_STATE, SSM_GROUP), 2 * SSM_GROUP),
        'ssm_c_re': nrm(ks[15], (L, SSM_GROUPS, SSM_GROUP, SSM_STATE), SSM_STATE),
        'ssm_c_im': nrm(ks[16], (L, SSM_GROUPS, SSM_GROUP, SSM_STATE), SSM_STATE),
        'd_skip': jax.random.normal(ks[17], (L, SSM_WIDTH), f32),
        'w_glu': nrm(ks[18], (L, SSM_WIDTH, SSM_WIDTH), SSM_WIDTH),
        'b_glu': 0.01 * jax.random.normal(ks[19], (L, SSM_WIDTH), f32),
        'w_branch_attn': nrm(ks[20], (L, ATTN_WIDTH, D_MODEL), ATTN_WIDTH),
        'w_branch_ssm': nrm(ks[21], (L, SSM_WIDTH, D_MODEL), SSM_WIDTH),
        'w_out': nrm(ks[22], (L, D_MODEL, D_MODEL), D_MODEL),
        'w_router_group': nrm(ks[23], (L, D_MODEL, N_EXPERT_GROUPS), D_MODEL),
        'b_router_group': 0.01 * jax.random.normal(ks[24], (L, N_EXPERT_GROUPS), f32),
        'w_router_expert': nrm(ks[25], (L, D_MODEL, N_EXPERTS), D_MODEL),
        'b_router_expert': 0.01 * jax.random.normal(ks[26], (L, N_EXPERTS), f32),
        'w1': nrm(ks[27], (L, N_EXPERTS, D_MODEL, D_EXPERT), D_MODEL),
        'w3': nrm(ks[28], (L, N_EXPERTS, D_MODEL, D_EXPERT), D_MODEL),
        'w2': nrm(ks[29], (L, N_EXPERTS, D_EXPERT, D_MODEL), D_EXPERT),
    }


def reference(x, c, w_mod, b_mod, g_pre_mix, g_post_mix, g_pre_ffn, g_post_ffn, w_in, rel_bias,
              a_re, a_im, log_dt, ssm_b_re, ssm_b_im, ssm_c_re, ssm_c_im, d_skip, w_glu, b_glu,
              w_branch_attn, w_branch_ssm, w_out, w_router_group, b_router_group,
              w_router_expert, b_router_expert, w1, w3, w2):
    bsz, seq, _ = x.shape
    splits = [ATTN_WIDTH, 2 * ATTN_WIDTH, 3 * ATTN_WIDTH, 3 * ATTN_WIDTH + SSM_WIDTH]
    for l in range(DEPTH):
        mod = jax.nn.silu(c) @ w_mod[l] + b_mod[l]
        sh1, sc1, gt1, sh2, sc2, gt2 = [m[:, None, :] for m in jnp.split(mod, 6, axis=-1)]
        h = rms_norm(x, g_pre_mix[l]) * (1.0 + sc1) + sh1
        proj = h @ w_in[l]
        q, k, v, u, gate_logits = jnp.split(proj, splits, axis=-1)
        heads = (bsz, seq, N_HEADS_ATTN, HEAD_DIM)
        attn = dilated_mixture_attention(q.reshape(heads), k.reshape(heads), v.reshape(heads), rel_bias)
        attn = attn.reshape(bsz, seq, ATTN_WIDTH).astype(x.dtype)
        ssm = s5_mixer(u, a_re[l], a_im[l], log_dt[l], ssm_b_re[l], ssm_b_im[l],
                       ssm_c_re[l], ssm_c_im[l], d_skip[l], w_glu[l], b_glu[l])
        g_attn, g_ssm = jnp.split(jax.nn.sigmoid(gate_logits), 2, axis=-1)
        merged = g_attn * (attn @ w_branch_attn[l]) + g_ssm * (ssm @ w_branch_ssm[l])
        y = merged @ w_out[l]
        x = x + gt1 * rms_norm(y, g_post_mix[l])
        h = rms_norm(x, g_pre_ffn[l]) * (1.0 + sc2) + sh2
        y = hierarchical_moe(h, w_router_group[l], b_router_group[l], w_router_expert[l],
                             b_router_expert[l], w1[l], w3[l], w2[l])
        x = x + gt2 * rms_norm(y, g_post_ffn[l])
    return x
```

```python
import functools
import math

import numpy as np
import jax
import jax.numpy as jnp
from jax import lax
from jax.experimental import pallas as pl
from jax.experimental.pallas import tpu as pltpu

F32 = jnp.float32
BF16 = jnp.bfloat16

N_HEADS = 8
HEAD_DIM = 64
ATTN_WIDTH = N_HEADS * HEAD_DIM
DILATION_PATTERNS = ((128, 1), (512, 4), (2048, 16))
NUM_BUCKETS = 32
MAX_DISTANCE = 2048
SSM_GROUP = 16
SSM_STATE = 64
N_EXPERT_GROUPS = 4
EXPERTS_PER_GROUP = 8
N_EXPERTS = N_EXPERT_GROUPS * EXPERTS_PER_GROUP
TOP_K = 2
RMS_EPS = 1e-6
NEG_INF = -1e30

LANES = 128
SUBLANES = 8
VMEM_LIMIT_BYTES = 56 * 1024 * 1024

ATTN_BLK = 128
INPROJ_ROWS = 512
MERGE_ROWS = 512
SSM_STEPS = 64
MOE_ROWS = 256
DISPATCH_ROWS = 512
COMBINE_ROWS = 256


def _cparams(*sem):
    return pltpu.CompilerParams(dimension_semantics=sem, vmem_limit_bytes=VMEM_LIMIT_BYTES)


def _sigmoid(x):
    return 1.0 / (1.0 + jnp.exp(-x))


def _dot(a, b):
    return jnp.dot(a, b, preferred_element_type=F32)


def _split_bf16(a):
    hi = a.astype(BF16)
    lo = (a - hi.astype(F32)).astype(BF16)
    return hi, lo


def _dot_split(a, w_hi, w_lo):
    a_hi, a_lo = _split_bf16(a)
    return _dot(a_hi, w_hi) + _dot(a_lo, w_hi) + _dot(a_hi, w_lo)


def _rms_norm(x, gain):
    ms = jnp.mean(x * x, axis=-1, keepdims=True)
    return x * lax.rsqrt(ms + RMS_EPS) * gain


def _mod_kernel(c_ref, w_ref, b_ref, o_ref):
    c = c_ref[...]
    a = c * _sigmoid(c)
    w_hi, w_lo = _split_bf16(w_ref[...])
    o_ref[...] = _dot_split(a, w_hi, w_lo) + b_ref[...]


def _modulation(c, w_mod, b_mod):
    bsz, d = c.shape
    n = w_mod.shape[1]
    tn = 1024
    return pl.pallas_call(
        _mod_kernel,
        out_shape=jax.ShapeDtypeStruct((bsz, n), F32),
        grid=(n // tn,),
        in_specs=[pl.BlockSpec((bsz, d), lambda j: (0, 0)),
                  pl.BlockSpec((d, tn), lambda j: (0, j)),
                  pl.BlockSpec((1, tn), lambda j: (0, j))],
        out_specs=pl.BlockSpec((bsz, tn), lambda j: (0, j)),
        compiler_params=_cparams("arbitrary"),
        name="mod",
    )(c, w_mod, b_mod.reshape(1, n))


def _inproj_kernel(x_ref, mod_ref, g_ref, w_ref, q_ref, k_ref, v_ref, u_ref, gate_ref):
    h = _rms_norm(x_ref[0], g_ref[...]) * (1.0 + mod_ref[0, 1:2, :]) + mod_ref[0, 0:1, :]
    hb = h.astype(BF16)
    aw = ATTN_WIDTH
    q_ref[0] = (_dot(hb, w_ref[:, 0:aw]) * (HEAD_DIM ** -0.5)).astype(BF16)
    k_ref[0] = _dot(hb, w_ref[:, aw:2 * aw]).astype(BF16)
    v_ref[0] = _dot(hb, w_ref[:, 2 * aw:3 * aw]).astype(BF16)
    sw = u_ref.shape[-1]
    u_ref[...] = _dot(hb, w_ref[:, 3 * aw:3 * aw + sw])
    gate_ref[0] = _sigmoid(_dot(hb, w_ref[:, 3 * aw + sw:])).astype(BF16)


def _input_projection(x, mod3, g_pre, w_in_bf16, ssm_width):
    bsz, seq, d = x.shape
    tm = INPROJ_ROWS
    n_in = w_in_bf16.shape[1]
    gw = n_in - 3 * ATTN_WIDTH - ssm_width
    qkv = jax.ShapeDtypeStruct((bsz, seq, ATTN_WIDTH), BF16)
    qkv_spec = pl.BlockSpec((1, tm, ATTN_WIDTH), lambda b, i: (b, i, 0))
    return pl.pallas_call(
        _inproj_kernel,
        out_shape=(qkv, qkv, qkv,
                   jax.ShapeDtypeStruct((seq, bsz * ssm_width), F32),
                   jax.ShapeDtypeStruct((bsz, seq, gw), BF16)),
        grid=(bsz, seq // tm),
        in_specs=[pl.BlockSpec((1, tm, d), lambda b, i: (b, i, 0)),
                  pl.BlockSpec((1, 6, d), lambda b, i: (b, 0, 0)),
                  pl.BlockSpec((1, d), lambda b, i: (0, 0)),
                  pl.BlockSpec((d, n_in), lambda b, i: (0, 0))],
        out_specs=(qkv_spec, qkv_spec, qkv_spec,
                   pl.BlockSpec((tm, ssm_width), lambda b, i: (i, b)),
                   pl.BlockSpec((1, tm, gw), lambda b, i: (b, i, 0))),
        compiler_params=_cparams("arbitrary", "arbitrary"),
        name="inproj",
    )(x, mod3, g_pre, w_in_bf16)


def _t5_bucket_np(dist):
    exact = NUM_BUCKETS // 2
    d_f = np.maximum(dist, exact).astype(np.float32)
    large = exact + (np.log(d_f / np.float32(exact)) / np.float32(math.log(MAX_DISTANCE / exact))
                     * np.float32(NUM_BUCKETS - exact)).astype(np.int32)
    return np.where(dist < exact, dist, np.minimum(large, NUM_BUCKETS - 1))


def _band_bias(rel_bias, dil):
    blk = ATTN_BLK
    qi = np.arange(blk)[:, None]
    ki = np.arange(2 * blk)[None, :]
    sub_dist = blk + qi - ki
    band = (sub_dist >= 0) & (sub_dist <= blk)
    bucket = _t5_bucket_np(np.maximum(sub_dist, 0) * dil)
    bias = jnp.transpose(rel_bias.astype(F32)[bucket], (2, 0, 1))
    later = jnp.where(band[None], bias, NEG_INF)
    first = jnp.where((band & (ki >= blk))[None], bias, NEG_INF)
    return jnp.stack([first, later])


def _attn_kernel(*refs, n_sub, has_state, emit_lse):
    q_ref, kc_ref, kp_ref, vc_ref, vp_ref, bias_ref = refs[:6]
    pos = 6
    if has_state:
        o_in_ref, lse_in_ref = refs[pos:pos + 2]
        pos += 2
    o_out_ref = refs[pos]
    pos += 1
    if emit_lse:
        lse_out_ref = refs[pos]
        pos += 1
    kbuf, vbuf = refs[pos:pos + 2]

    blk = ATTN_BLK
    first_step = pl.program_id(2) == 0
    kbuf[0:blk, :] = kp_ref[0]
    kbuf[blk:, :] = kc_ref[0]
    vbuf[0:blk, :] = vp_ref[0]
    vbuf[blk:, :] = vc_ref[0]

    lane = lax.broadcasted_iota(jnp.int32, (1, LANES), 1)
    lo_half = lane < HEAD_DIM
    lanes_per_head = LANES // N_HEADS

    def sub_block(i, carry):
        r0 = pl.multiple_of(i * blk, blk)
        q = q_ref[0, pl.ds(r0, blk), :]
        kk = kbuf[pl.ds(r0, 2 * blk), :]
        vv = vbuf[pl.ds(r0, 2 * blk), :]
        variant = jnp.where(jnp.logical_and(first_step, i == 0), 0, 1)
        lse_new = jnp.zeros((blk, LANES), F32)
        o_pairs = []
        for j in range(N_HEADS // 2):
            cols = slice(j * LANES, (j + 1) * LANES)
            qj, kj, vj = q[:, cols], kk[:, cols], vv[:, cols]
            probs, inv_l = [], []
            for hh in range(2):
                head = 2 * j + hh
                sel = lo_half if hh == 0 else jnp.logical_not(lo_half)
                qm = jnp.where(sel, qj, jnp.zeros_like(qj))
                s = lax.dot_general(qm, kj, (((1,), (1,)), ((), ())), preferred_element_type=F32)
                s = s + bias_ref[variant, head]
                m = jnp.max(s, axis=-1, keepdims=True)
                p = jnp.exp(s - m)
                l = jnp.sum(p, axis=-1, keepdims=True)
                probs.append(p.astype(BF16))
                inv_l.append(1.0 / l)
                lse_h = m + jnp.log(l)
                lse_new = jnp.where(lane // lanes_per_head == head, lse_h, lse_new)
            p2 = jnp.concatenate(probs, axis=1)
            v_bd = jnp.concatenate([jnp.where(lo_half, vj, jnp.zeros_like(vj)),
                                    jnp.where(lo_half, jnp.zeros_like(vj), vj)], axis=0)
            o2 = _dot(p2, v_bd)
            o_pairs.append(o2 * jnp.where(lo_half, inv_l[0], inv_l[1]))
        if has_state:
            lse_old = lse_in_ref[0, pl.ds(r0, blk), :]
            m_c = jnp.maximum(lse_old, lse_new)
            a_c = jnp.exp(lse_old - m_c)
            b_c = jnp.exp(lse_new - m_c)
            den = a_c + b_c
            w_old, w_new = a_c / den, b_c / den
            lse_tot = m_c + jnp.log(den)
            for j in range(N_HEADS // 2):
                c0 = 2 * j * lanes_per_head
                c1 = c0 + lanes_per_head
                wo = jnp.where(lo_half, w_old[:, c0:c0 + 1], w_old[:, c1:c1 + 1])
                wn = jnp.where(lo_half, w_new[:, c0:c0 + 1], w_new[:, c1:c1 + 1])
                cols = slice(j * LANES, (j + 1) * LANES)
                o_out_ref[0, pl.ds(r0, blk), cols] = (
                    wo * o_in_ref[0, pl.ds(r0, blk), cols] + wn * o_pairs[j])
        else:
            lse_tot = lse_new
            for j in range(N_HEADS // 2):
                o_out_ref[0, pl.ds(r0, blk), j * LANES:(j + 1) * LANES] = o_pairs[j]
        if emit_lse:
            lse_out_ref[0, pl.ds(r0, blk), :] = lse_tot
        return carry

    lax.fori_loop(0, n_sub, sub_block, 0)


def _attention_pattern(q, k, v, bias, dil, state, emit_lse):
    bsz, seq, aw = q.shape
    blk = ATTN_BLK
    sub_len = seq // dil
    assert seq % (dil * blk) == 0
    tq = min(512, sub_len)
    n_sub = tq // blk
    n_steps = sub_len // tq
    ratio = tq // blk

    def view(t, w):
        return t.reshape(bsz, sub_len, dil * w)

    cur = lambda b, r, n: (b, n, r)
    prev = lambda b, r, n: (b, jnp.maximum(n * ratio - 1, 0), r)
    in_specs = [pl.BlockSpec((1, tq, aw), cur), pl.BlockSpec((1, tq, aw), cur),
                pl.BlockSpec((1, blk, aw), prev), pl.BlockSpec((1, tq, aw), cur),
                pl.BlockSpec((1, blk, aw), prev),
                pl.BlockSpec(bias.shape, lambda b, r, n: (0, 0, 0, 0))]
    args = [view(q, aw), view(k, aw), view(k, aw), view(v, aw), view(v, aw), bias]
    has_state = state is not None
    if has_state:
        in_specs += [pl.BlockSpec((1, tq, aw), cur), pl.BlockSpec((1, tq, LANES), cur)]
        args += [view(state[0], aw), view(state[1], LANES)]
    out_shape = [jax.ShapeDtypeStruct((bsz, sub_len, dil * aw), F32)]
    out_specs = [pl.BlockSpec((1, tq, aw), cur)]
    if emit_lse:
        out_shape.append(jax.ShapeDtypeStruct((bsz, sub_len, dil * LANES), F32))
        out_specs.append(pl.BlockSpec((1, tq, LANES), cur))
    outs = pl.pallas_call(
        functools.partial(_attn_kernel, n_sub=n_sub, has_state=has_state, emit_lse=emit_lse),
        out_shape=out_shape,
        grid=(bsz, dil, n_steps),
        in_specs=in_specs,
        out_specs=out_specs,
        scratch_shapes=[pltpu.VMEM((tq + blk, aw), BF16), pltpu.VMEM((tq + blk, aw), BF16)],
        compiler_params=_cparams("arbitrary", "arbitrary", "arbitrary"),
        name=f"attn_d{dil}",
    )(*args)
    o = outs[0].reshape(bsz, seq, aw)
    lse = outs[1].reshape(bsz, seq, LANES) if emit_lse else None
    return o, lse


def _dilated_mixture_attention(q, k, v, rel_bias):
    state = None
    n_pat = len(DILATION_PATTERNS)
    for idx, (window, dil) in enumerate(DILATION_PATTERNS):
        assert window // dil == ATTN_BLK
        state = _attention_pattern(q, k, v, _band_bias(rel_bias, dil), dil, state,
                                   emit_lse=idx + 1 < n_pat)
    return state[0]


def _ssm_kernel(u_ref, bmat_ref, cmat_ref, ar_ref, ai_ref, dskip_ref, wglu_ref, bglu_ref,
                o_ref, hbuf, hstate, *, n_steps):
    @pl.when(pl.program_id(0) == 0)
    def _():
        hstate[...] = jnp.zeros_like(hstate)

    u = u_ref[...]
    hbuf[...] = _dot(u.astype(BF16), bmat_ref[...])
    n_state = hbuf.shape[1] // 2
    half = n_state // 2
    for part in range(2):
        re_cols = slice(part * half, (part + 1) * half)
        im_cols = slice(n_state + part * half, n_state + (part + 1) * half)
        ar = ar_ref[:, re_cols]
        ai = ai_ref[:, re_cols]

        def step(t, carry, re_cols=re_cols, im_cols=im_cols, ar=ar, ai=ai):
            hr, hi = carry
            r0 = pl.multiple_of(t * SUBLANES, SUBLANES)
            nr = ar * hr - ai * hi + hbuf[pl.ds(r0, SUBLANES), re_cols]
            ni = ar * hi + ai * hr + hbuf[pl.ds(r0, SUBLANES), im_cols]
            hbuf[pl.ds(r0, SUBLANES), re_cols] = nr
            hbuf[pl.ds(r0, SUBLANES), im_cols] = ni
            return nr, ni

        hr, hi = lax.fori_loop(0, n_steps, step, (hstate[:, re_cols], hstate[:, im_cols]),
                               unroll=2)
        hstate[:, re_cols] = hr
        hstate[:, im_cols] = hi

    y = _dot(hbuf[...].astype(BF16), cmat_ref[...]) + dskip_ref[...] * u
    y = 0.5 * y * (1.0 + jnp.tanh(math.sqrt(2.0 / math.pi) * (y + 0.044715 * (y * y * y))))
    z = _dot(y.astype(BF16), wglu_ref[...]) + bglu_ref[...]
    o_ref[...] = (y * _sigmoid(z)).astype(BF16)


def _ssm_params(a_re, a_im, log_dt, b_re, b_im, c_re, c_im, bsz):
    g, p = a_re.shape
    hg = b_re.shape[-1]
    dt = jnp.exp(log_dt.astype(F32))[:, None]
    a_re, a_im = a_re.astype(F32), a_im.astype(F32)
    mag = jnp.exp(a_re * dt)
    abar_re = mag * jnp.cos(a_im * dt)
    abar_im = mag * jnp.sin(a_im * dt)
    den = a_re * a_re + a_im * a_im
    q_re = ((abar_re - 1.0) * a_re + abar_im * a_im) / den
    q_im = (abar_im * a_re - (abar_re - 1.0) * a_im) / den
    b_re, b_im = b_re.astype(F32), b_im.astype(F32)
    bb_re = q_re[..., None] * b_re - q_im[..., None] * b_im
    bb_im = q_re[..., None] * b_im + q_im[..., None] * b_re
    eye = jnp.eye(g, dtype=F32)

    def in_mat(t):
        return jnp.einsum('gph,gk->ghkp', t, eye).reshape(g * hg, g * p)

    def out_mat(t):
        return jnp.einsum('ghp,gk->gpkh', t, eye).reshape(g * p, g * hg)

    bmat = jnp.concatenate([in_mat(bb_re), in_mat(bb_im)], axis=1).astype(BF16)
    cmat = jnp.concatenate([out_mat(c_re.astype(F32)), -out_mat(c_im.astype(F32))],
                           axis=0).astype(BF16)
    ar = jnp.broadcast_to(abar_re.reshape(1, g * p), (bsz, g * p))
    ai = jnp.broadcast_to(abar_im.reshape(1, g * p), (bsz, g * p))
    return bmat, cmat, ar, ai


def _ssm_branch(u_tm, bsz, a_re, a_im, log_dt, b_re, b_im, c_re, c_im, d_skip, w_glu, b_glu):
    seq = u_tm.shape[0]
    width = u_tm.shape[1] // bsz
    assert bsz == SUBLANES
    bmat, cmat, ar, ai = _ssm_params(a_re, a_im, log_dt, b_re, b_im, c_re, c_im, bsz)
    n_state2 = bmat.shape[1]
    rows = SSM_STEPS * bsz
    const = lambda c: (0, 0)
    out = pl.pallas_call(
        functools.partial(_ssm_kernel, n_steps=SSM_STEPS),
        out_shape=jax.ShapeDtypeStruct((seq * bsz, width), BF16),
        grid=(seq // SSM_STEPS,),
        in_specs=[pl.BlockSpec((rows, width), lambda c: (c, 0)),
                  pl.BlockSpec(bmat.shape, const), pl.BlockSpec(cmat.shape, const),
                  pl.BlockSpec(ar.shape, const), pl.BlockSpec(ai.shape, const),
                  pl.BlockSpec((1, width), const), pl.BlockSpec((width, width), const),
                  pl.BlockSpec((1, width), const)],
        out_specs=pl.BlockSpec((rows, width), lambda c: (c, 0)),
        scratch_shapes=[pltpu.VMEM((rows, n_state2), F32), pltpu.VMEM((bsz, n_state2), F32)],
        compiler_params=_cparams("arbitrary"),
        name="ssm",
    )(u_tm.reshape(seq * bsz, width), bmat, cmat, ar, ai, d_skip.reshape(1, width).astype(F32),
      w_glu.astype(BF16), b_glu.reshape(1, width).astype(F32))
    return out.reshape(seq, bsz * width)


ROUTE_LANES = {"id0": 0, "id1": 1, "rank0": 2, "rank1": 3, "w0": 4, "w1": 5}
GROUP_LANE0 = N_EXPERTS


def _merge_kernel(x_ref, gate_ref, attn_ref, ssm_ref, mod_ref, gpost_ref, gpre_ref,
                  wba_ref, wbs_ref, wout_ref, wr_hi_ref, wr_lo_ref, br_ref,
                  x1_ref, h2_ref, route_ref, count_ref, carry):
    @pl.when(jnp.logical_and(pl.program_id(0) == 0, pl.program_id(1) == 0))
    def _():
        carry[...] = jnp.zeros_like(carry)

    d = x_ref.shape[-1]
    g_attn = gate_ref[0, :, 0:d].astype(F32)
    g_ssm = gate_ref[0, :, d:].astype(F32)
    merged = (g_attn * _dot(attn_ref[0].astype(BF16), wba_ref[...])
              + g_ssm * _dot(ssm_ref[...], wbs_ref[...]))
    y = _dot(merged.astype(BF16), wout_ref[...])
    x1 = x_ref[0] + mod_ref[0, 2:3, :] * _rms_norm(y, gpost_ref[...])
    x1_ref[0] = x1
    h2 = _rms_norm(x1, gpre_ref[...]) * (1.0 + mod_ref[0, 4:5, :]) + mod_ref[0, 3:4, :]
    h2_ref[0] = h2

    logits = _dot_split(h2, wr_hi_ref[...], wr_lo_ref[...]) + br_ref[...]
    tm = logits.shape[0]
    lane = lax.broadcasted_iota(jnp.int32, (tm, LANES), 1).astype(F32)
    big = float(LANES)
    is_group = jnp.logical_and(lane >= GROUP_LANE0, lane < GROUP_LANE0 + N_EXPERT_GROUPS)
    gl = jnp.where(is_group, logits, -jnp.inf)
    g_max = jnp.max(gl, axis=-1, keepdims=True)
    g_sel = jnp.min(jnp.where(gl == g_max, lane, big), axis=-1, keepdims=True) - GROUP_LANE0
    g_gate = 1.0 / jnp.sum(jnp.exp(gl - g_max), axis=-1, keepdims=True)
    lo = g_sel * EXPERTS_PER_GROUP
    in_group = jnp.logical_and(lane >= lo, lane < lo + EXPERTS_PER_GROUP)
    el = jnp.where(in_group, logits, -jnp.inf)
    t0 = jnp.max(el, axis=-1, keepdims=True)
    i0 = jnp.min(jnp.where(el == t0, lane, big), axis=-1, keepdims=True)
    el1 = jnp.where(lane == i0, -jnp.inf, el)
    t1 = jnp.max(el1, axis=-1, keepdims=True)
    i1 = jnp.min(jnp.where(el1 == t1, lane, big), axis=-1, keepdims=True)
    e = jnp.exp(t1 - t0)
    w0 = g_gate / (1.0 + e)
    w1 = g_gate * e / (1.0 + e)

    hit0 = lane == i0
    hit1 = lane == i1
    onehot = jnp.logical_or(hit0, hit1).astype(F32)
    row = lax.broadcasted_iota(jnp.int32, (tm, tm), 0)
    col = lax.broadcasted_iota(jnp.int32, (tm, tm), 1)
    strict_lower = (col < row).astype(BF16)
    before = _dot(strict_lower, onehot.astype(BF16)) + carry[...]
    rank0 = jnp.sum(jnp.where(hit0, before, 0.0), axis=-1, keepdims=True)
    rank1 = jnp.sum(jnp.where(hit1, before, 0.0), axis=-1, keepdims=True)
    carry[...] = carry[...] + jnp.sum(onehot, axis=0, keepdims=True)
    count_ref[...] = jnp.broadcast_to(carry[...], count_ref.shape)

    route = jnp.zeros((tm, LANES), F32)
    for name, val in (("id0", i0), ("id1", i1), ("rank0", rank0), ("rank1", rank1),
                      ("w0", w0), ("w1", w1)):
        route = jnp.where(lane == ROUTE_LANES[name], val, route)
    route_ref[0] = route


def _merge_and_route(x, gates, attn, ssm_tm, mod3, g_post, g_pre, wba, wbs, wout, w_rg, b_rg,
                     w_re, b_re):
    bsz, seq, d = x.shape
    tm = MERGE_ROWS
    aw = attn.shape[-1]
    sw = ssm_tm.shape[1] // bsz
    wr = jnp.zeros((d, LANES), F32).at[:, :N_EXPERTS].set(w_re.astype(F32))
    wr = wr.at[:, GROUP_LANE0:GROUP_LANE0 + N_EXPERT_GROUPS].set(w_rg.astype(F32))
    br = jnp.zeros((1, LANES), F32).at[0, :N_EXPERTS].set(b_re.astype(F32))
    br = br.at[0, GROUP_LANE0:GROUP_LANE0 + N_EXPERT_GROUPS].set(b_rg.astype(F32))
    wr_hi, wr_lo = _split_bf16(wr)
    tok = lambda b, i: (b, i, 0)
    const = lambda b, i: (0, 0)
    return pl.pallas_call(
        _merge_kernel,
        out_shape=(jax.ShapeDtypeStruct((bsz, seq, d), F32),
                   jax.ShapeDtypeStruct((bsz, seq, d), F32),
                   jax.ShapeDtypeStruct((bsz, seq, LANES), F32),
                   jax.ShapeDtypeStruct((SUBLANES, LANES), F32)),
        grid=(bsz, seq // tm),
        in_specs=[pl.BlockSpec((1, tm, d), tok),
                  pl.BlockSpec((1, tm, gates.shape[-1]), tok),
                  pl.BlockSpec((1, tm, aw), tok),
                  pl.BlockSpec((tm, sw), lambda b, i: (i, b)),
                  pl.BlockSpec((1, 6, d), lambda b, i: (b, 0, 0)),
                  pl.BlockSpec((1, d), const), pl.BlockSpec((1, d), const),
                  pl.BlockSpec(wba.shape, const), pl.BlockSpec(wbs.shape, const),
                  pl.BlockSpec(wout.shape, const),
                  pl.BlockSpec((d, LANES), const), pl.BlockSpec((d, LANES), const),
                  pl.BlockSpec((1, LANES), const)],
        out_specs=(pl.BlockSpec((1, tm, d), tok), pl.BlockSpec((1, tm, d), tok),
                   pl.BlockSpec((1, tm, LANES), tok),
                   pl.BlockSpec((SUBLANES, LANES), const)),
        scratch_shapes=[pltpu.VMEM((1, LANES), F32)],
        compiler_params=_cparams("arbitrary", "arbitrary"),
        name="merge",
    )(x, gates, attn, ssm_tm, mod3, g_post, g_pre, wba, wbs, wout, wr_hi, wr_lo, br)


def _row_copy_out(src_ref, dst_hbm, sem, r, d):
    return pltpu.make_async_copy(src_ref.at[pl.ds(r, 1)], dst_hbm.at[pl.ds(d, 1)], sem)


def _dispatch_kernel(dest_ref, h_ref, xs_in_ref, xs_ref, sem):
    del xs_in_ref
    rows = h_ref.shape[0]
    base = pl.program_id(0) * rows

    def issue(r, carry):
        for k in range(TOP_K):
            _row_copy_out(h_ref, xs_ref, sem, r, dest_ref[TOP_K * (base + r) + k]).start()
        return carry

    lax.fori_loop(0, rows, issue, 0)
    for _ in range(TOP_K):
        pltpu.make_async_copy(h_ref, xs_ref.at[pl.ds(0, rows)], sem).wait()


def _dispatch(h2_flat, dest, cap):
    n_tok, d = h2_flat.shape
    rows = DISPATCH_ROWS
    xs0 = jnp.zeros((cap, d), h2_flat.dtype)
    return pl.pallas_call(
        _dispatch_kernel,
        out_shape=jax.ShapeDtypeStruct((cap, d), h2_flat.dtype),
        grid_spec=pltpu.PrefetchScalarGridSpec(
            num_scalar_prefetch=1,
            grid=(n_tok // rows,),
            in_specs=[pl.BlockSpec((rows, d), lambda i, dest: (i, 0)),
                      pl.BlockSpec(memory_space=pl.ANY)],
            out_specs=pl.BlockSpec(memory_space=pl.ANY),
            scratch_shapes=[pltpu.SemaphoreType.DMA]),
        input_output_aliases={2: 0},
        compiler_params=_cparams("arbitrary"),
        name="dispatch",
    )(dest, h2_flat, xs0)


def _expert_kernel(blk_expert_ref, n_used_ref, x_ref, w1_ref, w3_ref, w2_ref, y_ref):
    del blk_expert_ref
    used = pl.program_id(0) < n_used_ref[0]

    @pl.when(used)
    def _():
        xb = x_ref[...].astype(BF16)
        h1 = _dot(xb, w1_ref[0].astype(BF16))
        h3 = _dot(xb, w3_ref[0].astype(BF16))
        act = (h1 * _sigmoid(h1)) * h3
        y_ref[...] = _dot(act.astype(BF16), w2_ref[0].astype(BF16))

    @pl.when(jnp.logical_not(used))
    def _():
        y_ref[...] = jnp.zeros_like(y_ref)


def _experts(xs, blk_expert, n_used, w1, w3, w2):
    cap, d = xs.shape
    rows = MOE_ROWS
    de = w1.shape[-1]
    xblk = lambda i, be, nu: (jnp.minimum(i, nu[0] - 1), 0)
    wblk = lambda i, be, nu: (be[i], 0, 0)
    return pl.pallas_call(
        _expert_kernel,
        out_shape=jax.ShapeDtypeStruct((cap, d), F32),
        grid_spec=pltpu.PrefetchScalarGridSpec(
            num_scalar_prefetch=2,
            grid=(cap // rows,),
            in_specs=[pl.BlockSpec((rows, d), xblk),
                      pl.BlockSpec((1, d, de), wblk), pl.BlockSpec((1, d, de), wblk),
                      pl.BlockSpec((1, de, d), wblk)],
            out_specs=pl.BlockSpec((rows, d), lambda i, be, nu: (i, 0))),
        compiler_params=_cparams("arbitrary"),
        name="experts",
    )(blk_expert, n_used, xs, w1, w3, w2)


def _combine_kernel(dest_ref, ys_ref, x1_ref, route_ref, mod_ref, g_ref, o_ref, ybuf, sem):
    rows = x1_ref.shape[0]
    base = pl.program_id(0) * rows

    def issue(r, carry):
        for k in range(TOP_K):
            d = dest_ref[TOP_K * (base + r) + k]
            pltpu.make_async_copy(ys_ref.at[pl.ds(d, 1)], ybuf.at[k, pl.ds(r, 1)], sem).start()
        return carry

    lax.fori_loop(0, rows, issue, 0)
    for k in range(TOP_K):
        pltpu.make_async_copy(ys_ref.at[pl.ds(0, rows)], ybuf.at[k], sem).wait()
    route = route_ref[...]
    w0 = route[:, ROUTE_LANES["w0"]:ROUTE_LANES["w0"] + 1]
    w1 = route[:, ROUTE_LANES["w1"]:ROUTE_LANES["w1"] + 1]
    y = ybuf[0] * w0 + ybuf[1] * w1
    o_ref[...] = x1_ref[...] + mod_ref[0, 5:6, :] * _rms_norm(y, g_ref[...])


def _combine(ys, dest, x1_flat, route_flat, mod3, g_post, seq):
    n_tok, d = x1_flat.shape
    rows = COMBINE_ROWS
    assert seq % rows == 0
    return pl.pallas_call(
        _combine_kernel,
        out_shape=jax.ShapeDtypeStruct((n_tok, d), F32),
        grid_spec=pltpu.PrefetchScalarGridSpec(
            num_scalar_prefetch=1,
            grid=(n_tok // rows,),
            in_specs=[pl.BlockSpec(memory_space=pl.ANY),
                      pl.BlockSpec((rows, d), lambda i, dest: (i, 0)),
                      pl.BlockSpec((rows, LANES), lambda i, dest: (i, 0)),
                      pl.BlockSpec((1, 6, d), lambda i, dest: (i * rows // seq, 0, 0)),
                      pl.BlockSpec((1, d), lambda i, dest: (0, 0))],
            out_specs=pl.BlockSpec((rows, d), lambda i, dest: (i, 0)),
            scratch_shapes=[pltpu.VMEM((TOP_K, rows, d), F32), pltpu.SemaphoreType.DMA]),
        compiler_params=_cparams("arbitrary"),
        name="combine",
    )(dest, ys, x1_flat, route_flat, mod3, g_post)


def _moe_layout(route_flat, counts):
    rows = MOE_ROWS
    n_tok = route_flat.shape[0]
    ids = route_flat[:, 0:TOP_K].astype(jnp.int32)
    rank = route_flat[:, TOP_K:2 * TOP_K].astype(jnp.int32)
    counts = counts.astype(jnp.int32)
    padded = (counts + rows - 1) // rows * rows
    pad_ends = jnp.cumsum(padded)
    pad_starts = pad_ends - padded
    dest = (pad_starts[ids] + rank).reshape(n_tok * TOP_K)
    n_blocks = (n_tok * TOP_K + N_EXPERTS * (rows - 1) + rows - 1) // rows
    blk_expert = jnp.minimum(
        jnp.searchsorted(pad_ends, jnp.arange(n_blocks, dtype=jnp.int32) * rows, side='right'),
        N_EXPERTS - 1).astype(jnp.int32)
    n_used = (pad_ends[-1] // rows).reshape(1).astype(jnp.int32)
    return dest, blk_expert, n_used, n_blocks * rows


def kernel(x, c, w_mod, b_mod, g_pre_mix, g_post_mix, g_pre_ffn, g_post_ffn, w_in, rel_bias, a_re, a_im, log_dt, ssm_b_re, ssm_b_im, ssm_c_re, ssm_c_im, d_skip, w_glu, b_glu, w_branch_attn, w_branch_ssm, w_out, w_router_group, b_router_group, w_router_expert, b_router_expert, w1, w3, w2):
    bsz, seq, d = x.shape
    depth = w_mod.shape[0]
    ssm_width = w_glu.shape[-1]
    for l in range(depth):
        mod3 = _modulation(c, w_mod[l], b_mod[l]).reshape(bsz, 6, d)
        q, k, v, u_tm, gates = _input_projection(
            x, mod3, g_pre_mix[l].reshape(1, d), w_in[l].astype(BF16), ssm_width)
        attn = _dilated_mixture_attention(q, k, v, rel_bias)
        ssm_tm = _ssm_branch(u_tm, bsz, a_re[l], a_im[l], log_dt[l], ssm_b_re[l], ssm_b_im[l],
                             ssm_c_re[l], ssm_c_im[l], d_skip[l], w_glu[l], b_glu[l])
        x1, h2, route, counts = _merge_and_route(
            x, gates, attn, ssm_tm, mod3, g_post_mix[l].reshape(1, d), g_pre_ffn[l].reshape(1, d),
            w_branch_attn[l].astype(BF16), w_branch_ssm[l].astype(BF16), w_out[l].astype(BF16),
            w_router_group[l], b_router_group[l], w_router_expert[l], b_router_expert[l])
        route_flat = route.reshape(bsz * seq, LANES)
        dest, blk_expert, n_used, cap = _moe_layout(route_flat, counts[0, :N_EXPERTS])
        xs = _dispatch(h2.reshape(bsz * seq, d), dest, cap)
        ys = _experts(xs, blk_expert, n_used, w1[l], w3[l], w2[l])
        x = _combine(ys, dest, x1.reshape(bsz * seq, d), route_flat, mod3,
                     g_post_ffn[l].reshape(1, d), seq).reshape(bsz, seq, d)
    return x
```

```python
import functools
import math

import numpy as np
import jax
import jax.numpy as jnp
from jax import lax
from jax.experimental import pallas as pl
from jax.experimental.pallas import tpu as pltpu

F32 = jnp.float32
BF16 = jnp.bfloat16

N_HEADS = 8
HEAD_DIM = 64
ATTN_WIDTH = N_HEADS * HEAD_DIM
DILATION_PATTERNS = ((128, 1), (512, 4), (2048, 16))
NUM_BUCKETS = 32
MAX_DISTANCE = 2048
N_EXPERT_GROUPS = 4
EXPERTS_PER_GROUP = 8
N_EXPERTS = N_EXPERT_GROUPS * EXPERTS_PER_GROUP
TOP_K = 2
RMS_EPS = 1e-6
NEG_INF = -1e30

LANES = 128
SUBLANES = 8
VMEM_LIMIT_BYTES = 56 * 1024 * 1024

ATTN_BLK = 128
QKV_ROWS = 512
TIME_TILE = 128
MERGE_BATCH = 4
SSM_STEPS = 64
MOE_ROWS = 256
DISPATCH_ROWS = 512
COMBINE_ROWS = 256


def _cparams(*sem):
    return pltpu.CompilerParams(dimension_semantics=sem, vmem_limit_bytes=VMEM_LIMIT_BYTES)


def _sigmoid(x):
    return 1.0 / (1.0 + jnp.exp(-x))


def _dot(a, b):
    return jnp.dot(a, b, preferred_element_type=F32)


def _split_bf16(a):
    hi = a.astype(BF16)
    lo = (a - hi.astype(F32)).astype(BF16)
    return hi, lo


def _dot_split(a, w_hi, w_lo):
    a_hi, a_lo = _split_bf16(a)
    return _dot(a_hi, w_hi) + _dot(a_lo, w_hi) + _dot(a_hi, w_lo)


def _rms_norm(x, gain):
    ms = jnp.mean(x * x, axis=-1, keepdims=True)
    return x * lax.rsqrt(ms + RMS_EPS) * gain


def _lane_concat(ref_slabs):
    return jnp.concatenate(ref_slabs, axis=-1)


def _mod_kernel(c_ref, w_ref, b_ref, o_ref):
    c = c_ref[...]
    a = c * _sigmoid(c)
    w_hi, w_lo = _split_bf16(w_ref[...])
    o_ref[...] = _dot_split(a, w_hi, w_lo) + b_ref[...]


def _modulation(c, w_mod, b_mod):
    bsz, d = c.shape
    n = w_mod.shape[1]
    tn = 1024
    return pl.pallas_call(
        _mod_kernel,
        out_shape=jax.ShapeDtypeStruct((bsz, n), F32),
        grid=(n // tn,),
        in_specs=[pl.BlockSpec((bsz, d), lambda j: (0, 0)),
                  pl.BlockSpec((d, tn), lambda j: (0, j)),
                  pl.BlockSpec((1, tn), lambda j: (0, j))],
        out_specs=pl.BlockSpec((bsz, tn), lambda j: (0, j)),
        compiler_params=_cparams("arbitrary"),
        name="mod",
    )(c, w_mod, b_mod.reshape(1, n))


def _qkv_kernel(x_ref, mod_ref, g_ref, w_ref, *rest):
    n_pat = len(DILATION_PATTERNS)
    out_refs, slab = rest[:3 * n_pat], rest[3 * n_pat]
    h = _rms_norm(x_ref[0], g_ref[...]) * (1.0 + mod_ref[0, 1:2, :]) + mod_ref[0, 0:1, :]
    res = _dot(h.astype(BF16), w_ref[...])
    rows = res.shape[0]
    n_slab = res.shape[1] // LANES
    per_tensor = ATTN_WIDTH // LANES
    for s in range(n_slab):
        piece = res[:, s * LANES:(s + 1) * LANES]
        if s < per_tensor:
            piece = piece * (HEAD_DIM ** -0.5)
        slab[s] = piece
    for p, (_, dil) in enumerate(DILATION_PATTERNS):
        sub = rows // dil
        for t in range(3):
            out = out_refs[3 * p + t]
            for r in range(dil):
                pieces = [slab[t * per_tensor + s, pl.ds(r, sub, stride=dil), :]
                          if dil > 1 else slab[t * per_tensor + s]
                          for s in range(per_tensor)]
                out[0, r] = _lane_concat(pieces).astype(out.dtype)


def _qkv_projection(x, mod3, g_pre, w_qkv):
    bsz, seq, d = x.shape
    tm = QKV_ROWS
    out_shape, out_specs = [], []
    for _, dil in DILATION_PATTERNS:
        assert tm % (dil * 2 * SUBLANES) == 0
        for _ in range(3):
            out_shape.append(jax.ShapeDtypeStruct((bsz, dil, seq // dil, ATTN_WIDTH), BF16))
            out_specs.append(pl.BlockSpec((1, dil, tm // dil, ATTN_WIDTH),
                                          lambda b, i: (b, 0, i, 0)))
    return pl.pallas_call(
        _qkv_kernel,
        out_shape=out_shape,
        grid=(bsz, seq // tm),
        in_specs=[pl.BlockSpec((1, tm, d), lambda b, i: (b, i, 0)),
                  pl.BlockSpec((1, 6, d), lambda b, i: (b, 0, 0)),
                  pl.BlockSpec((1, d), lambda b, i: (0, 0)),
                  pl.BlockSpec(w_qkv.shape, lambda b, i: (0, 0))],
        out_specs=out_specs,
        scratch_shapes=[pltpu.VMEM((w_qkv.shape[1] // LANES, tm, LANES), F32)],
        compiler_params=_cparams("arbitrary", "arbitrary"),
        name="qkv",
    )(x, mod3, g_pre, w_qkv)


def _ugate_kernel(x_ref, mod_ref, g_ref, w_ref, u_ref, gate_ref):
    bsz, tt, d = x_ref.shape
    shift = mod_ref[:, 0, :][:, None, :]
    scale = mod_ref[:, 1, :][:, None, :]
    h = _rms_norm(x_ref[...], g_ref[...]) * (1.0 + scale) + shift
    hb = h.reshape(bsz * tt, d).astype(BF16)
    n_slab = u_ref.shape[0]
    sw = n_slab * LANES
    u = _dot(hb, w_ref[:, 0:sw])
    for b in range(bsz):
        for s in range(n_slab):
            u_ref[s, pl.ds(b, tt, stride=bsz), :] = u[b * tt:(b + 1) * tt, s * LANES:(s + 1) * LANES]
    gw = gate_ref.shape[-1]
    chunk = 512
    for c0 in range(0, gw, chunk):
        g = _sigmoid(_dot(hb, w_ref[:, sw + c0:sw + c0 + chunk]))
        gate_ref[:, :, c0:c0 + chunk] = g.reshape(bsz, tt, chunk).astype(BF16)


def _ugate_projection(x, mod3, g_pre, w_ug, ssm_width):
    bsz, seq, d = x.shape
    tt = TIME_TILE
    gw = w_ug.shape[1] - ssm_width
    n_slab = ssm_width // LANES
    return pl.pallas_call(
        _ugate_kernel,
        out_shape=(jax.ShapeDtypeStruct((n_slab, seq * bsz, LANES), F32),
                   jax.ShapeDtypeStruct((bsz, seq, gw), BF16)),
        grid=(seq // tt,),
        in_specs=[pl.BlockSpec((bsz, tt, d), lambda i: (0, i, 0)),
                  pl.BlockSpec((bsz, 6, d), lambda i: (0, 0, 0)),
                  pl.BlockSpec((1, d), lambda i: (0, 0)),
                  pl.BlockSpec(w_ug.shape, lambda i: (0, 0))],
        out_specs=(pl.BlockSpec((n_slab, tt * bsz, LANES), lambda i: (0, i, 0)),
                   pl.BlockSpec((bsz, tt, gw), lambda i: (0, i, 0))),
        compiler_params=_cparams("arbitrary"),
        name="ugate",
    )(x, mod3, g_pre, w_ug)


def _t5_bucket_np(dist):
    exact = NUM_BUCKETS // 2
    d_f = np.maximum(dist, exact).astype(np.float32)
    large = exact + (np.log(d_f / np.float32(exact)) / np.float32(math.log(MAX_DISTANCE / exact))
                     * np.float32(NUM_BUCKETS - exact)).astype(np.int32)
    return np.where(dist < exact, dist, np.minimum(large, NUM_BUCKETS - 1))


def _bucket_map_t(dil):
    blk = ATTN_BLK
    ki = np.arange(2 * blk)[:, None]
    qi = np.arange(blk)[None, :]
    return _t5_bucket_np(np.maximum(blk + qi - ki, 0) * dil).astype(np.int32)


def _attn_kernel(relb_ref, q_ref, kc_ref, kp_ref, vc_ref, vp_ref, bucket_ref,
                 o_ref, lse_ref, kbuf, vbuf, bias_t, *, n_sub):
    blk = ATTN_BLK
    first_call = jnp.logical_and(pl.program_id(0) == 0,
                                 jnp.logical_and(pl.program_id(1) == 0, pl.program_id(2) == 0))

    @pl.when(first_call)
    def _():
        bucket = bucket_ref[...]
        ki = lax.broadcasted_iota(jnp.int32, bucket.shape, 0)
        qi = lax.broadcasted_iota(jnp.int32, bucket.shape, 1)
        dist = blk + qi - ki
        band = jnp.logical_and(dist >= 0, dist <= blk)
        band_first = jnp.logical_and(band, ki >= blk)

        def per_head(h, carry):
            acc = jnp.zeros(bucket.shape, F32)
            for b in range(NUM_BUCKETS):
                acc = jnp.where(bucket == b, relb_ref[b, h], acc)
            bias_t[0, h] = jnp.where(band_first, acc, NEG_INF)
            bias_t[1, h] = jnp.where(band, acc, NEG_INF)
            return carry

        lax.fori_loop(0, N_HEADS, per_head, 0)

    first_step = pl.program_id(2) == 0
    kbuf[0:blk, :] = kp_ref[0, 0]
    kbuf[blk:, :] = kc_ref[0, 0]
    vbuf[0:blk, :] = vp_ref[0, 0]
    vbuf[blk:, :] = vc_ref[0, 0]

    lane = lax.broadcasted_iota(jnp.int32, (1, LANES), 1)
    lo_half = lane < HEAD_DIM
    bd_row = lax.broadcasted_iota(jnp.int32, (4 * blk, LANES), 0)
    bd_col = lax.broadcasted_iota(jnp.int32, (4 * blk, LANES), 1)
    ones_bd = ((bd_row < 2 * blk) == (bd_col < HEAD_DIM)).astype(F32).astype(BF16)
    contract_last = (((1,), (1,)), ((), ()))
    contract_first = (((0,), (0,)), ((), ()))

    def sub_block(i, carry):
        r0 = pl.multiple_of(i * blk, blk)
        q = q_ref[0, 0, pl.ds(r0, blk), :]
        kk = kbuf[pl.ds(r0, 2 * blk), :]
        vv = vbuf[pl.ds(r0, 2 * blk), :]
        variant = jnp.where(jnp.logical_and(first_step, i == 0), 0, 1)
        for j in range(N_HEADS // 2):
            cols = slice(j * LANES, (j + 1) * LANES)
            qj, kj, vj = q[:, cols], kk[:, cols], vv[:, cols]
            probs_t, maxes = [], []
            for hh in range(2):
                sel = lo_half if hh == 0 else jnp.logical_not(lo_half)
                qm = jnp.where(sel, qj, jnp.zeros_like(qj))
                s_t = lax.dot_general(kj, qm, contract_last, preferred_element_type=F32)
                s_t = s_t + bias_t[variant, 2 * j + hh]
                m = jnp.max(s_t, axis=0, keepdims=True)
                probs_t.append(jnp.exp(s_t - m).astype(BF16))
                maxes.append(m)
            p2_t = jnp.concatenate(probs_t, axis=0)
            v_bd = jnp.concatenate([jnp.where(lo_half, vj, jnp.zeros_like(vj)),
                                    jnp.where(lo_half, jnp.zeros_like(vj), vj)], axis=0)
            rhs = jnp.concatenate([v_bd, ones_bd], axis=1)
            ol = lax.dot_general(p2_t, rhs, contract_first, preferred_element_type=F32)
            o2, l2 = ol[:, :LANES], ol[:, LANES:]
            m_t = jnp.concatenate([jnp.broadcast_to(maxes[0], (HEAD_DIM, blk)),
                                   jnp.broadcast_to(maxes[1], (HEAD_DIM, blk))], axis=0)
            o_ref[0, 0, pl.ds(r0, blk), cols] = (o2 / l2).astype(o_ref.dtype)
            lse_ref[0, 0, pl.ds(r0, blk), cols] = m_t.T + jnp.log(l2)
        return carry

    lax.fori_loop(0, n_sub, sub_block, 0)


def _attention_pattern(q, k, v, rel_bias, dil):
    bsz, _, sub_len, aw = q.shape
    blk = ATTN_BLK
    assert sub_len % blk == 0
    tq = min(512, sub_len)
    n_sub = tq // blk
    ratio = tq // blk
    cur = lambda b, r, n: (b, r, n, 0)
    prev = lambda b, r, n: (b, r, jnp.maximum(n * ratio - 1, 0), 0)
    blk_cur = pl.BlockSpec((1, 1, tq, aw), cur)
    blk_prev = pl.BlockSpec((1, 1, blk, aw), prev)
    bucket = jnp.asarray(_bucket_map_t(dil))
    o_dtype = BF16 if TIME_TILE // dil >= 2 * SUBLANES else F32
    return pl.pallas_call(
        functools.partial(_attn_kernel, n_sub=n_sub),
        out_shape=(jax.ShapeDtypeStruct(q.shape, o_dtype), jax.ShapeDtypeStruct(q.shape, F32)),
        grid=(bsz, dil, sub_len // tq),
        in_specs=[pl.BlockSpec(memory_space=pltpu.SMEM),
                  blk_cur, blk_cur, blk_prev, blk_cur, blk_prev,
                  pl.BlockSpec(bucket.shape, lambda b, r, n: (0, 0))],
        out_specs=(blk_cur, blk_cur),
        scratch_shapes=[pltpu.VMEM((tq + blk, aw), BF16), pltpu.VMEM((tq + blk, aw), BF16),
                        pltpu.VMEM((2, N_HEADS, 2 * blk, blk), F32)],
        compiler_params=_cparams("arbitrary", "arbitrary", "arbitrary"),
        name=f"attn_dil{dil}",
    )(rel_bias.astype(F32), q, k, k, v, v, bucket)


def _ssm_kernel(u_ref, bmat_ref, cmat_ref, ar_ref, ai_ref, dskip_ref, wglu_ref, bglu_ref,
                o_ref, hbuf, hstate, *, n_steps):
    @pl.when(pl.program_id(0) == 0)
    def _():
        hstate[...] = jnp.zeros_like(hstate)

    n_slab = u_ref.shape[0]
    u = _lane_concat([u_ref[s] for s in range(n_slab)])
    hbuf[...] = _dot(u.astype(BF16), bmat_ref[...])
    n_state = hbuf.shape[1] // 2
    half = n_state // 2
    for part in range(2):
        re_cols = slice(part * half, (part + 1) * half)
        im_cols = slice(n_state + part * half, n_state + (part + 1) * half)
        ar = ar_ref[:, re_cols]
        ai = ai_ref[:, re_cols]

        def step(t, carry, re_cols=re_cols, im_cols=im_cols, ar=ar, ai=ai):
            hr, hi = carry
            r0 = pl.multiple_of(t * SUBLANES, SUBLANES)
            nr = ar * hr - ai * hi + hbuf[pl.ds(r0, SUBLANES), re_cols]
            ni = ar * hi + ai * hr + hbuf[pl.ds(r0, SUBLANES), im_cols]
            hbuf[pl.ds(r0, SUBLANES), re_cols] = nr
            hbuf[pl.ds(r0, SUBLANES), im_cols] = ni
            return nr, ni

        hr, hi = lax.fori_loop(0, n_steps, step, (hstate[:, re_cols], hstate[:, im_cols]),
                               unroll=2)
        hstate[:, re_cols] = hr
        hstate[:, im_cols] = hi

    y = _dot(hbuf[...].astype(BF16), cmat_ref[...]) + dskip_ref[...] * u
    y = 0.5 * y * (1.0 + jnp.tanh(math.sqrt(2.0 / math.pi) * (y + 0.044715 * (y * y * y))))
    z = _dot(y.astype(BF16), wglu_ref[...]) + bglu_ref[...]
    out = y * _sigmoid(z)
    for s in range(n_slab):
        o_ref[s] = out[:, s * LANES:(s + 1) * LANES]


def _ssm_params(a_re, a_im, log_dt, b_re, b_im, c_re, c_im, bsz):
    g, p = a_re.shape
    hg = b_re.shape[-1]
    dt = jnp.exp(log_dt.astype(F32))[:, None]
    a_re, a_im = a_re.astype(F32), a_im.astype(F32)
    mag = jnp.exp(a_re * dt)
    abar_re = mag * jnp.cos(a_im * dt)
    abar_im = mag * jnp.sin(a_im * dt)
    den = a_re * a_re + a_im * a_im
    q_re = ((abar_re - 1.0) * a_re + abar_im * a_im) / den
    q_im = (abar_im * a_re - (abar_re - 1.0) * a_im) / den
    b_re, b_im = b_re.astype(F32), b_im.astype(F32)
    bb_re = q_re[..., None] * b_re - q_im[..., None] * b_im
    bb_im = q_re[..., None] * b_im + q_im[..., None] * b_re
    eye = jnp.eye(g, dtype=F32)

    def in_mat(t):
        return jnp.einsum('gph,gk->ghkp', t, eye).reshape(g * hg, g * p)

    def out_mat(t):
        return jnp.einsum('ghp,gk->gpkh', t, eye).reshape(g * p, g * hg)

    bmat = jnp.concatenate([in_mat(bb_re), in_mat(bb_im)], axis=1).astype(BF16)
    cmat = jnp.concatenate([out_mat(c_re.astype(F32)), -out_mat(c_im.astype(F32))],
                           axis=0).astype(BF16)
    ar = jnp.broadcast_to(abar_re.reshape(1, g * p), (bsz, g * p))
    ai = jnp.broadcast_to(abar_im.reshape(1, g * p), (bsz, g * p))
    return bmat, cmat, ar, ai


def _ssm_branch(u_slabs, bsz, a_re, a_im, log_dt, b_re, b_im, c_re, c_im, d_skip, w_glu, b_glu):
    n_slab, n_rows, _ = u_slabs.shape
    width = n_slab * LANES
    assert bsz == SUBLANES
    bmat, cmat, ar, ai = _ssm_params(a_re, a_im, log_dt, b_re, b_im, c_re, c_im, bsz)
    n_state2 = bmat.shape[1]
    rows = SSM_STEPS * bsz
    const = lambda c: (0, 0)
    slab_spec = pl.BlockSpec((n_slab, rows, LANES), lambda c: (0, c, 0))
    return pl.pallas_call(
        functools.partial(_ssm_kernel, n_steps=SSM_STEPS),
        out_shape=jax.ShapeDtypeStruct(u_slabs.shape, F32),
        grid=(n_rows // rows,),
        in_specs=[slab_spec,
                  pl.BlockSpec(bmat.shape, const), pl.BlockSpec(cmat.shape, const),
                  pl.BlockSpec(ar.shape, const), pl.BlockSpec(ai.shape, const),
                  pl.BlockSpec((1, width), const), pl.BlockSpec((width, width), const),
                  pl.BlockSpec((1, width), const)],
        out_specs=slab_spec,
        scratch_shapes=[pltpu.VMEM((rows, n_state2), F32), pltpu.VMEM((bsz, n_state2), F32)],
        compiler_params=_cparams("arbitrary"),
        name="ssm",
    )(u_slabs, bmat, cmat, ar, ai, d_skip.reshape(1, width).astype(F32),
      w_glu.astype(BF16), b_glu.reshape(1, width).astype(F32))


ROUTE_LANES = {"id0": 0, "id1": 1, "rank0": 2, "rank1": 3, "w0": 4, "w1": 5}
GROUP_LANE0 = N_EXPERTS


def _merge_kernel(*refs, bsz_total):
    n_pat = len(DILATION_PATTERNS)
    x_ref, gate_ref = refs[0:2]
    attn_refs = refs[2:2 + 2 * n_pat]
    (ssm_ref, mod_ref, gpost_ref, gpre_ref, wba_ref, wbs_ref, wout_ref,
     wr_hi_ref, wr_lo_ref, br_ref) = refs[2 + 2 * n_pat:12 + 2 * n_pat]
    x1_ref, h2_ref, route_ref, count_ref = refs[12 + 2 * n_pat:16 + 2 * n_pat]
    o_tok, lse_tok, ssm_tok, carry = refs[16 + 2 * n_pat:]

    @pl.when(jnp.logical_and(pl.program_id(0) == 0, pl.program_id(1) == 0))
    def _():
        carry[...] = jnp.zeros_like(carry)

    nb, tt, d = x_ref.shape
    rows = nb * tt
    n_slab = ssm_ref.shape[0]
    b0 = pl.program_id(0) * nb

    for bb in range(nb):
        for s in range(n_slab):
            ssm_tok[bb * tt:(bb + 1) * tt, s * LANES:(s + 1) * LANES] = (
                ssm_ref[s, pl.ds(b0 + bb, tt, stride=bsz_total), :])

    acc = None
    for p, (_, dil) in enumerate(DILATION_PATTERNS):
        o_ref, lse_ref = attn_refs[2 * p], attn_refs[2 * p + 1]
        if dil == 1:
            o_p = o_ref[:, 0].astype(F32).reshape(rows, n_slab * LANES)
            lse_p = lse_ref[:, 0].reshape(rows, n_slab * LANES)
        else:
            sub = tt // dil
            for bb in range(nb):
                for r in range(dil):
                    o_blk = o_ref[bb, r].astype(F32)
                    l_blk = lse_ref[bb, r]
                    for s in range(n_slab):
                        dst = pl.ds(bb * tt + r, sub, stride=dil)
                        o_tok[s, dst, :] = o_blk[:, s * LANES:(s + 1) * LANES]
                        lse_tok[s, dst, :] = l_blk[:, s * LANES:(s + 1) * LANES]
            o_p = _lane_concat([o_tok[s] for s in range(n_slab)])
            lse_p = _lane_concat([lse_tok[s] for s in range(n_slab)])
        if acc is None:
            acc, lse_run = o_p, lse_p
        else:
            m = jnp.maximum(lse_run, lse_p)
            a = jnp.exp(lse_run - m)
            b = jnp.exp(lse_p - m)
            den = a + b
            acc = (a * acc + b * o_p) / den
            lse_run = m + jnp.log(den)
    attn = acc

    g_attn = gate_ref[:, :, 0:d].astype(F32).reshape(rows, d)
    g_ssm = gate_ref[:, :, d:].astype(F32).reshape(rows, d)
    merged = (g_attn * _dot(attn.astype(BF16), wba_ref[...])
              + g_ssm * _dot(ssm_tok[...].astype(BF16), wbs_ref[...]))
    y = _dot(merged.astype(BF16), wout_ref[...])
    gate1 = mod_ref[:, 2, :][:, None, :]
    shift2 = mod_ref[:, 3, :][:, None, :]
    scale2 = mod_ref[:, 4, :][:, None, :]
    x1 = x_ref[...] + gate1 * _rms_norm(y, gpost_ref[...]).reshape(nb, tt, d)
    x1_ref[...] = x1
    h2 = _rms_norm(x1, gpre_ref[...]) * (1.0 + scale2) + shift2
    h2_ref[...] = h2

    logits = _dot_split(h2.reshape(rows, d), wr_hi_ref[...], wr_lo_ref[...]) + br_ref[...]
    lane = lax.broadcasted_iota(jnp.int32, (rows, LANES), 1).astype(F32)
    big = float(LANES)
    is_group = jnp.logical_and(lane >= GROUP_LANE0, lane < GROUP_LANE0 + N_EXPERT_GROUPS)
    gl = jnp.where(is_group, logits, -jnp.inf)
    g_max = jnp.max(gl, axis=-1, keepdims=True)
    g_sel = jnp.min(jnp.where(gl == g_max, lane, big), axis=-1, keepdims=True) - GROUP_LANE0
    g_gate = 1.0 / jnp.sum(jnp.exp(gl - g_max), axis=-1, keepdims=True)
    lo = g_sel * EXPERTS_PER_GROUP
    in_group = jnp.logical_and(lane >= lo, lane < lo + EXPERTS_PER_GROUP)
    el = jnp.where(in_group, logits, -jnp.inf)
    t0 = jnp.max(el, axis=-1, keepdims=True)
    i0 = jnp.min(jnp.where(el == t0, lane, big), axis=-1, keepdims=True)
    el1 = jnp.where(lane == i0, -jnp.inf, el)
    t1 = jnp.max(el1, axis=-1, keepdims=True)
    i1 = jnp.min(jnp.where(el1 == t1, lane, big), axis=-1, keepdims=True)
    e = jnp.exp(t1 - t0)
    w0 = g_gate / (1.0 + e)
    w1 = g_gate * e / (1.0 + e)

    hit0 = lane == i0
    hit1 = lane == i1
    onehot = jnp.logical_or(hit0, hit1).astype(F32)
    row = lax.broadcasted_iota(jnp.int32, (rows, rows), 0)
    col = lax.broadcasted_iota(jnp.int32, (rows, rows), 1)
    strict_lower = (col < row).astype(BF16)
    before = _dot(strict_lower, onehot.astype(BF16)) + carry[...]
    rank0 = jnp.sum(jnp.where(hit0, before, 0.0), axis=-1, keepdims=True)
    rank1 = jnp.sum(jnp.where(hit1, before, 0.0), axis=-1, keepdims=True)
    carry[...] = carry[...] + jnp.sum(onehot, axis=0, keepdims=True)
    count_ref[...] = jnp.broadcast_to(carry[...], count_ref.shape)

    route = jnp.zeros((rows, LANES), F32)
    for name, val in (("id0", i0), ("id1", i1), ("rank0", rank0), ("rank1", rank1),
                      ("w0", w0), ("w1", w1)):
        route = jnp.where(lane == ROUTE_LANES[name], val, route)
    route_ref[...] = route.reshape(nb, tt, LANES)


def _merge_and_route(x, gates, attn_outs, ssm_slabs, mod3, g_post, g_pre, wba, wbs, wout,
                     w_rg, b_rg, w_re, b_re):
    bsz, seq, d = x.shape
    tt, nb = TIME_TILE, MERGE_BATCH
    aw = ATTN_WIDTH
    n_slab = ssm_slabs.shape[0]
    assert aw == n_slab * LANES
    wr = jnp.zeros((d, LANES), F32).at[:, :N_EXPERTS].set(w_re.astype(F32))
    wr = wr.at[:, GROUP_LANE0:GROUP_LANE0 + N_EXPERT_GROUPS].set(w_rg.astype(F32))
    br = jnp.zeros((1, LANES), F32).at[0, :N_EXPERTS].set(b_re.astype(F32))
    br = br.at[0, GROUP_LANE0:GROUP_LANE0 + N_EXPERT_GROUPS].set(b_rg.astype(F32))
    wr_hi, wr_lo = _split_bf16(wr)
    tok = lambda h, i: (h, i, 0)
    const = lambda h, i: (0, 0)
    attn_args, attn_specs = [], []
    for (o_p, lse_p), (_, dil) in zip(attn_outs, DILATION_PATTERNS):
        spec = pl.BlockSpec((nb, dil, tt // dil, aw), lambda h, i: (h, 0, i, 0))
        attn_args += [o_p, lse_p]
        attn_specs += [spec, spec]
    return pl.pallas_call(
        functools.partial(_merge_kernel, bsz_total=bsz),
        out_shape=(jax.ShapeDtypeStruct((bsz, seq, d), F32),
                   jax.ShapeDtypeStruct((bsz, seq, d), F32),
                   jax.ShapeDtypeStruct((bsz, seq, LANES), F32),
                   jax.ShapeDtypeStruct((SUBLANES, LANES), F32)),
        grid=(bsz // nb, seq // tt),
        in_specs=[pl.BlockSpec((nb, tt, d), tok),
                  pl.BlockSpec((nb, tt, gates.shape[-1]), tok)]
                 + attn_specs
                 + [pl.BlockSpec((n_slab, tt * bsz, LANES), lambda h, i: (0, i, 0)),
                    pl.BlockSpec((nb, 6, d), lambda h, i: (h, 0, 0)),
                    pl.BlockSpec((1, d), const), pl.BlockSpec((1, d), const),
                    pl.BlockSpec(wba.shape, const), pl.BlockSpec(wbs.shape, const),
                    pl.BlockSpec(wout.shape, const),
                    pl.BlockSpec((d, LANES), const), pl.BlockSpec((d, LANES), const),
                    pl.BlockSpec((1, LANES), const)],
        out_specs=(pl.BlockSpec((nb, tt, d), tok), pl.BlockSpec((nb, tt, d), tok),
                   pl.BlockSpec((nb, tt, LANES), tok),
                   pl.BlockSpec((SUBLANES, LANES), const)),
        scratch_shapes=[pltpu.VMEM((n_slab, nb * tt, LANES), F32),
                        pltpu.VMEM((n_slab, nb * tt, LANES), F32),
                        pltpu.VMEM((nb * tt, n_slab * LANES), F32),
                        pltpu.VMEM((1, LANES), F32)],
        compiler_params=_cparams("arbitrary", "arbitrary"),
        name="merge",
    )(x, gates, *attn_args, ssm_slabs, mod3, g_post, g_pre, wba, wbs, wout, wr_hi, wr_lo, br)


def _row_copy_out(src_ref, dst_hbm, sem, r, d):
    return pltpu.make_async_copy(src_ref.at[pl.ds(r, 1)], dst_hbm.at[pl.ds(d, 1)], sem)


def _dispatch_kernel(dest_ref, h_ref, xs_in_ref, xs_ref, sem):
    del xs_in_ref
    rows = h_ref.shape[0]
    base = pl.program_id(0) * rows

    def issue(r, carry):
        for k in range(TOP_K):
            _row_copy_out(h_ref, xs_ref, sem, r, dest_ref[TOP_K * (base + r) + k]).start()
        return carry

    lax.fori_loop(0, rows, issue, 0)
    for _ in range(TOP_K):
        pltpu.make_async_copy(h_ref, xs_ref.at[pl.ds(0, rows)], sem).wait()


def _dispatch(h2_flat, dest, cap):
    n_tok, d = h2_flat.shape
    rows = DISPATCH_ROWS
    xs0 = jnp.zeros((cap, d), h2_flat.dtype)
    return pl.pallas_call(
        _dispatch_kernel,
        out_shape=jax.ShapeDtypeStruct((cap, d), h2_flat.dtype),
        grid_spec=pltpu.PrefetchScalarGridSpec(
            num_scalar_prefetch=1,
            grid=(n_tok // rows,),
            in_specs=[pl.BlockSpec((rows, d), lambda i, dest: (i, 0)),
                      pl.BlockSpec(memory_space=pl.ANY)],
            out_specs=pl.BlockSpec(memory_space=pl.ANY),
            scratch_shapes=[pltpu.SemaphoreType.DMA]),
        input_output_aliases={2: 0},
        compiler_params=_cparams("arbitrary"),
        name="dispatch",
    )(dest, h2_flat, xs0)


def _expert_kernel(blk_expert_ref, n_used_ref, x_ref, w1_ref, w3_ref, w2_ref, y_ref):
    del blk_expert_ref
    used = pl.program_id(0) < n_used_ref[0]

    @pl.when(used)
    def _():
        xb = x_ref[...].astype(BF16)
        h1 = _dot(xb, w1_ref[0].astype(BF16))
        h3 = _dot(xb, w3_ref[0].astype(BF16))
        act = (h1 * _sigmoid(h1)) * h3
        y_ref[...] = _dot(act.astype(BF16), w2_ref[0].astype(BF16))

    @pl.when(jnp.logical_not(used))
    def _():
        y_ref[...] = jnp.zeros_like(y_ref)


def _experts(xs, blk_expert, n_used, w1, w3, w2):
    cap, d = xs.shape
    rows = MOE_ROWS
    de = w1.shape[-1]
    xblk = lambda i, be, nu: (jnp.minimum(i, nu[0] - 1), 0)
    wblk = lambda i, be, nu: (be[i], 0, 0)
    return pl.pallas_call(
        _expert_kernel,
        out_shape=jax.ShapeDtypeStruct((cap, d), F32),
        grid_spec=pltpu.PrefetchScalarGridSpec(
            num_scalar_prefetch=2,
            grid=(cap // rows,),
            in_specs=[pl.BlockSpec((rows, d), xblk),
                      pl.BlockSpec((1, d, de), wblk), pl.BlockSpec((1, d, de), wblk),
                      pl.BlockSpec((1, de, d), wblk)],
            out_specs=pl.BlockSpec((rows, d), lambda i, be, nu: (i, 0))),
        compiler_params=_cparams("arbitrary"),
        name="experts",
    )(blk_expert, n_used, xs, w1, w3, w2)


def _combine_kernel(dest_ref, ys_ref, x1_ref, route_ref, mod_ref, g_ref, o_ref, ybuf, sem):
    rows = x1_ref.shape[0]
    base = pl.program_id(0) * rows

    def issue(r, carry):
        for k in range(TOP_K):
            d = dest_ref[TOP_K * (base + r) + k]
            pltpu.make_async_copy(ys_ref.at[pl.ds(d, 1)], ybuf.at[k, pl.ds(r, 1)], sem).start()
        return carry

    lax.fori_loop(0, rows, issue, 0)
    for k in range(TOP_K):
        pltpu.make_async_copy(ys_ref.at[pl.ds(0, rows)], ybuf.at[k], sem).wait()
    route = route_ref[...]
    w0 = route[:, ROUTE_LANES["w0"]:ROUTE_LANES["w0"] + 1]
    w1 = route[:, ROUTE_LANES["w1"]:ROUTE_LANES["w1"] + 1]
    y = ybuf[0] * w0 + ybuf[1] * w1
    o_ref[...] = x1_ref[...] + mod_ref[0, 5:6, :] * _rms_norm(y, g_ref[...])


def _combine(ys, dest, x1_flat, route_flat, mod3, g_post, seq):
    n_tok, d = x1_flat.shape
    rows = COMBINE_ROWS
    assert seq % rows == 0
    return pl.pallas_call(
        _combine_kernel,
        out_shape=jax.ShapeDtypeStruct((n_tok, d), F32),
        grid_spec=pltpu.PrefetchScalarGridSpec(
            num_scalar_prefetch=1,
            grid=(n_tok // rows,),
            in_specs=[pl.BlockSpec(memory_space=pl.ANY),
                      pl.BlockSpec((rows, d), lambda i, dest: (i, 0)),
                      pl.BlockSpec((rows, LANES), lambda i, dest: (i, 0)),
                      pl.BlockSpec((1, 6, d), lambda i, dest: (i * rows // seq, 0, 0)),
                      pl.BlockSpec((1, d), lambda i, dest: (0, 0))],
            out_specs=pl.BlockSpec((rows, d), lambda i, dest: (i, 0)),
            scratch_shapes=[pltpu.VMEM((TOP_K, rows, d), F32), pltpu.SemaphoreType.DMA]),
        compiler_params=_cparams("arbitrary"),
        name="combine",
    )(dest, ys, x1_flat, route_flat, mod3, g_post)


def _moe_layout(route_flat, counts):
    rows = MOE_ROWS
    n_tok = route_flat.shape[0]
    ids = route_flat[:, 0:TOP_K].astype(jnp.int32)
    rank = route_flat[:, TOP_K:2 * TOP_K].astype(jnp.int32)
    counts = counts.astype(jnp.int32)
    padded = (counts + rows - 1) // rows * rows
    pad_ends = jnp.cumsum(padded)
    pad_starts = pad_ends - padded
    expert = jnp.arange(N_EXPERTS, dtype=jnp.int32)
    start_of = jnp.sum(jnp.where(ids[..., None] == expert, pad_starts, 0), axis=-1)
    dest = (start_of + rank).reshape(n_tok * TOP_K)
    n_blocks = (n_tok * TOP_K + N_EXPERTS * (rows - 1) + rows - 1) // rows
    blk_row0 = jnp.arange(n_blocks, dtype=jnp.int32) * rows
    blk_expert = jnp.minimum(jnp.sum(pad_ends[None, :] <= blk_row0[:, None], axis=-1),
                             N_EXPERTS - 1).astype(jnp.int32)
    n_used = (pad_ends[-1] // rows).reshape(1).astype(jnp.int32)
    return dest, blk_expert, n_used, n_blocks * rows


def kernel(x, c, w_mod, b_mod, g_pre_mix, g_post_mix, g_pre_ffn, g_post_ffn, w_in, rel_bias, a_re, a_im, log_dt, ssm_b_re, ssm_b_im, ssm_c_re, ssm_c_im, d_skip, w_glu, b_glu, w_branch_attn, w_branch_ssm, w_out, w_router_group, b_router_group, w_router_expert, b_router_expert, w1, w3, w2):
    bsz, seq, d = x.shape
    depth = w_mod.shape[0]
    ssm_width = w_glu.shape[-1]
    n_pat = len(DILATION_PATTERNS)
    for l in range(depth):
        mod3 = _modulation(c, w_mod[l], b_mod[l]).reshape(bsz, 6, d)
        w_in_l = w_in[l].astype(BF16)
        g_pre = g_pre_mix[l].reshape(1, d)
        qkv = _qkv_projection(x, mod3, g_pre, w_in_l[:, :3 * ATTN_WIDTH])
        u_slabs, gates = _ugate_projection(x, mod3, g_pre, w_in_l[:, 3 * ATTN_WIDTH:], ssm_width)
        attn_outs = [_attention_pattern(*qkv[3 * p:3 * p + 3], rel_bias, DILATION_PATTERNS[p][1])
                     for p in range(n_pat)]
        ssm_slabs = _ssm_branch(u_slabs, bsz, a_re[l], a_im[l], log_dt[l], ssm_b_re[l],
                                ssm_b_im[l], ssm_c_re[l], ssm_c_im[l], d_skip[l], w_glu[l], b_glu[l])
        x1, h2, route, counts = _merge_and_route(
            x, gates, attn_outs, ssm_slabs, mod3, g_post_mix[l].reshape(1, d),
            g_pre_ffn[l].reshape(1, d), w_branch_attn[l].astype(BF16),
            w_branch_ssm[l].astype(BF16), w_out[l].astype(BF16),
            w_router_group[l], b_router_group[l], w_router_expert[l], b_router_expert[l])
        route_flat = route.reshape(bsz * seq, LANES)
        dest, blk_expert, n_used, cap = _moe_layout(route_flat, counts[0, :N_EXPERTS])
        xs = _dispatch(h2.reshape(bsz * seq, d), dest, cap)
        ys = _experts(xs, blk_expert, n_used, w1[l], w3[l], w2[l])
        x = _combine(ys, dest, x1.reshape(bsz * seq, d), route_flat, mod3,
                     g_post_ffn[l].reshape(1, d), seq).reshape(bsz, seq, d)
    return x
```

```python
import functools
import math

import numpy as np
import jax
import jax.numpy as jnp
from jax import lax
from jax.experimental import pallas as pl
from jax.experimental.pallas import tpu as pltpu

F32 = jnp.float32
BF16 = jnp.bfloat16

N_HEADS = 8
HEAD_DIM = 64
ATTN_WIDTH = N_HEADS * HEAD_DIM
DILATION_PATTERNS = ((128, 1), (512, 4), (2048, 16))
NUM_BUCKETS = 32
MAX_DISTANCE = 2048
N_EXPERT_GROUPS = 4
EXPERTS_PER_GROUP = 8
N_EXPERTS = N_EXPERT_GROUPS * EXPERTS_PER_GROUP
TOP_K = 2
RMS_EPS = 1e-6
NEG_INF = -1e30

LANES = 128
SUBLANES = 8
VMEM_LIMIT_BYTES = 56 * 1024 * 1024

ATTN_BLK = 128
QKV_ROWS = 512
TIME_TILE = 128
MERGE_BATCH = 4
SSM_STEPS = 64
MOE_ROWS = 256
DISPATCH_ROWS = 2048
COMBINE_ROWS = 256


def _cparams(*sem):
    return pltpu.CompilerParams(dimension_semantics=sem, vmem_limit_bytes=VMEM_LIMIT_BYTES)


def _sigmoid(x):
    return 1.0 / (1.0 + jnp.exp(-x))


def _dot(a, b):
    return jnp.dot(a, b, preferred_element_type=F32)


def _split_bf16(a):
    hi = a.astype(BF16)
    lo = (a - hi.astype(F32)).astype(BF16)
    return hi, lo


def _dot_split(a, w_hi, w_lo):
    a_hi, a_lo = _split_bf16(a)
    return _dot(a_hi, w_hi) + _dot(a_lo, w_hi) + _dot(a_hi, w_lo)


def _rms_norm(x, gain):
    ms = jnp.mean(x * x, axis=-1, keepdims=True)
    return x * lax.rsqrt(ms + RMS_EPS) * gain


def _lane_concat(ref_slabs):
    return jnp.concatenate(ref_slabs, axis=-1)


def _mod_kernel(c_ref, w_ref, b_ref, o_ref):
    c = c_ref[...]
    a = c * _sigmoid(c)
    w_hi, w_lo = _split_bf16(w_ref[...])
    o_ref[...] = _dot_split(a, w_hi, w_lo) + b_ref[...]


def _modulation(c, w_mod, b_mod):
    bsz, d = c.shape
    n = w_mod.shape[1]
    tn = 1024
    return pl.pallas_call(
        _mod_kernel,
        out_shape=jax.ShapeDtypeStruct((bsz, n), F32),
        grid=(n // tn,),
        in_specs=[pl.BlockSpec((bsz, d), lambda j: (0, 0)),
                  pl.BlockSpec((d, tn), lambda j: (0, j)),
                  pl.BlockSpec((1, tn), lambda j: (0, j))],
        out_specs=pl.BlockSpec((bsz, tn), lambda j: (0, j)),
        compiler_params=_cparams("arbitrary"),
        name="mod",
    )(c, w_mod, b_mod.reshape(1, n))


def _qkv_kernel(x_ref, mod_ref, g_ref, w_ref, *rest):
    n_pat = len(DILATION_PATTERNS)
    out_refs, slab = rest[:3 * n_pat], rest[3 * n_pat]
    h = _rms_norm(x_ref[0], g_ref[...]) * (1.0 + mod_ref[0, 1:2, :]) + mod_ref[0, 0:1, :]
    res = _dot(h.astype(BF16), w_ref[...])
    rows = res.shape[0]
    n_slab = res.shape[1] // LANES
    per_tensor = ATTN_WIDTH // LANES
    for s in range(n_slab):
        piece = res[:, s * LANES:(s + 1) * LANES]
        if s < per_tensor:
            piece = piece * (HEAD_DIM ** -0.5)
        slab[s] = piece
    for p, (_, dil) in enumerate(DILATION_PATTERNS):
        sub = rows // dil
        for t in range(3):
            out = out_refs[3 * p + t]
            for r in range(dil):
                pieces = [slab[t * per_tensor + s, pl.ds(r, sub, stride=dil), :]
                          if dil > 1 else slab[t * per_tensor + s]
                          for s in range(per_tensor)]
                out[0, r] = _lane_concat(pieces).astype(out.dtype)


def _qkv_projection(x, mod3, g_pre, w_qkv):
    bsz, seq, d = x.shape
    tm = QKV_ROWS
    out_shape, out_specs = [], []
    for _, dil in DILATION_PATTERNS:
        assert tm % (dil * 2 * SUBLANES) == 0
        for _ in range(3):
            out_shape.append(jax.ShapeDtypeStruct((bsz, dil, seq // dil, ATTN_WIDTH), BF16))
            out_specs.append(pl.BlockSpec((1, dil, tm // dil, ATTN_WIDTH),
                                          lambda b, i: (b, 0, i, 0)))
    return pl.pallas_call(
        _qkv_kernel,
        out_shape=out_shape,
        grid=(bsz, seq // tm),
        in_specs=[pl.BlockSpec((1, tm, d), lambda b, i: (b, i, 0)),
                  pl.BlockSpec((1, 6, d), lambda b, i: (b, 0, 0)),
                  pl.BlockSpec((1, d), lambda b, i: (0, 0)),
                  pl.BlockSpec(w_qkv.shape, lambda b, i: (0, 0))],
        out_specs=out_specs,
        scratch_shapes=[pltpu.VMEM((w_qkv.shape[1] // LANES, tm, LANES), F32)],
        compiler_params=_cparams("arbitrary", "arbitrary"),
        name="qkv",
    )(x, mod3, g_pre, w_qkv)


def _ugate_kernel(x_ref, mod_ref, g_ref, w_ref, u_ref, gate_ref):
    bsz, tt, d = x_ref.shape
    shift = mod_ref[:, 0, :][:, None, :]
    scale = mod_ref[:, 1, :][:, None, :]
    h = _rms_norm(x_ref[...], g_ref[...]) * (1.0 + scale) + shift
    hb = h.reshape(bsz * tt, d).astype(BF16)
    n_slab = u_ref.shape[0]
    sw = n_slab * LANES
    u = _dot(hb, w_ref[:, 0:sw])
    for b in range(bsz):
        for s in range(n_slab):
            u_ref[s, pl.ds(b, tt, stride=bsz), :] = u[b * tt:(b + 1) * tt, s * LANES:(s + 1) * LANES]
    gw = gate_ref.shape[-1]
    chunk = 512
    for c0 in range(0, gw, chunk):
        g = _sigmoid(_dot(hb, w_ref[:, sw + c0:sw + c0 + chunk]))
        gate_ref[:, :, c0:c0 + chunk] = g.reshape(bsz, tt, chunk).astype(BF16)


def _ugate_projection(x, mod3, g_pre, w_ug, ssm_width):
    bsz, seq, d = x.shape
    tt = TIME_TILE
    gw = w_ug.shape[1] - ssm_width
    n_slab = ssm_width // LANES
    return pl.pallas_call(
        _ugate_kernel,
        out_shape=(jax.ShapeDtypeStruct((n_slab, seq * bsz, LANES), F32),
                   jax.ShapeDtypeStruct((bsz, seq, gw), BF16)),
        grid=(seq // tt,),
        in_specs=[pl.BlockSpec((bsz, tt, d), lambda i: (0, i, 0)),
                  pl.BlockSpec((bsz, 6, d), lambda i: (0, 0, 0)),
                  pl.BlockSpec((1, d), lambda i: (0, 0)),
                  pl.BlockSpec(w_ug.shape, lambda i: (0, 0))],
        out_specs=(pl.BlockSpec((n_slab, tt * bsz, LANES), lambda i: (0, i, 0)),
                   pl.BlockSpec((bsz, tt, gw), lambda i: (0, i, 0))),
        compiler_params=_cparams("arbitrary"),
        name="ugate",
    )(x, mod3, g_pre, w_ug)


def _t5_bucket_np(dist):
    exact = NUM_BUCKETS // 2
    d_f = np.maximum(dist, exact).astype(np.float32)
    large = exact + (np.log(d_f / np.float32(exact)) / np.float32(math.log(MAX_DISTANCE / exact))
                     * np.float32(NUM_BUCKETS - exact)).astype(np.int32)
    return np.where(dist < exact, dist, np.minimum(large, NUM_BUCKETS - 1))


def _bucket_map_t(dil):
    blk = ATTN_BLK
    ki = np.arange(2 * blk)[:, None]
    qi = np.arange(blk)[None, :]
    return _t5_bucket_np(np.maximum(blk + qi - ki, 0) * dil).astype(np.int32)


def _attn_kernel(relb_ref, q_ref, kc_ref, kp_ref, vc_ref, vp_ref, bucket_ref,
                 o_ref, lse_ref, kbuf, vbuf, bias_t, *, n_sub):
    blk = ATTN_BLK
    first_call = jnp.logical_and(pl.program_id(0) == 0,
                                 jnp.logical_and(pl.program_id(1) == 0, pl.program_id(2) == 0))

    @pl.when(first_call)
    def _():
        bucket = bucket_ref[...]
        ki = lax.broadcasted_iota(jnp.int32, bucket.shape, 0)
        qi = lax.broadcasted_iota(jnp.int32, bucket.shape, 1)
        dist = blk + qi - ki
        band = jnp.logical_and(dist >= 0, dist <= blk)
        band_first = jnp.logical_and(band, ki >= blk)

        def per_head(h, carry):
            acc = jnp.zeros(bucket.shape, F32)
            for b in range(NUM_BUCKETS):
                acc = jnp.where(bucket == b, relb_ref[b, h], acc)
            bias_t[0, h] = jnp.where(band_first, acc, NEG_INF)
            bias_t[1, h] = jnp.where(band, acc, NEG_INF)
            return carry

        lax.fori_loop(0, N_HEADS, per_head, 0)

    first_step = pl.program_id(2) == 0
    kbuf[0:blk, :] = kp_ref[0, 0]
    kbuf[blk:, :] = kc_ref[0, 0]
    vbuf[0:blk, :] = vp_ref[0, 0]
    vbuf[blk:, :] = vc_ref[0, 0]

    lane = lax.broadcasted_iota(jnp.int32, (1, LANES), 1)
    lo_half = lane < HEAD_DIM
    bd_row = lax.broadcasted_iota(jnp.int32, (4 * blk, LANES), 0)
    bd_col = lax.broadcasted_iota(jnp.int32, (4 * blk, LANES), 1)
    ones_bd = ((bd_row < 2 * blk) == (bd_col < HEAD_DIM)).astype(F32).astype(BF16)
    contract_last = (((1,), (1,)), ((), ()))
    contract_first = (((0,), (0,)), ((), ()))

    def sub_block(i, carry):
        r0 = pl.multiple_of(i * blk, blk)
        q = q_ref[0, 0, pl.ds(r0, blk), :]
        kk = kbuf[pl.ds(r0, 2 * blk), :]
        vv = vbuf[pl.ds(r0, 2 * blk), :]
        variant = jnp.where(jnp.logical_and(first_step, i == 0), 0, 1)
        for j in range(N_HEADS // 2):
            cols = slice(j * LANES, (j + 1) * LANES)
            qj, kj, vj = q[:, cols], kk[:, cols], vv[:, cols]
            probs_t, maxes = [], []
            for hh in range(2):
                sel = lo_half if hh == 0 else jnp.logical_not(lo_half)
                qm = jnp.where(sel, qj, jnp.zeros_like(qj))
                s_t = lax.dot_general(kj, qm, contract_last, preferred_element_type=F32)
                s_t = s_t + bias_t[variant, 2 * j + hh]
                m = jnp.max(s_t, axis=0, keepdims=True)
                probs_t.append(jnp.exp(s_t - m).astype(BF16))
                maxes.append(m)
            p2_t = jnp.concatenate(probs_t, axis=0)
            v_bd = jnp.concatenate([jnp.where(lo_half, vj, jnp.zeros_like(vj)),
                                    jnp.where(lo_half, jnp.zeros_like(vj), vj)], axis=0)
            rhs = jnp.concatenate([v_bd, ones_bd], axis=1)
            ol = lax.dot_general(p2_t, rhs, contract_first, preferred_element_type=F32)
            o2, l2 = ol[:, :LANES], ol[:, LANES:]
            m_t = jnp.concatenate([jnp.broadcast_to(maxes[0], (HEAD_DIM, blk)),
                                   jnp.broadcast_to(maxes[1], (HEAD_DIM, blk))], axis=0)
            o_ref[0, 0, pl.ds(r0, blk), cols] = (o2 / l2).astype(o_ref.dtype)
            lse_ref[0, 0, pl.ds(r0, blk), cols] = m_t.T + jnp.log(l2)
        return carry

    lax.fori_loop(0, n_sub, sub_block, 0)


def _attention_pattern(q, k, v, rel_bias, dil):
    bsz, _, sub_len, aw = q.shape
    blk = ATTN_BLK
    assert sub_len % blk == 0
    tq = min(512, sub_len)
    n_sub = tq // blk
    ratio = tq // blk
    cur = lambda b, r, n: (b, r, n, 0)
    prev = lambda b, r, n: (b, r, jnp.maximum(n * ratio - 1, 0), 0)
    blk_cur = pl.BlockSpec((1, 1, tq, aw), cur)
    blk_prev = pl.BlockSpec((1, 1, blk, aw), prev)
    bucket = jnp.asarray(_bucket_map_t(dil))
    o_dtype = BF16 if TIME_TILE // dil >= 2 * SUBLANES else F32
    return pl.pallas_call(
        functools.partial(_attn_kernel, n_sub=n_sub),
        out_shape=(jax.ShapeDtypeStruct(q.shape, o_dtype), jax.ShapeDtypeStruct(q.shape, F32)),
        grid=(bsz, dil, sub_len // tq),
        in_specs=[pl.BlockSpec(memory_space=pltpu.SMEM),
                  blk_cur, blk_cur, blk_prev, blk_cur, blk_prev,
                  pl.BlockSpec(bucket.shape, lambda b, r, n: (0, 0))],
        out_specs=(blk_cur, blk_cur),
        scratch_shapes=[pltpu.VMEM((tq + blk, aw), BF16), pltpu.VMEM((tq + blk, aw), BF16),
                        pltpu.VMEM((2, N_HEADS, 2 * blk, blk), F32)],
        compiler_params=_cparams("arbitrary", "arbitrary", "arbitrary"),
        name=f"attn_dil{dil}",
    )(rel_bias.astype(F32), q, k, k, v, v, bucket)


def _ssm_kernel(u_ref, bmat_ref, cmat_ref, ar_ref, ai_ref, dskip_ref, wglu_ref, bglu_ref,
                o_ref, hbuf, hstate, *, n_steps):
    @pl.when(pl.program_id(0) == 0)
    def _():
        hstate[...] = jnp.zeros_like(hstate)

    n_slab = u_ref.shape[0]
    u = _lane_concat([u_ref[s] for s in range(n_slab)])
    hbuf[...] = _dot(u.astype(BF16), bmat_ref[...])
    n_state = hbuf.shape[1] // 2
    half = n_state // 2
    for part in range(2):
        re_cols = slice(part * half, (part + 1) * half)
        im_cols = slice(n_state + part * half, n_state + (part + 1) * half)
        ar = ar_ref[:, re_cols]
        ai = ai_ref[:, re_cols]

        def step(t, carry, re_cols=re_cols, im_cols=im_cols, ar=ar, ai=ai):
            hr, hi = carry
            r0 = pl.multiple_of(t * SUBLANES, SUBLANES)
            nr = ar * hr - ai * hi + hbuf[pl.ds(r0, SUBLANES), re_cols]
            ni = ar * hi + ai * hr + hbuf[pl.ds(r0, SUBLANES), im_cols]
            hbuf[pl.ds(r0, SUBLANES), re_cols] = nr
            hbuf[pl.ds(r0, SUBLANES), im_cols] = ni
            return nr, ni

        hr, hi = lax.fori_loop(0, n_steps, step, (hstate[:, re_cols], hstate[:, im_cols]),
                               unroll=2)
        hstate[:, re_cols] = hr
        hstate[:, im_cols] = hi

    y = _dot(hbuf[...].astype(BF16), cmat_ref[...]) + dskip_ref[...] * u
    y = 0.5 * y * (1.0 + jnp.tanh(math.sqrt(2.0 / math.pi) * (y + 0.044715 * (y * y * y))))
    z = _dot(y.astype(BF16), wglu_ref[...]) + bglu_ref[...]
    out = y * _sigmoid(z)
    for s in range(n_slab):
        o_ref[s] = out[:, s * LANES:(s + 1) * LANES]


def _ssm_params(a_re, a_im, log_dt, b_re, b_im, c_re, c_im, bsz):
    g, p = a_re.shape
    hg = b_re.shape[-1]
    dt = jnp.exp(log_dt.astype(F32))[:, None]
    a_re, a_im = a_re.astype(F32), a_im.astype(F32)
    mag = jnp.exp(a_re * dt)
    abar_re = mag * jnp.cos(a_im * dt)
    abar_im = mag * jnp.sin(a_im * dt)
    den = a_re * a_re + a_im * a_im
    q_re = ((abar_re - 1.0) * a_re + abar_im * a_im) / den
    q_im = (abar_im * a_re - (abar_re - 1.0) * a_im) / den
    b_re, b_im = b_re.astype(F32), b_im.astype(F32)
    bb_re = q_re[..., None] * b_re - q_im[..., None] * b_im
    bb_im = q_re[..., None] * b_im + q_im[..., None] * b_re
    eye = jnp.eye(g, dtype=F32)

    def in_mat(t):
        return jnp.einsum('gph,gk->ghkp', t, eye).reshape(g * hg, g * p)

    def out_mat(t):
        return jnp.einsum('ghp,gk->gpkh', t, eye).reshape(g * p, g * hg)

    bmat = jnp.concatenate([in_mat(bb_re), in_mat(bb_im)], axis=1).astype(BF16)
    cmat = jnp.concatenate([out_mat(c_re.astype(F32)), -out_mat(c_im.astype(F32))],
                           axis=0).astype(BF16)
    ar = jnp.broadcast_to(abar_re.reshape(1, g * p), (bsz, g * p))
    ai = jnp.broadcast_to(abar_im.reshape(1, g * p), (bsz, g * p))
    return bmat, cmat, ar, ai


def _ssm_branch(u_slabs, bsz, a_re, a_im, log_dt, b_re, b_im, c_re, c_im, d_skip, w_glu, b_glu):
    n_slab, n_rows, _ = u_slabs.shape
    width = n_slab * LANES
    assert bsz == SUBLANES
    bmat, cmat, ar, ai = _ssm_params(a_re, a_im, log_dt, b_re, b_im, c_re, c_im, bsz)
    n_state2 = bmat.shape[1]
    rows = SSM_STEPS * bsz
    const = lambda c: (0, 0)
    slab_spec = pl.BlockSpec((n_slab, rows, LANES), lambda c: (0, c, 0))
    return pl.pallas_call(
        functools.partial(_ssm_kernel, n_steps=SSM_STEPS),
        out_shape=jax.ShapeDtypeStruct(u_slabs.shape, F32),
        grid=(n_rows // rows,),
        in_specs=[slab_spec,
                  pl.BlockSpec(bmat.shape, const), pl.BlockSpec(cmat.shape, const),
                  pl.BlockSpec(ar.shape, const), pl.BlockSpec(ai.shape, const),
                  pl.BlockSpec((1, width), const), pl.BlockSpec((width, width), const),
                  pl.BlockSpec((1, width), const)],
        out_specs=slab_spec,
        scratch_shapes=[pltpu.VMEM((rows, n_state2), F32), pltpu.VMEM((bsz, n_state2), F32)],
        compiler_params=_cparams("arbitrary"),
        name="ssm",
    )(u_slabs, bmat, cmat, ar, ai, d_skip.reshape(1, width).astype(F32),
      w_glu.astype(BF16), b_glu.reshape(1, width).astype(F32))


ROUTE_LANES = {"id0": 0, "id1": 1, "rank0": 2, "rank1": 3, "w0": 4, "w1": 5}
GROUP_LANE0 = N_EXPERTS


def _merge_kernel(*refs, bsz_total):
    n_pat = len(DILATION_PATTERNS)
    x_ref, gate_ref = refs[0:2]
    attn_refs = refs[2:2 + 2 * n_pat]
    (ssm_ref, mod_ref, gpost_ref, gpre_ref, wba_ref, wbs_ref, wout_ref,
     wr_hi_ref, wr_lo_ref, br_ref) = refs[2 + 2 * n_pat:12 + 2 * n_pat]
    x1_ref, h2_ref, route_ref, count_ref = refs[12 + 2 * n_pat:16 + 2 * n_pat]
    o_tok, lse_tok, ssm_tok, carry = refs[16 + 2 * n_pat:]

    @pl.when(jnp.logical_and(pl.program_id(0) == 0, pl.program_id(1) == 0))
    def _():
        carry[...] = jnp.zeros_like(carry)

    nb, tt, d = x_ref.shape
    rows = nb * tt
    n_slab = ssm_ref.shape[0]
    b0 = pl.program_id(0) * nb

    for bb in range(nb):
        for s in range(n_slab):
            ssm_tok[bb * tt:(bb + 1) * tt, s * LANES:(s + 1) * LANES] = (
                ssm_ref[s, pl.ds(b0 + bb, tt, stride=bsz_total), :])

    acc = None
    for p, (_, dil) in enumerate(DILATION_PATTERNS):
        o_ref, lse_ref = attn_refs[2 * p], attn_refs[2 * p + 1]
        if dil == 1:
            o_p = o_ref[:, 0].astype(F32).reshape(rows, n_slab * LANES)
            lse_p = lse_ref[:, 0].reshape(rows, n_slab * LANES)
        else:
            sub = tt // dil
            for bb in range(nb):
                for r in range(dil):
                    o_blk = o_ref[bb, r].astype(F32)
                    l_blk = lse_ref[bb, r]
                    for s in range(n_slab):
                        dst = pl.ds(bb * tt + r, sub, stride=dil)
                        o_tok[s, dst, :] = o_blk[:, s * LANES:(s + 1) * LANES]
                        lse_tok[s, dst, :] = l_blk[:, s * LANES:(s + 1) * LANES]
            o_p = _lane_concat([o_tok[s] for s in range(n_slab)])
            lse_p = _lane_concat([lse_tok[s] for s in range(n_slab)])
        if acc is None:
            acc, lse_run = o_p, lse_p
        else:
            m = jnp.maximum(lse_run, lse_p)
            a = jnp.exp(lse_run - m)
            b = jnp.exp(lse_p - m)
            den = a + b
            acc = (a * acc + b * o_p) / den
            lse_run = m + jnp.log(den)
    attn = acc

    g_attn = gate_ref[:, :, 0:d].astype(F32).reshape(rows, d)
    g_ssm = gate_ref[:, :, d:].astype(F32).reshape(rows, d)
    merged = (g_attn * _dot(attn.astype(BF16), wba_ref[...])
              + g_ssm * _dot(ssm_tok[...].astype(BF16), wbs_ref[...]))
    y = _dot(merged.astype(BF16), wout_ref[...])
    gate1 = mod_ref[:, 2, :][:, None, :]
    shift2 = mod_ref[:, 3, :][:, None, :]
    scale2 = mod_ref[:, 4, :][:, None, :]
    x1 = x_ref[...] + gate1 * _rms_norm(y, gpost_ref[...]).reshape(nb, tt, d)
    x1_ref[...] = x1
    h2 = _rms_norm(x1, gpre_ref[...]) * (1.0 + scale2) + shift2
    h2_ref[...] = h2

    logits = _dot_split(h2.reshape(rows, d), wr_hi_ref[...], wr_lo_ref[...]) + br_ref[...]
    lane = lax.broadcasted_iota(jnp.int32, (rows, LANES), 1).astype(F32)
    big = float(LANES)
    is_group = jnp.logical_and(lane >= GROUP_LANE0, lane < GROUP_LANE0 + N_EXPERT_GROUPS)
    gl = jnp.where(is_group, logits, -jnp.inf)
    g_max = jnp.max(gl, axis=-1, keepdims=True)
    g_sel = jnp.min(jnp.where(gl == g_max, lane, big), axis=-1, keepdims=True) - GROUP_LANE0
    g_gate = 1.0 / jnp.sum(jnp.exp(gl - g_max), axis=-1, keepdims=True)
    lo = g_sel * EXPERTS_PER_GROUP
    in_group = jnp.logical_and(lane >= lo, lane < lo + EXPERTS_PER_GROUP)
    el = jnp.where(in_group, logits, -jnp.inf)
    t0 = jnp.max(el, axis=-1, keepdims=True)
    i0 = jnp.min(jnp.where(el == t0, lane, big), axis=-1, keepdims=True)
    el1 = jnp.where(lane == i0, -jnp.inf, el)
    t1 = jnp.max(el1, axis=-1, keepdims=True)
    i1 = jnp.min(jnp.where(el1 == t1, lane, big), axis=-1, keepdims=True)
    e = jnp.exp(t1 - t0)
    w0 = g_gate / (1.0 + e)
    w1 = g_gate * e / (1.0 + e)

    hit0 = lane == i0
    hit1 = lane == i1
    onehot = jnp.logical_or(hit0, hit1).astype(F32)
    row = lax.broadcasted_iota(jnp.int32, (rows, rows), 0)
    col = lax.broadcasted_iota(jnp.int32, (rows, rows), 1)
    strict_lower = (col < row).astype(BF16)
    before = _dot(strict_lower, onehot.astype(BF16)) + carry[...]
    rank0 = jnp.sum(jnp.where(hit0, before, 0.0), axis=-1, keepdims=True)
    rank1 = jnp.sum(jnp.where(hit1, before, 0.0), axis=-1, keepdims=True)
    carry[...] = carry[...] + jnp.sum(onehot, axis=0, keepdims=True)
    count_ref[...] = jnp.broadcast_to(carry[...], count_ref.shape)

    route = jnp.zeros((rows, LANES), F32)
    for name, val in (("id0", i0), ("id1", i1), ("rank0", rank0), ("rank1", rank1),
                      ("w0", w0), ("w1", w1)):
        route = jnp.where(lane == ROUTE_LANES[name], val, route)
    route_ref[...] = route.reshape(nb, tt, LANES)


def _merge_and_route(x, gates, attn_outs, ssm_slabs, mod3, g_post, g_pre, wba, wbs, wout,
                     w_rg, b_rg, w_re, b_re):
    bsz, seq, d = x.shape
    tt, nb = TIME_TILE, MERGE_BATCH
    aw = ATTN_WIDTH
    n_slab = ssm_slabs.shape[0]
    assert aw == n_slab * LANES
    wr = jnp.zeros((d, LANES), F32).at[:, :N_EXPERTS].set(w_re.astype(F32))
    wr = wr.at[:, GROUP_LANE0:GROUP_LANE0 + N_EXPERT_GROUPS].set(w_rg.astype(F32))
    br = jnp.zeros((1, LANES), F32).at[0, :N_EXPERTS].set(b_re.astype(F32))
    br = br.at[0, GROUP_LANE0:GROUP_LANE0 + N_EXPERT_GROUPS].set(b_rg.astype(F32))
    wr_hi, wr_lo = _split_bf16(wr)
    tok = lambda h, i: (h, i, 0)
    const = lambda h, i: (0, 0)
    attn_args, attn_specs = [], []
    for (o_p, lse_p), (_, dil) in zip(attn_outs, DILATION_PATTERNS):
        spec = pl.BlockSpec((nb, dil, tt // dil, aw), lambda h, i: (h, 0, i, 0))
        attn_args += [o_p, lse_p]
        attn_specs += [spec, spec]
    return pl.pallas_call(
        functools.partial(_merge_kernel, bsz_total=bsz),
        out_shape=(jax.ShapeDtypeStruct((bsz, seq, d), F32),
                   jax.ShapeDtypeStruct((bsz, seq, d), F32),
                   jax.ShapeDtypeStruct((bsz, seq, LANES), F32),
                   jax.ShapeDtypeStruct((SUBLANES, LANES), F32)),
        grid=(bsz // nb, seq // tt),
        in_specs=[pl.BlockSpec((nb, tt, d), tok),
                  pl.BlockSpec((nb, tt, gates.shape[-1]), tok)]
                 + attn_specs
                 + [pl.BlockSpec((n_slab, tt * bsz, LANES), lambda h, i: (0, i, 0)),
                    pl.BlockSpec((nb, 6, d), lambda h, i: (h, 0, 0)),
                    pl.BlockSpec((1, d), const), pl.BlockSpec((1, d), const),
                    pl.BlockSpec(wba.shape, const), pl.BlockSpec(wbs.shape, const),
                    pl.BlockSpec(wout.shape, const),
                    pl.BlockSpec((d, LANES), const), pl.BlockSpec((d, LANES), const),
                    pl.BlockSpec((1, LANES), const)],
        out_specs=(pl.BlockSpec((nb, tt, d), tok), pl.BlockSpec((nb, tt, d), tok),
                   pl.BlockSpec((nb, tt, LANES), tok),
                   pl.BlockSpec((SUBLANES, LANES), const)),
        scratch_shapes=[pltpu.VMEM((n_slab, nb * tt, LANES), F32),
                        pltpu.VMEM((n_slab, nb * tt, LANES), F32),
                        pltpu.VMEM((nb * tt, n_slab * LANES), F32),
                        pltpu.VMEM((1, LANES), F32)],
        compiler_params=_cparams("arbitrary", "arbitrary"),
        name="merge",
    )(x, gates, *attn_args, ssm_slabs, mod3, g_post, g_pre, wba, wbs, wout, wr_hi, wr_lo, br)


ISSUE_UNROLL = 8


def _dispatch_kernel(dest_ref, seg_ref, h_ref, xs_ref, zero_buf, sem, zsem):
    rows = h_ref.shape[0]
    blk = zero_buf.shape[0]
    n_blocks = xs_ref.shape[0] // blk
    base = pl.program_id(0) * rows

    @pl.when(pl.program_id(0) == 0)
    def _():
        zero_buf[...] = jnp.zeros_like(zero_buf)

        def zero_copy(row0):
            return pltpu.make_async_copy(zero_buf, xs_ref.at[pl.ds(pl.multiple_of(row0, blk), blk)],
                                         zsem)

        def fill_tail(e, carry):
            @pl.when(seg_ref[N_EXPERTS + e] > 0)
            def _():
                zero_copy(seg_ref[e] - blk).start()
            return carry

        def fill_unused(j, carry):
            zero_copy(j * blk).start()
            return carry

        def wait_tail(e, carry):
            @pl.when(seg_ref[N_EXPERTS + e] > 0)
            def _():
                zero_copy(0).wait()
            return carry

        def wait_unused(j, carry):
            zero_copy(0).wait()
            return carry

        n_used = seg_ref[2 * N_EXPERTS]
        lax.fori_loop(0, N_EXPERTS, fill_tail, 0)
        lax.fori_loop(n_used, n_blocks, fill_unused, 0)
        lax.fori_loop(0, N_EXPERTS, wait_tail, 0)
        lax.fori_loop(n_used, n_blocks, wait_unused, 0)

    def issue(g, carry):
        for rr in range(ISSUE_UNROLL):
            r = g * ISSUE_UNROLL + rr
            for k in range(TOP_K):
                d = dest_ref[TOP_K * (base + r) + k]
                pltpu.make_async_copy(h_ref.at[pl.ds(r, 1)], xs_ref.at[pl.ds(d, 1)],
                                      sem).start(priority=k)
        return carry

    lax.fori_loop(0, rows // ISSUE_UNROLL, issue, 0)
    for _ in range(TOP_K):
        pltpu.make_async_copy(h_ref, xs_ref.at[pl.ds(0, rows)], sem).wait()


def _dispatch(h2_flat, dest, seg_info, cap):
    n_tok, d = h2_flat.shape
    rows = DISPATCH_ROWS
    return pl.pallas_call(
        _dispatch_kernel,
        out_shape=jax.ShapeDtypeStruct((cap, d), h2_flat.dtype),
        grid_spec=pltpu.PrefetchScalarGridSpec(
            num_scalar_prefetch=2,
            grid=(n_tok // rows,),
            in_specs=[pl.BlockSpec((rows, d), lambda i, dest, seg: (i, 0))],
            out_specs=pl.BlockSpec(memory_space=pl.ANY),
            scratch_shapes=[pltpu.VMEM((MOE_ROWS, d), h2_flat.dtype),
                            pltpu.SemaphoreType.DMA, pltpu.SemaphoreType.DMA]),
        compiler_params=_cparams("arbitrary"),
        name="dispatch",
    )(dest, seg_info, h2_flat)


def _expert_kernel(blk_expert_ref, n_used_ref, x_ref, w1_ref, w3_ref, w2_ref, y_ref):
    del blk_expert_ref
    used = pl.program_id(0) < n_used_ref[0]

    @pl.when(used)
    def _():
        xb = x_ref[...].astype(BF16)
        h1 = _dot(xb, w1_ref[0].astype(BF16))
        h3 = _dot(xb, w3_ref[0].astype(BF16))
        act = (h1 * _sigmoid(h1)) * h3
        y_ref[...] = _dot(act.astype(BF16), w2_ref[0].astype(BF16))

    @pl.when(jnp.logical_not(used))
    def _():
        y_ref[...] = jnp.zeros_like(y_ref)


def _experts(xs, blk_expert, n_used, w1, w3, w2):
    cap, d = xs.shape
    rows = MOE_ROWS
    de = w1.shape[-1]
    xblk = lambda i, be, nu: (jnp.minimum(i, nu[0] - 1), 0)
    wblk = lambda i, be, nu: (be[i], 0, 0)
    return pl.pallas_call(
        _expert_kernel,
        out_shape=jax.ShapeDtypeStruct((cap, d), F32),
        grid_spec=pltpu.PrefetchScalarGridSpec(
            num_scalar_prefetch=2,
            grid=(cap // rows,),
            in_specs=[pl.BlockSpec((rows, d), xblk),
                      pl.BlockSpec((1, d, de), wblk), pl.BlockSpec((1, d, de), wblk),
                      pl.BlockSpec((1, de, d), wblk)],
            out_specs=pl.BlockSpec((rows, d), lambda i, be, nu: (i, 0))),
        compiler_params=_cparams("arbitrary"),
        name="experts",
    )(blk_expert, n_used, xs, w1, w3, w2)


def _combine_kernel(dest_ref, ys_ref, x1_ref, route_ref, mod_ref, g_ref, o_ref, ybuf, sem):
    rows = x1_ref.shape[0]
    step = pl.program_id(0)
    n_step = pl.num_programs(0)

    def gather(tile, slot):
        base = tile * rows

        def issue(g, carry):
            for rr in range(ISSUE_UNROLL):
                r = g * ISSUE_UNROLL + rr
                for k in range(TOP_K):
                    d = dest_ref[TOP_K * (base + r) + k]
                    pltpu.make_async_copy(ys_ref.at[pl.ds(d, 1)], ybuf.at[slot, k, pl.ds(r, 1)],
                                          sem.at[slot]).start(priority=k)
            return carry

        lax.fori_loop(0, rows // ISSUE_UNROLL, issue, 0)

    slot = lax.rem(step, 2)

    @pl.when(step == 0)
    def _():
        gather(step, 0)

    @pl.when(step + 1 < n_step)
    def _():
        gather(step + 1, 1 - slot)

    for k in range(TOP_K):
        pltpu.make_async_copy(ys_ref.at[pl.ds(0, rows)], ybuf.at[slot, k], sem.at[slot]).wait()
    route = route_ref[...]
    w0 = route[:, ROUTE_LANES["w0"]:ROUTE_LANES["w0"] + 1]
    w1 = route[:, ROUTE_LANES["w1"]:ROUTE_LANES["w1"] + 1]
    y = ybuf[slot, 0] * w0 + ybuf[slot, 1] * w1
    o_ref[...] = x1_ref[...] + mod_ref[0, 5:6, :] * _rms_norm(y, g_ref[...])


def _combine(ys, dest, x1_flat, route_flat, mod3, g_post, seq):
    n_tok, d = x1_flat.shape
    rows = COMBINE_ROWS
    assert seq % rows == 0
    return pl.pallas_call(
        _combine_kernel,
        out_shape=jax.ShapeDtypeStruct((n_tok, d), F32),
        grid_spec=pltpu.PrefetchScalarGridSpec(
            num_scalar_prefetch=1,
            grid=(n_tok // rows,),
            in_specs=[pl.BlockSpec(memory_space=pl.ANY),
                      pl.BlockSpec((rows, d), lambda i, dest: (i, 0)),
                      pl.BlockSpec((rows, LANES), lambda i, dest: (i, 0)),
                      pl.BlockSpec((1, 6, d), lambda i, dest: (i * rows // seq, 0, 0)),
                      pl.BlockSpec((1, d), lambda i, dest: (0, 0))],
            out_specs=pl.BlockSpec((rows, d), lambda i, dest: (i, 0)),
            scratch_shapes=[pltpu.VMEM((2, TOP_K, rows, d), F32),
                            pltpu.SemaphoreType.DMA((2,))]),
        compiler_params=_cparams("arbitrary"),
        name="combine",
    )(dest, ys, x1_flat, route_flat, mod3, g_post)


def _moe_layout(route_flat, counts):
    rows = MOE_ROWS
    n_tok = route_flat.shape[0]
    ids = route_flat[:, 0:TOP_K].astype(jnp.int32)
    rank = route_flat[:, TOP_K:2 * TOP_K].astype(jnp.int32)
    counts = counts.astype(jnp.int32)
    padded = (counts + rows - 1) // rows * rows
    pad_ends = jnp.cumsum(padded)
    pad_starts = pad_ends - padded
    expert = jnp.arange(N_EXPERTS, dtype=jnp.int32)
    start_of = jnp.sum(jnp.where(ids[..., None] == expert, pad_starts, 0), axis=-1)
    dest = (start_of + rank).reshape(n_tok * TOP_K)
    n_blocks = (n_tok * TOP_K + N_EXPERTS * (rows - 1) + rows - 1) // rows
    blk_row0 = jnp.arange(n_blocks, dtype=jnp.int32) * rows
    blk_expert = jnp.minimum(jnp.sum(pad_ends[None, :] <= blk_row0[:, None], axis=-1),
                             N_EXPERTS - 1).astype(jnp.int32)
    n_used = (pad_ends[-1] // rows).reshape(1).astype(jnp.int32)
    seg_info = jnp.concatenate([pad_ends, padded, n_used]).astype(jnp.int32)
    return dest, blk_expert, n_used, seg_info, n_blocks * rows


def kernel(x, c, w_mod, b_mod, g_pre_mix, g_post_mix, g_pre_ffn, g_post_ffn, w_in, rel_bias, a_re, a_im, log_dt, ssm_b_re, ssm_b_im, ssm_c_re, ssm_c_im, d_skip, w_glu, b_glu, w_branch_attn, w_branch_ssm, w_out, w_router_group, b_router_group, w_router_expert, b_router_expert, w1, w3, w2):
    bsz, seq, d = x.shape
    depth = w_mod.shape[0]
    ssm_width = w_glu.shape[-1]
    n_pat = len(DILATION_PATTERNS)
    for l in range(depth):
        mod3 = _modulation(c, w_mod[l], b_mod[l]).reshape(bsz, 6, d)
        w_in_l = w_in[l].astype(BF16)
        g_pre = g_pre_mix[l].reshape(1, d)
        qkv = _qkv_projection(x, mod3, g_pre, w_in_l[:, :3 * ATTN_WIDTH])
        u_slabs, gates = _ugate_projection(x, mod3, g_pre, w_in_l[:, 3 * ATTN_WIDTH:], ssm_width)
        attn_outs = [_attention_pattern(*qkv[3 * p:3 * p + 3], rel_bias, DILATION_PATTERNS[p][1])
                     for p in range(n_pat)]
        ssm_slabs = _ssm_branch(u_slabs, bsz, a_re[l], a_im[l], log_dt[l], ssm_b_re[l],
                                ssm_b_im[l], ssm_c_re[l], ssm_c_im[l], d_skip[l], w_glu[l], b_glu[l])
        x1, h2, route, counts = _merge_and_route(
            x, gates, attn_outs, ssm_slabs, mod3, g_post_mix[l].reshape(1, d),
            g_pre_ffn[l].reshape(1, d), w_branch_attn[l].astype(BF16),
            w_branch_ssm[l].astype(BF16), w_out[l].astype(BF16),
            w_router_group[l], b_router_group[l], w_router_expert[l], b_router_expert[l])
        route_flat = route.reshape(bsz * seq, LANES)
        dest, blk_expert, n_used, seg_info, cap = _moe_layout(route_flat, counts[0, :N_EXPERTS])
        xs = _dispatch(h2.reshape(bsz * seq, d), dest, seg_info, cap)
        ys = _experts(xs, blk_expert, n_used, w1[l], w3[l], w2[l])
        x = _combine(ys, dest, x1.reshape(bsz * seq, d), route_flat, mod3,
                     g_post_ffn[l].reshape(1, d), seq).reshape(bsz, seq, d)
    return x
```

```python
import functools
import math

import numpy as np
import jax
import jax.numpy as jnp
from jax import lax
from jax.experimental import pallas as pl
from jax.experimental.pallas import tpu as pltpu

F32 = jnp.float32
BF16 = jnp.bfloat16

N_HEADS = 8
HEAD_DIM = 64
ATTN_WIDTH = N_HEADS * HEAD_DIM
DILATION_PATTERNS = ((128, 1), (512, 4), (2048, 16))
NUM_BUCKETS = 32
MAX_DISTANCE = 2048
N_EXPERT_GROUPS = 4
EXPERTS_PER_GROUP = 8
N_EXPERTS = N_EXPERT_GROUPS * EXPERTS_PER_GROUP
TOP_K = 2
RMS_EPS = 1e-6
NEG_INF = -1e30

LANES = 128
SUBLANES = 8
VMEM_LIMIT_BYTES = 56 * 1024 * 1024

ATTN_BLK = 128
QKV_ROWS = 512
TIME_TILE = 128
MERGE_BATCH = 4
SSM_STEPS = 64
MOE_ROWS = 256
DISPATCH_ROWS = 2048
COMBINE_ROWS = 256
ROW_TILE = 8


def _cparams(*sem):
    return pltpu.CompilerParams(dimension_semantics=sem, vmem_limit_bytes=VMEM_LIMIT_BYTES)


def _sigmoid(x):
    return 1.0 / (1.0 + jnp.exp(-x))


def _dot(a, b):
    return jnp.dot(a, b, preferred_element_type=F32)


def _split_bf16(a):
    hi = a.astype(BF16)
    lo = (a - hi.astype(F32)).astype(BF16)
    return hi, lo


def _dot_split(a, w_hi, w_lo):
    a_hi, a_lo = _split_bf16(a)
    return _dot(a_hi, w_hi) + _dot(a_lo, w_hi) + _dot(a_hi, w_lo)


def _rms_norm(x, gain):
    ms = jnp.mean(x * x, axis=-1, keepdims=True)
    return x * lax.rsqrt(ms + RMS_EPS) * gain


def _lane_concat(ref_slabs):
    return jnp.concatenate(ref_slabs, axis=-1)


def _mod_kernel(c_ref, w_ref, b_ref, o_ref):
    c = c_ref[...]
    a = c * _sigmoid(c)
    w_hi, w_lo = _split_bf16(w_ref[...])
    o_ref[...] = _dot_split(a, w_hi, w_lo) + b_ref[...]


def _modulation(c, w_mod, b_mod):
    bsz, d = c.shape
    n = w_mod.shape[1]
    tn = 1024
    return pl.pallas_call(
        _mod_kernel,
        out_shape=jax.ShapeDtypeStruct((bsz, n), F32),
        grid=(n // tn,),
        in_specs=[pl.BlockSpec((bsz, d), lambda j: (0, 0)),
                  pl.BlockSpec((d, tn), lambda j: (0, j)),
                  pl.BlockSpec((1, tn), lambda j: (0, j))],
        out_specs=pl.BlockSpec((bsz, tn), lambda j: (0, j)),
        compiler_params=_cparams("arbitrary"),
        name="mod",
    )(c, w_mod, b_mod.reshape(1, n))


def _qkv_kernel(x_ref, mod_ref, g_ref, w_ref, *rest):
    n_pat = len(DILATION_PATTERNS)
    out_refs, slab = rest[:3 * n_pat], rest[3 * n_pat]
    h = _rms_norm(x_ref[0], g_ref[...]) * (1.0 + mod_ref[0, 1:2, :]) + mod_ref[0, 0:1, :]
    res = _dot(h.astype(BF16), w_ref[...])
    rows = res.shape[0]
    n_slab = res.shape[1] // LANES
    per_tensor = ATTN_WIDTH // LANES
    for s in range(n_slab):
        piece = res[:, s * LANES:(s + 1) * LANES]
        if s < per_tensor:
            piece = piece * (HEAD_DIM ** -0.5)
        slab[s] = piece
    for p, (_, dil) in enumerate(DILATION_PATTERNS):
        sub = rows // dil
        for t in range(3):
            out = out_refs[3 * p + t]
            for r in range(dil):
                pieces = [slab[t * per_tensor + s, pl.ds(r, sub, stride=dil), :]
                          if dil > 1 else slab[t * per_tensor + s]
                          for s in range(per_tensor)]
                out[0, r] = _lane_concat(pieces).astype(out.dtype)


def _qkv_projection(x, mod3, g_pre, w_qkv):
    bsz, seq, d = x.shape
    tm = QKV_ROWS
    out_shape, out_specs = [], []
    for _, dil in DILATION_PATTERNS:
        assert tm % (dil * 2 * SUBLANES) == 0
        for _ in range(3):
            out_shape.append(jax.ShapeDtypeStruct((bsz, dil, seq // dil, ATTN_WIDTH), BF16))
            out_specs.append(pl.BlockSpec((1, dil, tm // dil, ATTN_WIDTH),
                                          lambda b, i: (b, 0, i, 0)))
    return pl.pallas_call(
        _qkv_kernel,
        out_shape=out_shape,
        grid=(bsz, seq // tm),
        in_specs=[pl.BlockSpec((1, tm, d), lambda b, i: (b, i, 0)),
                  pl.BlockSpec((1, 6, d), lambda b, i: (b, 0, 0)),
                  pl.BlockSpec((1, d), lambda b, i: (0, 0)),
                  pl.BlockSpec(w_qkv.shape, lambda b, i: (0, 0))],
        out_specs=out_specs,
        scratch_shapes=[pltpu.VMEM((w_qkv.shape[1] // LANES, tm, LANES), F32)],
        compiler_params=_cparams("arbitrary", "arbitrary"),
        name="qkv",
    )(x, mod3, g_pre, w_qkv)


def _ugate_kernel(x_ref, mod_ref, g_ref, w_ref, u_ref, gate_ref):
    bsz, tt, d = x_ref.shape
    shift = mod_ref[:, 0, :][:, None, :]
    scale = mod_ref[:, 1, :][:, None, :]
    h = _rms_norm(x_ref[...], g_ref[...]) * (1.0 + scale) + shift
    hb = h.reshape(bsz * tt, d).astype(BF16)
    n_slab = u_ref.shape[0]
    sw = n_slab * LANES
    u = _dot(hb, w_ref[:, 0:sw])
    for b in range(bsz):
        for s in range(n_slab):
            u_ref[s, pl.ds(b, tt, stride=bsz), :] = u[b * tt:(b + 1) * tt, s * LANES:(s + 1) * LANES]
    gw = gate_ref.shape[-1]
    chunk = 512
    for c0 in range(0, gw, chunk):
        g = _sigmoid(_dot(hb, w_ref[:, sw + c0:sw + c0 + chunk]))
        gate_ref[:, :, c0:c0 + chunk] = g.reshape(bsz, tt, chunk).astype(BF16)


def _ugate_projection(x, mod3, g_pre, w_ug, ssm_width):
    bsz, seq, d = x.shape
    tt = TIME_TILE
    gw = w_ug.shape[1] - ssm_width
    n_slab = ssm_width // LANES
    return pl.pallas_call(
        _ugate_kernel,
        out_shape=(jax.ShapeDtypeStruct((n_slab, seq * bsz, LANES), F32),
                   jax.ShapeDtypeStruct((bsz, seq, gw), BF16)),
        grid=(seq // tt,),
        in_specs=[pl.BlockSpec((bsz, tt, d), lambda i: (0, i, 0)),
                  pl.BlockSpec((bsz, 6, d), lambda i: (0, 0, 0)),
                  pl.BlockSpec((1, d), lambda i: (0, 0)),
                  pl.BlockSpec(w_ug.shape, lambda i: (0, 0))],
        out_specs=(pl.BlockSpec((n_slab, tt * bsz, LANES), lambda i: (0, i, 0)),
                   pl.BlockSpec((bsz, tt, gw), lambda i: (0, i, 0))),
        compiler_params=_cparams("arbitrary"),
        name="ugate",
    )(x, mod3, g_pre, w_ug)


def _t5_bucket_np(dist):
    exact = NUM_BUCKETS // 2
    d_f = np.maximum(dist, exact).astype(np.float32)
    large = exact + (np.log(d_f / np.float32(exact)) / np.float32(math.log(MAX_DISTANCE / exact))
                     * np.float32(NUM_BUCKETS - exact)).astype(np.int32)
    return np.where(dist < exact, dist, np.minimum(large, NUM_BUCKETS - 1))


def _bucket_map_t(dil):
    blk = ATTN_BLK
    ki = np.arange(2 * blk)[:, None]
    qi = np.arange(blk)[None, :]
    return _t5_bucket_np(np.maximum(blk + qi - ki, 0) * dil).astype(np.int32)


def _attn_kernel(relb_ref, q_ref, kc_ref, kp_ref, vc_ref, vp_ref, bucket_ref,
                 o_ref, lse_ref, kbuf, vbuf, bias_t, *, n_sub):
    blk = ATTN_BLK
    first_call = jnp.logical_and(pl.program_id(0) == 0,
                                 jnp.logical_and(pl.program_id(1) == 0, pl.program_id(2) == 0))

    @pl.when(first_call)
    def _():
        bucket = bucket_ref[...]
        ki = lax.broadcasted_iota(jnp.int32, bucket.shape, 0)
        qi = lax.broadcasted_iota(jnp.int32, bucket.shape, 1)
        dist = blk + qi - ki
        band = jnp.logical_and(dist >= 0, dist <= blk)
        band_first = jnp.logical_and(band, ki >= blk)

        def per_head(h, carry):
            acc = jnp.zeros(bucket.shape, F32)
            for b in range(NUM_BUCKETS):
                acc = jnp.where(bucket == b, relb_ref[b, h], acc)
            bias_t[0, h] = jnp.where(band_first, acc, NEG_INF)
            bias_t[1, h] = jnp.where(band, acc, NEG_INF)
            return carry

        lax.fori_loop(0, N_HEADS, per_head, 0)

    first_step = pl.program_id(2) == 0
    kbuf[0:blk, :] = kp_ref[0, 0]
    kbuf[blk:, :] = kc_ref[0, 0]
    vbuf[0:blk, :] = vp_ref[0, 0]
    vbuf[blk:, :] = vc_ref[0, 0]

    lane = lax.broadcasted_iota(jnp.int32, (1, LANES), 1)
    lo_half = lane < HEAD_DIM
    bd_row = lax.broadcasted_iota(jnp.int32, (4 * blk, LANES), 0)
    bd_col = lax.broadcasted_iota(jnp.int32, (4 * blk, LANES), 1)
    ones_bd = ((bd_row < 2 * blk) == (bd_col < HEAD_DIM)).astype(F32).astype(BF16)
    contract_last = (((1,), (1,)), ((), ()))
    contract_first = (((0,), (0,)), ((), ()))

    def sub_block(i, carry):
        r0 = pl.multiple_of(i * blk, blk)
        q = q_ref[0, 0, pl.ds(r0, blk), :]
        kk = kbuf[pl.ds(r0, 2 * blk), :]
        vv = vbuf[pl.ds(r0, 2 * blk), :]
        variant = jnp.where(jnp.logical_and(first_step, i == 0), 0, 1)
        for j in range(N_HEADS // 2):
            cols = slice(j * LANES, (j + 1) * LANES)
            qj, kj, vj = q[:, cols], kk[:, cols], vv[:, cols]
            probs_t, maxes = [], []
            for hh in range(2):
                sel = lo_half if hh == 0 else jnp.logical_not(lo_half)
                qm = jnp.where(sel, qj, jnp.zeros_like(qj))
                s_t = lax.dot_general(kj, qm, contract_last, preferred_element_type=F32)
                s_t = s_t + bias_t[variant, 2 * j + hh]
                m = jnp.max(s_t, axis=0, keepdims=True)
                probs_t.append(jnp.exp(s_t - m).astype(BF16))
                maxes.append(m)
            p2_t = jnp.concatenate(probs_t, axis=0)
            v_bd = jnp.concatenate([jnp.where(lo_half, vj, jnp.zeros_like(vj)),
                                    jnp.where(lo_half, jnp.zeros_like(vj), vj)], axis=0)
            rhs = jnp.concatenate([v_bd, ones_bd], axis=1)
            ol = lax.dot_general(p2_t, rhs, contract_first, preferred_element_type=F32)
            o2, l2 = ol[:, :LANES], ol[:, LANES:]
            m_t = jnp.concatenate([jnp.broadcast_to(maxes[0], (HEAD_DIM, blk)),
                                   jnp.broadcast_to(maxes[1], (HEAD_DIM, blk))], axis=0)
            o_ref[0, 0, pl.ds(r0, blk), cols] = (o2 / l2).astype(o_ref.dtype)
            lse_ref[0, 0, pl.ds(r0, blk), cols] = m_t.T + jnp.log(l2)
        return carry

    lax.fori_loop(0, n_sub, sub_block, 0)


def _attention_pattern(q, k, v, rel_bias, dil):
    bsz, _, sub_len, aw = q.shape
    blk = ATTN_BLK
    assert sub_len % blk == 0
    tq = min(512, sub_len)
    n_sub = tq // blk
    ratio = tq // blk
    cur = lambda b, r, n: (b, r, n, 0)
    prev = lambda b, r, n: (b, r, jnp.maximum(n * ratio - 1, 0), 0)
    blk_cur = pl.BlockSpec((1, 1, tq, aw), cur)
    blk_prev = pl.BlockSpec((1, 1, blk, aw), prev)
    bucket = jnp.asarray(_bucket_map_t(dil))
    o_dtype = BF16 if TIME_TILE // dil >= 2 * SUBLANES else F32
    return pl.pallas_call(
        functools.partial(_attn_kernel, n_sub=n_sub),
        out_shape=(jax.ShapeDtypeStruct(q.shape, o_dtype), jax.ShapeDtypeStruct(q.shape, F32)),
        grid=(bsz, dil, sub_len // tq),
        in_specs=[pl.BlockSpec(memory_space=pltpu.SMEM),
                  blk_cur, blk_cur, blk_prev, blk_cur, blk_prev,
                  pl.BlockSpec(bucket.shape, lambda b, r, n: (0, 0))],
        out_specs=(blk_cur, blk_cur),
        scratch_shapes=[pltpu.VMEM((tq + blk, aw), BF16), pltpu.VMEM((tq + blk, aw), BF16),
                        pltpu.VMEM((2, N_HEADS, 2 * blk, blk), F32)],
        compiler_params=_cparams("arbitrary", "arbitrary", "arbitrary"),
        name=f"attn_dil{dil}",
    )(rel_bias.astype(F32), q, k, k, v, v, bucket)


def _ssm_kernel(u_ref, bmat_ref, cmat_ref, ar_ref, ai_ref, dskip_ref, wglu_ref, bglu_ref,
                o_ref, hbuf, hstate, *, n_steps):
    @pl.when(pl.program_id(0) == 0)
    def _():
        hstate[...] = jnp.zeros_like(hstate)

    n_slab = u_ref.shape[0]
    u = _lane_concat([u_ref[s] for s in range(n_slab)])
    hbuf[...] = _dot(u.astype(BF16), bmat_ref[...])
    n_state = hbuf.shape[1] // 2
    half = n_state // 2
    for part in range(2):
        re_cols = slice(part * half, (part + 1) * half)
        im_cols = slice(n_state + part * half, n_state + (part + 1) * half)
        ar = ar_ref[:, re_cols]
        ai = ai_ref[:, re_cols]

        def step(t, carry, re_cols=re_cols, im_cols=im_cols, ar=ar, ai=ai):
            hr, hi = carry
            r0 = pl.multiple_of(t * SUBLANES, SUBLANES)
            nr = ar * hr - ai * hi + hbuf[pl.ds(r0, SUBLANES), re_cols]
            ni = ar * hi + ai * hr + hbuf[pl.ds(r0, SUBLANES), im_cols]
            hbuf[pl.ds(r0, SUBLANES), re_cols] = nr
            hbuf[pl.ds(r0, SUBLANES), im_cols] = ni
            return nr, ni

        hr, hi = lax.fori_loop(0, n_steps, step, (hstate[:, re_cols], hstate[:, im_cols]),
                               unroll=2)
        hstate[:, re_cols] = hr
        hstate[:, im_cols] = hi

    y = _dot(hbuf[...].astype(BF16), cmat_ref[...]) + dskip_ref[...] * u
    y = 0.5 * y * (1.0 + jnp.tanh(math.sqrt(2.0 / math.pi) * (y + 0.044715 * (y * y * y))))
    z = _dot(y.astype(BF16), wglu_ref[...]) + bglu_ref[...]
    out = y * _sigmoid(z)
    for s in range(n_slab):
        o_ref[s] = out[:, s * LANES:(s + 1) * LANES]


def _ssm_params(a_re, a_im, log_dt, b_re, b_im, c_re, c_im, bsz):
    g, p = a_re.shape
    hg = b_re.shape[-1]
    dt = jnp.exp(log_dt.astype(F32))[:, None]
    a_re, a_im = a_re.astype(F32), a_im.astype(F32)
    mag = jnp.exp(a_re * dt)
    abar_re = mag * jnp.cos(a_im * dt)
    abar_im = mag * jnp.sin(a_im * dt)
    den = a_re * a_re + a_im * a_im
    q_re = ((abar_re - 1.0) * a_re + abar_im * a_im) / den
    q_im = (abar_im * a_re - (abar_re - 1.0) * a_im) / den
    b_re, b_im = b_re.astype(F32), b_im.astype(F32)
    bb_re = q_re[..., None] * b_re - q_im[..., None] * b_im
    bb_im = q_re[..., None] * b_im + q_im[..., None] * b_re
    eye = jnp.eye(g, dtype=F32)

    def in_mat(t):
        return jnp.einsum('gph,gk->ghkp', t, eye).reshape(g * hg, g * p)

    def out_mat(t):
        return jnp.einsum('ghp,gk->gpkh', t, eye).reshape(g * p, g * hg)

    bmat = jnp.concatenate([in_mat(bb_re), in_mat(bb_im)], axis=1).astype(BF16)
    cmat = jnp.concatenate([out_mat(c_re.astype(F32)), -out_mat(c_im.astype(F32))],
                           axis=0).astype(BF16)
    ar = jnp.broadcast_to(abar_re.reshape(1, g * p), (bsz, g * p))
    ai = jnp.broadcast_to(abar_im.reshape(1, g * p), (bsz, g * p))
    return bmat, cmat, ar, ai


def _ssm_branch(u_slabs, bsz, a_re, a_im, log_dt, b_re, b_im, c_re, c_im, d_skip, w_glu, b_glu):
    n_slab, n_rows, _ = u_slabs.shape
    width = n_slab * LANES
    assert bsz == SUBLANES
    bmat, cmat, ar, ai = _ssm_params(a_re, a_im, log_dt, b_re, b_im, c_re, c_im, bsz)
    n_state2 = bmat.shape[1]
    rows = SSM_STEPS * bsz
    const = lambda c: (0, 0)
    slab_spec = pl.BlockSpec((n_slab, rows, LANES), lambda c: (0, c, 0))
    return pl.pallas_call(
        functools.partial(_ssm_kernel, n_steps=SSM_STEPS),
        out_shape=jax.ShapeDtypeStruct(u_slabs.shape, F32),
        grid=(n_rows // rows,),
        in_specs=[slab_spec,
                  pl.BlockSpec(bmat.shape, const), pl.BlockSpec(cmat.shape, const),
                  pl.BlockSpec(ar.shape, const), pl.BlockSpec(ai.shape, const),
                  pl.BlockSpec((1, width), const), pl.BlockSpec((width, width), const),
                  pl.BlockSpec((1, width), const)],
        out_specs=slab_spec,
        scratch_shapes=[pltpu.VMEM((rows, n_state2), F32), pltpu.VMEM((bsz, n_state2), F32)],
        compiler_params=_cparams("arbitrary"),
        name="ssm",
    )(u_slabs, bmat, cmat, ar, ai, d_skip.reshape(1, width).astype(F32),
      w_glu.astype(BF16), b_glu.reshape(1, width).astype(F32))


ROUTE_LANES = {"id0": 0, "id1": 1, "rank0": 2, "rank1": 3, "w0": 4, "w1": 5}
GROUP_LANE0 = N_EXPERTS


def _merge_kernel(*refs, bsz_total):
    n_pat = len(DILATION_PATTERNS)
    x_ref, gate_ref = refs[0:2]
    attn_refs = refs[2:2 + 2 * n_pat]
    (ssm_ref, mod_ref, gpost_ref, gpre_ref, wba_ref, wbs_ref, wout_ref,
     wr_hi_ref, wr_lo_ref, br_ref) = refs[2 + 2 * n_pat:12 + 2 * n_pat]
    x1_ref, h2_ref, route_ref, count_ref = refs[12 + 2 * n_pat:16 + 2 * n_pat]
    o_tok, lse_tok, ssm_tok, carry = refs[16 + 2 * n_pat:]

    @pl.when(jnp.logical_and(pl.program_id(0) == 0, pl.program_id(1) == 0))
    def _():
        carry[...] = jnp.zeros_like(carry)

    nb, tt, d = x_ref.shape
    rows = nb * tt
    n_slab = ssm_ref.shape[0]
    b0 = pl.program_id(0) * nb

    for bb in range(nb):
        for s in range(n_slab):
            ssm_tok[bb * tt:(bb + 1) * tt, s * LANES:(s + 1) * LANES] = (
                ssm_ref[s, pl.ds(b0 + bb, tt, stride=bsz_total), :])

    acc = None
    for p, (_, dil) in enumerate(DILATION_PATTERNS):
        o_ref, lse_ref = attn_refs[2 * p], attn_refs[2 * p + 1]
        if dil == 1:
            o_p = o_ref[:, 0].astype(F32).reshape(rows, n_slab * LANES)
            lse_p = lse_ref[:, 0].reshape(rows, n_slab * LANES)
        else:
            sub = tt // dil
            for bb in range(nb):
                for r in range(dil):
                    o_blk = o_ref[bb, r].astype(F32)
                    l_blk = lse_ref[bb, r]
                    for s in range(n_slab):
                        dst = pl.ds(bb * tt + r, sub, stride=dil)
                        o_tok[s, dst, :] = o_blk[:, s * LANES:(s + 1) * LANES]
                        lse_tok[s, dst, :] = l_blk[:, s * LANES:(s + 1) * LANES]
            o_p = _lane_concat([o_tok[s] for s in range(n_slab)])
            lse_p = _lane_concat([lse_tok[s] for s in range(n_slab)])
        if acc is None:
            acc, lse_run = o_p, lse_p
        else:
            m = jnp.maximum(lse_run, lse_p)
            a = jnp.exp(lse_run - m)
            b = jnp.exp(lse_p - m)
            den = a + b
            acc = (a * acc + b * o_p) / den
            lse_run = m + jnp.log(den)
    attn = acc

    g_attn = gate_ref[:, :, 0:d].astype(F32).reshape(rows, d)
    g_ssm = gate_ref[:, :, d:].astype(F32).reshape(rows, d)
    merged = (g_attn * _dot(attn.astype(BF16), wba_ref[...])
              + g_ssm * _dot(ssm_tok[...].astype(BF16), wbs_ref[...]))
    y = _dot(merged.astype(BF16), wout_ref[...])
    gate1 = mod_ref[:, 2, :][:, None, :]
    shift2 = mod_ref[:, 3, :][:, None, :]
    scale2 = mod_ref[:, 4, :][:, None, :]
    x1 = x_ref[...] + gate1 * _rms_norm(y, gpost_ref[...]).reshape(nb, tt, d)
    x1_ref[...] = x1
    h2 = _rms_norm(x1, gpre_ref[...]) * (1.0 + scale2) + shift2
    for bb in range(nb):
        for s in range(d // LANES):
            h2_ref[bb, pl.ds(s, tt, stride=ROW_TILE), :] = h2[bb, :, s * LANES:(s + 1) * LANES]

    logits = _dot_split(h2.reshape(rows, d), wr_hi_ref[...], wr_lo_ref[...]) + br_ref[...]
    lane = lax.broadcasted_iota(jnp.int32, (rows, LANES), 1).astype(F32)
    big = float(LANES)
    is_group = jnp.logical_and(lane >= GROUP_LANE0, lane < GROUP_LANE0 + N_EXPERT_GROUPS)
    gl = jnp.where(is_group, logits, -jnp.inf)
    g_max = jnp.max(gl, axis=-1, keepdims=True)
    g_sel = jnp.min(jnp.where(gl == g_max, lane, big), axis=-1, keepdims=True) - GROUP_LANE0
    g_gate = 1.0 / jnp.sum(jnp.exp(gl - g_max), axis=-1, keepdims=True)
    lo = g_sel * EXPERTS_PER_GROUP
    in_group = jnp.logical_and(lane >= lo, lane < lo + EXPERTS_PER_GROUP)
    el = jnp.where(in_group, logits, -jnp.inf)
    t0 = jnp.max(el, axis=-1, keepdims=True)
    i0 = jnp.min(jnp.where(el == t0, lane, big), axis=-1, keepdims=True)
    el1 = jnp.where(lane == i0, -jnp.inf, el)
    t1 = jnp.max(el1, axis=-1, keepdims=True)
    i1 = jnp.min(jnp.where(el1 == t1, lane, big), axis=-1, keepdims=True)
    e = jnp.exp(t1 - t0)
    w0 = g_gate / (1.0 + e)
    w1 = g_gate * e / (1.0 + e)

    hit0 = lane == i0
    hit1 = lane == i1
    onehot = jnp.logical_or(hit0, hit1).astype(F32)
    row = lax.broadcasted_iota(jnp.int32, (rows, rows), 0)
    col = lax.broadcasted_iota(jnp.int32, (rows, rows), 1)
    strict_lower = (col < row).astype(BF16)
    before = _dot(strict_lower, onehot.astype(BF16)) + carry[...]
    rank0 = jnp.sum(jnp.where(hit0, before, 0.0), axis=-1, keepdims=True)
    rank1 = jnp.sum(jnp.where(hit1, before, 0.0), axis=-1, keepdims=True)
    carry[...] = carry[...] + jnp.sum(onehot, axis=0, keepdims=True)
    count_ref[...] = jnp.broadcast_to(carry[...], count_ref.shape)

    route = jnp.zeros((rows, LANES), F32)
    for name, val in (("id0", i0), ("id1", i1), ("rank0", rank0), ("rank1", rank1),
                      ("w0", w0), ("w1", w1)):
        route = jnp.where(lane == ROUTE_LANES[name], val, route)
    route_ref[...] = route.reshape(nb, tt, LANES)


def _merge_and_route(x, gates, attn_outs, ssm_slabs, mod3, g_post, g_pre, wba, wbs, wout,
                     w_rg, b_rg, w_re, b_re):
    bsz, seq, d = x.shape
    tt, nb = TIME_TILE, MERGE_BATCH
    aw = ATTN_WIDTH
    n_slab = ssm_slabs.shape[0]
    assert aw == n_slab * LANES
    wr = jnp.zeros((d, LANES), F32).at[:, :N_EXPERTS].set(w_re.astype(F32))
    wr = wr.at[:, GROUP_LANE0:GROUP_LANE0 + N_EXPERT_GROUPS].set(w_rg.astype(F32))
    br = jnp.zeros((1, LANES), F32).at[0, :N_EXPERTS].set(b_re.astype(F32))
    br = br.at[0, GROUP_LANE0:GROUP_LANE0 + N_EXPERT_GROUPS].set(b_rg.astype(F32))
    wr_hi, wr_lo = _split_bf16(wr)
    tok = lambda h, i: (h, i, 0)
    const = lambda h, i: (0, 0)
    attn_args, attn_specs = [], []
    for (o_p, lse_p), (_, dil) in zip(attn_outs, DILATION_PATTERNS):
        spec = pl.BlockSpec((nb, dil, tt // dil, aw), lambda h, i: (h, 0, i, 0))
        attn_args += [o_p, lse_p]
        attn_specs += [spec, spec]
    return pl.pallas_call(
        functools.partial(_merge_kernel, bsz_total=bsz),
        out_shape=(jax.ShapeDtypeStruct((bsz, seq, d), F32),
                   jax.ShapeDtypeStruct((bsz, seq * ROW_TILE, LANES), F32),
                   jax.ShapeDtypeStruct((bsz, seq, LANES), F32),
                   jax.ShapeDtypeStruct((SUBLANES, LANES), F32)),
        grid=(bsz // nb, seq // tt),
        in_specs=[pl.BlockSpec((nb, tt, d), tok),
                  pl.BlockSpec((nb, tt, gates.shape[-1]), tok)]
                 + attn_specs
                 + [pl.BlockSpec((n_slab, tt * bsz, LANES), lambda h, i: (0, i, 0)),
                    pl.BlockSpec((nb, 6, d), lambda h, i: (h, 0, 0)),
                    pl.BlockSpec((1, d), const), pl.BlockSpec((1, d), const),
                    pl.BlockSpec(wba.shape, const), pl.BlockSpec(wbs.shape, const),
                    pl.BlockSpec(wout.shape, const),
                    pl.BlockSpec((d, LANES), const), pl.BlockSpec((d, LANES), const),
                    pl.BlockSpec((1, LANES), const)],
        out_specs=(pl.BlockSpec((nb, tt, d), tok), pl.BlockSpec((nb, tt * ROW_TILE, LANES), tok),
                   pl.BlockSpec((nb, tt, LANES), tok),
                   pl.BlockSpec((SUBLANES, LANES), const)),
        scratch_shapes=[pltpu.VMEM((n_slab, nb * tt, LANES), F32),
                        pltpu.VMEM((n_slab, nb * tt, LANES), F32),
                        pltpu.VMEM((nb * tt, n_slab * LANES), F32),
                        pltpu.VMEM((1, LANES), F32)],
        compiler_params=_cparams("arbitrary", "arbitrary"),
        name="merge",
    )(x, gates, *attn_args, ssm_slabs, mod3, g_post, g_pre, wba, wbs, wout, wr_hi, wr_lo, br)


ISSUE_UNROLL = 8


def _dispatch_kernel(dest_ref, seg_ref, h_ref, xs_ref, zero_buf, sem, zsem):
    rows = h_ref.shape[0] // ROW_TILE
    blk = zero_buf.shape[0]
    n_blocks = xs_ref.shape[0] // blk
    base = pl.program_id(0) * rows

    @pl.when(pl.program_id(0) == 0)
    def _():
        zero_buf[...] = jnp.zeros_like(zero_buf)

        def zero_copy(row0):
            return pltpu.make_async_copy(zero_buf, xs_ref.at[pl.ds(pl.multiple_of(row0, blk), blk)],
                                         zsem)

        def fill_tail(e, carry):
            @pl.when(seg_ref[N_EXPERTS + e] > 0)
            def _():
                zero_copy(seg_ref[e] * ROW_TILE - blk).start()
            return carry

        def fill_unused(j, carry):
            zero_copy(j * blk).start()
            return carry

        def wait_tail(e, carry):
            @pl.when(seg_ref[N_EXPERTS + e] > 0)
            def _():
                zero_copy(0).wait()
            return carry

        def wait_unused(j, carry):
            zero_copy(0).wait()
            return carry

        n_used = seg_ref[2 * N_EXPERTS]
        lax.fori_loop(0, N_EXPERTS, fill_tail, 0)
        lax.fori_loop(n_used, n_blocks, fill_unused, 0)
        lax.fori_loop(0, N_EXPERTS, wait_tail, 0)
        lax.fori_loop(n_used, n_blocks, wait_unused, 0)

    group = ISSUE_UNROLL * ROW_TILE

    def issue(g, carry):
        g0 = pl.multiple_of(g * group, group)
        for rr in range(ISSUE_UNROLL):
            src = h_ref.at[pl.ds(g0 + rr * ROW_TILE, ROW_TILE)]
            for k in range(TOP_K):
                d = dest_ref[TOP_K * (base + g * ISSUE_UNROLL + rr) + k]
                dst = xs_ref.at[pl.ds(pl.multiple_of(d * ROW_TILE, ROW_TILE), ROW_TILE)]
                pltpu.make_async_copy(src, dst, sem).start(priority=k)
        return carry

    lax.fori_loop(0, rows // ISSUE_UNROLL, issue, 0)
    for _ in range(TOP_K):
        pltpu.make_async_copy(h_ref, xs_ref.at[pl.ds(0, rows * ROW_TILE)], sem).wait()


def _dispatch(h2_tiles, dest, seg_info, cap):
    n_tok = h2_tiles.shape[0] // ROW_TILE
    rows = DISPATCH_ROWS
    return pl.pallas_call(
        _dispatch_kernel,
        out_shape=jax.ShapeDtypeStruct((cap * ROW_TILE, LANES), h2_tiles.dtype),
        grid_spec=pltpu.PrefetchScalarGridSpec(
            num_scalar_prefetch=2,
            grid=(n_tok // rows,),
            in_specs=[pl.BlockSpec((rows * ROW_TILE, LANES), lambda i, dest, seg: (i, 0))],
            out_specs=pl.BlockSpec(memory_space=pl.ANY),
            scratch_shapes=[pltpu.VMEM((MOE_ROWS * ROW_TILE, LANES), h2_tiles.dtype),
                            pltpu.SemaphoreType.DMA, pltpu.SemaphoreType.DMA]),
        compiler_params=_cparams("arbitrary"),
        name="dispatch",
    )(dest, seg_info, h2_tiles)


def _expert_kernel(blk_expert_ref, n_used_ref, x_ref, w1_ref, w3_ref, w2_ref, y_ref):
    del blk_expert_ref
    used = pl.program_id(0) < n_used_ref[0]

    rows = x_ref.shape[0] // ROW_TILE

    @pl.when(used)
    def _():
        x = _lane_concat([x_ref[pl.ds(s, rows, stride=ROW_TILE), :] for s in range(ROW_TILE)])
        xb = x.astype(BF16)
        h1 = _dot(xb, w1_ref[0].astype(BF16))
        h3 = _dot(xb, w3_ref[0].astype(BF16))
        act = (h1 * _sigmoid(h1)) * h3
        y = _dot(act.astype(BF16), w2_ref[0].astype(BF16))
        for s in range(ROW_TILE):
            y_ref[pl.ds(s, rows, stride=ROW_TILE), :] = y[:, s * LANES:(s + 1) * LANES]

    @pl.when(jnp.logical_not(used))
    def _():
        y_ref[...] = jnp.zeros_like(y_ref)


def _experts(xs, blk_expert, n_used, w1, w3, w2):
    d, de = w1.shape[-2:]
    assert d == ROW_TILE * LANES
    rows = MOE_ROWS
    tile_rows = rows * ROW_TILE
    xblk = lambda i, be, nu: (jnp.minimum(i, nu[0] - 1), 0)
    wblk = lambda i, be, nu: (be[i], 0, 0)
    return pl.pallas_call(
        _expert_kernel,
        out_shape=jax.ShapeDtypeStruct(xs.shape, F32),
        grid_spec=pltpu.PrefetchScalarGridSpec(
            num_scalar_prefetch=2,
            grid=(xs.shape[0] // tile_rows,),
            in_specs=[pl.BlockSpec((tile_rows, LANES), xblk),
                      pl.BlockSpec((1, d, de), wblk), pl.BlockSpec((1, d, de), wblk),
                      pl.BlockSpec((1, de, d), wblk)],
            out_specs=pl.BlockSpec((tile_rows, LANES), lambda i, be, nu: (i, 0))),
        compiler_params=_cparams("arbitrary"),
        name="experts",
    )(blk_expert, n_used, xs, w1, w3, w2)


def _combine_kernel(dest_ref, ys_ref, x1_ref, route_ref, mod_ref, g_ref, o_ref, ybuf, sem):
    rows = x1_ref.shape[0]
    step = pl.program_id(0)
    n_step = pl.num_programs(0)

    group = ISSUE_UNROLL * ROW_TILE

    def gather(tile, slot):
        base = tile * rows

        def issue(g, carry):
            g0 = pl.multiple_of(g * group, group)
            for rr in range(ISSUE_UNROLL):
                for k in range(TOP_K):
                    d = dest_ref[TOP_K * (base + g * ISSUE_UNROLL + rr) + k]
                    src = ys_ref.at[pl.ds(pl.multiple_of(d * ROW_TILE, ROW_TILE), ROW_TILE)]
                    dst = ybuf.at[slot, k, pl.ds(g0 + rr * ROW_TILE, ROW_TILE)]
                    pltpu.make_async_copy(src, dst, sem.at[slot]).start(priority=k)
            return carry

        lax.fori_loop(0, rows // ISSUE_UNROLL, issue, 0)

    slot = lax.rem(step, 2)

    @pl.when(step == 0)
    def _():
        gather(step, 0)

    @pl.when(step + 1 < n_step)
    def _():
        gather(step + 1, 1 - slot)

    for k in range(TOP_K):
        pltpu.make_async_copy(ys_ref.at[pl.ds(0, rows * ROW_TILE)], ybuf.at[slot, k],
                              sem.at[slot]).wait()
    route = route_ref[...]
    w0 = route[:, ROUTE_LANES["w0"]:ROUTE_LANES["w0"] + 1]
    w1 = route[:, ROUTE_LANES["w1"]:ROUTE_LANES["w1"] + 1]

    def rows_of(k):
        return _lane_concat([ybuf[slot, k, pl.ds(s, rows, stride=ROW_TILE), :]
                             for s in range(ROW_TILE)])

    y = rows_of(0) * w0 + rows_of(1) * w1
    o_ref[...] = x1_ref[...] + mod_ref[0, 5:6, :] * _rms_norm(y, g_ref[...])


def _combine(ys, dest, x1_flat, route_flat, mod3, g_post, seq):
    n_tok, d = x1_flat.shape
    rows = COMBINE_ROWS
    assert seq % rows == 0
    return pl.pallas_call(
        _combine_kernel,
        out_shape=jax.ShapeDtypeStruct((n_tok, d), F32),
        grid_spec=pltpu.PrefetchScalarGridSpec(
            num_scalar_prefetch=1,
            grid=(n_tok // rows,),
            in_specs=[pl.BlockSpec(memory_space=pl.ANY),
                      pl.BlockSpec((rows, d), lambda i, dest: (i, 0)),
                      pl.BlockSpec((rows, LANES), lambda i, dest: (i, 0)),
                      pl.BlockSpec((1, 6, d), lambda i, dest: (i * rows // seq, 0, 0)),
                      pl.BlockSpec((1, d), lambda i, dest: (0, 0))],
            out_specs=pl.BlockSpec((rows, d), lambda i, dest: (i, 0)),
            scratch_shapes=[pltpu.VMEM((2, TOP_K, rows * ROW_TILE, LANES), F32),
                            pltpu.SemaphoreType.DMA((2,))]),
        compiler_params=_cparams("arbitrary"),
        name="combine",
    )(dest, ys, x1_flat, route_flat, mod3, g_post)


def _moe_layout(route_flat, counts):
    rows = MOE_ROWS
    n_tok = route_flat.shape[0]
    ids = route_flat[:, 0:TOP_K].astype(jnp.int32)
    rank = route_flat[:, TOP_K:2 * TOP_K].astype(jnp.int32)
    counts = counts.astype(jnp.int32)
    padded = (counts + rows - 1) // rows * rows
    pad_ends = jnp.cumsum(padded)
    pad_starts = pad_ends - padded
    expert = jnp.arange(N_EXPERTS, dtype=jnp.int32)
    start_of = jnp.sum(jnp.where(ids[..., None] == expert, pad_starts, 0), axis=-1)
    dest = (start_of + rank).reshape(n_tok * TOP_K)
    n_blocks = (n_tok * TOP_K + N_EXPERTS * (rows - 1) + rows - 1) // rows
    blk_row0 = jnp.arange(n_blocks, dtype=jnp.int32) * rows
    blk_expert = jnp.minimum(jnp.sum(pad_ends[None, :] <= blk_row0[:, None], axis=-1),
                             N_EXPERTS - 1).astype(jnp.int32)
    n_used = (pad_ends[-1] // rows).reshape(1).astype(jnp.int32)
    seg_info = jnp.concatenate([pad_ends, padded, n_used]).astype(jnp.int32)
    return dest, blk_expert, n_used, seg_info, n_blocks * rows


def kernel(x, c, w_mod, b_mod, g_pre_mix, g_post_mix, g_pre_ffn, g_post_ffn, w_in, rel_bias, a_re, a_im, log_dt, ssm_b_re, ssm_b_im, ssm_c_re, ssm_c_im, d_skip, w_glu, b_glu, w_branch_attn, w_branch_ssm, w_out, w_router_group, b_router_group, w_router_expert, b_router_expert, w1, w3, w2):
    bsz, seq, d = x.shape
    depth = w_mod.shape[0]
    ssm_width = w_glu.shape[-1]
    n_pat = len(DILATION_PATTERNS)
    for l in range(depth):
        mod3 = _modulation(c, w_mod[l], b_mod[l]).reshape(bsz, 6, d)
        w_in_l = w_in[l].astype(BF16)
        g_pre = g_pre_mix[l].reshape(1, d)
        qkv = _qkv_projection(x, mod3, g_pre, w_in_l[:, :3 * ATTN_WIDTH])
        u_slabs, gates = _ugate_projection(x, mod3, g_pre, w_in_l[:, 3 * ATTN_WIDTH:], ssm_width)
        attn_outs = [_attention_pattern(*qkv[3 * p:3 * p + 3], rel_bias, DILATION_PATTERNS[p][1])
                     for p in range(n_pat)]
        ssm_slabs = _ssm_branch(u_slabs, bsz, a_re[l], a_im[l], log_dt[l], ssm_b_re[l],
                                ssm_b_im[l], ssm_c_re[l], ssm_c_im[l], d_skip[l], w_glu[l], b_glu[l])
        x1, h2, route, counts = _merge_and_route(
            x, gates, attn_outs, ssm_slabs, mod3, g_post_mix[l].reshape(1, d),
            g_pre_ffn[l].reshape(1, d), w_branch_attn[l].astype(BF16),
            w_branch_ssm[l].astype(BF16), w_out[l].astype(BF16),
            w_router_group[l], b_router_group[l], w_router_expert[l], b_router_expert[l])
        route_flat = route.reshape(bsz * seq, LANES)
        dest, blk_expert, n_used, seg_info, cap = _moe_layout(route_flat, counts[0, :N_EXPERTS])
        xs = _dispatch(h2.reshape(bsz * seq * ROW_TILE, LANES), dest, seg_info, cap)
        ys = _experts(xs, blk_expert, n_used, w1[l], w3[l], w2[l])
        x = _combine(ys, dest, x1.reshape(bsz * seq, d), route_flat, mod3,
                     g_post_ffn[l].reshape(1, d), seq).reshape(bsz, seq, d)
    return x
```

```python
import functools
import math

import numpy as np
import jax
import jax.numpy as jnp
from jax import lax
from jax.experimental import pallas as pl
from jax.experimental.pallas import tpu as pltpu

F32 = jnp.float32
BF16 = jnp.bfloat16

N_HEADS = 8
HEAD_DIM = 64
ATTN_WIDTH = N_HEADS * HEAD_DIM
DILATION_PATTERNS = ((128, 1), (512, 4), (2048, 16))
NUM_BUCKETS = 32
MAX_DISTANCE = 2048
N_EXPERT_GROUPS = 4
EXPERTS_PER_GROUP = 8
N_EXPERTS = N_EXPERT_GROUPS * EXPERTS_PER_GROUP
TOP_K = 2
RMS_EPS = 1e-6
NEG_INF = -1e30

LANES = 128
SUBLANES = 8
VMEM_LIMIT_BYTES = 56 * 1024 * 1024

ATTN_BLK = 128
QKV_ROWS = 512
TIME_TILE = 128
MERGE_BATCH = 4
SSM_STEPS = 64
MOE_ROWS = 256
DISPATCH_ROWS = 2048
COMBINE_ROWS = 256
ROW_TILE = 8


def _cparams(*sem):
    return pltpu.CompilerParams(dimension_semantics=sem, vmem_limit_bytes=VMEM_LIMIT_BYTES)


def _sigmoid(x):
    return 1.0 / (1.0 + jnp.exp(-x))


def _dot(a, b):
    return jnp.dot(a, b, preferred_element_type=F32)


def _split_bf16(a):
    hi = a.astype(BF16)
    lo = (a - hi.astype(F32)).astype(BF16)
    return hi, lo


def _dot_split(a, w_hi, w_lo):
    a_hi, a_lo = _split_bf16(a)
    return _dot(a_hi, w_hi) + _dot(a_lo, w_hi) + _dot(a_hi, w_lo)


def _rms_norm(x, gain):
    ms = jnp.mean(x * x, axis=-1, keepdims=True)
    return x * lax.rsqrt(ms + RMS_EPS) * gain


def _lane_concat(ref_slabs):
    return jnp.concatenate(ref_slabs, axis=-1)


def _mod_kernel(c_ref, w_ref, b_ref, o_ref):
    c = c_ref[...]
    a = c * _sigmoid(c)
    w_hi, w_lo = _split_bf16(w_ref[...])
    o_ref[...] = _dot_split(a, w_hi, w_lo) + b_ref[...]


def _modulation(c, w_mod, b_mod):
    bsz, d = c.shape
    n = w_mod.shape[1]
    tn = 1024
    return pl.pallas_call(
        _mod_kernel,
        out_shape=jax.ShapeDtypeStruct((bsz, n), F32),
        grid=(n // tn,),
        in_specs=[pl.BlockSpec((bsz, d), lambda j: (0, 0)),
                  pl.BlockSpec((d, tn), lambda j: (0, j)),
                  pl.BlockSpec((1, tn), lambda j: (0, j))],
        out_specs=pl.BlockSpec((bsz, tn), lambda j: (0, j)),
        compiler_params=_cparams("arbitrary"),
        name="mod",
    )(c, w_mod, b_mod.reshape(1, n))


def _qkv_kernel(x_ref, mod_ref, g_ref, w_ref, *rest):
    n_pat = len(DILATION_PATTERNS)
    out_refs, slab = rest[:3 * n_pat], rest[3 * n_pat]
    h = _rms_norm(x_ref[0], g_ref[...]) * (1.0 + mod_ref[0, 1:2, :]) + mod_ref[0, 0:1, :]
    res = _dot(h.astype(BF16), w_ref[...])
    rows = res.shape[0]
    n_slab = res.shape[1] // LANES
    per_tensor = ATTN_WIDTH // LANES
    for s in range(n_slab):
        piece = res[:, s * LANES:(s + 1) * LANES]
        if s < per_tensor:
            piece = piece * (HEAD_DIM ** -0.5)
        slab[s] = piece
    for p, (_, dil) in enumerate(DILATION_PATTERNS):
        sub = rows // dil
        for t in range(3):
            out = out_refs[3 * p + t]
            for r in range(dil):
                pieces = [slab[t * per_tensor + s, pl.ds(r, sub, stride=dil), :]
                          if dil > 1 else slab[t * per_tensor + s]
                          for s in range(per_tensor)]
                out[0, r] = _lane_concat(pieces).astype(out.dtype)


def _qkv_projection(x, mod3, g_pre, w_qkv):
    bsz, seq, d = x.shape
    tm = QKV_ROWS
    out_shape, out_specs = [], []
    for _, dil in DILATION_PATTERNS:
        assert tm % (dil * 2 * SUBLANES) == 0
        for _ in range(3):
            out_shape.append(jax.ShapeDtypeStruct((bsz, dil, seq // dil, ATTN_WIDTH), BF16))
            out_specs.append(pl.BlockSpec((1, dil, tm // dil, ATTN_WIDTH),
                                          lambda b, i: (b, 0, i, 0)))
    return pl.pallas_call(
        _qkv_kernel,
        out_shape=out_shape,
        grid=(bsz, seq // tm),
        in_specs=[pl.BlockSpec((1, tm, d), lambda b, i: (b, i, 0)),
                  pl.BlockSpec((1, 6, d), lambda b, i: (b, 0, 0)),
                  pl.BlockSpec((1, d), lambda b, i: (0, 0)),
                  pl.BlockSpec(w_qkv.shape, lambda b, i: (0, 0))],
        out_specs=out_specs,
        scratch_shapes=[pltpu.VMEM((w_qkv.shape[1] // LANES, tm, LANES), F32)],
        compiler_params=_cparams("arbitrary", "arbitrary"),
        name="qkv",
    )(x, mod3, g_pre, w_qkv)


def _ugate_kernel(x_ref, mod_ref, g_ref, w_ref, u_ref, gate_ref):
    bsz, tt, d = x_ref.shape
    shift = mod_ref[:, 0, :][:, None, :]
    scale = mod_ref[:, 1, :][:, None, :]
    h = _rms_norm(x_ref[...], g_ref[...]) * (1.0 + scale) + shift
    hb = h.reshape(bsz * tt, d).astype(BF16)
    n_slab = u_ref.shape[0]
    sw = n_slab * LANES
    u = _dot(hb, w_ref[:, 0:sw])
    for b in range(bsz):
        for s in range(n_slab):
            u_ref[s, pl.ds(b, tt, stride=bsz), :] = u[b * tt:(b + 1) * tt, s * LANES:(s + 1) * LANES]
    gw = gate_ref.shape[-1]
    chunk = 512
    for c0 in range(0, gw, chunk):
        g = _sigmoid(_dot(hb, w_ref[:, sw + c0:sw + c0 + chunk]))
        gate_ref[:, :, c0:c0 + chunk] = g.reshape(bsz, tt, chunk).astype(BF16)


def _ugate_projection(x, mod3, g_pre, w_ug, ssm_width):
    bsz, seq, d = x.shape
    tt = TIME_TILE
    gw = w_ug.shape[1] - ssm_width
    n_slab = ssm_width // LANES
    return pl.pallas_call(
        _ugate_kernel,
        out_shape=(jax.ShapeDtypeStruct((n_slab, seq * bsz, LANES), F32),
                   jax.ShapeDtypeStruct((bsz, seq, gw), BF16)),
        grid=(seq // tt,),
        in_specs=[pl.BlockSpec((bsz, tt, d), lambda i: (0, i, 0)),
                  pl.BlockSpec((bsz, 6, d), lambda i: (0, 0, 0)),
                  pl.BlockSpec((1, d), lambda i: (0, 0)),
                  pl.BlockSpec(w_ug.shape, lambda i: (0, 0))],
        out_specs=(pl.BlockSpec((n_slab, tt * bsz, LANES), lambda i: (0, i, 0)),
                   pl.BlockSpec((bsz, tt, gw), lambda i: (0, i, 0))),
        compiler_params=_cparams("arbitrary"),
        name="ugate",
    )(x, mod3, g_pre, w_ug)


def _t5_bucket_np(dist):
    exact = NUM_BUCKETS // 2
    d_f = np.maximum(dist, exact).astype(np.float32)
    large = exact + (np.log(d_f / np.float32(exact)) / np.float32(math.log(MAX_DISTANCE / exact))
                     * np.float32(NUM_BUCKETS - exact)).astype(np.int32)
    return np.where(dist < exact, dist, np.minimum(large, NUM_BUCKETS - 1))


def _bucket_map_t(dil):
    blk = ATTN_BLK
    ki = np.arange(2 * blk)[:, None]
    qi = np.arange(blk)[None, :]
    return _t5_bucket_np(np.maximum(blk + qi - ki, 0) * dil).astype(np.int32)


def _attn_kernel(relb_ref, q_ref, kc_ref, kp_ref, vc_ref, vp_ref, bucket_ref,
                 o_ref, lse_ref, kbuf, vbuf, bias_t, *, n_sub):
    blk = ATTN_BLK
    first_call = jnp.logical_and(pl.program_id(0) == 0,
                                 jnp.logical_and(pl.program_id(1) == 0, pl.program_id(2) == 0))

    @pl.when(first_call)
    def _():
        bucket = bucket_ref[...]
        ki = lax.broadcasted_iota(jnp.int32, bucket.shape, 0)
        qi = lax.broadcasted_iota(jnp.int32, bucket.shape, 1)
        dist = blk + qi - ki
        band = jnp.logical_and(dist >= 0, dist <= blk)
        band_first = jnp.logical_and(band, ki >= blk)

        def per_head(h, carry):
            acc = jnp.zeros(bucket.shape, F32)
            for b in range(NUM_BUCKETS):
                acc = jnp.where(bucket == b, relb_ref[b, h], acc)
            bias_t[0, h] = jnp.where(band_first, acc, NEG_INF)
            bias_t[1, h] = jnp.where(band, acc, NEG_INF)
            return carry

        lax.fori_loop(0, N_HEADS, per_head, 0)

    first_step = pl.program_id(2) == 0
    kbuf[0:blk, :] = kp_ref[0, 0]
    kbuf[blk:, :] = kc_ref[0, 0]
    vbuf[0:blk, :] = vp_ref[0, 0]
    vbuf[blk:, :] = vc_ref[0, 0]

    lane = lax.broadcasted_iota(jnp.int32, (1, LANES), 1)
    lo_half = lane < HEAD_DIM
    bd_row = lax.broadcasted_iota(jnp.int32, (4 * blk, LANES), 0)
    bd_col = lax.broadcasted_iota(jnp.int32, (4 * blk, LANES), 1)
    ones_bd = ((bd_row < 2 * blk) == (bd_col < HEAD_DIM)).astype(F32).astype(BF16)
    contract_last = (((1,), (1,)), ((), ()))
    contract_first = (((0,), (0,)), ((), ()))

    def sub_block(i, carry):
        r0 = pl.multiple_of(i * blk, blk)
        q = q_ref[0, 0, pl.ds(r0, blk), :]
        kk = kbuf[pl.ds(r0, 2 * blk), :]
        vv = vbuf[pl.ds(r0, 2 * blk), :]
        variant = jnp.where(jnp.logical_and(first_step, i == 0), 0, 1)
        for j in range(N_HEADS // 2):
            cols = slice(j * LANES, (j + 1) * LANES)
            qj, kj, vj = q[:, cols], kk[:, cols], vv[:, cols]
            probs_t, maxes = [], []
            for hh in range(2):
                sel = lo_half if hh == 0 else jnp.logical_not(lo_half)
                qm = jnp.where(sel, qj, jnp.zeros_like(qj))
                s_t = lax.dot_general(kj, qm, contract_last, preferred_element_type=F32)
                s_t = s_t + bias_t[variant, 2 * j + hh]
                m = jnp.max(s_t, axis=0, keepdims=True)
                probs_t.append(jnp.exp(s_t - m).astype(BF16))
                maxes.append(m)
            p2_t = jnp.concatenate(probs_t, axis=0)
            v_bd = jnp.concatenate([jnp.where(lo_half, vj, jnp.zeros_like(vj)),
                                    jnp.where(lo_half, jnp.zeros_like(vj), vj)], axis=0)
            rhs = jnp.concatenate([v_bd, ones_bd], axis=1)
            ol = lax.dot_general(p2_t, rhs, contract_first, preferred_element_type=F32)
            o2, l2 = ol[:, :LANES], ol[:, LANES:]
            m_t = jnp.concatenate([jnp.broadcast_to(maxes[0], (HEAD_DIM, blk)),
                                   jnp.broadcast_to(maxes[1], (HEAD_DIM, blk))], axis=0)
            o_ref[0, 0, pl.ds(r0, blk), cols] = (o2 / l2).astype(o_ref.dtype)
            lse_ref[0, 0, pl.ds(r0, blk), cols] = m_t.T + jnp.log(l2)
        return carry

    lax.fori_loop(0, n_sub, sub_block, 0, unroll=True)


def _attention_pattern(q, k, v, rel_bias, dil):
    bsz, _, sub_len, aw = q.shape
    blk = ATTN_BLK
    assert sub_len % blk == 0
    tq = min(512, sub_len)
    n_sub = tq // blk
    ratio = tq // blk
    cur = lambda b, r, n: (b, r, n, 0)
    prev = lambda b, r, n: (b, r, jnp.maximum(n * ratio - 1, 0), 0)
    blk_cur = pl.BlockSpec((1, 1, tq, aw), cur)
    blk_prev = pl.BlockSpec((1, 1, blk, aw), prev)
    bucket = jnp.asarray(_bucket_map_t(dil))
    o_dtype = BF16 if TIME_TILE // dil >= 2 * SUBLANES else F32
    return pl.pallas_call(
        functools.partial(_attn_kernel, n_sub=n_sub),
        out_shape=(jax.ShapeDtypeStruct(q.shape, o_dtype), jax.ShapeDtypeStruct(q.shape, F32)),
        grid=(bsz, dil, sub_len // tq),
        in_specs=[pl.BlockSpec(memory_space=pltpu.SMEM),
                  blk_cur, blk_cur, blk_prev, blk_cur, blk_prev,
                  pl.BlockSpec(bucket.shape, lambda b, r, n: (0, 0))],
        out_specs=(blk_cur, blk_cur),
        scratch_shapes=[pltpu.VMEM((tq + blk, aw), BF16), pltpu.VMEM((tq + blk, aw), BF16),
                        pltpu.VMEM((2, N_HEADS, 2 * blk, blk), F32)],
        compiler_params=_cparams("arbitrary", "arbitrary", "arbitrary"),
        name=f"attn_dil{dil}",
    )(rel_bias.astype(F32), q, k, k, v, v, bucket)


def _ssm_kernel(u_ref, bmat_ref, cmat_ref, ar_ref, ai_ref, dskip_ref, wglu_ref, bglu_ref,
                o_ref, hbuf, hstate, *, n_steps):
    @pl.when(pl.program_id(0) == 0)
    def _():
        hstate[...] = jnp.zeros_like(hstate)

    n_slab = u_ref.shape[0]
    n_state = hbuf.shape[1] // 2
    per = n_state // n_slab
    us = [u_ref[s] for s in range(n_slab)]
    for s in range(n_slab):
        bu = _dot(us[s].astype(BF16), bmat_ref[s])
        hbuf[:, s * per:(s + 1) * per] = bu[:, :per]
        hbuf[:, n_state + s * per:n_state + (s + 1) * per] = bu[:, per:]
    half = n_state // 2
    for part in range(2):
        re_cols = slice(part * half, (part + 1) * half)
        im_cols = slice(n_state + part * half, n_state + (part + 1) * half)
        ar = ar_ref[:, re_cols]
        ai = ai_ref[:, re_cols]

        def step(t, carry, re_cols=re_cols, im_cols=im_cols, ar=ar, ai=ai):
            hr, hi = carry
            r0 = pl.multiple_of(t * SUBLANES, SUBLANES)
            nr = ar * hr - ai * hi + hbuf[pl.ds(r0, SUBLANES), re_cols]
            ni = ar * hi + ai * hr + hbuf[pl.ds(r0, SUBLANES), im_cols]
            hbuf[pl.ds(r0, SUBLANES), re_cols] = nr
            hbuf[pl.ds(r0, SUBLANES), im_cols] = ni
            return nr, ni

        hr, hi = lax.fori_loop(0, n_steps, step, (hstate[:, re_cols], hstate[:, im_cols]),
                               unroll=2)
        hstate[:, re_cols] = hr
        hstate[:, im_cols] = hi

    ys = []
    for s in range(n_slab):
        h_s = _lane_concat([hbuf[:, s * per:(s + 1) * per],
                            hbuf[:, n_state + s * per:n_state + (s + 1) * per]])
        ys.append(_dot(h_s.astype(BF16), cmat_ref[s])
                  + dskip_ref[:, s * LANES:(s + 1) * LANES] * us[s])
    y = _lane_concat(ys)
    y = 0.5 * y * (1.0 + jnp.tanh(math.sqrt(2.0 / math.pi) * (y + 0.044715 * (y * y * y))))
    z = _dot(y.astype(BF16), wglu_ref[...]) + bglu_ref[...]
    out = y * _sigmoid(z)
    for s in range(n_slab):
        o_ref[s] = out[:, s * LANES:(s + 1) * LANES]


def _ssm_params(a_re, a_im, log_dt, b_re, b_im, c_re, c_im, bsz):
    g, p = a_re.shape
    hg = b_re.shape[-1]
    dt = jnp.exp(log_dt.astype(F32))[:, None]
    a_re, a_im = a_re.astype(F32), a_im.astype(F32)
    mag = jnp.exp(a_re * dt)
    abar_re = mag * jnp.cos(a_im * dt)
    abar_im = mag * jnp.sin(a_im * dt)
    den = a_re * a_re + a_im * a_im
    q_re = ((abar_re - 1.0) * a_re + abar_im * a_im) / den
    q_im = (abar_im * a_re - (abar_re - 1.0) * a_im) / den
    b_re, b_im = b_re.astype(F32), b_im.astype(F32)
    bb_re = q_re[..., None] * b_re - q_im[..., None] * b_im
    bb_im = q_re[..., None] * b_im + q_im[..., None] * b_re
    gs = LANES // hg
    n_slab = g // gs
    eye = jnp.eye(gs, dtype=F32)

    def in_mat(t):
        t = t.reshape(n_slab, gs, p, hg)
        return jnp.einsum('sgph,gk->sghkp', t, eye).reshape(n_slab, gs * hg, gs * p)

    def out_mat(t):
        t = t.reshape(n_slab, gs, hg, p)
        return jnp.einsum('sghp,gk->sgpkh', t, eye).reshape(n_slab, gs * p, gs * hg)

    bmat = jnp.concatenate([in_mat(bb_re), in_mat(bb_im)], axis=2).astype(BF16)
    cmat = jnp.concatenate([out_mat(c_re.astype(F32)), -out_mat(c_im.astype(F32))],
                           axis=1).astype(BF16)
    ar = jnp.broadcast_to(abar_re.reshape(1, g * p), (bsz, g * p))
    ai = jnp.broadcast_to(abar_im.reshape(1, g * p), (bsz, g * p))
    return bmat, cmat, ar, ai


def _ssm_branch(u_slabs, bsz, a_re, a_im, log_dt, b_re, b_im, c_re, c_im, d_skip, w_glu, b_glu):
    n_slab, n_rows, _ = u_slabs.shape
    width = n_slab * LANES
    assert bsz == SUBLANES
    bmat, cmat, ar, ai = _ssm_params(a_re, a_im, log_dt, b_re, b_im, c_re, c_im, bsz)
    n_state2 = n_slab * bmat.shape[2]
    rows = SSM_STEPS * bsz
    const = lambda c: (0, 0)
    const3 = lambda c: (0, 0, 0)
    slab_spec = pl.BlockSpec((n_slab, rows, LANES), lambda c: (0, c, 0))
    return pl.pallas_call(
        functools.partial(_ssm_kernel, n_steps=SSM_STEPS),
        out_shape=jax.ShapeDtypeStruct(u_slabs.shape, F32),
        grid=(n_rows // rows,),
        in_specs=[slab_spec,
                  pl.BlockSpec(bmat.shape, const3), pl.BlockSpec(cmat.shape, const3),
                  pl.BlockSpec(ar.shape, const), pl.BlockSpec(ai.shape, const),
                  pl.BlockSpec((1, width), const), pl.BlockSpec((width, width), const),
                  pl.BlockSpec((1, width), const)],
        out_specs=slab_spec,
        scratch_shapes=[pltpu.VMEM((rows, n_state2), F32), pltpu.VMEM((bsz, n_state2), F32)],
        compiler_params=_cparams("arbitrary"),
        name="ssm",
    )(u_slabs, bmat, cmat, ar, ai, d_skip.reshape(1, width).astype(F32),
      w_glu.astype(BF16), b_glu.reshape(1, width).astype(F32))


ROUTE_LANES = {"id0": 0, "id1": 1, "rank0": 2, "rank1": 3, "w0": 4, "w1": 5}
GROUP_LANE0 = N_EXPERTS


def _merge_kernel(*refs, bsz_total):
    n_pat = len(DILATION_PATTERNS)
    x_ref, gate_ref = refs[0:2]
    attn_refs = refs[2:2 + 2 * n_pat]
    (ssm_ref, mod_ref, gpost_ref, gpre_ref, wba_ref, wbs_ref, wout_ref,
     wr_cat_ref, wr_hi_ref, br_ref) = refs[2 + 2 * n_pat:12 + 2 * n_pat]
    x1_ref, h2_ref, route_ref, count_ref = refs[12 + 2 * n_pat:16 + 2 * n_pat]
    o_tok, lse_tok, ssm_tok, carry = refs[16 + 2 * n_pat:]

    @pl.when(jnp.logical_and(pl.program_id(0) == 0, pl.program_id(1) == 0))
    def _():
        carry[...] = jnp.zeros_like(carry)

    nb, tt, d = x_ref.shape
    rows = nb * tt
    n_slab = ssm_ref.shape[0]
    b0 = pl.program_id(0) * nb

    for bb in range(nb):
        for s in range(n_slab):
            ssm_tok[bb * tt:(bb + 1) * tt, s * LANES:(s + 1) * LANES] = (
                ssm_ref[s, pl.ds(b0 + bb, tt, stride=bsz_total), :])

    slot = 0
    sources = []
    for p, (_, dil) in enumerate(DILATION_PATTERNS):
        o_ref, lse_ref = attn_refs[2 * p], attn_refs[2 * p + 1]
        if dil == 1:
            sources.append((o_ref, lse_ref, None))
            continue
        sub = tt // dil
        for bb in range(nb):
            for r in range(dil):
                o_blk = o_ref[bb, r].astype(F32)
                l_blk = lse_ref[bb, r]
                for s in range(n_slab):
                    dst = pl.ds(bb * tt + r, sub, stride=dil)
                    o_tok[slot, s, dst, :] = o_blk[:, s * LANES:(s + 1) * LANES]
                    lse_tok[slot, s, dst, :] = l_blk[:, s * LANES:(s + 1) * LANES]
        sources.append((o_ref, lse_ref, slot))
        slot += 1
    attn_slabs = []
    for s in range(n_slab):
        cols = slice(s * LANES, (s + 1) * LANES)
        o_ps, lse_ps = [], []
        for o_ref, lse_ref, src_slot in sources:
            if src_slot is None:
                o_ps.append(o_ref[:, 0, :, cols].astype(F32).reshape(rows, LANES))
                lse_ps.append(lse_ref[:, 0, :, cols].reshape(rows, LANES))
            else:
                o_ps.append(o_tok[src_slot, s])
                lse_ps.append(lse_tok[src_slot, s])
        m = functools.reduce(jnp.maximum, lse_ps)
        es = [jnp.exp(l - m) for l in lse_ps]
        num = functools.reduce(lambda a, b: a + b, [e * o for e, o in zip(es, o_ps)])
        den = functools.reduce(lambda a, b: a + b, es)
        attn_slabs.append(num / den)
    attn = _lane_concat(attn_slabs)

    g_attn = gate_ref[:, :, 0:d].astype(F32).reshape(rows, d)
    g_ssm = gate_ref[:, :, d:].astype(F32).reshape(rows, d)
    merged = (g_attn * _dot(attn.astype(BF16), wba_ref[...])
              + g_ssm * _dot(ssm_tok[...].astype(BF16), wbs_ref[...]))
    y = _dot(merged.astype(BF16), wout_ref[...])
    gate1 = mod_ref[:, 2, :][:, None, :]
    shift2 = mod_ref[:, 3, :][:, None, :]
    scale2 = mod_ref[:, 4, :][:, None, :]
    x1 = x_ref[...] + gate1 * _rms_norm(y, gpost_ref[...]).reshape(nb, tt, d)
    x1_ref[...] = x1
    h2 = _rms_norm(x1, gpre_ref[...]) * (1.0 + scale2) + shift2
    for bb in range(nb):
        for s in range(d // LANES):
            h2_ref[bb, pl.ds(s, tt, stride=ROW_TILE), :] = h2[bb, :, s * LANES:(s + 1) * LANES]

    a_hi, a_lo = _split_bf16(h2.reshape(rows, d))
    hi_pass = _dot(a_hi, wr_cat_ref[...])
    logits = hi_pass[:, :LANES] + _dot(a_lo, wr_hi_ref[...]) + hi_pass[:, LANES:] + br_ref[...]
    lane = lax.broadcasted_iota(jnp.int32, (rows, LANES), 1).astype(F32)
    big = float(LANES)
    is_group = jnp.logical_and(lane >= GROUP_LANE0, lane < GROUP_LANE0 + N_EXPERT_GROUPS)
    gl = jnp.where(is_group, logits, -jnp.inf)
    g_max = jnp.max(gl, axis=-1, keepdims=True)
    g_sel = jnp.min(jnp.where(gl == g_max, lane, big), axis=-1, keepdims=True) - GROUP_LANE0
    g_gate = 1.0 / jnp.sum(jnp.exp(gl - g_max), axis=-1, keepdims=True)
    lo = g_sel * EXPERTS_PER_GROUP
    in_group = jnp.logical_and(lane >= lo, lane < lo + EXPERTS_PER_GROUP)
    el = jnp.where(in_group, logits, -jnp.inf)
    t0 = jnp.max(el, axis=-1, keepdims=True)
    i0 = jnp.min(jnp.where(el == t0, lane, big), axis=-1, keepdims=True)
    el1 = jnp.where(lane == i0, -jnp.inf, el)
    t1 = jnp.max(el1, axis=-1, keepdims=True)
    i1 = jnp.min(jnp.where(el1 == t1, lane, big), axis=-1, keepdims=True)
    e = jnp.exp(t1 - t0)
    w0 = g_gate / (1.0 + e)
    w1 = g_gate * e / (1.0 + e)

    hit0 = lane == i0
    hit1 = lane == i1
    onehot = jnp.logical_or(hit0, hit1).astype(F32)
    row = lax.broadcasted_iota(jnp.int32, (rows, rows), 0)
    col = lax.broadcasted_iota(jnp.int32, (rows, rows), 1)
    strict_lower = (col < row).astype(BF16)
    before = _dot(strict_lower, onehot.astype(BF16)) + carry[...]
    rank0 = jnp.sum(jnp.where(hit0, before, 0.0), axis=-1, keepdims=True)
    rank1 = jnp.sum(jnp.where(hit1, before, 0.0), axis=-1, keepdims=True)
    carry[...] = carry[...] + jnp.sum(onehot, axis=0, keepdims=True)
    count_ref[...] = jnp.broadcast_to(carry[...], count_ref.shape)

    route = jnp.zeros((rows, LANES), F32)
    for name, val in (("id0", i0), ("id1", i1), ("rank0", rank0), ("rank1", rank1),
                      ("w0", w0), ("w1", w1)):
        route = jnp.where(lane == ROUTE_LANES[name], val, route)
    route_ref[...] = route.reshape(nb, tt, LANES)


def _merge_and_route(x, gates, attn_outs, ssm_slabs, mod3, g_post, g_pre, wba, wbs, wout,
                     w_rg, b_rg, w_re, b_re):
    bsz, seq, d = x.shape
    tt, nb = TIME_TILE, MERGE_BATCH
    aw = ATTN_WIDTH
    n_slab = ssm_slabs.shape[0]
    assert aw == n_slab * LANES
    wr = jnp.zeros((d, LANES), F32).at[:, :N_EXPERTS].set(w_re.astype(F32))
    wr = wr.at[:, GROUP_LANE0:GROUP_LANE0 + N_EXPERT_GROUPS].set(w_rg.astype(F32))
    br = jnp.zeros((1, LANES), F32).at[0, :N_EXPERTS].set(b_re.astype(F32))
    br = br.at[0, GROUP_LANE0:GROUP_LANE0 + N_EXPERT_GROUPS].set(b_rg.astype(F32))
    wr_hi, wr_lo = _split_bf16(wr)
    wr_cat = jnp.concatenate([wr_hi, wr_lo], axis=1)
    n_strided = sum(1 for _, dil in DILATION_PATTERNS if dil > 1)
    tok = lambda h, i: (h, i, 0)
    const = lambda h, i: (0, 0)
    attn_args, attn_specs = [], []
    for (o_p, lse_p), (_, dil) in zip(attn_outs, DILATION_PATTERNS):
        spec = pl.BlockSpec((nb, dil, tt // dil, aw), lambda h, i: (h, 0, i, 0))
        attn_args += [o_p, lse_p]
        attn_specs += [spec, spec]
    return pl.pallas_call(
        functools.partial(_merge_kernel, bsz_total=bsz),
        out_shape=(jax.ShapeDtypeStruct((bsz, seq, d), F32),
                   jax.ShapeDtypeStruct((bsz, seq * ROW_TILE, LANES), F32),
                   jax.ShapeDtypeStruct((bsz, seq, LANES), F32),
                   jax.ShapeDtypeStruct((SUBLANES, LANES), F32)),
        grid=(bsz // nb, seq // tt),
        in_specs=[pl.BlockSpec((nb, tt, d), tok),
                  pl.BlockSpec((nb, tt, gates.shape[-1]), tok)]
                 + attn_specs
                 + [pl.BlockSpec((n_slab, tt * bsz, LANES), lambda h, i: (0, i, 0)),
                    pl.BlockSpec((nb, 6, d), lambda h, i: (h, 0, 0)),
                    pl.BlockSpec((1, d), const), pl.BlockSpec((1, d), const),
                    pl.BlockSpec(wba.shape, const), pl.BlockSpec(wbs.shape, const),
                    pl.BlockSpec(wout.shape, const),
                    pl.BlockSpec((d, 2 * LANES), const), pl.BlockSpec((d, LANES), const),
                    pl.BlockSpec((1, LANES), const)],
        out_specs=(pl.BlockSpec((nb, tt, d), tok), pl.BlockSpec((nb, tt * ROW_TILE, LANES), tok),
                   pl.BlockSpec((nb, tt, LANES), tok),
                   pl.BlockSpec((SUBLANES, LANES), const)),
        scratch_shapes=[pltpu.VMEM((n_strided, n_slab, nb * tt, LANES), F32),
                        pltpu.VMEM((n_strided, n_slab, nb * tt, LANES), F32),
                        pltpu.VMEM((nb * tt, n_slab * LANES), F32),
                        pltpu.VMEM((1, LANES), F32)],
        compiler_params=_cparams("arbitrary", "arbitrary"),
        name="merge",
    )(x, gates, *attn_args, ssm_slabs, mod3, g_post, g_pre, wba, wbs, wout, wr_cat, wr_hi, br)


ISSUE_UNROLL = 8


def _dispatch_kernel(dest_ref, seg_ref, h_ref, xs_ref, zero_buf, sem, zsem):
    rows = h_ref.shape[0] // ROW_TILE
    blk = zero_buf.shape[0]
    n_blocks = xs_ref.shape[0] // blk
    base = pl.program_id(0) * rows

    @pl.when(pl.program_id(0) == 0)
    def _():
        zero_buf[...] = jnp.zeros_like(zero_buf)

        def zero_copy(row0):
            return pltpu.make_async_copy(zero_buf, xs_ref.at[pl.ds(pl.multiple_of(row0, blk), blk)],
                                         zsem)

        def fill_tail(e, carry):
            @pl.when(seg_ref[N_EXPERTS + e] > 0)
            def _():
                zero_copy(seg_ref[e] * ROW_TILE - blk).start()
            return carry

        def fill_unused(j, carry):
            zero_copy(j * blk).start()
            return carry

        def wait_tail(e, carry):
            @pl.when(seg_ref[N_EXPERTS + e] > 0)
            def _():
                zero_copy(0).wait()
            return carry

        def wait_unused(j, carry):
            zero_copy(0).wait()
            return carry

        n_used = seg_ref[2 * N_EXPERTS]
        lax.fori_loop(0, N_EXPERTS, fill_tail, 0)
        lax.fori_loop(n_used, n_blocks, fill_unused, 0)
        lax.fori_loop(0, N_EXPERTS, wait_tail, 0)
        lax.fori_loop(n_used, n_blocks, wait_unused, 0)

    group = ISSUE_UNROLL * ROW_TILE

    def issue(g, carry):
        g0 = pl.multiple_of(g * group, group)
        for rr in range(ISSUE_UNROLL):
            src = h_ref.at[pl.ds(g0 + rr * ROW_TILE, ROW_TILE)]
            for k in range(TOP_K):
                d = dest_ref[TOP_K * (base + g * ISSUE_UNROLL + rr) + k]
                dst = xs_ref.at[pl.ds(pl.multiple_of(d * ROW_TILE, ROW_TILE), ROW_TILE)]
                pltpu.make_async_copy(src, dst, sem).start(priority=k)
        return carry

    lax.fori_loop(0, rows // ISSUE_UNROLL, issue, 0)
    for _ in range(TOP_K):
        pltpu.make_async_copy(h_ref, xs_ref.at[pl.ds(0, rows * ROW_TILE)], sem).wait()


def _dispatch(h2_tiles, dest, seg_info, cap):
    n_tok = h2_tiles.shape[0] // ROW_TILE
    rows = DISPATCH_ROWS
    return pl.pallas_call(
        _dispatch_kernel,
        out_shape=jax.ShapeDtypeStruct((cap * ROW_TILE, LANES), h2_tiles.dtype),
        grid_spec=pltpu.PrefetchScalarGridSpec(
            num_scalar_prefetch=2,
            grid=(n_tok // rows,),
            in_specs=[pl.BlockSpec((rows * ROW_TILE, LANES), lambda i, dest, seg: (i, 0))],
            out_specs=pl.BlockSpec(memory_space=pl.ANY),
            scratch_shapes=[pltpu.VMEM((MOE_ROWS * ROW_TILE, LANES), h2_tiles.dtype),
                            pltpu.SemaphoreType.DMA, pltpu.SemaphoreType.DMA]),
        compiler_params=_cparams("arbitrary"),
        name="dispatch",
    )(dest, seg_info, h2_tiles)


def _expert_kernel(blk_expert_ref, n_used_ref, x_ref, w1_ref, w3_ref, w2_ref, y_ref,
                   w1b, w3b, w2b):
    i = pl.program_id(0)
    used = i < n_used_ref[0]
    rows = x_ref.shape[0] // ROW_TILE
    new_expert = jnp.logical_or(
        i == 0, blk_expert_ref[i] != blk_expert_ref[jnp.maximum(i - 1, 0)])

    @pl.when(jnp.logical_and(used, new_expert))
    def _():
        w1b[...] = w1_ref[0].astype(BF16)
        w3b[...] = w3_ref[0].astype(BF16)
        w2b[...] = w2_ref[0].astype(BF16)

    @pl.when(used)
    def _():
        x = _lane_concat([x_ref[pl.ds(s, rows, stride=ROW_TILE), :] for s in range(ROW_TILE)])
        xb = x.astype(BF16)
        h1 = _dot(xb, w1b[...])
        h3 = _dot(xb, w3b[...])
        act = (h1 * _sigmoid(h1)) * h3
        y = _dot(act.astype(BF16), w2b[...])
        for s in range(ROW_TILE):
            y_ref[pl.ds(s, rows, stride=ROW_TILE), :] = y[:, s * LANES:(s + 1) * LANES]

    @pl.when(jnp.logical_not(used))
    def _():
        y_ref[...] = jnp.zeros_like(y_ref)


def _experts(xs, blk_expert, n_used, w1, w3, w2):
    d, de = w1.shape[-2:]
    assert d == ROW_TILE * LANES
    rows = MOE_ROWS
    tile_rows = rows * ROW_TILE
    xblk = lambda i, be, nu: (jnp.minimum(i, nu[0] - 1), 0)
    wblk = lambda i, be, nu: (be[i], 0, 0)
    return pl.pallas_call(
        _expert_kernel,
        out_shape=jax.ShapeDtypeStruct(xs.shape, F32),
        grid_spec=pltpu.PrefetchScalarGridSpec(
            num_scalar_prefetch=2,
            grid=(xs.shape[0] // tile_rows,),
            in_specs=[pl.BlockSpec((tile_rows, LANES), xblk),
                      pl.BlockSpec((1, d, de), wblk), pl.BlockSpec((1, d, de), wblk),
                      pl.BlockSpec((1, de, d), wblk)],
            out_specs=pl.BlockSpec((tile_rows, LANES), lambda i, be, nu: (i, 0)),
            scratch_shapes=[pltpu.VMEM((d, de), BF16), pltpu.VMEM((d, de), BF16),
                            pltpu.VMEM((de, d), BF16)]),
        compiler_params=_cparams("arbitrary"),
        name="experts",
    )(blk_expert, n_used, xs, w1, w3, w2)


def _combine_kernel(dest_ref, ys_ref, x1_ref, route_ref, mod_ref, g_ref, o_ref, ybuf, sem):
    rows = x1_ref.shape[0]
    step = pl.program_id(0)
    n_step = pl.num_programs(0)

    group = ISSUE_UNROLL * ROW_TILE

    def gather(tile, slot):
        base = tile * rows

        def issue(g, carry):
            g0 = pl.multiple_of(g * group, group)
            for rr in range(ISSUE_UNROLL):
                for k in range(TOP_K):
                    d = dest_ref[TOP_K * (base + g * ISSUE_UNROLL + rr) + k]
                    src = ys_ref.at[pl.ds(pl.multiple_of(d * ROW_TILE, ROW_TILE), ROW_TILE)]
                    dst = ybuf.at[slot, k, pl.ds(g0 + rr * ROW_TILE, ROW_TILE)]
                    pltpu.make_async_copy(src, dst, sem.at[slot]).start(priority=k)
            return carry

        lax.fori_loop(0, rows // ISSUE_UNROLL, issue, 0)

    slot = lax.rem(step, 2)

    @pl.when(step == 0)
    def _():
        gather(step, 0)

    @pl.when(step + 1 < n_step)
    def _():
        gather(step + 1, 1 - slot)

    for k in range(TOP_K):
        pltpu.make_async_copy(ys_ref.at[pl.ds(0, rows * ROW_TILE)], ybuf.at[slot, k],
                              sem.at[slot]).wait()
    route = route_ref[...]
    w0 = route[:, ROUTE_LANES["w0"]:ROUTE_LANES["w0"] + 1]
    w1 = route[:, ROUTE_LANES["w1"]:ROUTE_LANES["w1"] + 1]

    def rows_of(k):
        return _lane_concat([ybuf[slot, k, pl.ds(s, rows, stride=ROW_TILE), :]
                             for s in range(ROW_TILE)])

    y = rows_of(0) * w0 + rows_of(1) * w1
    o_ref[...] = x1_ref[...] + mod_ref[0, 5:6, :] * _rms_norm(y, g_ref[...])


def _combine(ys, dest, x1_flat, route_flat, mod3, g_post, seq):
    n_tok, d = x1_flat.shape
    rows = COMBINE_ROWS
    assert seq % rows == 0
    return pl.pallas_call(
        _combine_kernel,
        out_shape=jax.ShapeDtypeStruct((n_tok, d), F32),
        grid_spec=pltpu.PrefetchScalarGridSpec(
            num_scalar_prefetch=1,
            grid=(n_tok // rows,),
            in_specs=[pl.BlockSpec(memory_space=pl.ANY),
                      pl.BlockSpec((rows, d), lambda i, dest: (i, 0)),
                      pl.BlockSpec((rows, LANES), lambda i, dest: (i, 0)),
                      pl.BlockSpec((1, 6, d), lambda i, dest: (i * rows // seq, 0, 0)),
                      pl.BlockSpec((1, d), lambda i, dest: (0, 0))],
            out_specs=pl.BlockSpec((rows, d), lambda i, dest: (i, 0)),
            scratch_shapes=[pltpu.VMEM((2, TOP_K, rows * ROW_TILE, LANES), F32),
                            pltpu.SemaphoreType.DMA((2,))]),
        compiler_params=_cparams("arbitrary"),
        name="combine",
    )(dest, ys, x1_flat, route_flat, mod3, g_post)


def _moe_layout(route_flat, counts):
    rows = MOE_ROWS
    n_tok = route_flat.shape[0]
    ids = route_flat[:, 0:TOP_K].astype(jnp.int32)
    rank = route_flat[:, TOP_K:2 * TOP_K].astype(jnp.int32)
    counts = counts.astype(jnp.int32)
    padded = (counts + rows - 1) // rows * rows
    pad_ends = jnp.cumsum(padded)
    pad_starts = pad_ends - padded
    expert = jnp.arange(N_EXPERTS, dtype=jnp.int32)
    start_of = jnp.sum(jnp.where(ids[..., None] == expert, pad_starts, 0), axis=-1)
    dest = (start_of + rank).reshape(n_tok * TOP_K)
    n_blocks = (n_tok * TOP_K + N_EXPERTS * (rows - 1) + rows - 1) // rows
    blk_row0 = jnp.arange(n_blocks, dtype=jnp.int32) * rows
    blk_expert = jnp.minimum(jnp.sum(pad_ends[None, :] <= blk_row0[:, None], axis=-1),
                             N_EXPERTS - 1).astype(jnp.int32)
    n_used = (pad_ends[-1] // rows).reshape(1).astype(jnp.int32)
    seg_info = jnp.concatenate([pad_ends, padded, n_used]).astype(jnp.int32)
    return dest, blk_expert, n_used, seg_info, n_blocks * rows


def kernel(x, c, w_mod, b_mod, g_pre_mix, g_post_mix, g_pre_ffn, g_post_ffn, w_in, rel_bias, a_re, a_im, log_dt, ssm_b_re, ssm_b_im, ssm_c_re, ssm_c_im, d_skip, w_glu, b_glu, w_branch_attn, w_branch_ssm, w_out, w_router_group, b_router_group, w_router_expert, b_router_expert, w1, w3, w2):
    bsz, seq, d = x.shape
    depth = w_mod.shape[0]
    ssm_width = w_glu.shape[-1]
    n_pat = len(DILATION_PATTERNS)
    for l in range(depth):
        mod3 = _modulation(c, w_mod[l], b_mod[l]).reshape(bsz, 6, d)
        w_in_l = w_in[l].astype(BF16)
        g_pre = g_pre_mix[l].reshape(1, d)
        qkv = _qkv_projection(x, mod3, g_pre, w_in_l[:, :3 * ATTN_WIDTH])
        u_slabs, gates = _ugate_projection(x, mod3, g_pre, w_in_l[:, 3 * ATTN_WIDTH:], ssm_width)
        attn_outs = [_attention_pattern(*qkv[3 * p:3 * p + 3], rel_bias, DILATION_PATTERNS[p][1])
                     for p in range(n_pat)]
        ssm_slabs = _ssm_branch(u_slabs, bsz, a_re[l], a_im[l], log_dt[l], ssm_b_re[l],
                                ssm_b_im[l], ssm_c_re[l], ssm_c_im[l], d_skip[l], w_glu[l], b_glu[l])
        x1, h2, route, counts = _merge_and_route(
            x, gates, attn_outs, ssm_slabs, mod3, g_post_mix[l].reshape(1, d),
            g_pre_ffn[l].reshape(1, d), w_branch_attn[l].astype(BF16),
            w_branch_ssm[l].astype(BF16), w_out[l].astype(BF16),
            w_router_group[l], b_router_group[l], w_router_expert[l], b_router_expert[l])
        route_flat = route.reshape(bsz * seq, LANES)
        dest, blk_expert, n_used, seg_info, cap = _moe_layout(route_flat, counts[0, :N_EXPERTS])
        xs = _dispatch(h2.reshape(bsz * seq * ROW_TILE, LANES), dest, seg_info, cap)
        ys = _experts(xs, blk_expert, n_used, w1[l], w3[l], w2[l])
        x = _combine(ys, dest, x1.reshape(bsz * seq, d), route_flat, mod3,
                     g_post_ffn[l].reshape(1, d), seq).reshape(bsz, seq, d)
    return x
```

```python
import functools
import math

import numpy as np
import jax
import jax.numpy as jnp
from jax import lax
from jax.experimental import pallas as pl
from jax.experimental.pallas import tpu as pltpu

F32 = jnp.float32
BF16 = jnp.bfloat16

N_HEADS = 8
HEAD_DIM = 64
ATTN_WIDTH = N_HEADS * HEAD_DIM
DILATION_PATTERNS = ((128, 1), (512, 4), (2048, 16))
NUM_BUCKETS = 32
MAX_DISTANCE = 2048
N_EXPERT_GROUPS = 4
EXPERTS_PER_GROUP = 8
N_EXPERTS = N_EXPERT_GROUPS * EXPERTS_PER_GROUP
TOP_K = 2
RMS_EPS = 1e-6
NEG_INF = -1e30

LANES = 128
SUBLANES = 8
VMEM_LIMIT_BYTES = 56 * 1024 * 1024

ATTN_BLK = 128
QKV_ROWS = 512
TIME_TILE = 128
MERGE_BATCH = 4
SSM_STEPS = 64
MOE_ROWS = 256
DISPATCH_ROWS = 2048
COMBINE_ROWS = 256
ROW_TILE = 4
U32 = jnp.uint32
HI_HALF = 0xFFFF0000


def _pack_rows(x):
    w = x.shape[1] // 2
    lo = lax.bitcast_convert_type(x[:, :w].astype(BF16).astype(F32), U32) >> 16
    hi = lax.bitcast_convert_type(x[:, w:].astype(BF16).astype(F32), U32) & U32(HI_HALF)
    return hi | lo


def _unpack_rows(p):
    lo = lax.bitcast_convert_type(p << 16, F32)
    hi = lax.bitcast_convert_type(p & U32(HI_HALF), F32)
    return jnp.concatenate([lo, hi], axis=1)


def _store_row_tiles(ref, packed):
    rows = packed.shape[0]
    for s in range(ROW_TILE):
        ref[pl.ds(s, rows, stride=ROW_TILE), :] = packed[:, s * LANES:(s + 1) * LANES]


def _load_row_tiles(ref, rows):
    return _lane_concat([ref[pl.ds(s, rows, stride=ROW_TILE), :] for s in range(ROW_TILE)])


def _cparams(*sem):
    return pltpu.CompilerParams(dimension_semantics=sem, vmem_limit_bytes=VMEM_LIMIT_BYTES)


def _sigmoid(x):
    return 1.0 / (1.0 + jnp.exp(-x))


def _dot(a, b):
    return jnp.dot(a, b, preferred_element_type=F32)


def _split_bf16(a):
    hi = a.astype(BF16)
    lo = (a - hi.astype(F32)).astype(BF16)
    return hi, lo


def _dot_split(a, w_hi, w_lo):
    a_hi, a_lo = _split_bf16(a)
    return _dot(a_hi, w_hi) + _dot(a_lo, w_hi) + _dot(a_hi, w_lo)


def _rms_norm(x, gain):
    ms = jnp.mean(x * x, axis=-1, keepdims=True)
    return x * lax.rsqrt(ms + RMS_EPS) * gain


def _lane_concat(ref_slabs):
    return jnp.concatenate(ref_slabs, axis=-1)


def _mod_kernel(c_ref, w_ref, b_ref, o_ref):
    c = c_ref[...]
    a = c * _sigmoid(c)
    w_hi, w_lo = _split_bf16(w_ref[...])
    o_ref[...] = _dot_split(a, w_hi, w_lo) + b_ref[...]


def _modulation(c, w_mod, b_mod):
    bsz, d = c.shape
    n = w_mod.shape[1]
    tn = 1024
    return pl.pallas_call(
        _mod_kernel,
        out_shape=jax.ShapeDtypeStruct((bsz, n), F32),
        grid=(n // tn,),
        in_specs=[pl.BlockSpec((bsz, d), lambda j: (0, 0)),
                  pl.BlockSpec((d, tn), lambda j: (0, j)),
                  pl.BlockSpec((1, tn), lambda j: (0, j))],
        out_specs=pl.BlockSpec((bsz, tn), lambda j: (0, j)),
        compiler_params=_cparams("arbitrary"),
        name="mod",
    )(c, w_mod, b_mod.reshape(1, n))


def _qkv_kernel(x_ref, mod_ref, g_ref, w_ref, *rest):
    n_pat = len(DILATION_PATTERNS)
    out_refs, slab = rest[:3 * n_pat], rest[3 * n_pat]
    h = _rms_norm(x_ref[0], g_ref[...]) * (1.0 + mod_ref[0, 1:2, :]) + mod_ref[0, 0:1, :]
    res = _dot(h.astype(BF16), w_ref[...])
    rows = res.shape[0]
    n_slab = res.shape[1] // LANES
    per_tensor = ATTN_WIDTH // LANES
    for s in range(n_slab):
        piece = res[:, s * LANES:(s + 1) * LANES]
        if s < per_tensor:
            piece = piece * (HEAD_DIM ** -0.5)
        slab[s] = piece
    for p, (_, dil) in enumerate(DILATION_PATTERNS):
        sub = rows // dil
        for t in range(3):
            out = out_refs[3 * p + t]
            for r in range(dil):
                pieces = [slab[t * per_tensor + s, pl.ds(r, sub, stride=dil), :]
                          if dil > 1 else slab[t * per_tensor + s]
                          for s in range(per_tensor)]
                out[0, r] = _lane_concat(pieces).astype(out.dtype)


def _qkv_projection(x, mod3, g_pre, w_qkv):
    bsz, seq, d = x.shape
    tm = QKV_ROWS
    out_shape, out_specs = [], []
    for _, dil in DILATION_PATTERNS:
        assert tm % (dil * 2 * SUBLANES) == 0
        for _ in range(3):
            out_shape.append(jax.ShapeDtypeStruct((bsz, dil, seq // dil, ATTN_WIDTH), BF16))
            out_specs.append(pl.BlockSpec((1, dil, tm // dil, ATTN_WIDTH),
                                          lambda b, i: (b, 0, i, 0)))
    return pl.pallas_call(
        _qkv_kernel,
        out_shape=out_shape,
        grid=(bsz, seq // tm),
        in_specs=[pl.BlockSpec((1, tm, d), lambda b, i: (b, i, 0)),
                  pl.BlockSpec((1, 6, d), lambda b, i: (b, 0, 0)),
                  pl.BlockSpec((1, d), lambda b, i: (0, 0)),
                  pl.BlockSpec(w_qkv.shape, lambda b, i: (0, 0))],
        out_specs=out_specs,
        scratch_shapes=[pltpu.VMEM((w_qkv.shape[1] // LANES, tm, LANES), F32)],
        compiler_params=_cparams("arbitrary", "arbitrary"),
        name="qkv",
    )(x, mod3, g_pre, w_qkv)


def _ugate_kernel(x_ref, mod_ref, g_ref, w_ref, u_ref, gate_ref):
    bsz, tt, d = x_ref.shape
    shift = mod_ref[:, 0, :][:, None, :]
    scale = mod_ref[:, 1, :][:, None, :]
    h = _rms_norm(x_ref[...], g_ref[...]) * (1.0 + scale) + shift
    hb = h.reshape(bsz * tt, d).astype(BF16)
    n_slab = u_ref.shape[0]
    sw = n_slab * LANES
    u = _dot(hb, w_ref[:, 0:sw])
    for b in range(bsz):
        for s in range(n_slab):
            u_ref[s, pl.ds(b, tt, stride=bsz), :] = u[b * tt:(b + 1) * tt, s * LANES:(s + 1) * LANES]
    gw = gate_ref.shape[-1]
    chunk = 512
    for c0 in range(0, gw, chunk):
        g = _sigmoid(_dot(hb, w_ref[:, sw + c0:sw + c0 + chunk]))
        gate_ref[:, :, c0:c0 + chunk] = g.reshape(bsz, tt, chunk).astype(BF16)


def _ugate_projection(x, mod3, g_pre, w_ug, ssm_width):
    bsz, seq, d = x.shape
    tt = TIME_TILE
    gw = w_ug.shape[1] - ssm_width
    n_slab = ssm_width // LANES
    return pl.pallas_call(
        _ugate_kernel,
        out_shape=(jax.ShapeDtypeStruct((n_slab, seq * bsz, LANES), F32),
                   jax.ShapeDtypeStruct((bsz, seq, gw), BF16)),
        grid=(seq // tt,),
        in_specs=[pl.BlockSpec((bsz, tt, d), lambda i: (0, i, 0)),
                  pl.BlockSpec((bsz, 6, d), lambda i: (0, 0, 0)),
                  pl.BlockSpec((1, d), lambda i: (0, 0)),
                  pl.BlockSpec(w_ug.shape, lambda i: (0, 0))],
        out_specs=(pl.BlockSpec((n_slab, tt * bsz, LANES), lambda i: (0, i, 0)),
                   pl.BlockSpec((bsz, tt, gw), lambda i: (0, i, 0))),
        compiler_params=_cparams("arbitrary"),
        name="ugate",
    )(x, mod3, g_pre, w_ug)


def _t5_bucket_np(dist):
    exact = NUM_BUCKETS // 2
    d_f = np.maximum(dist, exact).astype(np.float32)
    large = exact + (np.log(d_f / np.float32(exact)) / np.float32(math.log(MAX_DISTANCE / exact))
                     * np.float32(NUM_BUCKETS - exact)).astype(np.int32)
    return np.where(dist < exact, dist, np.minimum(large, NUM_BUCKETS - 1))


def _bucket_map_t(dil):
    blk = ATTN_BLK
    ki = np.arange(2 * blk)[:, None]
    qi = np.arange(blk)[None, :]
    return _t5_bucket_np(np.maximum(blk + qi - ki, 0) * dil).astype(np.int32)


def _attn_kernel(relb_ref, q_ref, kc_ref, kp_ref, vc_ref, vp_ref, bucket_ref,
                 o_ref, lse_ref, kbuf, vbuf, bias_t, *, n_sub):
    blk = ATTN_BLK
    first_call = jnp.logical_and(pl.program_id(0) == 0,
                                 jnp.logical_and(pl.program_id(1) == 0, pl.program_id(2) == 0))

    @pl.when(first_call)
    def _():
        bucket = bucket_ref[...]
        ki = lax.broadcasted_iota(jnp.int32, bucket.shape, 0)
        qi = lax.broadcasted_iota(jnp.int32, bucket.shape, 1)
        dist = blk + qi - ki
        band = jnp.logical_and(dist >= 0, dist <= blk)
        band_first = jnp.logical_and(band, ki >= blk)

        def per_head(h, carry):
            acc = jnp.zeros(bucket.shape, F32)
            for b in range(NUM_BUCKETS):
                acc = jnp.where(bucket == b, relb_ref[b, h], acc)
            bias_t[0, h] = jnp.where(band_first, acc, NEG_INF)
            bias_t[1, h] = jnp.where(band, acc, NEG_INF)
            return carry

        lax.fori_loop(0, N_HEADS, per_head, 0)

    first_step = pl.program_id(2) == 0
    kbuf[0:blk, :] = kp_ref[0, 0]
    kbuf[blk:, :] = kc_ref[0, 0]
    vbuf[0:blk, :] = vp_ref[0, 0]
    vbuf[blk:, :] = vc_ref[0, 0]

    lane = lax.broadcasted_iota(jnp.int32, (1, LANES), 1)
    lo_half = lane < HEAD_DIM
    bd_row = lax.broadcasted_iota(jnp.int32, (4 * blk, LANES), 0)
    bd_col = lax.broadcasted_iota(jnp.int32, (4 * blk, LANES), 1)
    ones_bd = ((bd_row < 2 * blk) == (bd_col < HEAD_DIM)).astype(F32).astype(BF16)
    contract_last = (((1,), (1,)), ((), ()))
    contract_first = (((0,), (0,)), ((), ()))

    def sub_block(i, carry):
        r0 = pl.multiple_of(i * blk, blk)
        q = q_ref[0, 0, pl.ds(r0, blk), :]
        kk = kbuf[pl.ds(r0, 2 * blk), :]
        vv = vbuf[pl.ds(r0, 2 * blk), :]
        variant = jnp.where(jnp.logical_and(first_step, i == 0), 0, 1)
        for j in range(N_HEADS // 2):
            cols = slice(j * LANES, (j + 1) * LANES)
            qj, kj, vj = q[:, cols], kk[:, cols], vv[:, cols]
            probs_t, maxes = [], []
            for hh in range(2):
                sel = lo_half if hh == 0 else jnp.logical_not(lo_half)
                qm = jnp.where(sel, qj, jnp.zeros_like(qj))
                s_t = lax.dot_general(kj, qm, contract_last, preferred_element_type=F32)
                s_t = s_t + bias_t[variant, 2 * j + hh]
                m = jnp.max(s_t, axis=0, keepdims=True)
                probs_t.append(jnp.exp(s_t - m).astype(BF16))
                maxes.append(m)
            p2_t = jnp.concatenate(probs_t, axis=0)
            v_bd = jnp.concatenate([jnp.where(lo_half, vj, jnp.zeros_like(vj)),
                                    jnp.where(lo_half, jnp.zeros_like(vj), vj)], axis=0)
            rhs = jnp.concatenate([v_bd, ones_bd], axis=1)
            ol = lax.dot_general(p2_t, rhs, contract_first, preferred_element_type=F32)
            o2, l2 = ol[:, :LANES], ol[:, LANES:]
            m_t = jnp.concatenate([jnp.broadcast_to(maxes[0], (HEAD_DIM, blk)),
                                   jnp.broadcast_to(maxes[1], (HEAD_DIM, blk))], axis=0)
            o_ref[0, 0, pl.ds(r0, blk), cols] = (o2 / l2).astype(o_ref.dtype)
            lse_ref[0, 0, pl.ds(r0, blk), cols] = m_t.T + jnp.log(l2)
        return carry

    lax.fori_loop(0, n_sub, sub_block, 0, unroll=True)


def _attention_pattern(q, k, v, rel_bias, dil):
    bsz, _, sub_len, aw = q.shape
    blk = ATTN_BLK
    assert sub_len % blk == 0
    tq = min(512, sub_len)
    n_sub = tq // blk
    ratio = tq // blk
    cur = lambda b, r, n: (b, r, n, 0)
    prev = lambda b, r, n: (b, r, jnp.maximum(n * ratio - 1, 0), 0)
    blk_cur = pl.BlockSpec((1, 1, tq, aw), cur)
    blk_prev = pl.BlockSpec((1, 1, blk, aw), prev)
    bucket = jnp.asarray(_bucket_map_t(dil))
    o_dtype = BF16 if TIME_TILE // dil >= 2 * SUBLANES else F32
    return pl.pallas_call(
        functools.partial(_attn_kernel, n_sub=n_sub),
        out_shape=(jax.ShapeDtypeStruct(q.shape, o_dtype), jax.ShapeDtypeStruct(q.shape, F32)),
        grid=(bsz, dil, sub_len // tq),
        in_specs=[pl.BlockSpec(memory_space=pltpu.SMEM),
                  blk_cur, blk_cur, blk_prev, blk_cur, blk_prev,
                  pl.BlockSpec(bucket.shape, lambda b, r, n: (0, 0))],
        out_specs=(blk_cur, blk_cur),
        scratch_shapes=[pltpu.VMEM((tq + blk, aw), BF16), pltpu.VMEM((tq + blk, aw), BF16),
                        pltpu.VMEM((2, N_HEADS, 2 * blk, blk), F32)],
        compiler_params=_cparams("arbitrary", "arbitrary", "arbitrary"),
        name=f"attn_dil{dil}",
    )(rel_bias.astype(F32), q, k, k, v, v, bucket)


def _ssm_kernel(u_ref, bmat_ref, cmat_ref, ar_ref, ai_ref, dskip_ref, wglu_ref, bglu_ref,
                o_ref, hbuf, hstate, *, n_steps):
    @pl.when(pl.program_id(0) == 0)
    def _():
        hstate[...] = jnp.zeros_like(hstate)

    n_slab = u_ref.shape[0]
    n_state = hbuf.shape[1] // 2
    per = n_state // n_slab
    us = [u_ref[s] for s in range(n_slab)]
    for s in range(n_slab):
        bu = _dot(us[s].astype(BF16), bmat_ref[s])
        hbuf[:, s * per:(s + 1) * per] = bu[:, :per]
        hbuf[:, n_state + s * per:n_state + (s + 1) * per] = bu[:, per:]
    half = n_state // 2
    for part in range(2):
        re_cols = slice(part * half, (part + 1) * half)
        im_cols = slice(n_state + part * half, n_state + (part + 1) * half)
        ar = ar_ref[:, re_cols]
        ai = ai_ref[:, re_cols]

        def step(t, carry, re_cols=re_cols, im_cols=im_cols, ar=ar, ai=ai):
            hr, hi = carry
            r0 = pl.multiple_of(t * SUBLANES, SUBLANES)
            nr = ar * hr - ai * hi + hbuf[pl.ds(r0, SUBLANES), re_cols]
            ni = ar * hi + ai * hr + hbuf[pl.ds(r0, SUBLANES), im_cols]
            hbuf[pl.ds(r0, SUBLANES), re_cols] = nr
            hbuf[pl.ds(r0, SUBLANES), im_cols] = ni
            return nr, ni

        hr, hi = lax.fori_loop(0, n_steps, step, (hstate[:, re_cols], hstate[:, im_cols]),
                               unroll=2)
        hstate[:, re_cols] = hr
        hstate[:, im_cols] = hi

    ys = []
    for s in range(n_slab):
        h_s = _lane_concat([hbuf[:, s * per:(s + 1) * per],
                            hbuf[:, n_state + s * per:n_state + (s + 1) * per]])
        ys.append(_dot(h_s.astype(BF16), cmat_ref[s])
                  + dskip_ref[:, s * LANES:(s + 1) * LANES] * us[s])
    y = _lane_concat(ys)
    y = 0.5 * y * (1.0 + jnp.tanh(math.sqrt(2.0 / math.pi) * (y + 0.044715 * (y * y * y))))
    z = _dot(y.astype(BF16), wglu_ref[...]) + bglu_ref[...]
    out = y * _sigmoid(z)
    for s in range(n_slab):
        o_ref[s] = out[:, s * LANES:(s + 1) * LANES]


def _ssm_params(a_re, a_im, log_dt, b_re, b_im, c_re, c_im, bsz):
    g, p = a_re.shape
    hg = b_re.shape[-1]
    dt = jnp.exp(log_dt.astype(F32))[:, None]
    a_re, a_im = a_re.astype(F32), a_im.astype(F32)
    mag = jnp.exp(a_re * dt)
    abar_re = mag * jnp.cos(a_im * dt)
    abar_im = mag * jnp.sin(a_im * dt)
    den = a_re * a_re + a_im * a_im
    q_re = ((abar_re - 1.0) * a_re + abar_im * a_im) / den
    q_im = (abar_im * a_re - (abar_re - 1.0) * a_im) / den
    b_re, b_im = b_re.astype(F32), b_im.astype(F32)
    bb_re = q_re[..., None] * b_re - q_im[..., None] * b_im
    bb_im = q_re[..., None] * b_im + q_im[..., None] * b_re
    gs = LANES // hg
    n_slab = g // gs
    eye = jnp.eye(gs, dtype=F32)

    def in_mat(t):
        t = t.reshape(n_slab, gs, p, hg)
        return jnp.einsum('sgph,gk->sghkp', t, eye).reshape(n_slab, gs * hg, gs * p)

    def out_mat(t):
        t = t.reshape(n_slab, gs, hg, p)
        return jnp.einsum('sghp,gk->sgpkh', t, eye).reshape(n_slab, gs * p, gs * hg)

    bmat = jnp.concatenate([in_mat(bb_re), in_mat(bb_im)], axis=2).astype(BF16)
    cmat = jnp.concatenate([out_mat(c_re.astype(F32)), -out_mat(c_im.astype(F32))],
                           axis=1).astype(BF16)
    ar = jnp.broadcast_to(abar_re.reshape(1, g * p), (bsz, g * p))
    ai = jnp.broadcast_to(abar_im.reshape(1, g * p), (bsz, g * p))
    return bmat, cmat, ar, ai


def _ssm_branch(u_slabs, bsz, a_re, a_im, log_dt, b_re, b_im, c_re, c_im, d_skip, w_glu, b_glu):
    n_slab, n_rows, _ = u_slabs.shape
    width = n_slab * LANES
    assert bsz == SUBLANES
    bmat, cmat, ar, ai = _ssm_params(a_re, a_im, log_dt, b_re, b_im, c_re, c_im, bsz)
    n_state2 = n_slab * bmat.shape[2]
    rows = SSM_STEPS * bsz
    const = lambda c: (0, 0)
    const3 = lambda c: (0, 0, 0)
    slab_spec = pl.BlockSpec((n_slab, rows, LANES), lambda c: (0, c, 0))
    return pl.pallas_call(
        functools.partial(_ssm_kernel, n_steps=SSM_STEPS),
        out_shape=jax.ShapeDtypeStruct(u_slabs.shape, F32),
        grid=(n_rows // rows,),
        in_specs=[slab_spec,
                  pl.BlockSpec(bmat.shape, const3), pl.BlockSpec(cmat.shape, const3),
                  pl.BlockSpec(ar.shape, const), pl.BlockSpec(ai.shape, const),
                  pl.BlockSpec((1, width), const), pl.BlockSpec((width, width), const),
                  pl.BlockSpec((1, width), const)],
        out_specs=slab_spec,
        scratch_shapes=[pltpu.VMEM((rows, n_state2), F32), pltpu.VMEM((bsz, n_state2), F32)],
        compiler_params=_cparams("arbitrary"),
        name="ssm",
    )(u_slabs, bmat, cmat, ar, ai, d_skip.reshape(1, width).astype(F32),
      w_glu.astype(BF16), b_glu.reshape(1, width).astype(F32))


ROUTE_LANES = {"id0": 0, "id1": 1, "rank0": 2, "rank1": 3, "w0": 4, "w1": 5}
GROUP_LANE0 = N_EXPERTS


def _merge_kernel(*refs, bsz_total):
    n_pat = len(DILATION_PATTERNS)
    x_ref, gate_ref = refs[0:2]
    attn_refs = refs[2:2 + 2 * n_pat]
    (ssm_ref, mod_ref, gpost_ref, gpre_ref, wba_ref, wbs_ref, wout_ref,
     wr_cat_ref, wr_hi_ref, br_ref) = refs[2 + 2 * n_pat:12 + 2 * n_pat]
    x1_ref, h2_ref, route_ref, count_ref = refs[12 + 2 * n_pat:16 + 2 * n_pat]
    o_tok, lse_tok, ssm_tok, carry = refs[16 + 2 * n_pat:]

    @pl.when(jnp.logical_and(pl.program_id(0) == 0, pl.program_id(1) == 0))
    def _():
        carry[...] = jnp.zeros_like(carry)

    nb, tt, d = x_ref.shape
    rows = nb * tt
    n_slab = ssm_ref.shape[0]
    b0 = pl.program_id(0) * nb

    for bb in range(nb):
        for s in range(n_slab):
            ssm_tok[bb * tt:(bb + 1) * tt, s * LANES:(s + 1) * LANES] = (
                ssm_ref[s, pl.ds(b0 + bb, tt, stride=bsz_total), :])

    slot = 0
    sources = []
    for p, (_, dil) in enumerate(DILATION_PATTERNS):
        o_ref, lse_ref = attn_refs[2 * p], attn_refs[2 * p + 1]
        if dil == 1:
            sources.append((o_ref, lse_ref, None))
            continue
        sub = tt // dil
        for bb in range(nb):
            for r in range(dil):
                o_blk = o_ref[bb, r].astype(F32)
                l_blk = lse_ref[bb, r]
                for s in range(n_slab):
                    dst = pl.ds(bb * tt + r, sub, stride=dil)
                    o_tok[slot, s, dst, :] = o_blk[:, s * LANES:(s + 1) * LANES]
                    lse_tok[slot, s, dst, :] = l_blk[:, s * LANES:(s + 1) * LANES]
        sources.append((o_ref, lse_ref, slot))
        slot += 1
    attn_slabs = []
    for s in range(n_slab):
        cols = slice(s * LANES, (s + 1) * LANES)
        o_ps, lse_ps = [], []
        for o_ref, lse_ref, src_slot in sources:
            if src_slot is None:
                o_ps.append(o_ref[:, 0, :, cols].astype(F32).reshape(rows, LANES))
                lse_ps.append(lse_ref[:, 0, :, cols].reshape(rows, LANES))
            else:
                o_ps.append(o_tok[src_slot, s])
                lse_ps.append(lse_tok[src_slot, s])
        m = functools.reduce(jnp.maximum, lse_ps)
        es = [jnp.exp(l - m) for l in lse_ps]
        num = functools.reduce(lambda a, b: a + b, [e * o for e, o in zip(es, o_ps)])
        den = functools.reduce(lambda a, b: a + b, es)
        attn_slabs.append(num / den)
    attn = _lane_concat(attn_slabs)

    g_attn = gate_ref[:, :, 0:d].astype(F32).reshape(rows, d)
    g_ssm = gate_ref[:, :, d:].astype(F32).reshape(rows, d)
    merged = (g_attn * _dot(attn.astype(BF16), wba_ref[...])
              + g_ssm * _dot(ssm_tok[...].astype(BF16), wbs_ref[...]))
    y = _dot(merged.astype(BF16), wout_ref[...])
    gate1 = mod_ref[:, 2, :][:, None, :]
    shift2 = mod_ref[:, 3, :][:, None, :]
    scale2 = mod_ref[:, 4, :][:, None, :]
    x1 = x_ref[...] + gate1 * _rms_norm(y, gpost_ref[...]).reshape(nb, tt, d)
    x1_ref[...] = x1
    h2 = _rms_norm(x1, gpre_ref[...]) * (1.0 + scale2) + shift2
    for bb in range(nb):
        _store_row_tiles(h2_ref.at[bb], _pack_rows(h2[bb]))

    a_hi, a_lo = _split_bf16(h2.reshape(rows, d))
    hi_pass = _dot(a_hi, wr_cat_ref[...])
    logits = hi_pass[:, :LANES] + _dot(a_lo, wr_hi_ref[...]) + hi_pass[:, LANES:] + br_ref[...]
    lane = lax.broadcasted_iota(jnp.int32, (rows, LANES), 1).astype(F32)
    big = float(LANES)
    is_group = jnp.logical_and(lane >= GROUP_LANE0, lane < GROUP_LANE0 + N_EXPERT_GROUPS)
    gl = jnp.where(is_group, logits, -jnp.inf)
    g_max = jnp.max(gl, axis=-1, keepdims=True)
    g_sel = jnp.min(jnp.where(gl == g_max, lane, big), axis=-1, keepdims=True) - GROUP_LANE0
    g_gate = 1.0 / jnp.sum(jnp.exp(gl - g_max), axis=-1, keepdims=True)
    lo = g_sel * EXPERTS_PER_GROUP
    in_group = jnp.logical_and(lane >= lo, lane < lo + EXPERTS_PER_GROUP)
    el = jnp.where(in_group, logits, -jnp.inf)
    t0 = jnp.max(el, axis=-1, keepdims=True)
    i0 = jnp.min(jnp.where(el == t0, lane, big), axis=-1, keepdims=True)
    el1 = jnp.where(lane == i0, -jnp.inf, el)
    t1 = jnp.max(el1, axis=-1, keepdims=True)
    i1 = jnp.min(jnp.where(el1 == t1, lane, big), axis=-1, keepdims=True)
    e = jnp.exp(t1 - t0)
    w0 = g_gate / (1.0 + e)
    w1 = g_gate * e / (1.0 + e)

    hit0 = lane == i0
    hit1 = lane == i1
    onehot = jnp.logical_or(hit0, hit1).astype(F32)
    row = lax.broadcasted_iota(jnp.int32, (rows, rows), 0)
    col = lax.broadcasted_iota(jnp.int32, (rows, rows), 1)
    strict_lower = (col < row).astype(BF16)
    before = _dot(strict_lower, onehot.astype(BF16)) + carry[...]
    rank0 = jnp.sum(jnp.where(hit0, before, 0.0), axis=-1, keepdims=True)
    rank1 = jnp.sum(jnp.where(hit1, before, 0.0), axis=-1, keepdims=True)
    carry[...] = carry[...] + jnp.sum(onehot, axis=0, keepdims=True)
    count_ref[...] = jnp.broadcast_to(carry[...], count_ref.shape)

    route = jnp.zeros((rows, LANES), F32)
    for name, val in (("id0", i0), ("id1", i1), ("rank0", rank0), ("rank1", rank1),
                      ("w0", w0), ("w1", w1)):
        route = jnp.where(lane == ROUTE_LANES[name], val, route)
    route_ref[...] = route.reshape(nb, tt, LANES)


def _merge_and_route(x, gates, attn_outs, ssm_slabs, mod3, g_post, g_pre, wba, wbs, wout,
                     w_rg, b_rg, w_re, b_re):
    bsz, seq, d = x.shape
    tt, nb = TIME_TILE, MERGE_BATCH
    aw = ATTN_WIDTH
    n_slab = ssm_slabs.shape[0]
    assert aw == n_slab * LANES
    wr = jnp.zeros((d, LANES), F32).at[:, :N_EXPERTS].set(w_re.astype(F32))
    wr = wr.at[:, GROUP_LANE0:GROUP_LANE0 + N_EXPERT_GROUPS].set(w_rg.astype(F32))
    br = jnp.zeros((1, LANES), F32).at[0, :N_EXPERTS].set(b_re.astype(F32))
    br = br.at[0, GROUP_LANE0:GROUP_LANE0 + N_EXPERT_GROUPS].set(b_rg.astype(F32))
    wr_hi, wr_lo = _split_bf16(wr)
    wr_cat = jnp.concatenate([wr_hi, wr_lo], axis=1)
    n_strided = sum(1 for _, dil in DILATION_PATTERNS if dil > 1)
    tok = lambda h, i: (h, i, 0)
    const = lambda h, i: (0, 0)
    attn_args, attn_specs = [], []
    for (o_p, lse_p), (_, dil) in zip(attn_outs, DILATION_PATTERNS):
        spec = pl.BlockSpec((nb, dil, tt // dil, aw), lambda h, i: (h, 0, i, 0))
        attn_args += [o_p, lse_p]
        attn_specs += [spec, spec]
    return pl.pallas_call(
        functools.partial(_merge_kernel, bsz_total=bsz),
        out_shape=(jax.ShapeDtypeStruct((bsz, seq, d), F32),
                   jax.ShapeDtypeStruct((bsz, seq * ROW_TILE, LANES), U32),
                   jax.ShapeDtypeStruct((bsz, seq, LANES), F32),
                   jax.ShapeDtypeStruct((SUBLANES, LANES), F32)),
        grid=(bsz // nb, seq // tt),
        in_specs=[pl.BlockSpec((nb, tt, d), tok),
                  pl.BlockSpec((nb, tt, gates.shape[-1]), tok)]
                 + attn_specs
                 + [pl.BlockSpec((n_slab, tt * bsz, LANES), lambda h, i: (0, i, 0)),
                    pl.BlockSpec((nb, 6, d), lambda h, i: (h, 0, 0)),
                    pl.BlockSpec((1, d), const), pl.BlockSpec((1, d), const),
                    pl.BlockSpec(wba.shape, const), pl.BlockSpec(wbs.shape, const),
                    pl.BlockSpec(wout.shape, const),
                    pl.BlockSpec((d, 2 * LANES), const), pl.BlockSpec((d, LANES), const),
                    pl.BlockSpec((1, LANES), const)],
        out_specs=(pl.BlockSpec((nb, tt, d), tok), pl.BlockSpec((nb, tt * ROW_TILE, LANES), tok),
                   pl.BlockSpec((nb, tt, LANES), tok),
                   pl.BlockSpec((SUBLANES, LANES), const)),
        scratch_shapes=[pltpu.VMEM((n_strided, n_slab, nb * tt, LANES), F32),
                        pltpu.VMEM((n_strided, n_slab, nb * tt, LANES), F32),
                        pltpu.VMEM((nb * tt, n_slab * LANES), F32),
                        pltpu.VMEM((1, LANES), F32)],
        compiler_params=_cparams("arbitrary", "arbitrary"),
        name="merge",
    )(x, gates, *attn_args, ssm_slabs, mod3, g_post, g_pre, wba, wbs, wout, wr_cat, wr_hi, br)


ISSUE_UNROLL = 8


def _dispatch_kernel(dest_ref, seg_ref, h_ref, xs_ref, zero_buf, sem, zsem):
    rows = h_ref.shape[0] // ROW_TILE
    blk = zero_buf.shape[0]
    n_blocks = xs_ref.shape[0] // blk
    base = pl.program_id(0) * rows

    @pl.when(pl.program_id(0) == 0)
    def _():
        zero_buf[...] = jnp.zeros_like(zero_buf)

        def zero_copy(row0):
            return pltpu.make_async_copy(zero_buf, xs_ref.at[pl.ds(pl.multiple_of(row0, blk), blk)],
                                         zsem)

        def fill_tail(e, carry):
            @pl.when(seg_ref[N_EXPERTS + e] > 0)
            def _():
                zero_copy(seg_ref[e] * ROW_TILE - blk).start()
            return carry

        def fill_unused(j, carry):
            zero_copy(j * blk).start()
            return carry

        def wait_tail(e, carry):
            @pl.when(seg_ref[N_EXPERTS + e] > 0)
            def _():
                zero_copy(0).wait()
            return carry

        def wait_unused(j, carry):
            zero_copy(0).wait()
            return carry

        n_used = seg_ref[2 * N_EXPERTS]
        lax.fori_loop(0, N_EXPERTS, fill_tail, 0)
        lax.fori_loop(n_used, n_blocks, fill_unused, 0)
        lax.fori_loop(0, N_EXPERTS, wait_tail, 0)
        lax.fori_loop(n_used, n_blocks, wait_unused, 0)

    group = ISSUE_UNROLL * ROW_TILE

    def issue(g, carry):
        g0 = pl.multiple_of(g * group, group)
        for rr in range(ISSUE_UNROLL):
            src = h_ref.at[pl.ds(g0 + rr * ROW_TILE, ROW_TILE)]
            for k in range(TOP_K):
                d = dest_ref[TOP_K * (base + g * ISSUE_UNROLL + rr) + k]
                dst = xs_ref.at[pl.ds(pl.multiple_of(d * ROW_TILE, ROW_TILE), ROW_TILE)]
                pltpu.make_async_copy(src, dst, sem).start(priority=k)
        return carry

    lax.fori_loop(0, rows // ISSUE_UNROLL, issue, 0)
    for _ in range(TOP_K):
        pltpu.make_async_copy(h_ref, xs_ref.at[pl.ds(0, rows * ROW_TILE)], sem).wait()


def _dispatch(h2_tiles, dest, seg_info, cap):
    n_tok = h2_tiles.shape[0] // ROW_TILE
    rows = DISPATCH_ROWS
    return pl.pallas_call(
        _dispatch_kernel,
        out_shape=jax.ShapeDtypeStruct((cap * ROW_TILE, LANES), h2_tiles.dtype),
        grid_spec=pltpu.PrefetchScalarGridSpec(
            num_scalar_prefetch=2,
            grid=(n_tok // rows,),
            in_specs=[pl.BlockSpec((rows * ROW_TILE, LANES), lambda i, dest, seg: (i, 0))],
            out_specs=pl.BlockSpec(memory_space=pl.ANY),
            scratch_shapes=[pltpu.VMEM((MOE_ROWS * ROW_TILE, LANES), h2_tiles.dtype),
                            pltpu.SemaphoreType.DMA, pltpu.SemaphoreType.DMA]),
        compiler_params=_cparams("arbitrary"),
        name="dispatch",
    )(dest, seg_info, h2_tiles)


def _expert_kernel(blk_expert_ref, n_used_ref, x_ref, w1_ref, w3_ref, w2_ref, y_ref,
                   w1b, w3b, w2b):
    i = pl.program_id(0)
    used = i < n_used_ref[0]
    rows = x_ref.shape[0] // ROW_TILE
    new_expert = jnp.logical_or(
        i == 0, blk_expert_ref[i] != blk_expert_ref[jnp.maximum(i - 1, 0)])

    @pl.when(jnp.logical_and(used, new_expert))
    def _():
        w1b[...] = w1_ref[0].astype(BF16)
        w3b[...] = w3_ref[0].astype(BF16)
        w2b[...] = w2_ref[0].astype(BF16)

    @pl.when(used)
    def _():
        xb = _unpack_rows(_load_row_tiles(x_ref, rows)).astype(BF16)
        h1 = _dot(xb, w1b[...])
        h3 = _dot(xb, w3b[...])
        act = (h1 * _sigmoid(h1)) * h3
        y = _dot(act.astype(BF16), w2b[...])
        _store_row_tiles(y_ref, _pack_rows(y))

    @pl.when(jnp.logical_not(used))
    def _():
        y_ref[...] = jnp.zeros_like(y_ref)


def _experts(xs, blk_expert, n_used, w1, w3, w2):
    d, de = w1.shape[-2:]
    assert d == 2 * ROW_TILE * LANES
    rows = MOE_ROWS
    tile_rows = rows * ROW_TILE
    xblk = lambda i, be, nu: (jnp.minimum(i, nu[0] - 1), 0)
    wblk = lambda i, be, nu: (be[i], 0, 0)
    return pl.pallas_call(
        _expert_kernel,
        out_shape=jax.ShapeDtypeStruct(xs.shape, U32),
        grid_spec=pltpu.PrefetchScalarGridSpec(
            num_scalar_prefetch=2,
            grid=(xs.shape[0] // tile_rows,),
            in_specs=[pl.BlockSpec((tile_rows, LANES), xblk),
                      pl.BlockSpec((1, d, de), wblk), pl.BlockSpec((1, d, de), wblk),
                      pl.BlockSpec((1, de, d), wblk)],
            out_specs=pl.BlockSpec((tile_rows, LANES), lambda i, be, nu: (i, 0)),
            scratch_shapes=[pltpu.VMEM((d, de), BF16), pltpu.VMEM((d, de), BF16),
                            pltpu.VMEM((de, d), BF16)]),
        compiler_params=_cparams("arbitrary"),
        name="experts",
    )(blk_expert, n_used, xs, w1, w3, w2)


def _combine_kernel(dest_ref, ys_ref, x1_ref, route_ref, mod_ref, g_ref, o_ref, ybuf, sem):
    rows = x1_ref.shape[0]
    step = pl.program_id(0)
    n_step = pl.num_programs(0)

    group = ISSUE_UNROLL * ROW_TILE

    def gather(tile, slot):
        base = tile * rows

        def issue(g, carry):
            g0 = pl.multiple_of(g * group, group)
            for rr in range(ISSUE_UNROLL):
                for k in range(TOP_K):
                    d = dest_ref[TOP_K * (base + g * ISSUE_UNROLL + rr) + k]
                    src = ys_ref.at[pl.ds(pl.multiple_of(d * ROW_TILE, ROW_TILE), ROW_TILE)]
                    dst = ybuf.at[slot, k, pl.ds(g0 + rr * ROW_TILE, ROW_TILE)]
                    pltpu.make_async_copy(src, dst, sem.at[slot]).start(priority=k)
            return carry

        lax.fori_loop(0, rows // ISSUE_UNROLL, issue, 0)

    slot = lax.rem(step, 2)

    @pl.when(step == 0)
    def _():
        gather(step, 0)

    @pl.when(step + 1 < n_step)
    def _():
        gather(step + 1, 1 - slot)

    for k in range(TOP_K):
        pltpu.make_async_copy(ys_ref.at[pl.ds(0, rows * ROW_TILE)], ybuf.at[slot, k],
                              sem.at[slot]).wait()
    route = route_ref[...]
    w0 = route[:, ROUTE_LANES["w0"]:ROUTE_LANES["w0"] + 1]
    w1 = route[:, ROUTE_LANES["w1"]:ROUTE_LANES["w1"] + 1]

    def rows_of(k):
        return _unpack_rows(_load_row_tiles(ybuf.at[slot, k], rows))

    y = rows_of(0) * w0 + rows_of(1) * w1
    o_ref[...] = x1_ref[...] + mod_ref[0, 5:6, :] * _rms_norm(y, g_ref[...])


def _combine(ys, dest, x1_flat, route_flat, mod3, g_post, seq):
    n_tok, d = x1_flat.shape
    rows = COMBINE_ROWS
    assert seq % rows == 0
    return pl.pallas_call(
        _combine_kernel,
        out_shape=jax.ShapeDtypeStruct((n_tok, d), F32),
        grid_spec=pltpu.PrefetchScalarGridSpec(
            num_scalar_prefetch=1,
            grid=(n_tok // rows,),
            in_specs=[pl.BlockSpec(memory_space=pl.ANY),
                      pl.BlockSpec((rows, d), lambda i, dest: (i, 0)),
                      pl.BlockSpec((rows, LANES), lambda i, dest: (i, 0)),
                      pl.BlockSpec((1, 6, d), lambda i, dest: (i * rows // seq, 0, 0)),
                      pl.BlockSpec((1, d), lambda i, dest: (0, 0))],
            out_specs=pl.BlockSpec((rows, d), lambda i, dest: (i, 0)),
            scratch_shapes=[pltpu.VMEM((2, TOP_K, rows * ROW_TILE, LANES), U32),
                            pltpu.SemaphoreType.DMA((2,))]),
        compiler_params=_cparams("arbitrary"),
        name="combine",
    )(dest, ys, x1_flat, route_flat, mod3, g_post)


def _moe_layout(route_flat, counts):
    rows = MOE_ROWS
    n_tok = route_flat.shape[0]
    ids = route_flat[:, 0:TOP_K].astype(jnp.int32)
    rank = route_flat[:, TOP_K:2 * TOP_K].astype(jnp.int32)
    counts = counts.astype(jnp.int32)
    padded = (counts + rows - 1) // rows * rows
    pad_ends = jnp.cumsum(padded)
    pad_starts = pad_ends - padded
    expert = jnp.arange(N_EXPERTS, dtype=jnp.int32)
    start_of = jnp.sum(jnp.where(ids[..., None] == expert, pad_starts, 0), axis=-1)
    dest = (start_of + rank).reshape(n_tok * TOP_K)
    n_blocks = (n_tok * TOP_K + N_EXPERTS * (rows - 1) + rows - 1) // rows
    blk_row0 = jnp.arange(n_blocks, dtype=jnp.int32) * rows
    blk_expert = jnp.minimum(jnp.sum(pad_ends[None, :] <= blk_row0[:, None], axis=-1),
                             N_EXPERTS - 1).astype(jnp.int32)
    n_used = (pad_ends[-1] // rows).reshape(1).astype(jnp.int32)
    seg_info = jnp.concatenate([pad_ends, padded, n_used]).astype(jnp.int32)
    return dest, blk_expert, n_used, seg_info, n_blocks * rows


def kernel(x, c, w_mod, b_mod, g_pre_mix, g_post_mix, g_pre_ffn, g_post_ffn, w_in, rel_bias, a_re, a_im, log_dt, ssm_b_re, ssm_b_im, ssm_c_re, ssm_c_im, d_skip, w_glu, b_glu, w_branch_attn, w_branch_ssm, w_out, w_router_group, b_router_group, w_router_expert, b_router_expert, w1, w3, w2):
    bsz, seq, d = x.shape
    depth = w_mod.shape[0]
    ssm_width = w_glu.shape[-1]
    n_pat = len(DILATION_PATTERNS)
    for l in range(depth):
        mod3 = _modulation(c, w_mod[l], b_mod[l]).reshape(bsz, 6, d)
        w_in_l = w_in[l].astype(BF16)
        g_pre = g_pre_mix[l].reshape(1, d)
        qkv = _qkv_projection(x, mod3, g_pre, w_in_l[:, :3 * ATTN_WIDTH])
        u_slabs, gates = _ugate_projection(x, mod3, g_pre, w_in_l[:, 3 * ATTN_WIDTH:], ssm_width)
        attn_outs = [_attention_pattern(*qkv[3 * p:3 * p + 3], rel_bias, DILATION_PATTERNS[p][1])
                     for p in range(n_pat)]
        ssm_slabs = _ssm_branch(u_slabs, bsz, a_re[l], a_im[l], log_dt[l], ssm_b_re[l],
                                ssm_b_im[l], ssm_c_re[l], ssm_c_im[l], d_skip[l], w_glu[l], b_glu[l])
        x1, h2, route, counts = _merge_and_route(
            x, gates, attn_outs, ssm_slabs, mod3, g_post_mix[l].reshape(1, d),
            g_pre_ffn[l].reshape(1, d), w_branch_attn[l].astype(BF16),
            w_branch_ssm[l].astype(BF16), w_out[l].astype(BF16),
            w_router_group[l], b_router_group[l], w_router_expert[l], b_router_expert[l])
        route_flat = route.reshape(bsz * seq, LANES)
        dest, blk_expert, n_used, seg_info, cap = _moe_layout(route_flat, counts[0, :N_EXPERTS])
        xs = _dispatch(h2.reshape(bsz * seq * ROW_TILE, LANES), dest, seg_info, cap)
        ys = _experts(xs, blk_expert, n_used, w1[l], w3[l], w2[l])
        x = _combine(ys, dest, x1.reshape(bsz * seq, d), route_flat, mod3,
                     g_post_ffn[l].reshape(1, d), seq).reshape(bsz, seq, d)
    return x
```

```python
import functools
import math

import numpy as np
import jax
import jax.numpy as jnp
from jax import lax
from jax.experimental import pallas as pl
from jax.experimental.pallas import tpu as pltpu

F32 = jnp.float32
BF16 = jnp.bfloat16

N_HEADS = 8
HEAD_DIM = 64
ATTN_WIDTH = N_HEADS * HEAD_DIM
DILATION_PATTERNS = ((128, 1), (512, 4), (2048, 16))
NUM_BUCKETS = 32
MAX_DISTANCE = 2048
N_EXPERT_GROUPS = 4
EXPERTS_PER_GROUP = 8
N_EXPERTS = N_EXPERT_GROUPS * EXPERTS_PER_GROUP
TOP_K = 2
RMS_EPS = 1e-6
NEG_INF = -1e30

LANES = 128
SUBLANES = 8
VMEM_LIMIT_BYTES = 56 * 1024 * 1024

ATTN_BLK = 128
ATTN_STEP_ROWS = 512
QKV_ROWS = 512
QKV_MID_DIL = 4
TIME_TILE = 128
MERGE_BATCH = 4
MERGE_CHUNK_BATCH = 4
SSM_STEPS = 64
MOE_ROWS = 256
DISPATCH_ROWS = 2048
COMBINE_ROWS = 256
ROW_TILE = 4
U32 = jnp.uint32
HI_HALF = 0xFFFF0000


def _pack_rows(x):
    w = x.shape[1] // 2
    lo = lax.bitcast_convert_type(x[:, :w].astype(BF16).astype(F32), U32) >> 16
    hi = lax.bitcast_convert_type(x[:, w:].astype(BF16).astype(F32), U32) & U32(HI_HALF)
    return hi | lo


def _unpack_rows(p):
    lo = lax.bitcast_convert_type(p << 16, F32)
    hi = lax.bitcast_convert_type(p & U32(HI_HALF), F32)
    return jnp.concatenate([lo, hi], axis=1)


def _store_row_tiles(ref, packed):
    rows = packed.shape[0]
    for s in range(ROW_TILE):
        ref[pl.ds(s, rows, stride=ROW_TILE), :] = packed[:, s * LANES:(s + 1) * LANES]


def _load_row_tiles(ref, rows):
    return _lane_concat([ref[pl.ds(s, rows, stride=ROW_TILE), :] for s in range(ROW_TILE)])


def _cparams(*sem):
    return pltpu.CompilerParams(dimension_semantics=sem, vmem_limit_bytes=VMEM_LIMIT_BYTES)


def _sigmoid(x):
    return 1.0 / (1.0 + jnp.exp(-x))


def _dot(a, b):
    return jnp.dot(a, b, preferred_element_type=F32)


def _split_bf16(a):
    hi = a.astype(BF16)
    lo = (a - hi.astype(F32)).astype(BF16)
    return hi, lo


def _dot_split(a, w_hi, w_lo):
    a_hi, a_lo = _split_bf16(a)
    return _dot(a_hi, w_hi) + _dot(a_lo, w_hi) + _dot(a_hi, w_lo)


def _rms_norm(x, gain):
    ms = jnp.mean(x * x, axis=-1, keepdims=True)
    return x * lax.rsqrt(ms + RMS_EPS) * gain


def _lane_concat(ref_slabs):
    return jnp.concatenate(ref_slabs, axis=-1)


def _mod_kernel(c_ref, w_ref, b_ref, o_ref):
    c = c_ref[...]
    a = c * _sigmoid(c)
    w_hi, w_lo = _split_bf16(w_ref[...])
    o_ref[...] = _dot_split(a, w_hi, w_lo) + b_ref[...]


def _modulation(c, w_mod, b_mod):
    bsz, d = c.shape
    n = w_mod.shape[1]
    tn = 1024
    return pl.pallas_call(
        _mod_kernel,
        out_shape=jax.ShapeDtypeStruct((bsz, n), F32),
        grid=(n // tn,),
        in_specs=[pl.BlockSpec((bsz, d), lambda j: (0, 0)),
                  pl.BlockSpec((d, tn), lambda j: (0, j)),
                  pl.BlockSpec((1, tn), lambda j: (0, j))],
        out_specs=pl.BlockSpec((bsz, tn), lambda j: (0, j)),
        compiler_params=_cparams("arbitrary"),
        name="mod",
    )(c, w_mod, b_mod.reshape(1, n))


def _qkv_kernel(x_ref, mod_ref, g_ref, w_ref, *rest):
    n_pat = len(DILATION_PATTERNS)
    out_refs, slab, mid = rest[:3 * n_pat], rest[3 * n_pat], rest[3 * n_pat + 1]
    h = _rms_norm(x_ref[0], g_ref[...]) * (1.0 + mod_ref[0, 1:2, :]) + mod_ref[0, 0:1, :]
    res = _dot(h.astype(BF16), w_ref[...])
    rows = res.shape[0]
    n_slab = res.shape[1] // LANES
    per_tensor = ATTN_WIDTH // LANES
    for s in range(n_slab):
        piece = res[:, s * LANES:(s + 1) * LANES]
        if s < per_tensor:
            piece = piece * (HEAD_DIM ** -0.5)
        slab[s] = piece
    mid_dil = QKV_MID_DIL
    for p, (_, dil) in enumerate(DILATION_PATTERNS):
        sub = rows // dil
        for t in range(3):
            out = out_refs[3 * p + t]
            for r in range(dil):
                pieces = []
                for s in range(per_tensor):
                    ts = t * per_tensor + s
                    if dil == 1:
                        piece = slab[ts]
                    elif dil == mid_dil:
                        piece = slab[ts, pl.ds(r, sub, stride=dil), :]
                        mid[ts, r] = piece
                    else:
                        ratio = dil // mid_dil
                        piece = mid[ts, r % mid_dil, pl.ds(r // mid_dil, sub, stride=ratio), :]
                    pieces.append(piece)
                out[0, r] = _lane_concat(pieces).astype(out.dtype)


def _qkv_projection(x, mod3, g_pre, w_qkv):
    bsz, seq, d = x.shape
    tm = QKV_ROWS
    out_shape, out_specs = [], []
    for _, dil in DILATION_PATTERNS:
        assert tm % (dil * 2 * SUBLANES) == 0
        for _ in range(3):
            out_shape.append(jax.ShapeDtypeStruct((bsz, dil, seq // dil, ATTN_WIDTH), BF16))
            out_specs.append(pl.BlockSpec((1, dil, tm // dil, ATTN_WIDTH),
                                          lambda b, i: (b, 0, i, 0)))
    return pl.pallas_call(
        _qkv_kernel,
        out_shape=out_shape,
        grid=(bsz, seq // tm),
        in_specs=[pl.BlockSpec((1, tm, d), lambda b, i: (b, i, 0)),
                  pl.BlockSpec((1, 6, d), lambda b, i: (b, 0, 0)),
                  pl.BlockSpec((1, d), lambda b, i: (0, 0)),
                  pl.BlockSpec(w_qkv.shape, lambda b, i: (0, 0))],
        out_specs=out_specs,
        scratch_shapes=[pltpu.VMEM((w_qkv.shape[1] // LANES, tm, LANES), F32),
                        pltpu.VMEM((w_qkv.shape[1] // LANES, QKV_MID_DIL, tm // QKV_MID_DIL, LANES),
                                   F32)],
        compiler_params=_cparams("arbitrary", "arbitrary"),
        name="qkv",
    )(x, mod3, g_pre, w_qkv)


def _ugate_kernel(x_ref, mod_ref, g_ref, w_ref, u_ref, gate_ref):
    bsz, tt, d = x_ref.shape
    shift = mod_ref[:, 0, :][:, None, :]
    scale = mod_ref[:, 1, :][:, None, :]
    h = _rms_norm(x_ref[...], g_ref[...]) * (1.0 + scale) + shift
    hb = h.reshape(bsz * tt, d).astype(BF16)
    n_slab = u_ref.shape[0]
    sw = n_slab * LANES
    u = _dot(hb, w_ref[:, 0:sw])
    for b in range(bsz):
        for s in range(n_slab):
            u_ref[s, pl.ds(b, tt, stride=bsz), :] = u[b * tt:(b + 1) * tt, s * LANES:(s + 1) * LANES]
    gw = gate_ref.shape[-1]
    chunk = 512
    for c0 in range(0, gw, chunk):
        g = _sigmoid(_dot(hb, w_ref[:, sw + c0:sw + c0 + chunk]))
        gate_ref[:, :, c0:c0 + chunk] = g.reshape(bsz, tt, chunk).astype(BF16)


def _ugate_projection(x, mod3, g_pre, w_ug, ssm_width):
    bsz, seq, d = x.shape
    tt = TIME_TILE
    gw = w_ug.shape[1] - ssm_width
    n_slab = ssm_width // LANES
    return pl.pallas_call(
        _ugate_kernel,
        out_shape=(jax.ShapeDtypeStruct((n_slab, seq * bsz, LANES), F32),
                   jax.ShapeDtypeStruct((bsz, seq, gw), BF16)),
        grid=(seq // tt,),
        in_specs=[pl.BlockSpec((bsz, tt, d), lambda i: (0, i, 0)),
                  pl.BlockSpec((bsz, 6, d), lambda i: (0, 0, 0)),
                  pl.BlockSpec((1, d), lambda i: (0, 0)),
                  pl.BlockSpec(w_ug.shape, lambda i: (0, 0))],
        out_specs=(pl.BlockSpec((n_slab, tt * bsz, LANES), lambda i: (0, i, 0)),
                   pl.BlockSpec((bsz, tt, gw), lambda i: (0, i, 0))),
        compiler_params=_cparams("arbitrary"),
        name="ugate",
    )(x, mod3, g_pre, w_ug)


def _t5_bucket_np(dist):
    exact = NUM_BUCKETS // 2
    d_f = np.maximum(dist, exact).astype(np.float32)
    large = exact + (np.log(d_f / np.float32(exact)) / np.float32(math.log(MAX_DISTANCE / exact))
                     * np.float32(NUM_BUCKETS - exact)).astype(np.int32)
    return np.where(dist < exact, dist, np.minimum(large, NUM_BUCKETS - 1))


def _bucket_map_t(dil):
    blk = ATTN_BLK
    ki = np.arange(2 * blk)[:, None]
    qi = np.arange(blk)[None, :]
    return _t5_bucket_np(np.maximum(blk + qi - ki, 0) * dil).astype(np.int32)


def _attn_kernel(relb_ref, q_ref, kc_ref, kp_ref, vc_ref, vp_ref, bucket_ref,
                 o_ref, lse_ref, kbuf, vbuf, bias_t, *, n_sub):
    blk = ATTN_BLK
    first_call = jnp.logical_and(pl.program_id(0) == 0,
                                 jnp.logical_and(pl.program_id(1) == 0, pl.program_id(2) == 0))

    @pl.when(first_call)
    def _():
        bucket = bucket_ref[...]
        ki = lax.broadcasted_iota(jnp.int32, bucket.shape, 0)
        qi = lax.broadcasted_iota(jnp.int32, bucket.shape, 1)
        dist = blk + qi - ki
        band = jnp.logical_and(dist >= 0, dist <= blk)
        band_first = jnp.logical_and(band, ki >= blk)

        def per_head(h, carry):
            acc = jnp.zeros(bucket.shape, F32)
            for b in range(NUM_BUCKETS):
                acc = jnp.where(bucket == b, relb_ref[b, h], acc)
            bias_t[0, h] = jnp.where(band_first, acc, NEG_INF)
            bias_t[1, h] = jnp.where(band, acc, NEG_INF)
            return carry

        lax.fori_loop(0, N_HEADS, per_head, 0)

    first_variant = jnp.where(pl.program_id(2) == 0, 0, 1)
    n_res = q_ref.shape[1]
    for g in range(n_res):
        kbuf[g, 0:blk, :] = kp_ref[0, g]
        kbuf[g, blk:, :] = kc_ref[0, g]
        vbuf[g, 0:blk, :] = vp_ref[0, g]
        vbuf[g, blk:, :] = vc_ref[0, g]

    lane = lax.broadcasted_iota(jnp.int32, (1, LANES), 1)
    lo_half = lane < HEAD_DIM
    bd_row = lax.broadcasted_iota(jnp.int32, (4 * blk, LANES), 0)
    bd_col = lax.broadcasted_iota(jnp.int32, (4 * blk, LANES), 1)
    ones_bd = ((bd_row < 2 * blk) == (bd_col < HEAD_DIM)).astype(F32).astype(BF16)
    contract_last = (((1,), (1,)), ((), ()))
    contract_first = (((0,), (0,)), ((), ()))

    def sub_block(g, i):
        r0 = i * blk
        q = q_ref[0, g, r0:r0 + blk, :]
        kk = kbuf[g, r0:r0 + 2 * blk, :]
        vv = vbuf[g, r0:r0 + 2 * blk, :]
        variant = first_variant if i == 0 else 1
        for j in range(N_HEADS // 2):
            cols = slice(j * LANES, (j + 1) * LANES)
            qj, kj, vj = q[:, cols], kk[:, cols], vv[:, cols]
            probs_t, maxes = [], []
            for hh in range(2):
                sel = lo_half if hh == 0 else jnp.logical_not(lo_half)
                qm = jnp.where(sel, qj, jnp.zeros_like(qj))
                s_t = lax.dot_general(kj, qm, contract_last, preferred_element_type=F32)
                s_t = s_t + bias_t[variant, 2 * j + hh]
                m = jnp.max(s_t, axis=0, keepdims=True)
                probs_t.append(jnp.exp(s_t - m).astype(BF16))
                maxes.append(m)
            p2_t = jnp.concatenate(probs_t, axis=0)
            v_bd = jnp.concatenate([jnp.where(lo_half, vj, jnp.zeros_like(vj)),
                                    jnp.where(lo_half, jnp.zeros_like(vj), vj)], axis=0)
            rhs = jnp.concatenate([v_bd, ones_bd], axis=1)
            ol = lax.dot_general(p2_t, rhs, contract_first, preferred_element_type=F32)
            o2, l2 = ol[:, :LANES], ol[:, LANES:]
            m_t = jnp.concatenate([jnp.broadcast_to(maxes[0], (HEAD_DIM, blk)),
                                   jnp.broadcast_to(maxes[1], (HEAD_DIM, blk))], axis=0)
            o_ref[0, g, r0:r0 + blk, cols] = (o2 / l2).astype(o_ref.dtype)
            lse_ref[0, g, r0:r0 + blk, cols] = m_t.T + jnp.log(l2)

    for g in range(n_res):
        for i in range(n_sub):
            sub_block(g, i)


def _attention_pattern(q, k, v, rel_bias, dil):
    bsz, _, sub_len, aw = q.shape
    blk = ATTN_BLK
    assert sub_len % blk == 0
    tq = min(ATTN_STEP_ROWS, sub_len)
    n_sub = tq // blk
    ratio = tq // blk
    n_res = min(dil, ATTN_STEP_ROWS // tq)
    cur = lambda b, r, n: (b, r, n, 0)
    prev = lambda b, r, n: (b, r, jnp.maximum(n * ratio - 1, 0), 0)
    blk_cur = pl.BlockSpec((1, n_res, tq, aw), cur)
    blk_prev = pl.BlockSpec((1, n_res, blk, aw), prev)
    bucket = jnp.asarray(_bucket_map_t(dil))
    o_dtype = BF16 if TIME_TILE // dil >= 2 * SUBLANES else F32
    return pl.pallas_call(
        functools.partial(_attn_kernel, n_sub=n_sub),
        out_shape=(jax.ShapeDtypeStruct(q.shape, o_dtype), jax.ShapeDtypeStruct(q.shape, F32)),
        grid=(bsz, dil // n_res, sub_len // tq),
        in_specs=[pl.BlockSpec(memory_space=pltpu.SMEM),
                  blk_cur, blk_cur, blk_prev, blk_cur, blk_prev,
                  pl.BlockSpec(bucket.shape, lambda b, r, n: (0, 0))],
        out_specs=(blk_cur, blk_cur),
        scratch_shapes=[pltpu.VMEM((n_res, tq + blk, aw), BF16),
                        pltpu.VMEM((n_res, tq + blk, aw), BF16),
                        pltpu.VMEM((2, N_HEADS, 2 * blk, blk), F32)],
        compiler_params=_cparams("arbitrary", "arbitrary", "arbitrary"),
        name=f"attn_dil{dil}",
    )(rel_bias.astype(F32), q, k, k, v, v, bucket)


def _ssm_kernel(u_ref, bmat_ref, cmat_ref, ar_ref, ai_ref, dskip_ref, wglu_ref, bglu_ref,
                o_ref, hbuf, hstate, *, n_steps):
    @pl.when(pl.program_id(0) == 0)
    def _():
        hstate[...] = jnp.zeros_like(hstate)

    n_slab = u_ref.shape[0]
    n_state = hbuf.shape[1] // 2
    per = n_state // n_slab
    us = [u_ref[s] for s in range(n_slab)]
    for s in range(n_slab):
        bu = _dot(us[s].astype(BF16), bmat_ref[s])
        hbuf[:, s * per:(s + 1) * per] = bu[:, :per]
        hbuf[:, n_state + s * per:n_state + (s + 1) * per] = bu[:, per:]
    for s in range(n_slab):
        re_cols = slice(s * per, (s + 1) * per)
        im_cols = slice(n_state + s * per, n_state + (s + 1) * per)
        ar = ar_ref[:, re_cols]
        ai = ai_ref[:, re_cols]
        hr = hstate[:, re_cols]
        hi = hstate[:, im_cols]
        for t in range(n_steps):
            trow = slice(t * SUBLANES, (t + 1) * SUBLANES)
            nr = ar * hr - ai * hi + hbuf[trow, re_cols]
            ni = ar * hi + ai * hr + hbuf[trow, im_cols]
            hbuf[trow, re_cols] = nr
            hbuf[trow, im_cols] = ni
            hr, hi = nr, ni
        hstate[:, re_cols] = hr
        hstate[:, im_cols] = hi

    ys = []
    for s in range(n_slab):
        h_s = _lane_concat([hbuf[:, s * per:(s + 1) * per],
                            hbuf[:, n_state + s * per:n_state + (s + 1) * per]])
        ys.append(_dot(h_s.astype(BF16), cmat_ref[s])
                  + dskip_ref[:, s * LANES:(s + 1) * LANES] * us[s])
    y = _lane_concat(ys)
    y = 0.5 * y * (1.0 + jnp.tanh(math.sqrt(2.0 / math.pi) * (y + 0.044715 * (y * y * y))))
    z = _dot(y.astype(BF16), wglu_ref[...]) + bglu_ref[...]
    out = y * _sigmoid(z)
    for s in range(n_slab):
        o_ref[s] = out[:, s * LANES:(s + 1) * LANES]


def _ssm_params(a_re, a_im, log_dt, b_re, b_im, c_re, c_im, bsz):
    g, p = a_re.shape
    hg = b_re.shape[-1]
    dt = jnp.exp(log_dt.astype(F32))[:, None]
    a_re, a_im = a_re.astype(F32), a_im.astype(F32)
    mag = jnp.exp(a_re * dt)
    abar_re = mag * jnp.cos(a_im * dt)
    abar_im = mag * jnp.sin(a_im * dt)
    den = a_re * a_re + a_im * a_im
    q_re = ((abar_re - 1.0) * a_re + abar_im * a_im) / den
    q_im = (abar_im * a_re - (abar_re - 1.0) * a_im) / den
    b_re, b_im = b_re.astype(F32), b_im.astype(F32)
    bb_re = q_re[..., None] * b_re - q_im[..., None] * b_im
    bb_im = q_re[..., None] * b_im + q_im[..., None] * b_re
    gs = LANES // hg
    n_slab = g // gs
    eye = jnp.eye(gs, dtype=F32)

    def in_mat(t):
        t = t.reshape(n_slab, gs, p, hg)
        return jnp.einsum('sgph,gk->sghkp', t, eye).reshape(n_slab, gs * hg, gs * p)

    def out_mat(t):
        t = t.reshape(n_slab, gs, hg, p)
        return jnp.einsum('sghp,gk->sgpkh', t, eye).reshape(n_slab, gs * p, gs * hg)

    bmat = jnp.concatenate([in_mat(bb_re), in_mat(bb_im)], axis=2).astype(BF16)
    cmat = jnp.concatenate([out_mat(c_re.astype(F32)), -out_mat(c_im.astype(F32))],
                           axis=1).astype(BF16)
    ar = jnp.broadcast_to(abar_re.reshape(1, g * p), (bsz, g * p))
    ai = jnp.broadcast_to(abar_im.reshape(1, g * p), (bsz, g * p))
    return bmat, cmat, ar, ai


def _ssm_branch(u_slabs, bsz, a_re, a_im, log_dt, b_re, b_im, c_re, c_im, d_skip, w_glu, b_glu):
    n_slab, n_rows, _ = u_slabs.shape
    width = n_slab * LANES
    assert bsz == SUBLANES
    bmat, cmat, ar, ai = _ssm_params(a_re, a_im, log_dt, b_re, b_im, c_re, c_im, bsz)
    n_state2 = n_slab * bmat.shape[2]
    rows = SSM_STEPS * bsz
    const = lambda c: (0, 0)
    const3 = lambda c: (0, 0, 0)
    slab_spec = pl.BlockSpec((n_slab, rows, LANES), lambda c: (0, c, 0))
    return pl.pallas_call(
        functools.partial(_ssm_kernel, n_steps=SSM_STEPS),
        out_shape=jax.ShapeDtypeStruct(u_slabs.shape, F32),
        grid=(n_rows // rows,),
        in_specs=[slab_spec,
                  pl.BlockSpec(bmat.shape, const3), pl.BlockSpec(cmat.shape, const3),
                  pl.BlockSpec(ar.shape, const), pl.BlockSpec(ai.shape, const),
                  pl.BlockSpec((1, width), const), pl.BlockSpec((width, width), const),
                  pl.BlockSpec((1, width), const)],
        out_specs=slab_spec,
        scratch_shapes=[pltpu.VMEM((rows, n_state2), F32), pltpu.VMEM((bsz, n_state2), F32)],
        compiler_params=_cparams("arbitrary"),
        name="ssm",
    )(u_slabs, bmat, cmat, ar, ai, d_skip.reshape(1, width).astype(F32),
      w_glu.astype(BF16), b_glu.reshape(1, width).astype(F32))


ROUTE_LANES = {"id0": 0, "id1": 1, "rank0": 2, "rank1": 3, "w0": 4, "w1": 5}
GROUP_LANE0 = N_EXPERTS


def _merge_kernel(*refs, bsz_total):
    n_pat = len(DILATION_PATTERNS)
    x_ref, gate_ref = refs[0:2]
    attn_refs = refs[2:2 + 2 * n_pat]
    (ssm_ref, mod_ref, gpost_ref, gpre_ref, wba_ref, wbs_ref, wout_ref,
     wr_cat_ref, wr_hi_ref, br_ref) = refs[2 + 2 * n_pat:12 + 2 * n_pat]
    x1_ref, h2_ref, route_ref, count_ref = refs[12 + 2 * n_pat:16 + 2 * n_pat]
    o_tok, lse_tok, ssm_tok, carry = refs[16 + 2 * n_pat:]

    @pl.when(jnp.logical_and(pl.program_id(0) == 0, pl.program_id(1) == 0))
    def _():
        carry[...] = jnp.zeros_like(carry)

    nb, tt, d = x_ref.shape
    n_slab = ssm_ref.shape[0]
    b0 = pl.program_id(0) * nb
    cb = MERGE_CHUNK_BATCH
    rows = cb * tt
    lane = lax.broadcasted_iota(jnp.int32, (rows, LANES), 1).astype(F32)
    row = lax.broadcasted_iota(jnp.int32, (rows, rows), 0)
    col = lax.broadcasted_iota(jnp.int32, (rows, rows), 1)
    strict_lower = (col < row).astype(BF16)
    running = carry[...]

    for c0 in range(0, nb, cb):
        bbs = range(c0, c0 + cb)
        r_lo = c0 * tt
        chunk_rows = slice(r_lo, r_lo + rows)

        for bb in bbs:
            for s in range(n_slab):
                ssm_tok[bb * tt:(bb + 1) * tt, s * LANES:(s + 1) * LANES] = (
                    ssm_ref[s, pl.ds(b0 + bb, tt, stride=bsz_total), :])

        slot = 0
        sources = []
        for p, (_, dil) in enumerate(DILATION_PATTERNS):
            o_ref, lse_ref = attn_refs[2 * p], attn_refs[2 * p + 1]
            if dil == 1:
                sources.append((o_ref, lse_ref, None))
                continue
            sub = tt // dil
            for bb in bbs:
                for r in range(dil):
                    o_blk = o_ref[bb, r].astype(F32)
                    l_blk = lse_ref[bb, r]
                    for s in range(n_slab):
                        dst = pl.ds(bb * tt + r, sub, stride=dil)
                        o_tok[slot, s, dst, :] = o_blk[:, s * LANES:(s + 1) * LANES]
                        lse_tok[slot, s, dst, :] = l_blk[:, s * LANES:(s + 1) * LANES]
            sources.append((o_ref, lse_ref, slot))
            slot += 1
        attn_slabs = []
        for s in range(n_slab):
            cols = slice(s * LANES, (s + 1) * LANES)
            o_ps, lse_ps = [], []
            for o_ref, lse_ref, src_slot in sources:
                if src_slot is None:
                    o_ps.append(o_ref[c0:c0 + cb, 0, :, cols].astype(F32).reshape(rows, LANES))
                    lse_ps.append(lse_ref[c0:c0 + cb, 0, :, cols].reshape(rows, LANES))
                else:
                    o_ps.append(o_tok[src_slot, s, chunk_rows, :])
                    lse_ps.append(lse_tok[src_slot, s, chunk_rows, :])
            m = functools.reduce(jnp.maximum, lse_ps)
            es = [jnp.exp(l - m) for l in lse_ps]
            num = functools.reduce(lambda a, b: a + b, [e * o for e, o in zip(es, o_ps)])
            den = functools.reduce(lambda a, b: a + b, es)
            attn_slabs.append(num / den)
        attn = _lane_concat(attn_slabs)

        g_attn = gate_ref[c0:c0 + cb, :, 0:d].astype(F32).reshape(rows, d)
        g_ssm = gate_ref[c0:c0 + cb, :, d:].astype(F32).reshape(rows, d)
        merged = (g_attn * _dot(attn.astype(BF16), wba_ref[...])
                  + g_ssm * _dot(ssm_tok[chunk_rows, :].astype(BF16), wbs_ref[...]))
        y = _dot(merged.astype(BF16), wout_ref[...])
        gate1 = mod_ref[c0:c0 + cb, 2, :][:, None, :]
        shift2 = mod_ref[c0:c0 + cb, 3, :][:, None, :]
        scale2 = mod_ref[c0:c0 + cb, 4, :][:, None, :]
        x1 = x_ref[c0:c0 + cb] + gate1 * _rms_norm(y, gpost_ref[...]).reshape(cb, tt, d)
        x1_ref[c0:c0 + cb] = x1
        h2 = _rms_norm(x1, gpre_ref[...]) * (1.0 + scale2) + shift2
        for j, bb in enumerate(bbs):
            _store_row_tiles(h2_ref.at[bb], _pack_rows(h2[j]))

        a_hi, a_lo = _split_bf16(h2.reshape(rows, d))
        hi_pass = _dot(a_hi, wr_cat_ref[...])
        logits = (hi_pass[:, :LANES] + _dot(a_lo, wr_hi_ref[...]) + hi_pass[:, LANES:]
                  + br_ref[...])
        big = float(LANES)
        is_group = jnp.logical_and(lane >= GROUP_LANE0, lane < GROUP_LANE0 + N_EXPERT_GROUPS)
        gl = jnp.where(is_group, logits, -jnp.inf)
        g_max = jnp.max(gl, axis=-1, keepdims=True)
        g_sel = jnp.min(jnp.where(gl == g_max, lane, big), axis=-1, keepdims=True) - GROUP_LANE0
        g_gate = 1.0 / jnp.sum(jnp.exp(gl - g_max), axis=-1, keepdims=True)
        lo = g_sel * EXPERTS_PER_GROUP
        in_group = jnp.logical_and(lane >= lo, lane < lo + EXPERTS_PER_GROUP)
        el = jnp.where(in_group, logits, -jnp.inf)
        t0 = jnp.max(el, axis=-1, keepdims=True)
        i0 = jnp.min(jnp.where(el == t0, lane, big), axis=-1, keepdims=True)
        el1 = jnp.where(lane == i0, -jnp.inf, el)
        t1 = jnp.max(el1, axis=-1, keepdims=True)
        i1 = jnp.min(jnp.where(el1 == t1, lane, big), axis=-1, keepdims=True)
        e = jnp.exp(t1 - t0)
        w0 = g_gate / (1.0 + e)
        w1 = g_gate * e / (1.0 + e)

        hit0 = lane == i0
        hit1 = lane == i1
        onehot = jnp.logical_or(hit0, hit1).astype(F32)
        before = _dot(strict_lower, onehot.astype(BF16)) + running
        rank0 = jnp.sum(jnp.where(hit0, before, 0.0), axis=-1, keepdims=True)
        rank1 = jnp.sum(jnp.where(hit1, before, 0.0), axis=-1, keepdims=True)
        running = running + jnp.sum(onehot, axis=0, keepdims=True)

        route = jnp.zeros((rows, LANES), F32)
        for name, val in (("id0", i0), ("id1", i1), ("rank0", rank0), ("rank1", rank1),
                          ("w0", w0), ("w1", w1)):
            route = jnp.where(lane == ROUTE_LANES[name], val, route)
        route_ref[c0:c0 + cb] = route.reshape(cb, tt, LANES)

    carry[...] = running
    count_ref[...] = jnp.broadcast_to(running, count_ref.shape)


def _merge_and_route(x, gates, attn_outs, ssm_slabs, mod3, g_post, g_pre, wba, wbs, wout,
                     w_rg, b_rg, w_re, b_re):
    bsz, seq, d = x.shape
    tt, nb = TIME_TILE, MERGE_BATCH
    aw = ATTN_WIDTH
    n_slab = ssm_slabs.shape[0]
    assert aw == n_slab * LANES
    wr = jnp.zeros((d, LANES), F32).at[:, :N_EXPERTS].set(w_re.astype(F32))
    wr = wr.at[:, GROUP_LANE0:GROUP_LANE0 + N_EXPERT_GROUPS].set(w_rg.astype(F32))
    br = jnp.zeros((1, LANES), F32).at[0, :N_EXPERTS].set(b_re.astype(F32))
    br = br.at[0, GROUP_LANE0:GROUP_LANE0 + N_EXPERT_GROUPS].set(b_rg.astype(F32))
    wr_hi, wr_lo = _split_bf16(wr)
    wr_cat = jnp.concatenate([wr_hi, wr_lo], axis=1)
    n_strided = sum(1 for _, dil in DILATION_PATTERNS if dil > 1)
    tok = lambda h, i: (h, i, 0)
    const = lambda h, i: (0, 0)
    attn_args, attn_specs = [], []
    for (o_p, lse_p), (_, dil) in zip(attn_outs, DILATION_PATTERNS):
        spec = pl.BlockSpec((nb, dil, tt // dil, aw), lambda h, i: (h, 0, i, 0))
        attn_args += [o_p, lse_p]
        attn_specs += [spec, spec]
    return pl.pallas_call(
        functools.partial(_merge_kernel, bsz_total=bsz),
        out_shape=(jax.ShapeDtypeStruct((bsz, seq, d), F32),
                   jax.ShapeDtypeStruct((bsz, seq * ROW_TILE, LANES), U32),
                   jax.ShapeDtypeStruct((bsz, seq, LANES), F32),
                   jax.ShapeDtypeStruct((SUBLANES, LANES), F32)),
        grid=(bsz // nb, seq // tt),
        in_specs=[pl.BlockSpec((nb, tt, d), tok),
                  pl.BlockSpec((nb, tt, gates.shape[-1]), tok)]
                 + attn_specs
                 + [pl.BlockSpec((n_slab, tt * bsz, LANES), lambda h, i: (0, i, 0)),
                    pl.BlockSpec((nb, 6, d), lambda h, i: (h, 0, 0)),
                    pl.BlockSpec((1, d), const), pl.BlockSpec((1, d), const),
                    pl.BlockSpec(wba.shape, const), pl.BlockSpec(wbs.shape, const),
                    pl.BlockSpec(wout.shape, const),
                    pl.BlockSpec((d, 2 * LANES), const), pl.BlockSpec((d, LANES), const),
                    pl.BlockSpec((1, LANES), const)],
        out_specs=(pl.BlockSpec((nb, tt, d), tok), pl.BlockSpec((nb, tt * ROW_TILE, LANES), tok),
                   pl.BlockSpec((nb, tt, LANES), tok),
                   pl.BlockSpec((SUBLANES, LANES), const)),
        scratch_shapes=[pltpu.VMEM((n_strided, n_slab, nb * tt, LANES), F32),
                        pltpu.VMEM((n_strided, n_slab, nb * tt, LANES), F32),
                        pltpu.VMEM((nb * tt, n_slab * LANES), F32),
                        pltpu.VMEM((1, LANES), F32)],
        compiler_params=_cparams("arbitrary", "arbitrary"),
        name="merge",
    )(x, gates, *attn_args, ssm_slabs, mod3, g_post, g_pre, wba, wbs, wout, wr_cat, wr_hi, br)


ISSUE_UNROLL = 8


def _dispatch_kernel(dest_ref, seg_ref, h_ref, xs_ref, zero_buf, sem, zsem):
    rows = h_ref.shape[0] // ROW_TILE
    blk = zero_buf.shape[0]
    n_blocks = xs_ref.shape[0] // blk
    base = pl.program_id(0) * rows

    @pl.when(pl.program_id(0) == 0)
    def _():
        zero_buf[...] = jnp.zeros_like(zero_buf)

        def zero_copy(row0):
            return pltpu.make_async_copy(zero_buf, xs_ref.at[pl.ds(pl.multiple_of(row0, blk), blk)],
                                         zsem)

        def fill_tail(e, carry):
            @pl.when(seg_ref[N_EXPERTS + e] > 0)
            def _():
                zero_copy(seg_ref[e] * ROW_TILE - blk).start()
            return carry

        def fill_unused(j, carry):
            zero_copy(j * blk).start()
            return carry

        def wait_tail(e, carry):
            @pl.when(seg_ref[N_EXPERTS + e] > 0)
            def _():
                zero_copy(0).wait()
            return carry

        def wait_unused(j, carry):
            zero_copy(0).wait()
            return carry

        n_used = seg_ref[2 * N_EXPERTS]
        lax.fori_loop(0, N_EXPERTS, fill_tail, 0)
        lax.fori_loop(n_used, n_blocks, fill_unused, 0)
        lax.fori_loop(0, N_EXPERTS, wait_tail, 0)
        lax.fori_loop(n_used, n_blocks, wait_unused, 0)

    group = ISSUE_UNROLL * ROW_TILE

    def issue(g, carry):
        g0 = pl.multiple_of(g * group, group)
        for rr in range(ISSUE_UNROLL):
            src = h_ref.at[pl.ds(g0 + rr * ROW_TILE, ROW_TILE)]
            for k in range(TOP_K):
                d = dest_ref[TOP_K * (base + g * ISSUE_UNROLL + rr) + k]
                dst = xs_ref.at[pl.ds(pl.multiple_of(d * ROW_TILE, ROW_TILE), ROW_TILE)]
                pltpu.make_async_copy(src, dst, sem).start(priority=k)
        return carry

    lax.fori_loop(0, rows // ISSUE_UNROLL, issue, 0)
    for _ in range(TOP_K):
        pltpu.make_async_copy(h_ref, xs_ref.at[pl.ds(0, rows * ROW_TILE)], sem).wait()


def _dispatch(h2_tiles, dest, seg_info, cap):
    n_tok = h2_tiles.shape[0] // ROW_TILE
    rows = DISPATCH_ROWS
    return pl.pallas_call(
        _dispatch_kernel,
        out_shape=jax.ShapeDtypeStruct((cap * ROW_TILE, LANES), h2_tiles.dtype),
        grid_spec=pltpu.PrefetchScalarGridSpec(
            num_scalar_prefetch=2,
            grid=(n_tok // rows,),
            in_specs=[pl.BlockSpec((rows * ROW_TILE, LANES), lambda i, dest, seg: (i, 0))],
            out_specs=pl.BlockSpec(memory_space=pl.ANY),
            scratch_shapes=[pltpu.VMEM((MOE_ROWS * ROW_TILE, LANES), h2_tiles.dtype),
                            pltpu.SemaphoreType.DMA, pltpu.SemaphoreType.DMA]),
        compiler_params=_cparams("arbitrary"),
        name="dispatch",
    )(dest, seg_info, h2_tiles)


def _expert_kernel(blk_expert_ref, n_used_ref, x_ref, w1_ref, w3_ref, w2_ref, y_ref,
                   w1b, w3b, w2b):
    i = pl.program_id(0)
    used = i < n_used_ref[0]
    rows = x_ref.shape[0] // ROW_TILE
    new_expert = jnp.logical_or(
        i == 0, blk_expert_ref[i] != blk_expert_ref[jnp.maximum(i - 1, 0)])

    @pl.when(jnp.logical_and(used, new_expert))
    def _():
        w1b[...] = w1_ref[0].astype(BF16)
        w3b[...] = w3_ref[0].astype(BF16)
        w2b[...] = w2_ref[0].astype(BF16)

    @pl.when(used)
    def _():
        xb = _unpack_rows(_load_row_tiles(x_ref, rows)).astype(BF16)
        h1 = _dot(xb, w1b[...])
        h3 = _dot(xb, w3b[...])
        act = (h1 * _sigmoid(h1)) * h3
        y = _dot(act.astype(BF16), w2b[...])
        _store_row_tiles(y_ref, _pack_rows(y))

    @pl.when(jnp.logical_not(used))
    def _():
        y_ref[...] = jnp.zeros_like(y_ref)


def _experts(xs, blk_expert, n_used, w1, w3, w2):
    d, de = w1.shape[-2:]
    assert d == 2 * ROW_TILE * LANES
    rows = MOE_ROWS
    tile_rows = rows * ROW_TILE
    xblk = lambda i, be, nu: (jnp.minimum(i, nu[0] - 1), 0)
    wblk = lambda i, be, nu: (be[i], 0, 0)
    return pl.pallas_call(
        _expert_kernel,
        out_shape=jax.ShapeDtypeStruct(xs.shape, U32),
        grid_spec=pltpu.PrefetchScalarGridSpec(
            num_scalar_prefetch=2,
            grid=(xs.shape[0] // tile_rows,),
            in_specs=[pl.BlockSpec((tile_rows, LANES), xblk),
                      pl.BlockSpec((1, d, de), wblk), pl.BlockSpec((1, d, de), wblk),
                      pl.BlockSpec((1, de, d), wblk)],
            out_specs=pl.BlockSpec((tile_rows, LANES), lambda i, be, nu: (i, 0)),
            scratch_shapes=[pltpu.VMEM((d, de), BF16), pltpu.VMEM((d, de), BF16),
                            pltpu.VMEM((de, d), BF16)]),
        compiler_params=_cparams("arbitrary"),
        name="experts",
    )(blk_expert, n_used, xs, w1, w3, w2)


def _combine_kernel(dest_ref, ys_ref, x1_ref, route_ref, mod_ref, g_ref, o_ref, ybuf, sem):
    rows = x1_ref.shape[0]
    step = pl.program_id(0)
    n_step = pl.num_programs(0)

    group = ISSUE_UNROLL * ROW_TILE

    def gather(tile, slot):
        base = tile * rows

        def issue(g, carry):
            g0 = pl.multiple_of(g * group, group)
            for rr in range(ISSUE_UNROLL):
                for k in range(TOP_K):
                    d = dest_ref[TOP_K * (base + g * ISSUE_UNROLL + rr) + k]
                    src = ys_ref.at[pl.ds(pl.multiple_of(d * ROW_TILE, ROW_TILE), ROW_TILE)]
                    dst = ybuf.at[slot, k, pl.ds(g0 + rr * ROW_TILE, ROW_TILE)]
                    pltpu.make_async_copy(src, dst, sem.at[slot]).start(priority=k)
            return carry

        lax.fori_loop(0, rows // ISSUE_UNROLL, issue, 0)

    slot = lax.rem(step, 2)

    @pl.when(step == 0)
    def _():
        gather(step, 0)

    @pl.when(step + 1 < n_step)
    def _():
        gather(step + 1, 1 - slot)

    for k in range(TOP_K):
        pltpu.make_async_copy(ys_ref.at[pl.ds(0, rows * ROW_TILE)], ybuf.at[slot, k],
                              sem.at[slot]).wait()
    route = route_ref[...]
    w0 = route[:, ROUTE_LANES["w0"]:ROUTE_LANES["w0"] + 1]
    w1 = route[:, ROUTE_LANES["w1"]:ROUTE_LANES["w1"] + 1]

    def rows_of(k):
        return _unpack_rows(_load_row_tiles(ybuf.at[slot, k], rows))

    y = rows_of(0) * w0 + rows_of(1) * w1
    o_ref[...] = x1_ref[...] + mod_ref[0, 5:6, :] * _rms_norm(y, g_ref[...])


def _combine(ys, dest, x1_flat, route_flat, mod3, g_post, seq):
    n_tok, d = x1_flat.shape
    rows = COMBINE_ROWS
    assert seq % rows == 0
    return pl.pallas_call(
        _combine_kernel,
        out_shape=jax.ShapeDtypeStruct((n_tok, d), F32),
        grid_spec=pltpu.PrefetchScalarGridSpec(
            num_scalar_prefetch=1,
            grid=(n_tok // rows,),
            in_specs=[pl.BlockSpec(memory_space=pl.ANY),
                      pl.BlockSpec((rows, d), lambda i, dest: (i, 0)),
                      pl.BlockSpec((rows, LANES), lambda i, dest: (i, 0)),
                      pl.BlockSpec((1, 6, d), lambda i, dest: (i * rows // seq, 0, 0)),
                      pl.BlockSpec((1, d), lambda i, dest: (0, 0))],
            out_specs=pl.BlockSpec((rows, d), lambda i, dest: (i, 0)),
            scratch_shapes=[pltpu.VMEM((2, TOP_K, rows * ROW_TILE, LANES), U32),
                            pltpu.SemaphoreType.DMA((2,))]),
        compiler_params=_cparams("arbitrary"),
        name="combine",
    )(dest, ys, x1_flat, route_flat, mod3, g_post)


def _moe_layout(route_flat, counts):
    rows = MOE_ROWS
    n_tok = route_flat.shape[0]
    ids = route_flat[:, 0:TOP_K].astype(jnp.int32)
    rank = route_flat[:, TOP_K:2 * TOP_K].astype(jnp.int32)
    counts = counts.astype(jnp.int32)
    padded = (counts + rows - 1) // rows * rows
    pad_ends = jnp.cumsum(padded)
    pad_starts = pad_ends - padded
    expert = jnp.arange(N_EXPERTS, dtype=jnp.int32)
    start_of = jnp.sum(jnp.where(ids[..., None] == expert, pad_starts, 0), axis=-1)
    dest = (start_of + rank).reshape(n_tok * TOP_K)
    n_blocks = (n_tok * TOP_K + N_EXPERTS * (rows - 1) + rows - 1) // rows
    blk_row0 = jnp.arange(n_blocks, dtype=jnp.int32) * rows
    blk_expert = jnp.minimum(jnp.sum(pad_ends[None, :] <= blk_row0[:, None], axis=-1),
                             N_EXPERTS - 1).astype(jnp.int32)
    n_used = (pad_ends[-1] // rows).reshape(1).astype(jnp.int32)
    seg_info = jnp.concatenate([pad_ends, padded, n_used]).astype(jnp.int32)
    return dest, blk_expert, n_used, seg_info, n_blocks * rows


def kernel(x, c, w_mod, b_mod, g_pre_mix, g_post_mix, g_pre_ffn, g_post_ffn, w_in, rel_bias, a_re, a_im, log_dt, ssm_b_re, ssm_b_im, ssm_c_re, ssm_c_im, d_skip, w_glu, b_glu, w_branch_attn, w_branch_ssm, w_out, w_router_group, b_router_group, w_router_expert, b_router_expert, w1, w3, w2):
    bsz, seq, d = x.shape
    depth = w_mod.shape[0]
    ssm_width = w_glu.shape[-1]
    n_pat = len(DILATION_PATTERNS)
    for l in range(depth):
        mod3 = _modulation(c, w_mod[l], b_mod[l]).reshape(bsz, 6, d)
        w_in_l = w_in[l].astype(BF16)
        g_pre = g_pre_mix[l].reshape(1, d)
        qkv = _qkv_projection(x, mod3, g_pre, w_in_l[:, :3 * ATTN_WIDTH])
        u_slabs, gates = _ugate_projection(x, mod3, g_pre, w_in_l[:, 3 * ATTN_WIDTH:], ssm_width)
        attn_outs = [_attention_pattern(*qkv[3 * p:3 * p + 3], rel_bias, DILATION_PATTERNS[p][1])
                     for p in range(n_pat)]
        ssm_slabs = _ssm_branch(u_slabs, bsz, a_re[l], a_im[l], log_dt[l], ssm_b_re[l],
                                ssm_b_im[l], ssm_c_re[l], ssm_c_im[l], d_skip[l], w_glu[l], b_glu[l])
        x1, h2, route, counts = _merge_and_route(
            x, gates, attn_outs, ssm_slabs, mod3, g_post_mix[l].reshape(1, d),
            g_pre_ffn[l].reshape(1, d), w_branch_attn[l].astype(BF16),
            w_branch_ssm[l].astype(BF16), w_out[l].astype(BF16),
            w_router_group[l], b_router_group[l], w_router_expert[l], b_router_expert[l])
        route_flat = route.reshape(bsz * seq, LANES)
        dest, blk_expert, n_used, seg_info, cap = _moe_layout(route_flat, counts[0, :N_EXPERTS])
        xs = _dispatch(h2.reshape(bsz * seq * ROW_TILE, LANES), dest, seg_info, cap)
        ys = _experts(xs, blk_expert, n_used, w1[l], w3[l], w2[l])
        x = _combine(ys, dest, x1.reshape(bsz * seq, d), route_flat, mod3,
                     g_post_ffn[l].reshape(1, d), seq).reshape(bsz, seq, d)
    return x
```

```python
import functools
import math

import numpy as np
import jax
import jax.numpy as jnp
from jax import lax
from jax.experimental import pallas as pl
from jax.experimental.pallas import tpu as pltpu

F32 = jnp.float32
BF16 = jnp.bfloat16

N_HEADS = 8
HEAD_DIM = 64
ATTN_WIDTH = N_HEADS * HEAD_DIM
DILATION_PATTERNS = ((128, 1), (512, 4), (2048, 16))
NUM_BUCKETS = 32
MAX_DISTANCE = 2048
N_EXPERT_GROUPS = 4
EXPERTS_PER_GROUP = 8
N_EXPERTS = N_EXPERT_GROUPS * EXPERTS_PER_GROUP
TOP_K = 2
RMS_EPS = 1e-6
NEG_INF = -1e30
LOG2_E = math.log2(math.e)
LN_2 = math.log(2.0)

LANES = 128
SUBLANES = 8
VMEM_LIMIT_BYTES = 56 * 1024 * 1024

ATTN_BLK = 128
ATTN_STEP_ROWS = 512
QKV_ROWS = 512
QKV_MID_DIL = 4
TIME_TILE = 128
MERGE_BATCH = 4
MERGE_CHUNK_BATCH = 4
SSM_STEPS = 64
MOE_ROWS = 512
DISPATCH_ROWS = 2048
COMBINE_ROWS = 512
ROW_TILE = 4
U32 = jnp.uint32
HI_HALF = 0xFFFF0000


def _pack_rows(x):
    w = x.shape[1] // 2
    lo = lax.bitcast_convert_type(x[:, :w].astype(BF16).astype(F32), U32) >> 16
    hi = lax.bitcast_convert_type(x[:, w:].astype(BF16).astype(F32), U32) & U32(HI_HALF)
    return hi | lo


def _unpack_rows(p):
    lo = lax.bitcast_convert_type(p << 16, F32)
    hi = lax.bitcast_convert_type(p & U32(HI_HALF), F32)
    return jnp.concatenate([lo, hi], axis=1)


def _store_row_tiles(ref, packed):
    rows = packed.shape[0]
    for s in range(ROW_TILE):
        ref[pl.ds(s, rows, stride=ROW_TILE), :] = packed[:, s * LANES:(s + 1) * LANES]


def _load_row_tiles(ref, rows):
    return _lane_concat([ref[pl.ds(s, rows, stride=ROW_TILE), :] for s in range(ROW_TILE)])


def _cparams(*sem):
    return pltpu.CompilerParams(dimension_semantics=sem, vmem_limit_bytes=VMEM_LIMIT_BYTES)


def _sigmoid(x):
    return 1.0 / (1.0 + jnp.exp(-x))


def _dot(a, b):
    return jnp.dot(a, b, preferred_element_type=F32)


def _split_bf16(a):
    hi = a.astype(BF16)
    lo = (a - hi.astype(F32)).astype(BF16)
    return hi, lo


def _dot_split(a, w_hi, w_lo):
    a_hi, a_lo = _split_bf16(a)
    return _dot(a_hi, w_hi) + _dot(a_lo, w_hi) + _dot(a_hi, w_lo)


def _rms_norm(x, gain):
    ms = jnp.mean(x * x, axis=-1, keepdims=True)
    return x * lax.rsqrt(ms + RMS_EPS) * gain


def _lane_concat(ref_slabs):
    return jnp.concatenate(ref_slabs, axis=-1)


def _mod_kernel(c_ref, w_ref, b_ref, o_ref):
    c = c_ref[...]
    a = c * _sigmoid(c)
    w_hi, w_lo = _split_bf16(w_ref[...])
    o_ref[...] = _dot_split(a, w_hi, w_lo) + b_ref[...]


def _modulation(c, w_mod, b_mod):
    bsz, d = c.shape
    n = w_mod.shape[1]
    tn = 1024
    return pl.pallas_call(
        _mod_kernel,
        out_shape=jax.ShapeDtypeStruct((bsz, n), F32),
        grid=(n // tn,),
        in_specs=[pl.BlockSpec((bsz, d), lambda j: (0, 0)),
                  pl.BlockSpec((d, tn), lambda j: (0, j)),
                  pl.BlockSpec((1, tn), lambda j: (0, j))],
        out_specs=pl.BlockSpec((bsz, tn), lambda j: (0, j)),
        compiler_params=_cparams("arbitrary"),
        name="mod",
    )(c, w_mod, b_mod.reshape(1, n))


def _qkv_kernel(x_ref, mod_ref, g_ref, w_ref, *rest):
    n_pat = len(DILATION_PATTERNS)
    out_refs, slab, mid = rest[:3 * n_pat], rest[3 * n_pat], rest[3 * n_pat + 1]
    h = _rms_norm(x_ref[0], g_ref[...]) * (1.0 + mod_ref[0, 1:2, :]) + mod_ref[0, 0:1, :]
    res = _dot(h.astype(BF16), w_ref[...])
    rows = res.shape[0]
    n_slab = res.shape[1] // LANES
    per_tensor = ATTN_WIDTH // LANES
    for s in range(n_slab):
        piece = res[:, s * LANES:(s + 1) * LANES]
        if s < per_tensor:
            piece = piece * (HEAD_DIM ** -0.5 * LOG2_E)
        slab[s] = piece
    mid_dil = QKV_MID_DIL
    for p, (_, dil) in enumerate(DILATION_PATTERNS):
        sub = rows // dil
        for t in range(3):
            out = out_refs[3 * p + t]
            for r in range(dil):
                pieces = []
                for s in range(per_tensor):
                    ts = t * per_tensor + s
                    if dil == 1:
                        piece = slab[ts]
                    elif dil == mid_dil:
                        piece = slab[ts, pl.ds(r, sub, stride=dil), :]
                        mid[ts, r] = piece
                    else:
                        ratio = dil // mid_dil
                        piece = mid[ts, r % mid_dil, pl.ds(r // mid_dil, sub, stride=ratio), :]
                    pieces.append(piece)
                out[0, r] = _lane_concat(pieces).astype(out.dtype)


def _qkv_projection(x, mod3, g_pre, w_qkv):
    bsz, seq, d = x.shape
    tm = QKV_ROWS
    out_shape, out_specs = [], []
    for _, dil in DILATION_PATTERNS:
        assert tm % (dil * 2 * SUBLANES) == 0
        for _ in range(3):
            out_shape.append(jax.ShapeDtypeStruct((bsz, dil, seq // dil, ATTN_WIDTH), BF16))
            out_specs.append(pl.BlockSpec((1, dil, tm // dil, ATTN_WIDTH),
                                          lambda b, i: (b, 0, i, 0)))
    return pl.pallas_call(
        _qkv_kernel,
        out_shape=out_shape,
        grid=(bsz, seq // tm),
        in_specs=[pl.BlockSpec((1, tm, d), lambda b, i: (b, i, 0)),
                  pl.BlockSpec((1, 6, d), lambda b, i: (b, 0, 0)),
                  pl.BlockSpec((1, d), lambda b, i: (0, 0)),
                  pl.BlockSpec(w_qkv.shape, lambda b, i: (0, 0))],
        out_specs=out_specs,
        scratch_shapes=[pltpu.VMEM((w_qkv.shape[1] // LANES, tm, LANES), F32),
                        pltpu.VMEM((w_qkv.shape[1] // LANES, QKV_MID_DIL, tm // QKV_MID_DIL, LANES),
                                   F32)],
        compiler_params=_cparams("arbitrary", "arbitrary"),
        name="qkv",
    )(x, mod3, g_pre, w_qkv)


def _ugate_kernel(x_ref, mod_ref, g_ref, w_ref, u_ref, gate_ref):
    bsz, tt, d = x_ref.shape
    shift = mod_ref[:, 0, :][:, None, :]
    scale = mod_ref[:, 1, :][:, None, :]
    h = _rms_norm(x_ref[...], g_ref[...]) * (1.0 + scale) + shift
    hb = h.reshape(bsz * tt, d).astype(BF16)
    n_slab = u_ref.shape[0]
    sw = n_slab * LANES
    u = _dot(hb, w_ref[:, 0:sw])
    for b in range(bsz):
        for s in range(n_slab):
            u_ref[s, pl.ds(b, tt, stride=bsz), :] = u[b * tt:(b + 1) * tt, s * LANES:(s + 1) * LANES]
    gw = gate_ref.shape[-1]
    chunk = 512
    for c0 in range(0, gw, chunk):
        g = _sigmoid(_dot(hb, w_ref[:, sw + c0:sw + c0 + chunk]))
        gate_ref[:, :, c0:c0 + chunk] = g.reshape(bsz, tt, chunk).astype(BF16)


def _ugate_projection(x, mod3, g_pre, w_ug, ssm_width):
    bsz, seq, d = x.shape
    tt = TIME_TILE
    gw = w_ug.shape[1] - ssm_width
    n_slab = ssm_width // LANES
    return pl.pallas_call(
        _ugate_kernel,
        out_shape=(jax.ShapeDtypeStruct((n_slab, seq * bsz, LANES), F32),
                   jax.ShapeDtypeStruct((bsz, seq, gw), BF16)),
        grid=(seq // tt,),
        in_specs=[pl.BlockSpec((bsz, tt, d), lambda i: (0, i, 0)),
                  pl.BlockSpec((bsz, 6, d), lambda i: (0, 0, 0)),
                  pl.BlockSpec((1, d), lambda i: (0, 0)),
                  pl.BlockSpec(w_ug.shape, lambda i: (0, 0))],
        out_specs=(pl.BlockSpec((n_slab, tt * bsz, LANES), lambda i: (0, i, 0)),
                   pl.BlockSpec((bsz, tt, gw), lambda i: (0, i, 0))),
        compiler_params=_cparams("arbitrary"),
        name="ugate",
    )(x, mod3, g_pre, w_ug)


def _t5_bucket_np(dist):
    exact = NUM_BUCKETS // 2
    d_f = np.maximum(dist, exact).astype(np.float32)
    large = exact + (np.log(d_f / np.float32(exact)) / np.float32(math.log(MAX_DISTANCE / exact))
                     * np.float32(NUM_BUCKETS - exact)).astype(np.int32)
    return np.where(dist < exact, dist, np.minimum(large, NUM_BUCKETS - 1))


def _bucket_map_t(dil):
    blk = ATTN_BLK
    ki = np.arange(2 * blk)[:, None]
    qi = np.arange(blk)[None, :]
    return _t5_bucket_np(np.maximum(blk + qi - ki, 0) * dil).astype(np.int32)


def _attn_kernel(relb_ref, q_ref, kc_ref, kp_ref, vc_ref, vp_ref, bucket_ref,
                 o_ref, lse_ref, kbuf, vbuf, bias_t, *, n_sub):
    blk = ATTN_BLK
    first_call = jnp.logical_and(pl.program_id(0) == 0,
                                 jnp.logical_and(pl.program_id(1) == 0, pl.program_id(2) == 0))

    @pl.when(first_call)
    def _():
        bucket = bucket_ref[...]
        ki = lax.broadcasted_iota(jnp.int32, bucket.shape, 0)
        qi = lax.broadcasted_iota(jnp.int32, bucket.shape, 1)
        dist = blk + qi - ki
        band = jnp.logical_and(dist >= 0, dist <= blk)
        band_first = jnp.logical_and(band, ki >= blk)

        def per_head(h, carry):
            acc = jnp.zeros(bucket.shape, F32)
            for b in range(NUM_BUCKETS):
                acc = jnp.where(bucket == b, relb_ref[b, h] * LOG2_E, acc)
            bias_t[0, h] = jnp.where(band_first, acc, NEG_INF)
            bias_t[1, h] = jnp.where(band, acc, NEG_INF)
            return carry

        lax.fori_loop(0, N_HEADS, per_head, 0)

    first_variant = jnp.where(pl.program_id(2) == 0, 0, 1)
    n_res = q_ref.shape[1]
    for g in range(n_res):
        kbuf[g, 0:blk, :] = kp_ref[0, g]
        kbuf[g, blk:, :] = kc_ref[0, g]
        vbuf[g, 0:blk, :] = vp_ref[0, g]
        vbuf[g, blk:, :] = vc_ref[0, g]

    lane = lax.broadcasted_iota(jnp.int32, (1, LANES), 1)
    lo_half = lane < HEAD_DIM
    bd_row = lax.broadcasted_iota(jnp.int32, (4 * blk, LANES), 0)
    bd_col = lax.broadcasted_iota(jnp.int32, (4 * blk, LANES), 1)
    ones_bd = ((bd_row < 2 * blk) == (bd_col < HEAD_DIM)).astype(F32).astype(BF16)
    contract_last = (((1,), (1,)), ((), ()))
    contract_first = (((0,), (0,)), ((), ()))

    def sub_block(g, i):
        r0 = i * blk
        q = q_ref[0, g, r0:r0 + blk, :]
        kk = kbuf[g, r0:r0 + 2 * blk, :]
        vv = vbuf[g, r0:r0 + 2 * blk, :]
        variant = first_variant if i == 0 else 1
        for j in range(N_HEADS // 2):
            cols = slice(j * LANES, (j + 1) * LANES)
            qj, kj, vj = q[:, cols], kk[:, cols], vv[:, cols]
            probs_t, maxes = [], []
            for hh in range(2):
                sel = lo_half if hh == 0 else jnp.logical_not(lo_half)
                qm = jnp.where(sel, qj, jnp.zeros_like(qj))
                s_t = lax.dot_general(kj, qm, contract_last, preferred_element_type=F32)
                s_t = s_t + bias_t[variant, 2 * j + hh]
                m = jnp.max(s_t, axis=0, keepdims=True)
                probs_t.append(jnp.exp2(s_t - m).astype(BF16))
                maxes.append(m)
            p2_t = jnp.concatenate(probs_t, axis=0)
            v_bd = jnp.concatenate([jnp.where(lo_half, vj, jnp.zeros_like(vj)),
                                    jnp.where(lo_half, jnp.zeros_like(vj), vj)], axis=0)
            rhs = jnp.concatenate([v_bd, ones_bd], axis=1)
            ol = lax.dot_general(p2_t, rhs, contract_first, preferred_element_type=F32)
            o2, l2 = ol[:, :LANES], ol[:, LANES:]
            m_t = jnp.concatenate([jnp.broadcast_to(maxes[0], (HEAD_DIM, blk)),
                                   jnp.broadcast_to(maxes[1], (HEAD_DIM, blk))], axis=0)
            o_ref[0, g, r0:r0 + blk, cols] = (o2 / l2).astype(o_ref.dtype)
            lse_ref[0, g, r0:r0 + blk, cols] = (m_t.T + jnp.log2(l2)) * LN_2

    for g in range(n_res):
        for i in range(n_sub):
            sub_block(g, i)


def _attention_pattern(q, k, v, rel_bias, dil):
    bsz, _, sub_len, aw = q.shape
    blk = ATTN_BLK
    assert sub_len % blk == 0
    tq = min(ATTN_STEP_ROWS, sub_len)
    n_sub = tq // blk
    ratio = tq // blk
    n_res = min(dil, ATTN_STEP_ROWS // tq)
    cur = lambda b, r, n: (b, r, n, 0)
    prev = lambda b, r, n: (b, r, jnp.maximum(n * ratio - 1, 0), 0)
    blk_cur = pl.BlockSpec((1, n_res, tq, aw), cur)
    blk_prev = pl.BlockSpec((1, n_res, blk, aw), prev)
    bucket = jnp.asarray(_bucket_map_t(dil))
    o_dtype = BF16 if TIME_TILE // dil >= 2 * SUBLANES else F32
    return pl.pallas_call(
        functools.partial(_attn_kernel, n_sub=n_sub),
        out_shape=(jax.ShapeDtypeStruct(q.shape, o_dtype), jax.ShapeDtypeStruct(q.shape, F32)),
        grid=(bsz, dil // n_res, sub_len // tq),
        in_specs=[pl.BlockSpec(memory_space=pltpu.SMEM),
                  blk_cur, blk_cur, blk_prev, blk_cur, blk_prev,
                  pl.BlockSpec(bucket.shape, lambda b, r, n: (0, 0))],
        out_specs=(blk_cur, blk_cur),
        scratch_shapes=[pltpu.VMEM((n_res, tq + blk, aw), BF16),
                        pltpu.VMEM((n_res, tq + blk, aw), BF16),
                        pltpu.VMEM((2, N_HEADS, 2 * blk, blk), F32)],
        compiler_params=_cparams("arbitrary", "arbitrary", "arbitrary"),
        name=f"attn_dil{dil}",
    )(rel_bias.astype(F32), q, k, k, v, v, bucket)


def _ssm_kernel(u_ref, bmat_ref, cmat_ref, ar_ref, ai_ref, dskip_ref, wglu_ref, bglu_ref,
                o_ref, hbuf, hstate, *, n_steps):
    @pl.when(pl.program_id(0) == 0)
    def _():
        hstate[...] = jnp.zeros_like(hstate)

    n_slab = u_ref.shape[0]
    n_state = hbuf.shape[1] // 2
    per = n_state // n_slab
    us = [u_ref[s] for s in range(n_slab)]
    for s in range(n_slab):
        bu = _dot(us[s].astype(BF16), bmat_ref[s])
        hbuf[:, s * per:(s + 1) * per] = bu[:, :per]
        hbuf[:, n_state + s * per:n_state + (s + 1) * per] = bu[:, per:]
    for s in range(n_slab):
        re_cols = slice(s * per, (s + 1) * per)
        im_cols = slice(n_state + s * per, n_state + (s + 1) * per)
        ar = ar_ref[:, re_cols]
        ai = ai_ref[:, re_cols]
        hr = hstate[:, re_cols]
        hi = hstate[:, im_cols]
        for t in range(n_steps):
            trow = slice(t * SUBLANES, (t + 1) * SUBLANES)
            nr = ar * hr - ai * hi + hbuf[trow, re_cols]
            ni = ar * hi + ai * hr + hbuf[trow, im_cols]
            hbuf[trow, re_cols] = nr
            hbuf[trow, im_cols] = ni
            hr, hi = nr, ni
        hstate[:, re_cols] = hr
        hstate[:, im_cols] = hi

    ys = []
    for s in range(n_slab):
        h_s = _lane_concat([hbuf[:, s * per:(s + 1) * per],
                            hbuf[:, n_state + s * per:n_state + (s + 1) * per]])
        ys.append(_dot(h_s.astype(BF16), cmat_ref[s])
                  + dskip_ref[:, s * LANES:(s + 1) * LANES] * us[s])
    y = _lane_concat(ys)
    y = 0.5 * y * (1.0 + jnp.tanh(math.sqrt(2.0 / math.pi) * (y + 0.044715 * (y * y * y))))
    z = _dot(y.astype(BF16), wglu_ref[...]) + bglu_ref[...]
    out = y * _sigmoid(z)
    for s in range(n_slab):
        o_ref[s] = out[:, s * LANES:(s + 1) * LANES]


def _ssm_params(a_re, a_im, log_dt, b_re, b_im, c_re, c_im, bsz):
    g, p = a_re.shape
    hg = b_re.shape[-1]
    dt = jnp.exp(log_dt.astype(F32))[:, None]
    a_re, a_im = a_re.astype(F32), a_im.astype(F32)
    mag = jnp.exp(a_re * dt)
    abar_re = mag * jnp.cos(a_im * dt)
    abar_im = mag * jnp.sin(a_im * dt)
    den = a_re * a_re + a_im * a_im
    q_re = ((abar_re - 1.0) * a_re + abar_im * a_im) / den
    q_im = (abar_im * a_re - (abar_re - 1.0) * a_im) / den
    b_re, b_im = b_re.astype(F32), b_im.astype(F32)
    bb_re = q_re[..., None] * b_re - q_im[..., None] * b_im
    bb_im = q_re[..., None] * b_im + q_im[..., None] * b_re
    gs = LANES // hg
    n_slab = g // gs
    eye = jnp.eye(gs, dtype=F32)

    def in_mat(t):
        t = t.reshape(n_slab, gs, p, hg)
        return jnp.einsum('sgph,gk->sghkp', t, eye).reshape(n_slab, gs * hg, gs * p)

    def out_mat(t):
        t = t.reshape(n_slab, gs, hg, p)
        return jnp.einsum('sghp,gk->sgpkh', t, eye).reshape(n_slab, gs * p, gs * hg)

    bmat = jnp.concatenate([in_mat(bb_re), in_mat(bb_im)], axis=2).astype(BF16)
    cmat = jnp.concatenate([out_mat(c_re.astype(F32)), -out_mat(c_im.astype(F32))],
                           axis=1).astype(BF16)
    ar = jnp.broadcast_to(abar_re.reshape(1, g * p), (bsz, g * p))
    ai = jnp.broadcast_to(abar_im.reshape(1, g * p), (bsz, g * p))
    return bmat, cmat, ar, ai


def _ssm_branch(u_slabs, bsz, a_re, a_im, log_dt, b_re, b_im, c_re, c_im, d_skip, w_glu, b_glu):
    n_slab, n_rows, _ = u_slabs.shape
    width = n_slab * LANES
    assert bsz == SUBLANES
    bmat, cmat, ar, ai = _ssm_params(a_re, a_im, log_dt, b_re, b_im, c_re, c_im, bsz)
    n_state2 = n_slab * bmat.shape[2]
    rows = SSM_STEPS * bsz
    const = lambda c: (0, 0)
    const3 = lambda c: (0, 0, 0)
    slab_spec = pl.BlockSpec((n_slab, rows, LANES), lambda c: (0, c, 0))
    return pl.pallas_call(
        functools.partial(_ssm_kernel, n_steps=SSM_STEPS),
        out_shape=jax.ShapeDtypeStruct(u_slabs.shape, F32),
        grid=(n_rows // rows,),
        in_specs=[slab_spec,
                  pl.BlockSpec(bmat.shape, const3), pl.BlockSpec(cmat.shape, const3),
                  pl.BlockSpec(ar.shape, const), pl.BlockSpec(ai.shape, const),
                  pl.BlockSpec((1, width), const), pl.BlockSpec((width, width), const),
                  pl.BlockSpec((1, width), const)],
        out_specs=slab_spec,
        scratch_shapes=[pltpu.VMEM((rows, n_state2), F32), pltpu.VMEM((bsz, n_state2), F32)],
        compiler_params=_cparams("arbitrary"),
        name="ssm",
    )(u_slabs, bmat, cmat, ar, ai, d_skip.reshape(1, width).astype(F32),
      w_glu.astype(BF16), b_glu.reshape(1, width).astype(F32))


ROUTE_LANES = {"id0": 0, "id1": 1, "rank0": 2, "rank1": 3, "w0": 4, "w1": 5}
GROUP_LANE0 = N_EXPERTS


def _merge_kernel(*refs, bsz_total):
    n_pat = len(DILATION_PATTERNS)
    x_ref, gate_ref = refs[0:2]
    attn_refs = refs[2:2 + 2 * n_pat]
    (ssm_ref, mod_ref, gpost_ref, gpre_ref, wba_ref, wbs_ref, wout_ref,
     wr_cat_ref, wr_hi_ref, br_ref) = refs[2 + 2 * n_pat:12 + 2 * n_pat]
    x1_ref, h2_ref, route_ref, count_ref = refs[12 + 2 * n_pat:16 + 2 * n_pat]
    o_tok, lse_tok, ssm_tok, carry = refs[16 + 2 * n_pat:]

    @pl.when(jnp.logical_and(pl.program_id(0) == 0, pl.program_id(1) == 0))
    def _():
        carry[...] = jnp.zeros_like(carry)

    nb, tt, d = x_ref.shape
    n_slab = ssm_ref.shape[0]
    b0 = pl.program_id(0) * nb
    cb = MERGE_CHUNK_BATCH
    rows = cb * tt
    lane = lax.broadcasted_iota(jnp.int32, (rows, LANES), 1).astype(F32)
    row = lax.broadcasted_iota(jnp.int32, (rows, rows), 0)
    col = lax.broadcasted_iota(jnp.int32, (rows, rows), 1)
    strict_lower = (col < row).astype(BF16)
    running = carry[...]

    for c0 in range(0, nb, cb):
        bbs = range(c0, c0 + cb)
        r_lo = c0 * tt
        chunk_rows = slice(r_lo, r_lo + rows)

        for bb in bbs:
            for s in range(n_slab):
                ssm_tok[bb * tt:(bb + 1) * tt, s * LANES:(s + 1) * LANES] = (
                    ssm_ref[s, pl.ds(b0 + bb, tt, stride=bsz_total), :])

        slot = 0
        sources = []
        for p, (_, dil) in enumerate(DILATION_PATTERNS):
            o_ref, lse_ref = attn_refs[2 * p], attn_refs[2 * p + 1]
            if dil == 1:
                sources.append((o_ref, lse_ref, None))
                continue
            sub = tt // dil
            for bb in bbs:
                for r in range(dil):
                    o_blk = o_ref[bb, r].astype(F32)
                    l_blk = lse_ref[bb, r]
                    for s in range(n_slab):
                        dst = pl.ds(bb * tt + r, sub, stride=dil)
                        o_tok[slot, s, dst, :] = o_blk[:, s * LANES:(s + 1) * LANES]
                        lse_tok[slot, s, dst, :] = l_blk[:, s * LANES:(s + 1) * LANES]
            sources.append((o_ref, lse_ref, slot))
            slot += 1
        attn_slabs = []
        for s in range(n_slab):
            cols = slice(s * LANES, (s + 1) * LANES)
            o_ps, lse_ps = [], []
            for o_ref, lse_ref, src_slot in sources:
                if src_slot is None:
                    o_ps.append(o_ref[c0:c0 + cb, 0, :, cols].astype(F32).reshape(rows, LANES))
                    lse_ps.append(lse_ref[c0:c0 + cb, 0, :, cols].reshape(rows, LANES))
                else:
                    o_ps.append(o_tok[src_slot, s, chunk_rows, :])
                    lse_ps.append(lse_tok[src_slot, s, chunk_rows, :])
            m = functools.reduce(jnp.maximum, lse_ps)
            es = [jnp.exp(l - m) for l in lse_ps]
            num = functools.reduce(lambda a, b: a + b, [e * o for e, o in zip(es, o_ps)])
            den = functools.reduce(lambda a, b: a + b, es)
            attn_slabs.append(num / den)
        attn = _lane_concat(attn_slabs)

        g_attn = gate_ref[c0:c0 + cb, :, 0:d].astype(F32).reshape(rows, d)
        g_ssm = gate_ref[c0:c0 + cb, :, d:].astype(F32).reshape(rows, d)
        merged = (g_attn * _dot(attn.astype(BF16), wba_ref[...])
                  + g_ssm * _dot(ssm_tok[chunk_rows, :].astype(BF16), wbs_ref[...]))
        y = _dot(merged.astype(BF16), wout_ref[...])
        gate1 = mod_ref[c0:c0 + cb, 2, :][:, None, :]
        shift2 = mod_ref[c0:c0 + cb, 3, :][:, None, :]
        scale2 = mod_ref[c0:c0 + cb, 4, :][:, None, :]
        x1 = x_ref[c0:c0 + cb] + gate1 * _rms_norm(y, gpost_ref[...]).reshape(cb, tt, d)
        x1_ref[c0:c0 + cb] = x1
        h2 = _rms_norm(x1, gpre_ref[...]) * (1.0 + scale2) + shift2
        for j, bb in enumerate(bbs):
            _store_row_tiles(h2_ref.at[bb], _pack_rows(h2[j]))

        a_hi, a_lo = _split_bf16(h2.reshape(rows, d))
        hi_pass = _dot(a_hi, wr_cat_ref[...])
        logits = (hi_pass[:, :LANES] + _dot(a_lo, wr_hi_ref[...]) + hi_pass[:, LANES:]
                  + br_ref[...])
        big = float(LANES)
        is_group = jnp.logical_and(lane >= GROUP_LANE0, lane < GROUP_LANE0 + N_EXPERT_GROUPS)
        gl = jnp.where(is_group, logits, -jnp.inf)
        g_max = jnp.max(gl, axis=-1, keepdims=True)
        g_sel = jnp.min(jnp.where(gl == g_max, lane, big), axis=-1, keepdims=True) - GROUP_LANE0
        g_gate = 1.0 / jnp.sum(jnp.exp(gl - g_max), axis=-1, keepdims=True)
        lo = g_sel * EXPERTS_PER_GROUP
        in_group = jnp.logical_and(lane >= lo, lane < lo + EXPERTS_PER_GROUP)
        el = jnp.where(in_group, logits, -jnp.inf)
        t0 = jnp.max(el, axis=-1, keepdims=True)
        i0 = jnp.min(jnp.where(el == t0, lane, big), axis=-1, keepdims=True)
        el1 = jnp.where(lane == i0, -jnp.inf, el)
        t1 = jnp.max(el1, axis=-1, keepdims=True)
        i1 = jnp.min(jnp.where(el1 == t1, lane, big), axis=-1, keepdims=True)
        e = jnp.exp(t1 - t0)
        w0 = g_gate / (1.0 + e)
        w1 = g_gate * e / (1.0 + e)

        hit0 = lane == i0
        hit1 = lane == i1
        onehot = jnp.logical_or(hit0, hit1).astype(F32)
        before = _dot(strict_lower, onehot.astype(BF16)) + running
        rank0 = jnp.sum(jnp.where(hit0, before, 0.0), axis=-1, keepdims=True)
        rank1 = jnp.sum(jnp.where(hit1, before, 0.0), axis=-1, keepdims=True)
        running = running + jnp.sum(onehot, axis=0, keepdims=True)

        route = jnp.zeros((rows, LANES), F32)
        for name, val in (("id0", i0), ("id1", i1), ("rank0", rank0), ("rank1", rank1),
                          ("w0", w0), ("w1", w1)):
            route = jnp.where(lane == ROUTE_LANES[name], val, route)
        route_ref[c0:c0 + cb] = route.reshape(cb, tt, LANES)

    carry[...] = running
    count_ref[...] = jnp.broadcast_to(running, count_ref.shape)


def _merge_and_route(x, gates, attn_outs, ssm_slabs, mod3, g_post, g_pre, wba, wbs, wout,
                     w_rg, b_rg, w_re, b_re):
    bsz, seq, d = x.shape
    tt, nb = TIME_TILE, MERGE_BATCH
    aw = ATTN_WIDTH
    n_slab = ssm_slabs.shape[0]
    assert aw == n_slab * LANES
    wr = jnp.zeros((d, LANES), F32).at[:, :N_EXPERTS].set(w_re.astype(F32))
    wr = wr.at[:, GROUP_LANE0:GROUP_LANE0 + N_EXPERT_GROUPS].set(w_rg.astype(F32))
    br = jnp.zeros((1, LANES), F32).at[0, :N_EXPERTS].set(b_re.astype(F32))
    br = br.at[0, GROUP_LANE0:GROUP_LANE0 + N_EXPERT_GROUPS].set(b_rg.astype(F32))
    wr_hi, wr_lo = _split_bf16(wr)
    wr_cat = jnp.concatenate([wr_hi, wr_lo], axis=1)
    n_strided = sum(1 for _, dil in DILATION_PATTERNS if dil > 1)
    tok = lambda h, i: (h, i, 0)
    const = lambda h, i: (0, 0)
    attn_args, attn_specs = [], []
    for (o_p, lse_p), (_, dil) in zip(attn_outs, DILATION_PATTERNS):
        spec = pl.BlockSpec((nb, dil, tt // dil, aw), lambda h, i: (h, 0, i, 0))
        attn_args += [o_p, lse_p]
        attn_specs += [spec, spec]
    return pl.pallas_call(
        functools.partial(_merge_kernel, bsz_total=bsz),
        out_shape=(jax.ShapeDtypeStruct((bsz, seq, d), F32),
                   jax.ShapeDtypeStruct((bsz, seq * ROW_TILE, LANES), U32),
                   jax.ShapeDtypeStruct((bsz, seq, LANES), F32),
                   jax.ShapeDtypeStruct((SUBLANES, LANES), F32)),
        grid=(bsz // nb, seq // tt),
        in_specs=[pl.BlockSpec((nb, tt, d), tok),
                  pl.BlockSpec((nb, tt, gates.shape[-1]), tok)]
                 + attn_specs
                 + [pl.BlockSpec((n_slab, tt * bsz, LANES), lambda h, i: (0, i, 0)),
                    pl.BlockSpec((nb, 6, d), lambda h, i: (h, 0, 0)),
                    pl.BlockSpec((1, d), const), pl.BlockSpec((1, d), const),
                    pl.BlockSpec(wba.shape, const), pl.BlockSpec(wbs.shape, const),
                    pl.BlockSpec(wout.shape, const),
                    pl.BlockSpec((d, 2 * LANES), const), pl.BlockSpec((d, LANES), const),
                    pl.BlockSpec((1, LANES), const)],
        out_specs=(pl.BlockSpec((nb, tt, d), tok), pl.BlockSpec((nb, tt * ROW_TILE, LANES), tok),
                   pl.BlockSpec((nb, tt, LANES), tok),
                   pl.BlockSpec((SUBLANES, LANES), const)),
        scratch_shapes=[pltpu.VMEM((n_strided, n_slab, nb * tt, LANES), F32),
                        pltpu.VMEM((n_strided, n_slab, nb * tt, LANES), F32),
                        pltpu.VMEM((nb * tt, n_slab * LANES), F32),
                        pltpu.VMEM((1, LANES), F32)],
        compiler_params=_cparams("arbitrary", "arbitrary"),
        name="merge",
    )(x, gates, *attn_args, ssm_slabs, mod3, g_post, g_pre, wba, wbs, wout, wr_cat, wr_hi, br)


ISSUE_UNROLL = 8


def _dispatch_kernel(dest_ref, seg_ref, h_ref, xs_ref, zero_buf, sem, zsem):
    rows = h_ref.shape[0] // ROW_TILE
    blk = zero_buf.shape[0]
    n_blocks = xs_ref.shape[0] // blk
    base = pl.program_id(0) * rows

    @pl.when(pl.program_id(0) == 0)
    def _():
        zero_buf[...] = jnp.zeros_like(zero_buf)

        def zero_copy(row0):
            return pltpu.make_async_copy(zero_buf, xs_ref.at[pl.ds(pl.multiple_of(row0, blk), blk)],
                                         zsem)

        def fill_tail(e, carry):
            @pl.when(seg_ref[N_EXPERTS + e] > 0)
            def _():
                zero_copy(seg_ref[e] * ROW_TILE - blk).start()
            return carry

        def fill_unused(j, carry):
            zero_copy(j * blk).start()
            return carry

        def wait_tail(e, carry):
            @pl.when(seg_ref[N_EXPERTS + e] > 0)
            def _():
                zero_copy(0).wait()
            return carry

        def wait_unused(j, carry):
            zero_copy(0).wait()
            return carry

        n_used = seg_ref[2 * N_EXPERTS]
        lax.fori_loop(0, N_EXPERTS, fill_tail, 0)
        lax.fori_loop(n_used, n_blocks, fill_unused, 0)
        lax.fori_loop(0, N_EXPERTS, wait_tail, 0)
        lax.fori_loop(n_used, n_blocks, wait_unused, 0)

    group = ISSUE_UNROLL * ROW_TILE

    def issue(g, carry):
        g0 = pl.multiple_of(g * group, group)
        for rr in range(ISSUE_UNROLL):
            src = h_ref.at[pl.ds(g0 + rr * ROW_TILE, ROW_TILE)]
            for k in range(TOP_K):
                d = dest_ref[TOP_K * (base + g * ISSUE_UNROLL + rr) + k]
                dst = xs_ref.at[pl.ds(pl.multiple_of(d * ROW_TILE, ROW_TILE), ROW_TILE)]
                pltpu.make_async_copy(src, dst, sem).start(priority=k)
        return carry

    lax.fori_loop(0, rows // ISSUE_UNROLL, issue, 0)
    for _ in range(TOP_K):
        pltpu.make_async_copy(h_ref, xs_ref.at[pl.ds(0, rows * ROW_TILE)], sem).wait()


def _dispatch(h2_tiles, dest, seg_info, cap):
    n_tok = h2_tiles.shape[0] // ROW_TILE
    rows = DISPATCH_ROWS
    return pl.pallas_call(
        _dispatch_kernel,
        out_shape=jax.ShapeDtypeStruct((cap * ROW_TILE, LANES), h2_tiles.dtype),
        grid_spec=pltpu.PrefetchScalarGridSpec(
            num_scalar_prefetch=2,
            grid=(n_tok // rows,),
            in_specs=[pl.BlockSpec((rows * ROW_TILE, LANES), lambda i, dest, seg: (i, 0))],
            out_specs=pl.BlockSpec(memory_space=pl.ANY),
            scratch_shapes=[pltpu.VMEM((MOE_ROWS * ROW_TILE, LANES), h2_tiles.dtype),
                            pltpu.SemaphoreType.DMA, pltpu.SemaphoreType.DMA]),
        compiler_params=_cparams("arbitrary"),
        name="dispatch",
    )(dest, seg_info, h2_tiles)


def _expert_kernel(blk_expert_ref, n_used_ref, x_ref, w1_ref, w3_ref, w2_ref, y_ref,
                   w1b, w3b, w2b):
    i = pl.program_id(0)
    used = i < n_used_ref[0]
    rows = x_ref.shape[0] // ROW_TILE
    new_expert = jnp.logical_or(
        i == 0, blk_expert_ref[i] != blk_expert_ref[jnp.maximum(i - 1, 0)])

    @pl.when(jnp.logical_and(used, new_expert))
    def _():
        w1b[...] = w1_ref[0].astype(BF16)
        w3b[...] = w3_ref[0].astype(BF16)
        w2b[...] = w2_ref[0].astype(BF16)

    @pl.when(used)
    def _():
        xb = _unpack_rows(_load_row_tiles(x_ref, rows)).astype(BF16)
        h1 = _dot(xb, w1b[...])
        h3 = _dot(xb, w3b[...])
        act = (h1 * _sigmoid(h1)) * h3
        y = _dot(act.astype(BF16), w2b[...])
        _store_row_tiles(y_ref, _pack_rows(y))

    @pl.when(jnp.logical_not(used))
    def _():
        y_ref[...] = jnp.zeros_like(y_ref)


def _experts(xs, blk_expert, n_used, w1, w3, w2):
    d, de = w1.shape[-2:]
    assert d == 2 * ROW_TILE * LANES
    rows = MOE_ROWS
    tile_rows = rows * ROW_TILE
    xblk = lambda i, be, nu: (jnp.minimum(i, nu[0] - 1), 0)
    wblk = lambda i, be, nu: (be[i], 0, 0)
    return pl.pallas_call(
        _expert_kernel,
        out_shape=jax.ShapeDtypeStruct(xs.shape, U32),
        grid_spec=pltpu.PrefetchScalarGridSpec(
            num_scalar_prefetch=2,
            grid=(xs.shape[0] // tile_rows,),
            in_specs=[pl.BlockSpec((tile_rows, LANES), xblk),
                      pl.BlockSpec((1, d, de), wblk), pl.BlockSpec((1, d, de), wblk),
                      pl.BlockSpec((1, de, d), wblk)],
            out_specs=pl.BlockSpec((tile_rows, LANES), lambda i, be, nu: (i, 0)),
            scratch_shapes=[pltpu.VMEM((d, de), BF16), pltpu.VMEM((d, de), BF16),
                            pltpu.VMEM((de, d), BF16)]),
        compiler_params=_cparams("arbitrary"),
        name="experts",
    )(blk_expert, n_used, xs, w1, w3, w2)


def _combine_kernel(dest_ref, ys_ref, x1_ref, route_ref, mod_ref, g_ref, o_ref, ybuf, sem):
    rows = x1_ref.shape[0]
    step = pl.program_id(0)
    n_step = pl.num_programs(0)

    group = ISSUE_UNROLL * ROW_TILE

    def gather(tile, slot):
        base = tile * rows

        def issue(g, carry):
            g0 = pl.multiple_of(g * group, group)
            for rr in range(ISSUE_UNROLL):
                for k in range(TOP_K):
                    d = dest_ref[TOP_K * (base + g * ISSUE_UNROLL + rr) + k]
                    src = ys_ref.at[pl.ds(pl.multiple_of(d * ROW_TILE, ROW_TILE), ROW_TILE)]
                    dst = ybuf.at[slot, k, pl.ds(g0 + rr * ROW_TILE, ROW_TILE)]
                    pltpu.make_async_copy(src, dst, sem.at[slot]).start(priority=k)
            return carry

        lax.fori_loop(0, rows // ISSUE_UNROLL, issue, 0)

    slot = lax.rem(step, 2)

    @pl.when(step == 0)
    def _():
        gather(step, 0)

    @pl.when(step + 1 < n_step)
    def _():
        gather(step + 1, 1 - slot)

    for k in range(TOP_K):
        pltpu.make_async_copy(ys_ref.at[pl.ds(0, rows * ROW_TILE)], ybuf.at[slot, k],
                              sem.at[slot]).wait()
    route = route_ref[...]
    w0 = route[:, ROUTE_LANES["w0"]:ROUTE_LANES["w0"] + 1]
    w1 = route[:, ROUTE_LANES["w1"]:ROUTE_LANES["w1"] + 1]

    def rows_of(k):
        return _unpack_rows(_load_row_tiles(ybuf.at[slot, k], rows))

    y = rows_of(0) * w0 + rows_of(1) * w1
    o_ref[...] = x1_ref[...] + mod_ref[0, 5:6, :] * _rms_norm(y, g_ref[...])


def _combine(ys, dest, x1_flat, route_flat, mod3, g_post, seq):
    n_tok, d = x1_flat.shape
    rows = COMBINE_ROWS
    assert seq % rows == 0
    return pl.pallas_call(
        _combine_kernel,
        out_shape=jax.ShapeDtypeStruct((n_tok, d), F32),
        grid_spec=pltpu.PrefetchScalarGridSpec(
            num_scalar_prefetch=1,
            grid=(n_tok // rows,),
            in_specs=[pl.BlockSpec(memory_space=pl.ANY),
                      pl.BlockSpec((rows, d), lambda i, dest: (i, 0)),
                      pl.BlockSpec((rows, LANES), lambda i, dest: (i, 0)),
                      pl.BlockSpec((1, 6, d), lambda i, dest: (i * rows // seq, 0, 0)),
                      pl.BlockSpec((1, d), lambda i, dest: (0, 0))],
            out_specs=pl.BlockSpec((rows, d), lambda i, dest: (i, 0)),
            scratch_shapes=[pltpu.VMEM((2, TOP_K, rows * ROW_TILE, LANES), U32),
                            pltpu.SemaphoreType.DMA((2,))]),
        compiler_params=_cparams("arbitrary"),
        name="combine",
    )(dest, ys, x1_flat, route_flat, mod3, g_post)


def _moe_layout(route_flat, counts):
    rows = MOE_ROWS
    n_tok = route_flat.shape[0]
    ids = route_flat[:, 0:TOP_K].astype(jnp.int32)
    rank = route_flat[:, TOP_K:2 * TOP_K].astype(jnp.int32)
    counts = counts.astype(jnp.int32)
    padded = (counts + rows - 1) // rows * rows
    pad_ends = jnp.cumsum(padded)
    pad_starts = pad_ends - padded
    expert = jnp.arange(N_EXPERTS, dtype=jnp.int32)
    start_of = jnp.sum(jnp.where(ids[..., None] == expert, pad_starts, 0), axis=-1)
    dest = (start_of + rank).reshape(n_tok * TOP_K)
    n_blocks = (n_tok * TOP_K + N_EXPERTS * (rows - 1) + rows - 1) // rows
    blk_row0 = jnp.arange(n_blocks, dtype=jnp.int32) * rows
    blk_expert = jnp.minimum(jnp.sum(pad_ends[None, :] <= blk_row0[:, None], axis=-1),
                             N_EXPERTS - 1).astype(jnp.int32)
    n_used = (pad_ends[-1] // rows).reshape(1).astype(jnp.int32)
    seg_info = jnp.concatenate([pad_ends, padded, n_used]).astype(jnp.int32)
    return dest, blk_expert, n_used, seg_info, n_blocks * rows


def kernel(x, c, w_mod, b_mod, g_pre_mix, g_post_mix, g_pre_ffn, g_post_ffn, w_in, rel_bias, a_re, a_im, log_dt, ssm_b_re, ssm_b_im, ssm_c_re, ssm_c_im, d_skip, w_glu, b_glu, w_branch_attn, w_branch_ssm, w_out, w_router_group, b_router_group, w_router_expert, b_router_expert, w1, w3, w2):
    bsz, seq, d = x.shape
    depth = w_mod.shape[0]
    ssm_width = w_glu.shape[-1]
    n_pat = len(DILATION_PATTERNS)
    for l in range(depth):
        mod3 = _modulation(c, w_mod[l], b_mod[l]).reshape(bsz, 6, d)
        w_in_l = w_in[l].astype(BF16)
        g_pre = g_pre_mix[l].reshape(1, d)
        qkv = _qkv_projection(x, mod3, g_pre, w_in_l[:, :3 * ATTN_WIDTH])
        u_slabs, gates = _ugate_projection(x, mod3, g_pre, w_in_l[:, 3 * ATTN_WIDTH:], ssm_width)
        attn_outs = [_attention_pattern(*qkv[3 * p:3 * p + 3], rel_bias, DILATION_PATTERNS[p][1])
                     for p in range(n_pat)]
        ssm_slabs = _ssm_branch(u_slabs, bsz, a_re[l], a_im[l], log_dt[l], ssm_b_re[l],
                                ssm_b_im[l], ssm_c_re[l], ssm_c_im[l], d_skip[l], w_glu[l], b_glu[l])
        x1, h2, route, counts = _merge_and_route(
            x, gates, attn_outs, ssm_slabs, mod3, g_post_mix[l].reshape(1, d),
            g_pre_ffn[l].reshape(1, d), w_branch_attn[l].astype(BF16),
            w_branch_ssm[l].astype(BF16), w_out[l].astype(BF16),
            w_router_group[l], b_router_group[l], w_router_expert[l], b_router_expert[l])
        route_flat = route.reshape(bsz * seq, LANES)
        dest, blk_expert, n_used, seg_info, cap = _moe_layout(route_flat, counts[0, :N_EXPERTS])
        xs = _dispatch(h2.reshape(bsz * seq * ROW_TILE, LANES), dest, seg_info, cap)
        ys = _experts(xs, blk_expert, n_used, w1[l], w3[l], w2[l])
        x = _combine(ys, dest, x1.reshape(bsz * seq, d), route_flat, mod3,
                     g_post_ffn[l].reshape(1, d), seq).reshape(bsz, seq, d)
    return x
```

```python
import functools
import math

import numpy as np
import jax
import jax.numpy as jnp
from jax import lax
from jax.experimental import pallas as pl
from jax.experimental.pallas import tpu as pltpu

F32 = jnp.float32
BF16 = jnp.bfloat16

N_HEADS = 8
HEAD_DIM = 64
ATTN_WIDTH = N_HEADS * HEAD_DIM
DILATION_PATTERNS = ((128, 1), (512, 4), (2048, 16))
NUM_BUCKETS = 32
MAX_DISTANCE = 2048
N_EXPERT_GROUPS = 4
EXPERTS_PER_GROUP = 8
N_EXPERTS = N_EXPERT_GROUPS * EXPERTS_PER_GROUP
TOP_K = 2
RMS_EPS = 1e-6
NEG_INF = -1e30
LOG2_E = math.log2(math.e)
LN_2 = math.log(2.0)

LANES = 128
SUBLANES = 8
VMEM_LIMIT_BYTES = 56 * 1024 * 1024

ATTN_BLK = 128
ATTN_STEP_ROWS = 512
QKV_ROWS = 512
QKV_MID_DIL = 4
TIME_TILE = 128
MERGE_BATCH = 4
MERGE_CHUNK_BATCH = 2
SSM_STEPS = 64
MOE_ROWS = 512
DISPATCH_ROWS = 2048
COMBINE_ROWS = 512
ROW_TILE = 4
U32 = jnp.uint32
HI_HALF = 0xFFFF0000


def _pack_rows(x):
    w = x.shape[1] // 2
    lo = lax.bitcast_convert_type(x[:, :w].astype(BF16).astype(F32), U32) >> 16
    hi = lax.bitcast_convert_type(x[:, w:].astype(BF16).astype(F32), U32) & U32(HI_HALF)
    return hi | lo


def _unpack_rows(p):
    lo = lax.bitcast_convert_type(p << 16, F32)
    hi = lax.bitcast_convert_type(p & U32(HI_HALF), F32)
    return jnp.concatenate([lo, hi], axis=1)


def _store_row_tiles(ref, packed):
    rows = packed.shape[0]
    for s in range(ROW_TILE):
        ref[pl.ds(s, rows, stride=ROW_TILE), :] = packed[:, s * LANES:(s + 1) * LANES]


def _load_row_tiles(ref, rows):
    return _lane_concat([ref[pl.ds(s, rows, stride=ROW_TILE), :] for s in range(ROW_TILE)])


def _cparams(*sem):
    return pltpu.CompilerParams(dimension_semantics=sem, vmem_limit_bytes=VMEM_LIMIT_BYTES)


def _sigmoid(x):
    return 1.0 / (1.0 + jnp.exp(-x))


def _dot(a, b):
    return jnp.dot(a, b, preferred_element_type=F32)


def _split_bf16(a):
    hi = a.astype(BF16)
    lo = (a - hi.astype(F32)).astype(BF16)
    return hi, lo


def _dot_split(a, w_hi, w_lo):
    a_hi, a_lo = _split_bf16(a)
    return _dot(a_hi, w_hi) + _dot(a_lo, w_hi) + _dot(a_hi, w_lo)


def _rms_norm(x, gain):
    ms = jnp.mean(x * x, axis=-1, keepdims=True)
    return x * lax.rsqrt(ms + RMS_EPS) * gain


def _lane_concat(ref_slabs):
    return jnp.concatenate(ref_slabs, axis=-1)


def _mod_kernel(c_ref, w_ref, b_ref, o_ref):
    c = c_ref[...]
    a = c * _sigmoid(c)
    w_hi, w_lo = _split_bf16(w_ref[...])
    o_ref[...] = _dot_split(a, w_hi, w_lo) + b_ref[...]


def _modulation(c, w_mod, b_mod):
    bsz, d = c.shape
    n = w_mod.shape[1]
    tn = 1024
    return pl.pallas_call(
        _mod_kernel,
        out_shape=jax.ShapeDtypeStruct((bsz, n), F32),
        grid=(n // tn,),
        in_specs=[pl.BlockSpec((bsz, d), lambda j: (0, 0)),
                  pl.BlockSpec((d, tn), lambda j: (0, j)),
                  pl.BlockSpec((1, tn), lambda j: (0, j))],
        out_specs=pl.BlockSpec((bsz, tn), lambda j: (0, j)),
        compiler_params=_cparams("arbitrary"),
        name="mod",
    )(c, w_mod, b_mod.reshape(1, n))


def _qkv_kernel(x_ref, mod_ref, g_ref, w_ref, *rest):
    n_pat = len(DILATION_PATTERNS)
    out_refs, slab, mid = rest[:3 * n_pat], rest[3 * n_pat], rest[3 * n_pat + 1]
    h = _rms_norm(x_ref[0], g_ref[...]) * (1.0 + mod_ref[0, 1:2, :]) + mod_ref[0, 0:1, :]
    hb = h.astype(BF16)
    rows = hb.shape[0]
    per_tensor = ATTN_WIDTH // LANES
    mid_dil = QKV_MID_DIL

    def project(t):
        res = _dot(hb, w_ref[:, t * ATTN_WIDTH:(t + 1) * ATTN_WIDTH])
        if t == 0:
            res = res * (HEAD_DIM ** -0.5 * LOG2_E)
        for s in range(per_tensor):
            slab[t * per_tensor + s] = res[:, s * LANES:(s + 1) * LANES]

    def split(t):
        for p, (_, dil) in enumerate(DILATION_PATTERNS):
            sub = rows // dil
            out = out_refs[3 * p + t]
            for r in range(dil):
                pieces = []
                for s in range(per_tensor):
                    ts = t * per_tensor + s
                    if dil == 1:
                        piece = slab[ts]
                    elif dil == mid_dil:
                        piece = slab[ts, pl.ds(r, sub, stride=dil), :]
                        mid[ts, r] = piece
                    else:
                        ratio = dil // mid_dil
                        piece = mid[ts, r % mid_dil, pl.ds(r // mid_dil, sub, stride=ratio), :]
                    pieces.append(piece)
                out[0, r] = _lane_concat(pieces).astype(out.dtype)

    project(0)
    for t in range(3):
        if t + 1 < 3:
            project(t + 1)
        split(t)


def _qkv_projection(x, mod3, g_pre, w_qkv):
    bsz, seq, d = x.shape
    tm = QKV_ROWS
    out_shape, out_specs = [], []
    for _, dil in DILATION_PATTERNS:
        assert tm % (dil * 2 * SUBLANES) == 0
        for _ in range(3):
            out_shape.append(jax.ShapeDtypeStruct((bsz, dil, seq // dil, ATTN_WIDTH), BF16))
            out_specs.append(pl.BlockSpec((1, dil, tm // dil, ATTN_WIDTH),
                                          lambda b, i: (b, 0, i, 0)))
    return pl.pallas_call(
        _qkv_kernel,
        out_shape=out_shape,
        grid=(bsz, seq // tm),
        in_specs=[pl.BlockSpec((1, tm, d), lambda b, i: (b, i, 0)),
                  pl.BlockSpec((1, 6, d), lambda b, i: (b, 0, 0)),
                  pl.BlockSpec((1, d), lambda b, i: (0, 0)),
                  pl.BlockSpec(w_qkv.shape, lambda b, i: (0, 0))],
        out_specs=out_specs,
        scratch_shapes=[pltpu.VMEM((w_qkv.shape[1] // LANES, tm, LANES), F32),
                        pltpu.VMEM((w_qkv.shape[1] // LANES, QKV_MID_DIL, tm // QKV_MID_DIL, LANES),
                                   F32)],
        compiler_params=_cparams("arbitrary", "arbitrary"),
        name="qkv",
    )(x, mod3, g_pre, w_qkv)


def _ugate_kernel(x_ref, mod_ref, g_ref, w_ref, u_ref, gate_ref):
    bsz, tt, d = x_ref.shape
    shift = mod_ref[:, 0, :][:, None, :]
    scale = mod_ref[:, 1, :][:, None, :]
    h = _rms_norm(x_ref[...], g_ref[...]) * (1.0 + scale) + shift
    hb = h.reshape(bsz * tt, d).astype(BF16)
    n_slab = u_ref.shape[0]
    sw = n_slab * LANES
    u = _dot(hb, w_ref[:, 0:sw])
    for b in range(bsz):
        for s in range(n_slab):
            u_ref[s, pl.ds(b, tt, stride=bsz), :] = u[b * tt:(b + 1) * tt, s * LANES:(s + 1) * LANES]
    gw = gate_ref.shape[-1]
    chunk = 512
    for c0 in range(0, gw, chunk):
        g = _sigmoid(_dot(hb, w_ref[:, sw + c0:sw + c0 + chunk]))
        gate_ref[:, :, c0:c0 + chunk] = g.reshape(bsz, tt, chunk).astype(BF16)


def _ugate_projection(x, mod3, g_pre, w_ug, ssm_width):
    bsz, seq, d = x.shape
    tt = TIME_TILE
    gw = w_ug.shape[1] - ssm_width
    n_slab = ssm_width // LANES
    return pl.pallas_call(
        _ugate_kernel,
        out_shape=(jax.ShapeDtypeStruct((n_slab, seq * bsz, LANES), F32),
                   jax.ShapeDtypeStruct((bsz, seq, gw), BF16)),
        grid=(seq // tt,),
        in_specs=[pl.BlockSpec((bsz, tt, d), lambda i: (0, i, 0)),
                  pl.BlockSpec((bsz, 6, d), lambda i: (0, 0, 0)),
                  pl.BlockSpec((1, d), lambda i: (0, 0)),
                  pl.BlockSpec(w_ug.shape, lambda i: (0, 0))],
        out_specs=(pl.BlockSpec((n_slab, tt * bsz, LANES), lambda i: (0, i, 0)),
                   pl.BlockSpec((bsz, tt, gw), lambda i: (0, i, 0))),
        compiler_params=_cparams("arbitrary"),
        name="ugate",
    )(x, mod3, g_pre, w_ug)


def _t5_bucket_np(dist):
    exact = NUM_BUCKETS // 2
    d_f = np.maximum(dist, exact).astype(np.float32)
    large = exact + (np.log(d_f / np.float32(exact)) / np.float32(math.log(MAX_DISTANCE / exact))
                     * np.float32(NUM_BUCKETS - exact)).astype(np.int32)
    return np.where(dist < exact, dist, np.minimum(large, NUM_BUCKETS - 1))


def _bucket_map_t(dil):
    blk = ATTN_BLK
    ki = np.arange(2 * blk)[:, None]
    qi = np.arange(blk)[None, :]
    return _t5_bucket_np(np.maximum(blk + qi - ki, 0) * dil).astype(np.int32)


def _attn_kernel(relb_ref, q_ref, kc_ref, kp_ref, vc_ref, vp_ref, bucket_ref,
                 o_ref, lse_ref, kbuf, vbuf, bias_t, *, n_sub):
    blk = ATTN_BLK
    first_call = jnp.logical_and(pl.program_id(0) == 0,
                                 jnp.logical_and(pl.program_id(1) == 0, pl.program_id(2) == 0))

    @pl.when(first_call)
    def _():
        bucket = bucket_ref[...]
        ki = lax.broadcasted_iota(jnp.int32, bucket.shape, 0)
        qi = lax.broadcasted_iota(jnp.int32, bucket.shape, 1)
        dist = blk + qi - ki
        band = jnp.logical_and(dist >= 0, dist <= blk)
        band_first = jnp.logical_and(band, ki >= blk)

        def per_head(h, carry):
            acc = jnp.zeros(bucket.shape, F32)
            for b in range(NUM_BUCKETS):
                acc = jnp.where(bucket == b, relb_ref[b, h] * LOG2_E, acc)
            bias_t[0, h] = jnp.where(band_first, acc, NEG_INF)
            bias_t[1, h] = jnp.where(band, acc, NEG_INF)
            return carry

        lax.fori_loop(0, N_HEADS, per_head, 0)

    first_variant = jnp.where(pl.program_id(2) == 0, 0, 1)
    n_res = q_ref.shape[1]
    for g in range(n_res):
        kbuf[g, 0:blk, :] = kp_ref[0, g]
        kbuf[g, blk:, :] = kc_ref[0, g]
        vbuf[g, 0:blk, :] = vp_ref[0, g]
        vbuf[g, blk:, :] = vc_ref[0, g]

    lane = lax.broadcasted_iota(jnp.int32, (1, LANES), 1)
    lo_half = lane < HEAD_DIM
    bd_row = lax.broadcasted_iota(jnp.int32, (4 * blk, LANES), 0)
    bd_col = lax.broadcasted_iota(jnp.int32, (4 * blk, LANES), 1)
    ones_bd = ((bd_row < 2 * blk) == (bd_col < HEAD_DIM)).astype(F32).astype(BF16)
    contract_last = (((1,), (1,)), ((), ()))
    contract_first = (((0,), (0,)), ((), ()))

    def sub_block(g, i):
        r0 = i * blk
        q = q_ref[0, g, r0:r0 + blk, :]
        kk = kbuf[g, r0:r0 + 2 * blk, :]
        vv = vbuf[g, r0:r0 + 2 * blk, :]
        variant = first_variant if i == 0 else 1
        for j in range(N_HEADS // 2):
            cols = slice(j * LANES, (j + 1) * LANES)
            qj, kj, vj = q[:, cols], kk[:, cols], vv[:, cols]
            probs_t, maxes = [], []
            for hh in range(2):
                sel = lo_half if hh == 0 else jnp.logical_not(lo_half)
                qm = jnp.where(sel, qj, jnp.zeros_like(qj))
                s_t = lax.dot_general(kj, qm, contract_last, preferred_element_type=F32)
                s_t = s_t + bias_t[variant, 2 * j + hh]
                m = jnp.max(s_t, axis=0, keepdims=True)
                probs_t.append(jnp.exp2(s_t - m).astype(BF16))
                maxes.append(m)
            p2_t = jnp.concatenate(probs_t, axis=0)
            v_bd = jnp.concatenate([jnp.where(lo_half, vj, jnp.zeros_like(vj)),
                                    jnp.where(lo_half, jnp.zeros_like(vj), vj)], axis=0)
            rhs = jnp.concatenate([v_bd, ones_bd], axis=1)
            ol = lax.dot_general(p2_t, rhs, contract_first, preferred_element_type=F32)
            o2, l2 = ol[:, :LANES], ol[:, LANES:]
            m_t = jnp.concatenate([jnp.broadcast_to(maxes[0], (HEAD_DIM, blk)),
                                   jnp.broadcast_to(maxes[1], (HEAD_DIM, blk))], axis=0)
            o_ref[0, g, r0:r0 + blk, cols] = (o2 / l2).astype(o_ref.dtype)
            lse_ref[0, g, r0:r0 + blk, cols] = (m_t.T + jnp.log2(l2)) * LN_2

    for g in range(n_res):
        for i in range(n_sub):
            sub_block(g, i)


def _attention_pattern(q, k, v, rel_bias, dil):
    bsz, _, sub_len, aw = q.shape
    blk = ATTN_BLK
    assert sub_len % blk == 0
    tq = min(ATTN_STEP_ROWS, sub_len)
    n_sub = tq // blk
    ratio = tq // blk
    n_res = min(dil, ATTN_STEP_ROWS // tq)
    cur = lambda b, r, n: (b, r, n, 0)
    prev = lambda b, r, n: (b, r, jnp.maximum(n * ratio - 1, 0), 0)
    blk_cur = pl.BlockSpec((1, n_res, tq, aw), cur)
    blk_prev = pl.BlockSpec((1, n_res, blk, aw), prev)
    bucket = jnp.asarray(_bucket_map_t(dil))
    o_dtype = BF16 if TIME_TILE // dil >= 2 * SUBLANES else F32
    return pl.pallas_call(
        functools.partial(_attn_kernel, n_sub=n_sub),
        out_shape=(jax.ShapeDtypeStruct(q.shape, o_dtype), jax.ShapeDtypeStruct(q.shape, F32)),
        grid=(bsz, dil // n_res, sub_len // tq),
        in_specs=[pl.BlockSpec(memory_space=pltpu.SMEM),
                  blk_cur, blk_cur, blk_prev, blk_cur, blk_prev,
                  pl.BlockSpec(bucket.shape, lambda b, r, n: (0, 0))],
        out_specs=(blk_cur, blk_cur),
        scratch_shapes=[pltpu.VMEM((n_res, tq + blk, aw), BF16),
                        pltpu.VMEM((n_res, tq + blk, aw), BF16),
                        pltpu.VMEM((2, N_HEADS, 2 * blk, blk), F32)],
        compiler_params=_cparams("arbitrary", "arbitrary", "arbitrary"),
        name=f"attn_dil{dil}",
    )(rel_bias.astype(F32), q, k, k, v, v, bucket)


def _ssm_kernel(u_ref, bmat_ref, cmat_ref, ar_ref, ai_ref, dskip_ref, wglu_ref, bglu_ref,
                o_ref, hbuf, hstate, *, n_steps):
    @pl.when(pl.program_id(0) == 0)
    def _():
        hstate[...] = jnp.zeros_like(hstate)

    n_slab = u_ref.shape[0]
    n_state = hbuf.shape[1] // 2
    per = n_state // n_slab
    us = [u_ref[s] for s in range(n_slab)]
    for s in range(n_slab):
        bu = _dot(us[s].astype(BF16), bmat_ref[s])
        hbuf[:, s * per:(s + 1) * per] = bu[:, :per]
        hbuf[:, n_state + s * per:n_state + (s + 1) * per] = bu[:, per:]
    for s in range(n_slab):
        re_cols = slice(s * per, (s + 1) * per)
        im_cols = slice(n_state + s * per, n_state + (s + 1) * per)
        ar = ar_ref[:, re_cols]
        ai = ai_ref[:, re_cols]
        hr = hstate[:, re_cols]
        hi = hstate[:, im_cols]
        for t in range(n_steps):
            trow = slice(t * SUBLANES, (t + 1) * SUBLANES)
            nr = ar * hr - ai * hi + hbuf[trow, re_cols]
            ni = ar * hi + ai * hr + hbuf[trow, im_cols]
            hbuf[trow, re_cols] = nr
            hbuf[trow, im_cols] = ni
            hr, hi = nr, ni
        hstate[:, re_cols] = hr
        hstate[:, im_cols] = hi

    ys = []
    for s in range(n_slab):
        h_s = _lane_concat([hbuf[:, s * per:(s + 1) * per],
                            hbuf[:, n_state + s * per:n_state + (s + 1) * per]])
        ys.append(_dot(h_s.astype(BF16), cmat_ref[s])
                  + dskip_ref[:, s * LANES:(s + 1) * LANES] * us[s])
    y = _lane_concat(ys)
    y = 0.5 * y * (1.0 + jnp.tanh(math.sqrt(2.0 / math.pi) * (y + 0.044715 * (y * y * y))))
    z = _dot(y.astype(BF16), wglu_ref[...]) + bglu_ref[...]
    out = y * _sigmoid(z)
    for s in range(n_slab):
        o_ref[s] = out[:, s * LANES:(s + 1) * LANES]


def _ssm_params(a_re, a_im, log_dt, b_re, b_im, c_re, c_im, bsz):
    g, p = a_re.shape
    hg = b_re.shape[-1]
    dt = jnp.exp(log_dt.astype(F32))[:, None]
    a_re, a_im = a_re.astype(F32), a_im.astype(F32)
    mag = jnp.exp(a_re * dt)
    abar_re = mag * jnp.cos(a_im * dt)
    abar_im = mag * jnp.sin(a_im * dt)
    den = a_re * a_re + a_im * a_im
    q_re = ((abar_re - 1.0) * a_re + abar_im * a_im) / den
    q_im = (abar_im * a_re - (abar_re - 1.0) * a_im) / den
    b_re, b_im = b_re.astype(F32), b_im.astype(F32)
    bb_re = q_re[..., None] * b_re - q_im[..., None] * b_im
    bb_im = q_re[..., None] * b_im + q_im[..., None] * b_re
    gs = LANES // hg
    n_slab = g // gs
    eye = jnp.eye(gs, dtype=F32)

    def in_mat(t):
        t = t.reshape(n_slab, gs, p, hg)
        return jnp.einsum('sgph,gk->sghkp', t, eye).reshape(n_slab, gs * hg, gs * p)

    def out_mat(t):
        t = t.reshape(n_slab, gs, hg, p)
        return jnp.einsum('sghp,gk->sgpkh', t, eye).reshape(n_slab, gs * p, gs * hg)

    bmat = jnp.concatenate([in_mat(bb_re), in_mat(bb_im)], axis=2).astype(BF16)
    cmat = jnp.concatenate([out_mat(c_re.astype(F32)), -out_mat(c_im.astype(F32))],
                           axis=1).astype(BF16)
    ar = jnp.broadcast_to(abar_re.reshape(1, g * p), (bsz, g * p))
    ai = jnp.broadcast_to(abar_im.reshape(1, g * p), (bsz, g * p))
    return bmat, cmat, ar, ai


def _ssm_branch(u_slabs, bsz, a_re, a_im, log_dt, b_re, b_im, c_re, c_im, d_skip, w_glu, b_glu):
    n_slab, n_rows, _ = u_slabs.shape
    width = n_slab * LANES
    assert bsz == SUBLANES
    bmat, cmat, ar, ai = _ssm_params(a_re, a_im, log_dt, b_re, b_im, c_re, c_im, bsz)
    n_state2 = n_slab * bmat.shape[2]
    rows = SSM_STEPS * bsz
    const = lambda c: (0, 0)
    const3 = lambda c: (0, 0, 0)
    slab_spec = pl.BlockSpec((n_slab, rows, LANES), lambda c: (0, c, 0))
    return pl.pallas_call(
        functools.partial(_ssm_kernel, n_steps=SSM_STEPS),
        out_shape=jax.ShapeDtypeStruct(u_slabs.shape, F32),
        grid=(n_rows // rows,),
        in_specs=[slab_spec,
                  pl.BlockSpec(bmat.shape, const3), pl.BlockSpec(cmat.shape, const3),
                  pl.BlockSpec(ar.shape, const), pl.BlockSpec(ai.shape, const),
                  pl.BlockSpec((1, width), const), pl.BlockSpec((width, width), const),
                  pl.BlockSpec((1, width), const)],
        out_specs=slab_spec,
        scratch_shapes=[pltpu.VMEM((rows, n_state2), F32), pltpu.VMEM((bsz, n_state2), F32)],
        compiler_params=_cparams("arbitrary"),
        name="ssm",
    )(u_slabs, bmat, cmat, ar, ai, d_skip.reshape(1, width).astype(F32),
      w_glu.astype(BF16), b_glu.reshape(1, width).astype(F32))


ROUTE_LANES = {"id0": 0, "id1": 1, "rank0": 2, "rank1": 3, "w0": 4, "w1": 5}
GROUP_LANE0 = N_EXPERTS


def _merge_kernel(*refs, bsz_total):
    n_pat = len(DILATION_PATTERNS)
    x_ref, gate_ref = refs[0:2]
    attn_refs = refs[2:2 + 2 * n_pat]
    (ssm_ref, mod_ref, gpost_ref, gpre_ref, wba_ref, wbs_ref, wout_ref,
     wr_cat_ref, wr_hi_ref, br_ref) = refs[2 + 2 * n_pat:12 + 2 * n_pat]
    x1_ref, h2_ref, route_ref, count_ref = refs[12 + 2 * n_pat:16 + 2 * n_pat]
    scratch = refs[16 + 2 * n_pat:]
    carry = scratch[-1]

    @pl.when(jnp.logical_and(pl.program_id(0) == 0, pl.program_id(1) == 0))
    def _():
        carry[...] = jnp.zeros_like(carry)

    nb, tt, d = x_ref.shape
    n_slab = ssm_ref.shape[0]
    b0 = pl.program_id(0) * nb
    cb = MERGE_CHUNK_BATCH
    rows = cb * tt
    lane = lax.broadcasted_iota(jnp.int32, (rows, LANES), 1).astype(F32)
    row = lax.broadcasted_iota(jnp.int32, (rows, rows), 0)
    col = lax.broadcasted_iota(jnp.int32, (rows, rows), 1)
    strict_lower = (col < row).astype(BF16)
    running = [carry[...]]

    def chunk_phases(c0):
        bbs = range(c0, c0 + cb)
        o_tok, lse_tok, ssm_tok = scratch[3 * (c0 // cb):3 * (c0 // cb) + 3]

        for bb in bbs:
            for s in range(n_slab):
                ssm_tok[(bb - c0) * tt:(bb - c0 + 1) * tt, s * LANES:(s + 1) * LANES] = (
                    ssm_ref[s, pl.ds(b0 + bb, tt, stride=bsz_total), :])

        slot = 0
        sources = []
        for p, (_, dil) in enumerate(DILATION_PATTERNS):
            o_ref, lse_ref = attn_refs[2 * p], attn_refs[2 * p + 1]
            if dil == 1:
                sources.append((o_ref, lse_ref, None))
                continue
            sub = tt // dil
            for bb in bbs:
                for r in range(dil):
                    o_blk = o_ref[bb, r].astype(F32)
                    l_blk = lse_ref[bb, r]
                    for s in range(n_slab):
                        dst = pl.ds((bb - c0) * tt + r, sub, stride=dil)
                        o_tok[slot, s, dst, :] = o_blk[:, s * LANES:(s + 1) * LANES]
                        lse_tok[slot, s, dst, :] = l_blk[:, s * LANES:(s + 1) * LANES]
            sources.append((o_ref, lse_ref, slot))
            slot += 1
        attn_slabs = []
        for s in range(n_slab):
            cols = slice(s * LANES, (s + 1) * LANES)
            o_ps, lse_ps = [], []
            for o_ref, lse_ref, src_slot in sources:
                if src_slot is None:
                    o_ps.append(o_ref[c0:c0 + cb, 0, :, cols].astype(F32).reshape(rows, LANES))
                    lse_ps.append(lse_ref[c0:c0 + cb, 0, :, cols].reshape(rows, LANES))
                else:
                    o_ps.append(o_tok[src_slot, s])
                    lse_ps.append(lse_tok[src_slot, s])
            m = functools.reduce(jnp.maximum, lse_ps)
            es = [jnp.exp(l - m) for l in lse_ps]
            num = functools.reduce(lambda a, b: a + b, [e * o for e, o in zip(es, o_ps)])
            den = functools.reduce(lambda a, b: a + b, es)
            attn_slabs.append(num / den)
        attn_b = _lane_concat(attn_slabs).astype(BF16)
        ssm_b = ssm_tok[...].astype(BF16)
        yield
        branch_attn = _dot(attn_b, wba_ref[...])
        branch_ssm = _dot(ssm_b, wbs_ref[...])
        yield
        g_attn = gate_ref[c0:c0 + cb, :, 0:d].astype(F32).reshape(rows, d)
        g_ssm = gate_ref[c0:c0 + cb, :, d:].astype(F32).reshape(rows, d)
        merged_b = (g_attn * branch_attn + g_ssm * branch_ssm).astype(BF16)
        yield
        y = _dot(merged_b, wout_ref[...])
        yield
        gate1 = mod_ref[c0:c0 + cb, 2, :][:, None, :]
        shift2 = mod_ref[c0:c0 + cb, 3, :][:, None, :]
        scale2 = mod_ref[c0:c0 + cb, 4, :][:, None, :]
        x1 = x_ref[c0:c0 + cb] + gate1 * _rms_norm(y, gpost_ref[...]).reshape(cb, tt, d)
        x1_ref[c0:c0 + cb] = x1
        h2 = _rms_norm(x1, gpre_ref[...]) * (1.0 + scale2) + shift2
        for j, bb in enumerate(bbs):
            _store_row_tiles(h2_ref.at[bb], _pack_rows(h2[j]))

        a_hi, a_lo = _split_bf16(h2.reshape(rows, d))
        yield
        hi_pass = _dot(a_hi, wr_cat_ref[...])
        lo_pass = _dot(a_lo, wr_hi_ref[...])
        yield
        logits = hi_pass[:, :LANES] + lo_pass + hi_pass[:, LANES:] + br_ref[...]
        big = float(LANES)
        is_group = jnp.logical_and(lane >= GROUP_LANE0, lane < GROUP_LANE0 + N_EXPERT_GROUPS)
        gl = jnp.where(is_group, logits, -jnp.inf)
        g_max = jnp.max(gl, axis=-1, keepdims=True)
        g_sel = jnp.min(jnp.where(gl == g_max, lane, big), axis=-1, keepdims=True) - GROUP_LANE0
        g_gate = 1.0 / jnp.sum(jnp.exp(gl - g_max), axis=-1, keepdims=True)
        lo = g_sel * EXPERTS_PER_GROUP
        in_group = jnp.logical_and(lane >= lo, lane < lo + EXPERTS_PER_GROUP)
        el = jnp.where(in_group, logits, -jnp.inf)
        t0 = jnp.max(el, axis=-1, keepdims=True)
        i0 = jnp.min(jnp.where(el == t0, lane, big), axis=-1, keepdims=True)
        el1 = jnp.where(lane == i0, -jnp.inf, el)
        t1 = jnp.max(el1, axis=-1, keepdims=True)
        i1 = jnp.min(jnp.where(el1 == t1, lane, big), axis=-1, keepdims=True)
        e = jnp.exp(t1 - t0)
        w0 = g_gate / (1.0 + e)
        w1 = g_gate * e / (1.0 + e)

        hit0 = lane == i0
        hit1 = lane == i1
        onehot = jnp.logical_or(hit0, hit1).astype(F32)
        yield
        before = _dot(strict_lower, onehot.astype(BF16)) + running[0]
        rank0 = jnp.sum(jnp.where(hit0, before, 0.0), axis=-1, keepdims=True)
        rank1 = jnp.sum(jnp.where(hit1, before, 0.0), axis=-1, keepdims=True)
        running[0] = running[0] + jnp.sum(onehot, axis=0, keepdims=True)

        route = jnp.zeros((rows, LANES), F32)
        for name, val in (("id0", i0), ("id1", i1), ("rank0", rank0), ("rank1", rank1),
                          ("w0", w0), ("w1", w1)):
            route = jnp.where(lane == ROUTE_LANES[name], val, route)
        route_ref[c0:c0 + cb] = route.reshape(cb, tt, LANES)

    pending = [chunk_phases(c0) for c0 in range(0, nb, cb)]
    active = []
    while pending or active:
        if pending:
            active.append(pending.pop(0))
        for gen in list(active):
            if next(gen, StopIteration) is StopIteration:
                active.remove(gen)

    carry[...] = running[0]
    count_ref[...] = jnp.broadcast_to(running[0], count_ref.shape)


def _merge_and_route(x, gates, attn_outs, ssm_slabs, mod3, g_post, g_pre, wba, wbs, wout,
                     w_rg, b_rg, w_re, b_re):
    bsz, seq, d = x.shape
    tt, nb = TIME_TILE, MERGE_BATCH
    aw = ATTN_WIDTH
    n_slab = ssm_slabs.shape[0]
    assert aw == n_slab * LANES
    wr = jnp.zeros((d, LANES), F32).at[:, :N_EXPERTS].set(w_re.astype(F32))
    wr = wr.at[:, GROUP_LANE0:GROUP_LANE0 + N_EXPERT_GROUPS].set(w_rg.astype(F32))
    br = jnp.zeros((1, LANES), F32).at[0, :N_EXPERTS].set(b_re.astype(F32))
    br = br.at[0, GROUP_LANE0:GROUP_LANE0 + N_EXPERT_GROUPS].set(b_rg.astype(F32))
    wr_hi, wr_lo = _split_bf16(wr)
    wr_cat = jnp.concatenate([wr_hi, wr_lo], axis=1)
    n_strided = sum(1 for _, dil in DILATION_PATTERNS if dil > 1)
    chunk_rows = MERGE_CHUNK_BATCH * tt
    tok = lambda h, i: (h, i, 0)
    const = lambda h, i: (0, 0)
    attn_args, attn_specs = [], []
    for (o_p, lse_p), (_, dil) in zip(attn_outs, DILATION_PATTERNS):
        spec = pl.BlockSpec((nb, dil, tt // dil, aw), lambda h, i: (h, 0, i, 0))
        attn_args += [o_p, lse_p]
        attn_specs += [spec, spec]
    return pl.pallas_call(
        functools.partial(_merge_kernel, bsz_total=bsz),
        out_shape=(jax.ShapeDtypeStruct((bsz, seq, d), F32),
                   jax.ShapeDtypeStruct((bsz, seq * ROW_TILE, LANES), U32),
                   jax.ShapeDtypeStruct((bsz, seq, LANES), F32),
                   jax.ShapeDtypeStruct((SUBLANES, LANES), F32)),
        grid=(bsz // nb, seq // tt),
        in_specs=[pl.BlockSpec((nb, tt, d), tok),
                  pl.BlockSpec((nb, tt, gates.shape[-1]), tok)]
                 + attn_specs
                 + [pl.BlockSpec((n_slab, tt * bsz, LANES), lambda h, i: (0, i, 0)),
                    pl.BlockSpec((nb, 6, d), lambda h, i: (h, 0, 0)),
                    pl.BlockSpec((1, d), const), pl.BlockSpec((1, d), const),
                    pl.BlockSpec(wba.shape, const), pl.BlockSpec(wbs.shape, const),
                    pl.BlockSpec(wout.shape, const),
                    pl.BlockSpec((d, 2 * LANES), const), pl.BlockSpec((d, LANES), const),
                    pl.BlockSpec((1, LANES), const)],
        out_specs=(pl.BlockSpec((nb, tt, d), tok), pl.BlockSpec((nb, tt * ROW_TILE, LANES), tok),
                   pl.BlockSpec((nb, tt, LANES), tok),
                   pl.BlockSpec((SUBLANES, LANES), const)),
        scratch_shapes=[pltpu.VMEM((n_strided, n_slab, chunk_rows, LANES), F32),
                        pltpu.VMEM((n_strided, n_slab, chunk_rows, LANES), F32),
                        pltpu.VMEM((chunk_rows, n_slab * LANES), F32)] * (nb // MERGE_CHUNK_BATCH)
                       + [pltpu.VMEM((1, LANES), F32)],
        compiler_params=_cparams("arbitrary", "arbitrary"),
        name="merge",
    )(x, gates, *attn_args, ssm_slabs, mod3, g_post, g_pre, wba, wbs, wout, wr_cat, wr_hi, br)


ISSUE_UNROLL = 8


def _dispatch_kernel(dest_ref, seg_ref, h_ref, xs_ref, zero_buf, sem, zsem):
    rows = h_ref.shape[0] // ROW_TILE
    blk = zero_buf.shape[0]
    n_blocks = xs_ref.shape[0] // blk
    base = pl.program_id(0) * rows

    @pl.when(pl.program_id(0) == 0)
    def _():
        zero_buf[...] = jnp.zeros_like(zero_buf)

        def zero_copy(row0):
            return pltpu.make_async_copy(zero_buf, xs_ref.at[pl.ds(pl.multiple_of(row0, blk), blk)],
                                         zsem)

        def fill_tail(e, carry):
            @pl.when(seg_ref[N_EXPERTS + e] > 0)
            def _():
                zero_copy(seg_ref[e] * ROW_TILE - blk).start()
            return carry

        def fill_unused(j, carry):
            zero_copy(j * blk).start()
            return carry

        def wait_tail(e, carry):
            @pl.when(seg_ref[N_EXPERTS + e] > 0)
            def _():
                zero_copy(0).wait()
            return carry

        def wait_unused(j, carry):
            zero_copy(0).wait()
            return carry

        n_used = seg_ref[2 * N_EXPERTS]
        lax.fori_loop(0, N_EXPERTS, fill_tail, 0)
        lax.fori_loop(n_used, n_blocks, fill_unused, 0)
        lax.fori_loop(0, N_EXPERTS, wait_tail, 0)
        lax.fori_loop(n_used, n_blocks, wait_unused, 0)

    group = ISSUE_UNROLL * ROW_TILE

    def issue(g, carry):
        g0 = pl.multiple_of(g * group, group)
        for rr in range(ISSUE_UNROLL):
            src = h_ref.at[pl.ds(g0 + rr * ROW_TILE, ROW_TILE)]
            for k in range(TOP_K):
                d = dest_ref[TOP_K * (base + g * ISSUE_UNROLL + rr) + k]
                dst = xs_ref.at[pl.ds(pl.multiple_of(d * ROW_TILE, ROW_TILE), ROW_TILE)]
                pltpu.make_async_copy(src, dst, sem).start(priority=k)
        return carry

    lax.fori_loop(0, rows // ISSUE_UNROLL, issue, 0)
    for _ in range(TOP_K):
        pltpu.make_async_copy(h_ref, xs_ref.at[pl.ds(0, rows * ROW_TILE)], sem).wait()


def _dispatch(h2_tiles, dest, seg_info, cap):
    n_tok = h2_tiles.shape[0] // ROW_TILE
    rows = DISPATCH_ROWS
    return pl.pallas_call(
        _dispatch_kernel,
        out_shape=jax.ShapeDtypeStruct((cap * ROW_TILE, LANES), h2_tiles.dtype),
        grid_spec=pltpu.PrefetchScalarGridSpec(
            num_scalar_prefetch=2,
            grid=(n_tok // rows,),
            in_specs=[pl.BlockSpec((rows * ROW_TILE, LANES), lambda i, dest, seg: (i, 0))],
            out_specs=pl.BlockSpec(memory_space=pl.ANY),
            scratch_shapes=[pltpu.VMEM((MOE_ROWS * ROW_TILE, LANES), h2_tiles.dtype),
                            pltpu.SemaphoreType.DMA, pltpu.SemaphoreType.DMA]),
        compiler_params=_cparams("arbitrary"),
        name="dispatch",
    )(dest, seg_info, h2_tiles)


def _expert_kernel(blk_expert_ref, n_used_ref, x_ref, w1_ref, w3_ref, w2_ref, y_ref,
                   w1b, w3b, w2b):
    i = pl.program_id(0)
    used = i < n_used_ref[0]
    rows = x_ref.shape[0] // ROW_TILE
    new_expert = jnp.logical_or(
        i == 0, blk_expert_ref[i] != blk_expert_ref[jnp.maximum(i - 1, 0)])

    @pl.when(jnp.logical_and(used, new_expert))
    def _():
        w1b[...] = w1_ref[0].astype(BF16)
        w3b[...] = w3_ref[0].astype(BF16)
        w2b[...] = w2_ref[0].astype(BF16)

    @pl.when(used)
    def _():
        xb = _unpack_rows(_load_row_tiles(x_ref, rows)).astype(BF16)
        h1 = _dot(xb, w1b[...])
        h3 = _dot(xb, w3b[...])
        act = (h1 * _sigmoid(h1)) * h3
        y = _dot(act.astype(BF16), w2b[...])
        _store_row_tiles(y_ref, _pack_rows(y))

    @pl.when(jnp.logical_not(used))
    def _():
        y_ref[...] = jnp.zeros_like(y_ref)


def _experts(xs, blk_expert, n_used, w1, w3, w2):
    d, de = w1.shape[-2:]
    assert d == 2 * ROW_TILE * LANES
    rows = MOE_ROWS
    tile_rows = rows * ROW_TILE
    xblk = lambda i, be, nu: (jnp.minimum(i, nu[0] - 1), 0)
    wblk = lambda i, be, nu: (be[i], 0, 0)
    return pl.pallas_call(
        _expert_kernel,
        out_shape=jax.ShapeDtypeStruct(xs.shape, U32),
        grid_spec=pltpu.PrefetchScalarGridSpec(
            num_scalar_prefetch=2,
            grid=(xs.shape[0] // tile_rows,),
            in_specs=[pl.BlockSpec((tile_rows, LANES), xblk),
                      pl.BlockSpec((1, d, de), wblk), pl.BlockSpec((1, d, de), wblk),
                      pl.BlockSpec((1, de, d), wblk)],
            out_specs=pl.BlockSpec((tile_rows, LANES), lambda i, be, nu: (i, 0)),
            scratch_shapes=[pltpu.VMEM((d, de), BF16), pltpu.VMEM((d, de), BF16),
                            pltpu.VMEM((de, d), BF16)]),
        compiler_params=_cparams("arbitrary"),
        name="experts",
    )(blk_expert, n_used, xs, w1, w3, w2)


def _combine_kernel(dest_ref, ys_ref, x1_ref, route_ref, mod_ref, g_ref, o_ref, ybuf, sem):
    rows = x1_ref.shape[0]
    step = pl.program_id(0)
    n_step = pl.num_programs(0)

    group = ISSUE_UNROLL * ROW_TILE

    def gather(tile, slot):
        base = tile * rows

        def issue(g, carry):
            g0 = pl.multiple_of(g * group, group)
            for rr in range(ISSUE_UNROLL):
                for k in range(TOP_K):
                    d = dest_ref[TOP_K * (base + g * ISSUE_UNROLL + rr) + k]
                    src = ys_ref.at[pl.ds(pl.multiple_of(d * ROW_TILE, ROW_TILE), ROW_TILE)]
                    dst = ybuf.at[slot, k, pl.ds(g0 + rr * ROW_TILE, ROW_TILE)]
                    pltpu.make_async_copy(src, dst, sem.at[slot]).start(priority=k)
            return carry

        lax.fori_loop(0, rows // ISSUE_UNROLL, issue, 0)

    slot = lax.rem(step, 2)

    @pl.when(step == 0)
    def _():
        gather(step, 0)

    @pl.when(step + 1 < n_step)
    def _():
        gather(step + 1, 1 - slot)

    for k in range(TOP_K):
        pltpu.make_async_copy(ys_ref.at[pl.ds(0, rows * ROW_TILE)], ybuf.at[slot, k],
                              sem.at[slot]).wait()
    route = route_ref[...]
    w0 = route[:, ROUTE_LANES["w0"]:ROUTE_LANES["w0"] + 1]
    w1 = route[:, ROUTE_LANES["w1"]:ROUTE_LANES["w1"] + 1]

    def rows_of(k):
        return _unpack_rows(_load_row_tiles(ybuf.at[slot, k], rows))

    y = rows_of(0) * w0 + rows_of(1) * w1
    o_ref[...] = x1_ref[...] + mod_ref[0, 5:6, :] * _rms_norm(y, g_ref[...])


def _combine(ys, dest, x1_flat, route_flat, mod3, g_post, seq):
    n_tok, d = x1_flat.shape
    rows = COMBINE_ROWS
    assert seq % rows == 0
    return pl.pallas_call(
        _combine_kernel,
        out_shape=jax.ShapeDtypeStruct((n_tok, d), F32),
        grid_spec=pltpu.PrefetchScalarGridSpec(
            num_scalar_prefetch=1,
            grid=(n_tok // rows,),
            in_specs=[pl.BlockSpec(memory_space=pl.ANY),
                      pl.BlockSpec((rows, d), lambda i, dest: (i, 0)),
                      pl.BlockSpec((rows, LANES), lambda i, dest: (i, 0)),
                      pl.BlockSpec((1, 6, d), lambda i, dest: (i * rows // seq, 0, 0)),
                      pl.BlockSpec((1, d), lambda i, dest: (0, 0))],
            out_specs=pl.BlockSpec((rows, d), lambda i, dest: (i, 0)),
            scratch_shapes=[pltpu.VMEM((2, TOP_K, rows * ROW_TILE, LANES), U32),
                            pltpu.SemaphoreType.DMA((2,))]),
        compiler_params=_cparams("arbitrary"),
        name="combine",
    )(dest, ys, x1_flat, route_flat, mod3, g_post)


def _moe_layout(route_flat, counts):
    rows = MOE_ROWS
    n_tok = route_flat.shape[0]
    ids = route_flat[:, 0:TOP_K].astype(jnp.int32)
    rank = route_flat[:, TOP_K:2 * TOP_K].astype(jnp.int32)
    counts = counts.astype(jnp.int32)
    padded = (counts + rows - 1) // rows * rows
    pad_ends = jnp.cumsum(padded)
    pad_starts = pad_ends - padded
    expert = jnp.arange(N_EXPERTS, dtype=jnp.int32)
    start_of = jnp.sum(jnp.where(ids[..., None] == expert, pad_starts, 0), axis=-1)
    dest = (start_of + rank).reshape(n_tok * TOP_K)
    n_blocks = (n_tok * TOP_K + N_EXPERTS * (rows - 1) + rows - 1) // rows
    blk_row0 = jnp.arange(n_blocks, dtype=jnp.int32) * rows
    blk_expert = jnp.minimum(jnp.sum(pad_ends[None, :] <= blk_row0[:, None], axis=-1),
                             N_EXPERTS - 1).astype(jnp.int32)
    n_used = (pad_ends[-1] // rows).reshape(1).astype(jnp.int32)
    seg_info = jnp.concatenate([pad_ends, padded, n_used]).astype(jnp.int32)
    return dest, blk_expert, n_used, seg_info, n_blocks * rows


def kernel(x, c, w_mod, b_mod, g_pre_mix, g_post_mix, g_pre_ffn, g_post_ffn, w_in, rel_bias, a_re, a_im, log_dt, ssm_b_re, ssm_b_im, ssm_c_re, ssm_c_im, d_skip, w_glu, b_glu, w_branch_attn, w_branch_ssm, w_out, w_router_group, b_router_group, w_router_expert, b_router_expert, w1, w3, w2):
    bsz, seq, d = x.shape
    depth = w_mod.shape[0]
    ssm_width = w_glu.shape[-1]
    n_pat = len(DILATION_PATTERNS)
    for l in range(depth):
        mod3 = _modulation(c, w_mod[l], b_mod[l]).reshape(bsz, 6, d)
        w_in_l = w_in[l].astype(BF16)
        g_pre = g_pre_mix[l].reshape(1, d)
        qkv = _qkv_projection(x, mod3, g_pre, w_in_l[:, :3 * ATTN_WIDTH])
        u_slabs, gates = _ugate_projection(x, mod3, g_pre, w_in_l[:, 3 * ATTN_WIDTH:], ssm_width)
        attn_outs = [_attention_pattern(*qkv[3 * p:3 * p + 3], rel_bias, DILATION_PATTERNS[p][1])
                     for p in range(n_pat)]
        ssm_slabs = _ssm_branch(u_slabs, bsz, a_re[l], a_im[l], log_dt[l], ssm_b_re[l],
                                ssm_b_im[l], ssm_c_re[l], ssm_c_im[l], d_skip[l], w_glu[l], b_glu[l])
        x1, h2, route, counts = _merge_and_route(
            x, gates, attn_outs, ssm_slabs, mod3, g_post_mix[l].reshape(1, d),
            g_pre_ffn[l].reshape(1, d), w_branch_attn[l].astype(BF16),
            w_branch_ssm[l].astype(BF16), w_out[l].astype(BF16),
            w_router_group[l], b_router_group[l], w_router_expert[l], b_router_expert[l])
        route_flat = route.reshape(bsz * seq, LANES)
        dest, blk_expert, n_used, seg_info, cap = _moe_layout(route_flat, counts[0, :N_EXPERTS])
        xs = _dispatch(h2.reshape(bsz * seq * ROW_TILE, LANES), dest, seg_info, cap)
        ys = _experts(xs, blk_expert, n_used, w1[l], w3[l], w2[l])
        x = _combine(ys, dest, x1.reshape(bsz * seq, d), route_flat, mod3,
                     g_post_ffn[l].reshape(1, d), seq).reshape(bsz, seq, d)
    return x
```

```python
import functools
import math

import numpy as np
import jax
import jax.numpy as jnp
from jax import lax
from jax.experimental import pallas as pl
from jax.experimental.pallas import tpu as pltpu

F32 = jnp.float32
BF16 = jnp.bfloat16

N_HEADS = 8
HEAD_DIM = 64
ATTN_WIDTH = N_HEADS * HEAD_DIM
DILATION_PATTERNS = ((128, 1), (512, 4), (2048, 16))
NUM_BUCKETS = 32
MAX_DISTANCE = 2048
N_EXPERT_GROUPS = 4
EXPERTS_PER_GROUP = 8
N_EXPERTS = N_EXPERT_GROUPS * EXPERTS_PER_GROUP
TOP_K = 2
RMS_EPS = 1e-6
NEG_INF = -1e30
LOG2_E = math.log2(math.e)
LN_2 = math.log(2.0)

LANES = 128
SUBLANES = 8
VMEM_LIMIT_BYTES = 56 * 1024 * 1024

ATTN_BLK = 128
ATTN_STEP_ROWS = 512
QKV_ROWS = 512
QKV_MID_DIL = 4
TIME_TILE = 128
MERGE_BATCH = 4
MERGE_CHUNK_BATCH = 2
SSM_STEPS = 128
MOE_ROWS = 512
DISPATCH_ROWS = 2048
COMBINE_ROWS = 512
ROW_TILE = 4
U32 = jnp.uint32
HI_HALF = 0xFFFF0000


def _pack_rows(x):
    w = x.shape[1] // 2
    lo = lax.bitcast_convert_type(x[:, :w].astype(BF16).astype(F32), U32) >> 16
    hi = lax.bitcast_convert_type(x[:, w:].astype(BF16).astype(F32), U32) & U32(HI_HALF)
    return hi | lo


def _unpack_rows(p):
    lo = lax.bitcast_convert_type(p << 16, F32)
    hi = lax.bitcast_convert_type(p & U32(HI_HALF), F32)
    return jnp.concatenate([lo, hi], axis=1)


def _store_row_tiles(ref, packed):
    rows = packed.shape[0]
    for s in range(ROW_TILE):
        ref[pl.ds(s, rows, stride=ROW_TILE), :] = packed[:, s * LANES:(s + 1) * LANES]


def _load_row_tiles(ref, rows):
    return _lane_concat([ref[pl.ds(s, rows, stride=ROW_TILE), :] for s in range(ROW_TILE)])


def _cparams(*sem):
    return pltpu.CompilerParams(dimension_semantics=sem, vmem_limit_bytes=VMEM_LIMIT_BYTES)


def _sigmoid(x):
    return 1.0 / (1.0 + jnp.exp(-x))


def _dot(a, b):
    return jnp.dot(a, b, preferred_element_type=F32)


def _split_bf16(a):
    hi = a.astype(BF16)
    lo = (a - hi.astype(F32)).astype(BF16)
    return hi, lo


def _dot_split(a, w_hi, w_lo):
    a_hi, a_lo = _split_bf16(a)
    return _dot(a_hi, w_hi) + _dot(a_lo, w_hi) + _dot(a_hi, w_lo)


def _rms_norm(x, gain):
    ms = jnp.mean(x * x, axis=-1, keepdims=True)
    return x * lax.rsqrt(ms + RMS_EPS) * gain


def _lane_concat(ref_slabs):
    return jnp.concatenate(ref_slabs, axis=-1)


def _mod_kernel(c_ref, w_ref, b_ref, o_ref):
    c = c_ref[...]
    a = c * _sigmoid(c)
    w_hi, w_lo = _split_bf16(w_ref[...])
    o_ref[...] = _dot_split(a, w_hi, w_lo) + b_ref[...]


def _modulation(c, w_mod, b_mod):
    bsz, d = c.shape
    n = w_mod.shape[1]
    tn = 1024
    return pl.pallas_call(
        _mod_kernel,
        out_shape=jax.ShapeDtypeStruct((bsz, n), F32),
        grid=(n // tn,),
        in_specs=[pl.BlockSpec((bsz, d), lambda j: (0, 0)),
                  pl.BlockSpec((d, tn), lambda j: (0, j)),
                  pl.BlockSpec((1, tn), lambda j: (0, j))],
        out_specs=pl.BlockSpec((bsz, tn), lambda j: (0, j)),
        compiler_params=_cparams("arbitrary"),
        name="mod",
    )(c, w_mod, b_mod.reshape(1, n))


def _qkv_kernel(x_ref, mod_ref, g_ref, w_ref, *rest):
    n_pat = len(DILATION_PATTERNS)
    out_refs, slab, mid = rest[:3 * n_pat], rest[3 * n_pat], rest[3 * n_pat + 1]
    h = _rms_norm(x_ref[0], g_ref[...]) * (1.0 + mod_ref[0, 1:2, :]) + mod_ref[0, 0:1, :]
    hb = h.astype(BF16)
    rows = hb.shape[0]
    per_tensor = ATTN_WIDTH // LANES
    mid_dil = QKV_MID_DIL

    def project(t):
        res = _dot(hb, w_ref[:, t * ATTN_WIDTH:(t + 1) * ATTN_WIDTH])
        if t == 0:
            res = res * (HEAD_DIM ** -0.5 * LOG2_E)
        for s in range(per_tensor):
            slab[t * per_tensor + s] = res[:, s * LANES:(s + 1) * LANES]

    def split(t):
        for p, (_, dil) in enumerate(DILATION_PATTERNS):
            sub = rows // dil
            out = out_refs[3 * p + t]
            for r in range(dil):
                pieces = []
                for s in range(per_tensor):
                    ts = t * per_tensor + s
                    if dil == 1:
                        piece = slab[ts]
                    elif dil == mid_dil:
                        piece = slab[ts, pl.ds(r, sub, stride=dil), :]
                        mid[ts, r] = piece
                    else:
                        ratio = dil // mid_dil
                        piece = mid[ts, r % mid_dil, pl.ds(r // mid_dil, sub, stride=ratio), :]
                    pieces.append(piece)
                out[0, r] = _lane_concat(pieces).astype(out.dtype)

    project(0)
    for t in range(3):
        if t + 1 < 3:
            project(t + 1)
        split(t)


def _qkv_projection(x, mod3, g_pre, w_qkv):
    bsz, seq, d = x.shape
    tm = QKV_ROWS
    out_shape, out_specs = [], []
    for _, dil in DILATION_PATTERNS:
        assert tm % (dil * 2 * SUBLANES) == 0
        for _ in range(3):
            out_shape.append(jax.ShapeDtypeStruct((bsz, dil, seq // dil, ATTN_WIDTH), BF16))
            out_specs.append(pl.BlockSpec((1, dil, tm // dil, ATTN_WIDTH),
                                          lambda b, i: (b, 0, i, 0)))
    return pl.pallas_call(
        _qkv_kernel,
        out_shape=out_shape,
        grid=(bsz, seq // tm),
        in_specs=[pl.BlockSpec((1, tm, d), lambda b, i: (b, i, 0)),
                  pl.BlockSpec((1, 6, d), lambda b, i: (b, 0, 0)),
                  pl.BlockSpec((1, d), lambda b, i: (0, 0)),
                  pl.BlockSpec(w_qkv.shape, lambda b, i: (0, 0))],
        out_specs=out_specs,
        scratch_shapes=[pltpu.VMEM((w_qkv.shape[1] // LANES, tm, LANES), F32),
                        pltpu.VMEM((w_qkv.shape[1] // LANES, QKV_MID_DIL, tm // QKV_MID_DIL, LANES),
                                   F32)],
        compiler_params=_cparams("arbitrary", "arbitrary"),
        name="qkv",
    )(x, mod3, g_pre, w_qkv)


def _ugate_kernel(x_ref, mod_ref, g_ref, w_ref, u_ref, gate_ref):
    bsz, tt, d = x_ref.shape
    shift = mod_ref[:, 0, :][:, None, :]
    scale = mod_ref[:, 1, :][:, None, :]
    h = _rms_norm(x_ref[...], g_ref[...]) * (1.0 + scale) + shift
    hb = h.reshape(bsz * tt, d).astype(BF16)
    n_slab = u_ref.shape[0]
    sw = n_slab * LANES
    u = _dot(hb, w_ref[:, 0:sw])
    for b in range(bsz):
        for s in range(n_slab):
            u_ref[s, pl.ds(b, tt, stride=bsz), :] = u[b * tt:(b + 1) * tt, s * LANES:(s + 1) * LANES]
    gw = gate_ref.shape[-1]
    chunk = 512
    for c0 in range(0, gw, chunk):
        g = _sigmoid(_dot(hb, w_ref[:, sw + c0:sw + c0 + chunk]))
        gate_ref[:, :, c0:c0 + chunk] = g.reshape(bsz, tt, chunk).astype(BF16)


def _ugate_projection(x, mod3, g_pre, w_ug, ssm_width):
    bsz, seq, d = x.shape
    tt = TIME_TILE
    gw = w_ug.shape[1] - ssm_width
    n_slab = ssm_width // LANES
    return pl.pallas_call(
        _ugate_kernel,
        out_shape=(jax.ShapeDtypeStruct((n_slab, seq * bsz, LANES), F32),
                   jax.ShapeDtypeStruct((bsz, seq, gw), BF16)),
        grid=(seq // tt,),
        in_specs=[pl.BlockSpec((bsz, tt, d), lambda i: (0, i, 0)),
                  pl.BlockSpec((bsz, 6, d), lambda i: (0, 0, 0)),
                  pl.BlockSpec((1, d), lambda i: (0, 0)),
                  pl.BlockSpec(w_ug.shape, lambda i: (0, 0))],
        out_specs=(pl.BlockSpec((n_slab, tt * bsz, LANES), lambda i: (0, i, 0)),
                   pl.BlockSpec((bsz, tt, gw), lambda i: (0, i, 0))),
        compiler_params=_cparams("arbitrary"),
        name="ugate",
    )(x, mod3, g_pre, w_ug)


def _t5_bucket_np(dist):
    exact = NUM_BUCKETS // 2
    d_f = np.maximum(dist, exact).astype(np.float32)
    large = exact + (np.log(d_f / np.float32(exact)) / np.float32(math.log(MAX_DISTANCE / exact))
                     * np.float32(NUM_BUCKETS - exact)).astype(np.int32)
    return np.where(dist < exact, dist, np.minimum(large, NUM_BUCKETS - 1))


def _bucket_map_t(dil):
    blk = ATTN_BLK
    ki = np.arange(2 * blk)[:, None]
    qi = np.arange(blk)[None, :]
    return _t5_bucket_np(np.maximum(blk + qi - ki, 0) * dil).astype(np.int32)


def _attn_kernel(relb_ref, q_ref, kc_ref, kp_ref, vc_ref, vp_ref, bucket_ref,
                 o_ref, lse_ref, kbuf, vbuf, bias_t, *, n_sub):
    blk = ATTN_BLK
    first_call = jnp.logical_and(pl.program_id(0) == 0,
                                 jnp.logical_and(pl.program_id(1) == 0, pl.program_id(2) == 0))

    @pl.when(first_call)
    def _():
        bucket = bucket_ref[...]
        ki = lax.broadcasted_iota(jnp.int32, bucket.shape, 0)
        qi = lax.broadcasted_iota(jnp.int32, bucket.shape, 1)
        dist = blk + qi - ki
        band = jnp.logical_and(dist >= 0, dist <= blk)
        band_first = jnp.logical_and(band, ki >= blk)

        def per_head(h, carry):
            acc = jnp.zeros(bucket.shape, F32)
            for b in range(NUM_BUCKETS):
                acc = jnp.where(bucket == b, relb_ref[b, h] * LOG2_E, acc)
            bias_t[0, h] = jnp.where(band_first, acc, NEG_INF)
            bias_t[1, h] = jnp.where(band, acc, NEG_INF)
            return carry

        lax.fori_loop(0, N_HEADS, per_head, 0)

    first_variant = jnp.where(pl.program_id(2) == 0, 0, 1)
    n_res = q_ref.shape[1]
    for g in range(n_res):
        kbuf[g, 0:blk, :] = kp_ref[0, g]
        kbuf[g, blk:, :] = kc_ref[0, g]
        vbuf[g, 0:blk, :] = vp_ref[0, g]
        vbuf[g, blk:, :] = vc_ref[0, g]

    lane = lax.broadcasted_iota(jnp.int32, (1, LANES), 1)
    lo_half = lane < HEAD_DIM
    bd_row = lax.broadcasted_iota(jnp.int32, (4 * blk, LANES), 0)
    bd_col = lax.broadcasted_iota(jnp.int32, (4 * blk, LANES), 1)
    ones_bd = ((bd_row < 2 * blk) == (bd_col < HEAD_DIM)).astype(F32).astype(BF16)
    contract_last = (((1,), (1,)), ((), ()))
    contract_first = (((0,), (0,)), ((), ()))

    def sub_block(g, i):
        r0 = i * blk
        q = q_ref[0, g, r0:r0 + blk, :]
        kk = kbuf[g, r0:r0 + 2 * blk, :]
        vv = vbuf[g, r0:r0 + 2 * blk, :]
        variant = first_variant if i == 0 else 1
        for j in range(N_HEADS // 2):
            cols = slice(j * LANES, (j + 1) * LANES)
            qj, kj, vj = q[:, cols], kk[:, cols], vv[:, cols]
            probs_t, maxes = [], []
            for hh in range(2):
                sel = lo_half if hh == 0 else jnp.logical_not(lo_half)
                qm = jnp.where(sel, qj, jnp.zeros_like(qj))
                s_t = lax.dot_general(kj, qm, contract_last, preferred_element_type=F32)
                s_t = s_t + bias_t[variant, 2 * j + hh]
                m = jnp.max(s_t, axis=0, keepdims=True)
                probs_t.append(jnp.exp2(s_t - m).astype(BF16))
                maxes.append(m)
            p2_t = jnp.concatenate(probs_t, axis=0)
            v_bd = jnp.concatenate([jnp.where(lo_half, vj, jnp.zeros_like(vj)),
                                    jnp.where(lo_half, jnp.zeros_like(vj), vj)], axis=0)
            rhs = jnp.concatenate([v_bd, ones_bd], axis=1)
            ol = lax.dot_general(p2_t, rhs, contract_first, preferred_element_type=F32)
            o2, l2 = ol[:, :LANES], ol[:, LANES:]
            m_t = jnp.concatenate([jnp.broadcast_to(maxes[0], (HEAD_DIM, blk)),
                                   jnp.broadcast_to(maxes[1], (HEAD_DIM, blk))], axis=0)
            o_ref[0, g, r0:r0 + blk, cols] = (o2 / l2).astype(o_ref.dtype)
            lse_ref[0, g, r0:r0 + blk, cols] = (m_t.T + jnp.log2(l2)) * LN_2

    for g in range(n_res):
        for i in range(n_sub):
            sub_block(g, i)


def _attention_pattern(q, k, v, rel_bias, dil):
    bsz, _, sub_len, aw = q.shape
    blk = ATTN_BLK
    assert sub_len % blk == 0
    tq = min(ATTN_STEP_ROWS, sub_len)
    n_sub = tq // blk
    ratio = tq // blk
    n_res = min(dil, ATTN_STEP_ROWS // tq)
    cur = lambda b, r, n: (b, r, n, 0)
    prev = lambda b, r, n: (b, r, jnp.maximum(n * ratio - 1, 0), 0)
    blk_cur = pl.BlockSpec((1, n_res, tq, aw), cur)
    blk_prev = pl.BlockSpec((1, n_res, blk, aw), prev)
    bucket = jnp.asarray(_bucket_map_t(dil))
    o_dtype = BF16 if TIME_TILE // dil >= 2 * SUBLANES else F32
    return pl.pallas_call(
        functools.partial(_attn_kernel, n_sub=n_sub),
        out_shape=(jax.ShapeDtypeStruct(q.shape, o_dtype), jax.ShapeDtypeStruct(q.shape, F32)),
        grid=(bsz, dil // n_res, sub_len // tq),
        in_specs=[pl.BlockSpec(memory_space=pltpu.SMEM),
                  blk_cur, blk_cur, blk_prev, blk_cur, blk_prev,
                  pl.BlockSpec(bucket.shape, lambda b, r, n: (0, 0))],
        out_specs=(blk_cur, blk_cur),
        scratch_shapes=[pltpu.VMEM((n_res, tq + blk, aw), BF16),
                        pltpu.VMEM((n_res, tq + blk, aw), BF16),
                        pltpu.VMEM((2, N_HEADS, 2 * blk, blk), F32)],
        compiler_params=_cparams("arbitrary", "arbitrary", "arbitrary"),
        name=f"attn_dil{dil}",
    )(rel_bias.astype(F32), q, k, k, v, v, bucket)


def _ssm_kernel(u_ref, bmat_ref, cmat_ref, ar_ref, ai_ref, dskip_ref, wglu_ref, bglu_ref,
                o_ref, hbuf, hstate, *, n_steps):
    @pl.when(pl.program_id(0) == 0)
    def _():
        hstate[...] = jnp.zeros_like(hstate)

    n_slab = u_ref.shape[0]
    n_state = hbuf.shape[1] // 2
    per = n_state // n_slab
    us = [u_ref[s] for s in range(n_slab)]
    ys = [None] * n_slab

    def drive(s):
        bu = _dot(us[s].astype(BF16), bmat_ref[s])
        hbuf[:, s * per:(s + 1) * per] = bu[:, :per]
        hbuf[:, n_state + s * per:n_state + (s + 1) * per] = bu[:, per:]

    def scan(s):
        re_cols = slice(s * per, (s + 1) * per)
        im_cols = slice(n_state + s * per, n_state + (s + 1) * per)
        ar = ar_ref[:, re_cols]
        ai = ai_ref[:, re_cols]
        hr = hstate[:, re_cols]
        hi = hstate[:, im_cols]
        for t in range(n_steps):
            trow = slice(t * SUBLANES, (t + 1) * SUBLANES)
            nr = ar * hr - ai * hi + hbuf[trow, re_cols]
            ni = ar * hi + ai * hr + hbuf[trow, im_cols]
            hbuf[trow, re_cols] = nr
            hbuf[trow, im_cols] = ni
            hr, hi = nr, ni
        hstate[:, re_cols] = hr
        hstate[:, im_cols] = hi

    def read_out(s):
        h_s = _lane_concat([hbuf[:, s * per:(s + 1) * per],
                            hbuf[:, n_state + s * per:n_state + (s + 1) * per]])
        ys[s] = (_dot(h_s.astype(BF16), cmat_ref[s])
                 + dskip_ref[:, s * LANES:(s + 1) * LANES] * us[s])

    for tick in range(n_slab + 2):
        if tick < n_slab:
            drive(tick)
        if 0 <= tick - 1 < n_slab:
            scan(tick - 1)
        if 0 <= tick - 2 < n_slab:
            read_out(tick - 2)
    y = _lane_concat(ys)
    y = 0.5 * y * (1.0 + jnp.tanh(math.sqrt(2.0 / math.pi) * (y + 0.044715 * (y * y * y))))
    z = _dot(y.astype(BF16), wglu_ref[...]) + bglu_ref[...]
    out = y * _sigmoid(z)
    for s in range(n_slab):
        o_ref[s] = out[:, s * LANES:(s + 1) * LANES]


def _ssm_params(a_re, a_im, log_dt, b_re, b_im, c_re, c_im, bsz):
    g, p = a_re.shape
    hg = b_re.shape[-1]
    dt = jnp.exp(log_dt.astype(F32))[:, None]
    a_re, a_im = a_re.astype(F32), a_im.astype(F32)
    mag = jnp.exp(a_re * dt)
    abar_re = mag * jnp.cos(a_im * dt)
    abar_im = mag * jnp.sin(a_im * dt)
    den = a_re * a_re + a_im * a_im
    q_re = ((abar_re - 1.0) * a_re + abar_im * a_im) / den
    q_im = (abar_im * a_re - (abar_re - 1.0) * a_im) / den
    b_re, b_im = b_re.astype(F32), b_im.astype(F32)
    bb_re = q_re[..., None] * b_re - q_im[..., None] * b_im
    bb_im = q_re[..., None] * b_im + q_im[..., None] * b_re
    gs = LANES // hg
    n_slab = g // gs
    eye = jnp.eye(gs, dtype=F32)

    def in_mat(t):
        t = t.reshape(n_slab, gs, p, hg)
        return jnp.einsum('sgph,gk->sghkp', t, eye).reshape(n_slab, gs * hg, gs * p)

    def out_mat(t):
        t = t.reshape(n_slab, gs, hg, p)
        return jnp.einsum('sghp,gk->sgpkh', t, eye).reshape(n_slab, gs * p, gs * hg)

    bmat = jnp.concatenate([in_mat(bb_re), in_mat(bb_im)], axis=2).astype(BF16)
    cmat = jnp.concatenate([out_mat(c_re.astype(F32)), -out_mat(c_im.astype(F32))],
                           axis=1).astype(BF16)
    ar = jnp.broadcast_to(abar_re.reshape(1, g * p), (bsz, g * p))
    ai = jnp.broadcast_to(abar_im.reshape(1, g * p), (bsz, g * p))
    return bmat, cmat, ar, ai


def _ssm_branch(u_slabs, bsz, a_re, a_im, log_dt, b_re, b_im, c_re, c_im, d_skip, w_glu, b_glu):
    n_slab, n_rows, _ = u_slabs.shape
    width = n_slab * LANES
    assert bsz == SUBLANES
    bmat, cmat, ar, ai = _ssm_params(a_re, a_im, log_dt, b_re, b_im, c_re, c_im, bsz)
    n_state2 = n_slab * bmat.shape[2]
    rows = SSM_STEPS * bsz
    const = lambda c: (0, 0)
    const3 = lambda c: (0, 0, 0)
    slab_spec = pl.BlockSpec((n_slab, rows, LANES), lambda c: (0, c, 0))
    return pl.pallas_call(
        functools.partial(_ssm_kernel, n_steps=SSM_STEPS),
        out_shape=jax.ShapeDtypeStruct(u_slabs.shape, F32),
        grid=(n_rows // rows,),
        in_specs=[slab_spec,
                  pl.BlockSpec(bmat.shape, const3), pl.BlockSpec(cmat.shape, const3),
                  pl.BlockSpec(ar.shape, const), pl.BlockSpec(ai.shape, const),
                  pl.BlockSpec((1, width), const), pl.BlockSpec((width, width), const),
                  pl.BlockSpec((1, width), const)],
        out_specs=slab_spec,
        scratch_shapes=[pltpu.VMEM((rows, n_state2), F32), pltpu.VMEM((bsz, n_state2), F32)],
        compiler_params=_cparams("arbitrary"),
        name="ssm",
    )(u_slabs, bmat, cmat, ar, ai, d_skip.reshape(1, width).astype(F32),
      w_glu.astype(BF16), b_glu.reshape(1, width).astype(F32))


ROUTE_LANES = {"id0": 0, "id1": 1, "rank0": 2, "rank1": 3, "w0": 4, "w1": 5}
GROUP_LANE0 = N_EXPERTS


def _merge_kernel(*refs, bsz_total):
    n_pat = len(DILATION_PATTERNS)
    x_ref, gate_ref = refs[0:2]
    attn_refs = refs[2:2 + 2 * n_pat]
    (ssm_ref, mod_ref, gpost_ref, gpre_ref, wba_ref, wbs_ref, wout_ref,
     wr_cat_ref, wr_hi_ref, br_ref) = refs[2 + 2 * n_pat:12 + 2 * n_pat]
    x1_ref, h2_ref, route_ref, count_ref = refs[12 + 2 * n_pat:16 + 2 * n_pat]
    scratch = refs[16 + 2 * n_pat:]
    carry = scratch[-1]

    @pl.when(jnp.logical_and(pl.program_id(0) == 0, pl.program_id(1) == 0))
    def _():
        carry[...] = jnp.zeros_like(carry)

    nb, tt, d = x_ref.shape
    n_slab = ssm_ref.shape[0]
    b0 = pl.program_id(0) * nb
    cb = MERGE_CHUNK_BATCH
    rows = cb * tt
    lane = lax.broadcasted_iota(jnp.int32, (rows, LANES), 1).astype(F32)
    row = lax.broadcasted_iota(jnp.int32, (rows, rows), 0)
    col = lax.broadcasted_iota(jnp.int32, (rows, rows), 1)
    strict_lower = (col < row).astype(BF16)
    running = [carry[...]]

    def chunk_phases(c0):
        bbs = range(c0, c0 + cb)
        o_tok, lse_tok, ssm_tok = scratch[3 * (c0 // cb):3 * (c0 // cb) + 3]

        for bb in bbs:
            for s in range(n_slab):
                ssm_tok[(bb - c0) * tt:(bb - c0 + 1) * tt, s * LANES:(s + 1) * LANES] = (
                    ssm_ref[s, pl.ds(b0 + bb, tt, stride=bsz_total), :])

        slot = 0
        sources = []
        for p, (_, dil) in enumerate(DILATION_PATTERNS):
            o_ref, lse_ref = attn_refs[2 * p], attn_refs[2 * p + 1]
            if dil == 1:
                sources.append((o_ref, lse_ref, None))
                continue
            sub = tt // dil
            for bb in bbs:
                for r in range(dil):
                    o_blk = o_ref[bb, r].astype(F32)
                    l_blk = lse_ref[bb, r]
                    for s in range(n_slab):
                        dst = pl.ds((bb - c0) * tt + r, sub, stride=dil)
                        o_tok[slot, s, dst, :] = o_blk[:, s * LANES:(s + 1) * LANES]
                        lse_tok[slot, s, dst, :] = l_blk[:, s * LANES:(s + 1) * LANES]
            sources.append((o_ref, lse_ref, slot))
            slot += 1
        attn_slabs = []
        for s in range(n_slab):
            cols = slice(s * LANES, (s + 1) * LANES)
            o_ps, lse_ps = [], []
            for o_ref, lse_ref, src_slot in sources:
                if src_slot is None:
                    o_ps.append(o_ref[c0:c0 + cb, 0, :, cols].astype(F32).reshape(rows, LANES))
                    lse_ps.append(lse_ref[c0:c0 + cb, 0, :, cols].reshape(rows, LANES))
                else:
                    o_ps.append(o_tok[src_slot, s])
                    lse_ps.append(lse_tok[src_slot, s])
            m = functools.reduce(jnp.maximum, lse_ps)
            es = [jnp.exp(l - m) for l in lse_ps]
            num = functools.reduce(lambda a, b: a + b, [e * o for e, o in zip(es, o_ps)])
            den = functools.reduce(lambda a, b: a + b, es)
            attn_slabs.append(num / den)
        attn_b = _lane_concat(attn_slabs).astype(BF16)
        ssm_b = ssm_tok[...].astype(BF16)
        yield
        branch_attn = _dot(attn_b, wba_ref[...])
        branch_ssm = _dot(ssm_b, wbs_ref[...])
        yield
        g_attn = gate_ref[c0:c0 + cb, :, 0:d].astype(F32).reshape(rows, d)
        g_ssm = gate_ref[c0:c0 + cb, :, d:].astype(F32).reshape(rows, d)
        merged_b = (g_attn * branch_attn + g_ssm * branch_ssm).astype(BF16)
        yield
        y = _dot(merged_b, wout_ref[...])
        yield
        gate1 = mod_ref[c0:c0 + cb, 2, :][:, None, :]
        shift2 = mod_ref[c0:c0 + cb, 3, :][:, None, :]
        scale2 = mod_ref[c0:c0 + cb, 4, :][:, None, :]
        x1 = x_ref[c0:c0 + cb] + gate1 * _rms_norm(y, gpost_ref[...]).reshape(cb, tt, d)
        x1_ref[c0:c0 + cb] = x1
        h2 = _rms_norm(x1, gpre_ref[...]) * (1.0 + scale2) + shift2
        for j, bb in enumerate(bbs):
            _store_row_tiles(h2_ref.at[bb], _pack_rows(h2[j]))

        a_hi, a_lo = _split_bf16(h2.reshape(rows, d))
        yield
        hi_pass = _dot(a_hi, wr_cat_ref[...])
        lo_pass = _dot(a_lo, wr_hi_ref[...])
        yield
        logits = hi_pass[:, :LANES] + lo_pass + hi_pass[:, LANES:] + br_ref[...]
        big = float(LANES)
        is_group = jnp.logical_and(lane >= GROUP_LANE0, lane < GROUP_LANE0 + N_EXPERT_GROUPS)
        gl = jnp.where(is_group, logits, -jnp.inf)
        g_max = jnp.max(gl, axis=-1, keepdims=True)
        g_sel = jnp.min(jnp.where(gl == g_max, lane, big), axis=-1, keepdims=True) - GROUP_LANE0
        g_gate = 1.0 / jnp.sum(jnp.exp(gl - g_max), axis=-1, keepdims=True)
        lo = g_sel * EXPERTS_PER_GROUP
        in_group = jnp.logical_and(lane >= lo, lane < lo + EXPERTS_PER_GROUP)
        el = jnp.where(in_group, logits, -jnp.inf)
        t0 = jnp.max(el, axis=-1, keepdims=True)
        i0 = jnp.min(jnp.where(el == t0, lane, big), axis=-1, keepdims=True)
        el1 = jnp.where(lane == i0, -jnp.inf, el)
        t1 = jnp.max(el1, axis=-1, keepdims=True)
        i1 = jnp.min(jnp.where(el1 == t1, lane, big), axis=-1, keepdims=True)
        e = jnp.exp(t1 - t0)
        w0 = g_gate / (1.0 + e)
        w1 = g_gate * e / (1.0 + e)

        hit0 = lane == i0
        hit1 = lane == i1
        onehot = jnp.logical_or(hit0, hit1).astype(F32)
        yield
        before = _dot(strict_lower, onehot.astype(BF16)) + running[0]
        rank0 = jnp.sum(jnp.where(hit0, before, 0.0), axis=-1, keepdims=True)
        rank1 = jnp.sum(jnp.where(hit1, before, 0.0), axis=-1, keepdims=True)
        running[0] = running[0] + jnp.sum(onehot, axis=0, keepdims=True)

        route = jnp.zeros((rows, LANES), F32)
        for name, val in (("id0", i0), ("id1", i1), ("rank0", rank0), ("rank1", rank1),
                          ("w0", w0), ("w1", w1)):
            route = jnp.where(lane == ROUTE_LANES[name], val, route)
        route_ref[c0:c0 + cb] = route.reshape(cb, tt, LANES)

    pending = [chunk_phases(c0) for c0 in range(0, nb, cb)]
    active = []
    while pending or active:
        if pending:
            active.append(pending.pop(0))
        for gen in list(active):
            if next(gen, StopIteration) is StopIteration:
                active.remove(gen)

    carry[...] = running[0]
    count_ref[...] = jnp.broadcast_to(running[0], count_ref.shape)


def _merge_and_route(x, gates, attn_outs, ssm_slabs, mod3, g_post, g_pre, wba, wbs, wout,
                     w_rg, b_rg, w_re, b_re):
    bsz, seq, d = x.shape
    tt, nb = TIME_TILE, MERGE_BATCH
    aw = ATTN_WIDTH
    n_slab = ssm_slabs.shape[0]
    assert aw == n_slab * LANES
    wr = jnp.zeros((d, LANES), F32).at[:, :N_EXPERTS].set(w_re.astype(F32))
    wr = wr.at[:, GROUP_LANE0:GROUP_LANE0 + N_EXPERT_GROUPS].set(w_rg.astype(F32))
    br = jnp.zeros((1, LANES), F32).at[0, :N_EXPERTS].set(b_re.astype(F32))
    br = br.at[0, GROUP_LANE0:GROUP_LANE0 + N_EXPERT_GROUPS].set(b_rg.astype(F32))
    wr_hi, wr_lo = _split_bf16(wr)
    wr_cat = jnp.concatenate([wr_hi, wr_lo], axis=1)
    n_strided = sum(1 for _, dil in DILATION_PATTERNS if dil > 1)
    chunk_rows = MERGE_CHUNK_BATCH * tt
    tok = lambda h, i: (h, i, 0)
    const = lambda h, i: (0, 0)
    attn_args, attn_specs = [], []
    for (o_p, lse_p), (_, dil) in zip(attn_outs, DILATION_PATTERNS):
        spec = pl.BlockSpec((nb, dil, tt // dil, aw), lambda h, i: (h, 0, i, 0))
        attn_args += [o_p, lse_p]
        attn_specs += [spec, spec]
    return pl.pallas_call(
        functools.partial(_merge_kernel, bsz_total=bsz),
        out_shape=(jax.ShapeDtypeStruct((bsz, seq, d), F32),
                   jax.ShapeDtypeStruct((bsz, seq * ROW_TILE, LANES), U32),
                   jax.ShapeDtypeStruct((bsz, seq, LANES), F32),
                   jax.ShapeDtypeStruct((SUBLANES, LANES), F32)),
        grid=(bsz // nb, seq // tt),
        in_specs=[pl.BlockSpec((nb, tt, d), tok),
                  pl.BlockSpec((nb, tt, gates.shape[-1]), tok)]
                 + attn_specs
                 + [pl.BlockSpec((n_slab, tt * bsz, LANES), lambda h, i: (0, i, 0)),
                    pl.BlockSpec((nb, 6, d), lambda h, i: (h, 0, 0)),
                    pl.BlockSpec((1, d), const), pl.BlockSpec((1, d), const),
                    pl.BlockSpec(wba.shape, const), pl.BlockSpec(wbs.shape, const),
                    pl.BlockSpec(wout.shape, const),
                    pl.BlockSpec((d, 2 * LANES), const), pl.BlockSpec((d, LANES), const),
                    pl.BlockSpec((1, LANES), const)],
        out_specs=(pl.BlockSpec((nb, tt, d), tok), pl.BlockSpec((nb, tt * ROW_TILE, LANES), tok),
                   pl.BlockSpec((nb, tt, LANES), tok),
                   pl.BlockSpec((SUBLANES, LANES), const)),
        scratch_shapes=[pltpu.VMEM((n_strided, n_slab, chunk_rows, LANES), F32),
                        pltpu.VMEM((n_strided, n_slab, chunk_rows, LANES), F32),
                        pltpu.VMEM((chunk_rows, n_slab * LANES), F32)] * (nb // MERGE_CHUNK_BATCH)
                       + [pltpu.VMEM((1, LANES), F32)],
        compiler_params=_cparams("arbitrary", "arbitrary"),
        name="merge",
    )(x, gates, *attn_args, ssm_slabs, mod3, g_post, g_pre, wba, wbs, wout, wr_cat, wr_hi, br)


ISSUE_UNROLL = 8


def _dispatch_kernel(dest_ref, seg_ref, h_ref, xs_ref, zero_buf, sem, zsem):
    rows = h_ref.shape[0] // ROW_TILE
    blk = zero_buf.shape[0]
    n_blocks = xs_ref.shape[0] // blk
    base = pl.program_id(0) * rows

    @pl.when(pl.program_id(0) == 0)
    def _():
        zero_buf[...] = jnp.zeros_like(zero_buf)

        def zero_copy(row0):
            return pltpu.make_async_copy(zero_buf, xs_ref.at[pl.ds(pl.multiple_of(row0, blk), blk)],
                                         zsem)

        def fill_tail(e, carry):
            @pl.when(seg_ref[N_EXPERTS + e] > 0)
            def _():
                zero_copy(seg_ref[e] * ROW_TILE - blk).start()
            return carry

        def fill_unused(j, carry):
            zero_copy(j * blk).start()
            return carry

        def wait_tail(e, carry):
            @pl.when(seg_ref[N_EXPERTS + e] > 0)
            def _():
                zero_copy(0).wait()
            return carry

        def wait_unused(j, carry):
            zero_copy(0).wait()
            return carry

        n_used = seg_ref[2 * N_EXPERTS]
        lax.fori_loop(0, N_EXPERTS, fill_tail, 0)
        lax.fori_loop(n_used, n_blocks, fill_unused, 0)
        lax.fori_loop(0, N_EXPERTS, wait_tail, 0)
        lax.fori_loop(n_used, n_blocks, wait_unused, 0)

    group = ISSUE_UNROLL * ROW_TILE

    def issue(g, carry):
        g0 = pl.multiple_of(g * group, group)
        for rr in range(ISSUE_UNROLL):
            src = h_ref.at[pl.ds(g0 + rr * ROW_TILE, ROW_TILE)]
            for k in range(TOP_K):
                d = dest_ref[TOP_K * (base + g * ISSUE_UNROLL + rr) + k]
                dst = xs_ref.at[pl.ds(pl.multiple_of(d * ROW_TILE, ROW_TILE), ROW_TILE)]
                pltpu.make_async_copy(src, dst, sem).start(priority=k)
        return carry

    lax.fori_loop(0, rows // ISSUE_UNROLL, issue, 0)
    for _ in range(TOP_K):
        pltpu.make_async_copy(h_ref, xs_ref.at[pl.ds(0, rows * ROW_TILE)], sem).wait()


def _dispatch(h2_tiles, dest, seg_info, cap):
    n_tok = h2_tiles.shape[0] // ROW_TILE
    rows = DISPATCH_ROWS
    return pl.pallas_call(
        _dispatch_kernel,
        out_shape=jax.ShapeDtypeStruct((cap * ROW_TILE, LANES), h2_tiles.dtype),
        grid_spec=pltpu.PrefetchScalarGridSpec(
            num_scalar_prefetch=2,
            grid=(n_tok // rows,),
            in_specs=[pl.BlockSpec((rows * ROW_TILE, LANES), lambda i, dest, seg: (i, 0))],
            out_specs=pl.BlockSpec(memory_space=pl.ANY),
            scratch_shapes=[pltpu.VMEM((MOE_ROWS * ROW_TILE, LANES), h2_tiles.dtype),
                            pltpu.SemaphoreType.DMA, pltpu.SemaphoreType.DMA]),
        compiler_params=_cparams("arbitrary"),
        name="dispatch",
    )(dest, seg_info, h2_tiles)


def _expert_kernel(blk_expert_ref, n_used_ref, x_ref, w1_ref, w3_ref, w2_ref, y_ref,
                   w1b, w3b, w2b):
    i = pl.program_id(0)
    used = i < n_used_ref[0]
    rows = x_ref.shape[0] // ROW_TILE
    new_expert = jnp.logical_or(
        i == 0, blk_expert_ref[i] != blk_expert_ref[jnp.maximum(i - 1, 0)])

    @pl.when(jnp.logical_and(used, new_expert))
    def _():
        w1b[...] = w1_ref[0].astype(BF16)
        w3b[...] = w3_ref[0].astype(BF16)
        w2b[...] = w2_ref[0].astype(BF16)

    @pl.when(used)
    def _():
        xb = _unpack_rows(_load_row_tiles(x_ref, rows)).astype(BF16)
        h1 = _dot(xb, w1b[...])
        h3 = _dot(xb, w3b[...])
        act = (h1 * _sigmoid(h1)) * h3
        y = _dot(act.astype(BF16), w2b[...])
        _store_row_tiles(y_ref, _pack_rows(y))

    @pl.when(jnp.logical_not(used))
    def _():
        y_ref[...] = jnp.zeros_like(y_ref)


def _experts(xs, blk_expert, n_used, w1, w3, w2):
    d, de = w1.shape[-2:]
    assert d == 2 * ROW_TILE * LANES
    rows = MOE_ROWS
    tile_rows = rows * ROW_TILE
    xblk = lambda i, be, nu: (jnp.minimum(i, nu[0] - 1), 0)
    wblk = lambda i, be, nu: (be[i], 0, 0)
    return pl.pallas_call(
        _expert_kernel,
        out_shape=jax.ShapeDtypeStruct(xs.shape, U32),
        grid_spec=pltpu.PrefetchScalarGridSpec(
            num_scalar_prefetch=2,
            grid=(xs.shape[0] // tile_rows,),
            in_specs=[pl.BlockSpec((tile_rows, LANES), xblk),
                      pl.BlockSpec((1, d, de), wblk), pl.BlockSpec((1, d, de), wblk),
                      pl.BlockSpec((1, de, d), wblk)],
            out_specs=pl.BlockSpec((tile_rows, LANES), lambda i, be, nu: (i, 0)),
            scratch_shapes=[pltpu.VMEM((d, de), BF16), pltpu.VMEM((d, de), BF16),
                            pltpu.VMEM((de, d), BF16)]),
        compiler_params=_cparams("arbitrary"),
        name="experts",
    )(blk_expert, n_used, xs, w1, w3, w2)


def _combine_kernel(dest_ref, ys_ref, x1_ref, route_ref, mod_ref, g_ref, o_ref, ybuf, sem):
    rows = x1_ref.shape[0]
    step = pl.program_id(0)
    n_step = pl.num_programs(0)

    group = ISSUE_UNROLL * ROW_TILE

    def gather(tile, slot):
        base = tile * rows

        def issue(g, carry):
            g0 = pl.multiple_of(g * group, group)
            for rr in range(ISSUE_UNROLL):
                for k in range(TOP_K):
                    d = dest_ref[TOP_K * (base + g * ISSUE_UNROLL + rr) + k]
                    src = ys_ref.at[pl.ds(pl.multiple_of(d * ROW_TILE, ROW_TILE), ROW_TILE)]
                    dst = ybuf.at[slot, k, pl.ds(g0 + rr * ROW_TILE, ROW_TILE)]
                    pltpu.make_async_copy(src, dst, sem.at[slot]).start(priority=k)
            return carry

        lax.fori_loop(0, rows // ISSUE_UNROLL, issue, 0)

    slot = lax.rem(step, 2)

    @pl.when(step == 0)
    def _():
        gather(step, 0)

    @pl.when(step + 1 < n_step)
    def _():
        gather(step + 1, 1 - slot)

    for k in range(TOP_K):
        pltpu.make_async_copy(ys_ref.at[pl.ds(0, rows * ROW_TILE)], ybuf.at[slot, k],
                              sem.at[slot]).wait()
    route = route_ref[...]
    w0 = route[:, ROUTE_LANES["w0"]:ROUTE_LANES["w0"] + 1]
    w1 = route[:, ROUTE_LANES["w1"]:ROUTE_LANES["w1"] + 1]

    def rows_of(k):
        return _unpack_rows(_load_row_tiles(ybuf.at[slot, k], rows))

    y = rows_of(0) * w0 + rows_of(1) * w1
    o_ref[...] = x1_ref[...] + mod_ref[0, 5:6, :] * _rms_norm(y, g_ref[...])


def _combine(ys, dest, x1_flat, route_flat, mod3, g_post, seq):
    n_tok, d = x1_flat.shape
    rows = COMBINE_ROWS
    assert seq % rows == 0
    return pl.pallas_call(
        _combine_kernel,
        out_shape=jax.ShapeDtypeStruct((n_tok, d), F32),
        grid_spec=pltpu.PrefetchScalarGridSpec(
            num_scalar_prefetch=1,
            grid=(n_tok // rows,),
            in_specs=[pl.BlockSpec(memory_space=pl.ANY),
                      pl.BlockSpec((rows, d), lambda i, dest: (i, 0)),
                      pl.BlockSpec((rows, LANES), lambda i, dest: (i, 0)),
                      pl.BlockSpec((1, 6, d), lambda i, dest: (i * rows // seq, 0, 0)),
                      pl.BlockSpec((1, d), lambda i, dest: (0, 0))],
            out_specs=pl.BlockSpec((rows, d), lambda i, dest: (i, 0)),
            scratch_shapes=[pltpu.VMEM((2, TOP_K, rows * ROW_TILE, LANES), U32),
                            pltpu.SemaphoreType.DMA((2,))]),
        compiler_params=_cparams("arbitrary"),
        name="combine",
    )(dest, ys, x1_flat, route_flat, mod3, g_post)


def _moe_layout(route_flat, counts):
    rows = MOE_ROWS
    n_tok = route_flat.shape[0]
    ids = route_flat[:, 0:TOP_K].astype(jnp.int32)
    rank = route_flat[:, TOP_K:2 * TOP_K].astype(jnp.int32)
    counts = counts.astype(jnp.int32)
    padded = (counts + rows - 1) // rows * rows
    pad_ends = jnp.cumsum(padded)
    pad_starts = pad_ends - padded
    expert = jnp.arange(N_EXPERTS, dtype=jnp.int32)
    start_of = jnp.sum(jnp.where(ids[..., None] == expert, pad_starts, 0), axis=-1)
    dest = (start_of + rank).reshape(n_tok * TOP_K)
    n_blocks = (n_tok * TOP_K + N_EXPERTS * (rows - 1) + rows - 1) // rows
    blk_row0 = jnp.arange(n_blocks, dtype=jnp.int32) * rows
    blk_expert = jnp.minimum(jnp.sum(pad_ends[None, :] <= blk_row0[:, None], axis=-1),
                             N_EXPERTS - 1).astype(jnp.int32)
    n_used = (pad_ends[-1] // rows).reshape(1).astype(jnp.int32)
    seg_info = jnp.concatenate([pad_ends, padded, n_used]).astype(jnp.int32)
    return dest, blk_expert, n_used, seg_info, n_blocks * rows


def kernel(x, c, w_mod, b_mod, g_pre_mix, g_post_mix, g_pre_ffn, g_post_ffn, w_in, rel_bias, a_re, a_im, log_dt, ssm_b_re, ssm_b_im, ssm_c_re, ssm_c_im, d_skip, w_glu, b_glu, w_branch_attn, w_branch_ssm, w_out, w_router_group, b_router_group, w_router_expert, b_router_expert, w1, w3, w2):
    bsz, seq, d = x.shape
    depth = w_mod.shape[0]
    ssm_width = w_glu.shape[-1]
    n_pat = len(DILATION_PATTERNS)
    for l in range(depth):
        mod3 = _modulation(c, w_mod[l], b_mod[l]).reshape(bsz, 6, d)
        w_in_l = w_in[l].astype(BF16)
        g_pre = g_pre_mix[l].reshape(1, d)
        qkv = _qkv_projection(x, mod3, g_pre, w_in_l[:, :3 * ATTN_WIDTH])
        u_slabs, gates = _ugate_projection(x, mod3, g_pre, w_in_l[:, 3 * ATTN_WIDTH:], ssm_width)
        attn_outs = [_attention_pattern(*qkv[3 * p:3 * p + 3], rel_bias, DILATION_PATTERNS[p][1])
                     for p in range(n_pat)]
        ssm_slabs = _ssm_branch(u_slabs, bsz, a_re[l], a_im[l], log_dt[l], ssm_b_re[l],
                                ssm_b_im[l], ssm_c_re[l], ssm_c_im[l], d_skip[l], w_glu[l], b_glu[l])
        x1, h2, route, counts = _merge_and_route(
            x, gates, attn_outs, ssm_slabs, mod3, g_post_mix[l].reshape(1, d),
            g_pre_ffn[l].reshape(1, d), w_branch_attn[l].astype(BF16),
            w_branch_ssm[l].astype(BF16), w_out[l].astype(BF16),
            w_router_group[l], b_router_group[l], w_router_expert[l], b_router_expert[l])
        route_flat = route.reshape(bsz * seq, LANES)
        dest, blk_expert, n_used, seg_info, cap = _moe_layout(route_flat, counts[0, :N_EXPERTS])
        xs = _dispatch(h2.reshape(bsz * seq * ROW_TILE, LANES), dest, seg_info, cap)
        ys = _experts(xs, blk_expert, n_used, w1[l], w3[l], w2[l])
        x = _combine(ys, dest, x1.reshape(bsz * seq, d), route_flat, mod3,
                     g_post_ffn[l].reshape(1, d), seq).reshape(bsz, seq, d)
    return x
```

```python
import functools
import math

import numpy as np
import jax
import jax.numpy as jnp
from jax import lax
from jax.experimental import pallas as pl
from jax.experimental.pallas import tpu as pltpu

F32 = jnp.float32
BF16 = jnp.bfloat16

N_HEADS = 8
HEAD_DIM = 64
ATTN_WIDTH = N_HEADS * HEAD_DIM
DILATION_PATTERNS = ((128, 1), (512, 4), (2048, 16))
NUM_BUCKETS = 32
MAX_DISTANCE = 2048
N_EXPERT_GROUPS = 4
EXPERTS_PER_GROUP = 8
N_EXPERTS = N_EXPERT_GROUPS * EXPERTS_PER_GROUP
TOP_K = 2
RMS_EPS = 1e-6
NEG_INF = -1e30
LOG2_E = math.log2(math.e)
LN_2 = math.log(2.0)

LANES = 128
SUBLANES = 8
VMEM_LIMIT_BYTES = 56 * 1024 * 1024

ATTN_BLK = 128
ATTN_STEP_ROWS = 512
QKV_ROWS = 512
QKV_MID_DIL = 4
TIME_TILE = 128
MERGE_BATCH = 4
MERGE_CHUNK_BATCH = 2
SSM_STEPS = 128
MOE_ROWS = 512
DISPATCH_ROWS = 2048
COMBINE_ROWS = 512
ROW_TILE = 4
U32 = jnp.uint32
HI_HALF = 0xFFFF0000


def _pack_rows(x):
    w = x.shape[1] // 2
    lo = lax.bitcast_convert_type(x[:, :w].astype(BF16).astype(F32), U32) >> 16
    hi = lax.bitcast_convert_type(x[:, w:].astype(BF16).astype(F32), U32) & U32(HI_HALF)
    return hi | lo


def _unpack_rows(p):
    lo = lax.bitcast_convert_type(p << 16, F32)
    hi = lax.bitcast_convert_type(p & U32(HI_HALF), F32)
    return jnp.concatenate([lo, hi], axis=1)


def _store_row_tiles(ref, packed):
    rows = packed.shape[0]
    for s in range(ROW_TILE):
        ref[pl.ds(s, rows, stride=ROW_TILE), :] = packed[:, s * LANES:(s + 1) * LANES]


def _load_row_tiles(ref, rows):
    return _lane_concat([ref[pl.ds(s, rows, stride=ROW_TILE), :] for s in range(ROW_TILE)])


def _cparams(*sem):
    return pltpu.CompilerParams(dimension_semantics=sem, vmem_limit_bytes=VMEM_LIMIT_BYTES)


def _sigmoid(x):
    return 1.0 / (1.0 + jnp.exp(-x))


def _dot(a, b):
    return jnp.dot(a, b, preferred_element_type=F32)


def _split_bf16(a):
    hi = a.astype(BF16)
    lo = (a - hi.astype(F32)).astype(BF16)
    return hi, lo


def _dot_split(a, w_hi, w_lo):
    a_hi, a_lo = _split_bf16(a)
    return _dot(a_hi, w_hi) + _dot(a_lo, w_hi) + _dot(a_hi, w_lo)


def _rms_norm(x, gain):
    ms = jnp.mean(x * x, axis=-1, keepdims=True)
    return x * lax.rsqrt(ms + RMS_EPS) * gain


def _lane_concat(ref_slabs):
    return jnp.concatenate(ref_slabs, axis=-1)


def _mod_kernel(c_ref, w_ref, b_ref, o_ref):
    c = c_ref[...]
    a = c * _sigmoid(c)
    w_hi, w_lo = _split_bf16(w_ref[...])
    o_ref[...] = _dot_split(a, w_hi, w_lo) + b_ref[...]


def _modulation(c, w_mod, b_mod):
    bsz, d = c.shape
    n = w_mod.shape[1]
    tn = 1024
    return pl.pallas_call(
        _mod_kernel,
        out_shape=jax.ShapeDtypeStruct((bsz, n), F32),
        grid=(n // tn,),
        in_specs=[pl.BlockSpec((bsz, d), lambda j: (0, 0)),
                  pl.BlockSpec((d, tn), lambda j: (0, j)),
                  pl.BlockSpec((1, tn), lambda j: (0, j))],
        out_specs=pl.BlockSpec((bsz, tn), lambda j: (0, j)),
        compiler_params=_cparams("arbitrary"),
        name="mod",
    )(c, w_mod, b_mod.reshape(1, n))


def _qkv_kernel(x_ref, mod_ref, g_ref, w_ref, *rest):
    n_pat = len(DILATION_PATTERNS)
    out_refs, slab, mid = rest[:3 * n_pat], rest[3 * n_pat], rest[3 * n_pat + 1]
    h = _rms_norm(x_ref[0], g_ref[...]) * (1.0 + mod_ref[0, 1:2, :]) + mod_ref[0, 0:1, :]
    hb = h.astype(BF16)
    rows = hb.shape[0]
    per_tensor = ATTN_WIDTH // LANES
    mid_dil = QKV_MID_DIL

    def project(t):
        res = _dot(hb, w_ref[:, t * ATTN_WIDTH:(t + 1) * ATTN_WIDTH])
        if t == 0:
            res = res * (HEAD_DIM ** -0.5 * LOG2_E)
        for s in range(per_tensor):
            slab[t * per_tensor + s] = res[:, s * LANES:(s + 1) * LANES]

    def split(t):
        for p, (_, dil) in enumerate(DILATION_PATTERNS):
            sub = rows // dil
            out = out_refs[3 * p + t]
            for r in range(dil):
                pieces = []
                for s in range(per_tensor):
                    ts = t * per_tensor + s
                    if dil == 1:
                        piece = slab[ts]
                    elif dil == mid_dil:
                        piece = slab[ts, pl.ds(r, sub, stride=dil), :]
                        mid[ts, r] = piece
                    else:
                        ratio = dil // mid_dil
                        piece = mid[ts, r % mid_dil, pl.ds(r // mid_dil, sub, stride=ratio), :]
                    pieces.append(piece)
                out[0, r] = _lane_concat(pieces).astype(out.dtype)

    project(0)
    for t in range(3):
        if t + 1 < 3:
            project(t + 1)
        split(t)


def _qkv_projection(x, mod3, g_pre, w_qkv):
    bsz, seq, d = x.shape
    tm = QKV_ROWS
    out_shape, out_specs = [], []
    for _, dil in DILATION_PATTERNS:
        assert tm % (dil * 2 * SUBLANES) == 0
        for _ in range(3):
            out_shape.append(jax.ShapeDtypeStruct((bsz, dil, seq // dil, ATTN_WIDTH), BF16))
            out_specs.append(pl.BlockSpec((1, dil, tm // dil, ATTN_WIDTH),
                                          lambda b, i: (b, 0, i, 0)))
    return pl.pallas_call(
        _qkv_kernel,
        out_shape=out_shape,
        grid=(bsz, seq // tm),
        in_specs=[pl.BlockSpec((1, tm, d), lambda b, i: (b, i, 0)),
                  pl.BlockSpec((1, 6, d), lambda b, i: (b, 0, 0)),
                  pl.BlockSpec((1, d), lambda b, i: (0, 0)),
                  pl.BlockSpec(w_qkv.shape, lambda b, i: (0, 0))],
        out_specs=out_specs,
        scratch_shapes=[pltpu.VMEM((w_qkv.shape[1] // LANES, tm, LANES), F32),
                        pltpu.VMEM((w_qkv.shape[1] // LANES, QKV_MID_DIL, tm // QKV_MID_DIL, LANES),
                                   F32)],
        compiler_params=_cparams("arbitrary", "arbitrary"),
        name="qkv",
    )(x, mod3, g_pre, w_qkv)


def _ugate_kernel(x_ref, mod_ref, g_ref, w_ref, u_ref, gate_ref):
    bsz, tt, d = x_ref.shape
    shift = mod_ref[:, 0, :][:, None, :]
    scale = mod_ref[:, 1, :][:, None, :]
    h = _rms_norm(x_ref[...], g_ref[...]) * (1.0 + scale) + shift
    hb = h.reshape(bsz * tt, d).astype(BF16)
    n_slab = u_ref.shape[0]
    sw = n_slab * LANES
    u = _dot(hb, w_ref[:, 0:sw])
    for b in range(bsz):
        for s in range(n_slab):
            u_ref[s, pl.ds(b, tt, stride=bsz), :] = u[b * tt:(b + 1) * tt, s * LANES:(s + 1) * LANES]
    gw = gate_ref.shape[-1]
    chunk = 512
    for c0 in range(0, gw, chunk):
        g = _sigmoid(_dot(hb, w_ref[:, sw + c0:sw + c0 + chunk]))
        gate_ref[:, :, c0:c0 + chunk] = g.reshape(bsz, tt, chunk).astype(BF16)


def _ugate_projection(x, mod3, g_pre, w_ug, ssm_width):
    bsz, seq, d = x.shape
    tt = TIME_TILE
    gw = w_ug.shape[1] - ssm_width
    n_slab = ssm_width // LANES
    return pl.pallas_call(
        _ugate_kernel,
        out_shape=(jax.ShapeDtypeStruct((n_slab, seq * bsz, LANES), F32),
                   jax.ShapeDtypeStruct((bsz, seq, gw), BF16)),
        grid=(seq // tt,),
        in_specs=[pl.BlockSpec((bsz, tt, d), lambda i: (0, i, 0)),
                  pl.BlockSpec((bsz, 6, d), lambda i: (0, 0, 0)),
                  pl.BlockSpec((1, d), lambda i: (0, 0)),
                  pl.BlockSpec(w_ug.shape, lambda i: (0, 0))],
        out_specs=(pl.BlockSpec((n_slab, tt * bsz, LANES), lambda i: (0, i, 0)),
                   pl.BlockSpec((bsz, tt, gw), lambda i: (0, i, 0))),
        compiler_params=_cparams("arbitrary"),
        name="ugate",
    )(x, mod3, g_pre, w_ug)


def _t5_bucket_np(dist):
    exact = NUM_BUCKETS // 2
    d_f = np.maximum(dist, exact).astype(np.float32)
    large = exact + (np.log(d_f / np.float32(exact)) / np.float32(math.log(MAX_DISTANCE / exact))
                     * np.float32(NUM_BUCKETS - exact)).astype(np.int32)
    return np.where(dist < exact, dist, np.minimum(large, NUM_BUCKETS - 1))


def _bucket_map_t(dil):
    blk = ATTN_BLK
    ki = np.arange(2 * blk)[:, None]
    qi = np.arange(blk)[None, :]
    return _t5_bucket_np(np.maximum(blk + qi - ki, 0) * dil).astype(np.int32)


def _attn_kernel(relb_ref, q_ref, kc_ref, kp_ref, vc_ref, vp_ref, bucket_ref,
                 o_ref, lse_ref, kbuf, vbuf, bias_t, *, n_sub):
    blk = ATTN_BLK
    first_call = jnp.logical_and(pl.program_id(0) == 0,
                                 jnp.logical_and(pl.program_id(1) == 0, pl.program_id(2) == 0))

    @pl.when(first_call)
    def _():
        bucket = bucket_ref[...]
        ki = lax.broadcasted_iota(jnp.int32, bucket.shape, 0)
        qi = lax.broadcasted_iota(jnp.int32, bucket.shape, 1)
        dist = blk + qi - ki
        band = jnp.logical_and(dist >= 0, dist <= blk)
        band_first = jnp.logical_and(band, ki >= blk)

        def per_head(h, carry):
            acc = jnp.zeros(bucket.shape, F32)
            for b in range(NUM_BUCKETS):
                acc = jnp.where(bucket == b, relb_ref[b, h] * LOG2_E, acc)
            bias_t[0, h] = jnp.where(band_first, acc, NEG_INF)
            bias_t[1, h] = jnp.where(band, acc, NEG_INF)
            return carry

        lax.fori_loop(0, N_HEADS, per_head, 0)

    first_variant = jnp.where(pl.program_id(2) == 0, 0, 1)
    n_res = q_ref.shape[1]
    for g in range(n_res):
        kbuf[g, 0:blk, :] = kp_ref[0, g]
        kbuf[g, blk:, :] = kc_ref[0, g]
        vbuf[g, 0:blk, :] = vp_ref[0, g]
        vbuf[g, blk:, :] = vc_ref[0, g]

    lane = lax.broadcasted_iota(jnp.int32, (1, LANES), 1)
    lo_half = lane < HEAD_DIM
    bd_row = lax.broadcasted_iota(jnp.int32, (4 * blk, LANES), 0)
    bd_col = lax.broadcasted_iota(jnp.int32, (4 * blk, LANES), 1)
    ones_bd = ((bd_row < 2 * blk) == (bd_col < HEAD_DIM)).astype(F32).astype(BF16)
    contract_last = (((1,), (1,)), ((), ()))
    contract_first = (((0,), (0,)), ((), ()))

    def sub_block(g, i):
        r0 = i * blk
        q = q_ref[0, g, r0:r0 + blk, :]
        kk = kbuf[g, r0:r0 + 2 * blk, :]
        vv = vbuf[g, r0:r0 + 2 * blk, :]
        variant = first_variant if i == 0 else 1
        for j in range(N_HEADS // 2):
            cols = slice(j * LANES, (j + 1) * LANES)
            qj, kj, vj = q[:, cols], kk[:, cols], vv[:, cols]
            probs_t, maxes = [], []
            for hh in range(2):
                sel = lo_half if hh == 0 else jnp.logical_not(lo_half)
                qm = jnp.where(sel, qj, jnp.zeros_like(qj))
                s_t = lax.dot_general(kj, qm, contract_last, preferred_element_type=F32)
                s_t = s_t + bias_t[variant, 2 * j + hh]
                m = jnp.max(s_t, axis=0, keepdims=True)
                probs_t.append(jnp.exp2(s_t - m).astype(BF16))
                maxes.append(m)
            p2_t = jnp.concatenate(probs_t, axis=0)
            v_bd = jnp.concatenate([jnp.where(lo_half, vj, jnp.zeros_like(vj)),
                                    jnp.where(lo_half, jnp.zeros_like(vj), vj)], axis=0)
            rhs = jnp.concatenate([v_bd, ones_bd], axis=1)
            ol = lax.dot_general(p2_t, rhs, contract_first, preferred_element_type=F32)
            o2, l2 = ol[:, :LANES], ol[:, LANES:]
            m_t = jnp.concatenate([jnp.broadcast_to(maxes[0], (HEAD_DIM, blk)),
                                   jnp.broadcast_to(maxes[1], (HEAD_DIM, blk))], axis=0)
            o_ref[0, g, r0:r0 + blk, cols] = (o2 / l2).astype(o_ref.dtype)
            lse_ref[0, g, r0:r0 + blk, cols] = (m_t.T + jnp.log2(l2)) * LN_2

    for g in range(n_res):
        for i in range(n_sub):
            sub_block(g, i)


def _attention_pattern(q, k, v, rel_bias, dil):
    bsz, _, sub_len, aw = q.shape
    blk = ATTN_BLK
    assert sub_len % blk == 0
    tq = min(ATTN_STEP_ROWS, sub_len)
    n_sub = tq // blk
    ratio = tq // blk
    n_res = min(dil, ATTN_STEP_ROWS // tq)
    cur = lambda b, r, n: (b, r, n, 0)
    prev = lambda b, r, n: (b, r, jnp.maximum(n * ratio - 1, 0), 0)
    blk_cur = pl.BlockSpec((1, n_res, tq, aw), cur)
    blk_prev = pl.BlockSpec((1, n_res, blk, aw), prev)
    bucket = jnp.asarray(_bucket_map_t(dil))
    o_dtype = BF16 if TIME_TILE // dil >= 2 * SUBLANES else F32
    return pl.pallas_call(
        functools.partial(_attn_kernel, n_sub=n_sub),
        out_shape=(jax.ShapeDtypeStruct(q.shape, o_dtype), jax.ShapeDtypeStruct(q.shape, F32)),
        grid=(bsz, dil // n_res, sub_len // tq),
        in_specs=[pl.BlockSpec(memory_space=pltpu.SMEM),
                  blk_cur, blk_cur, blk_prev, blk_cur, blk_prev,
                  pl.BlockSpec(bucket.shape, lambda b, r, n: (0, 0))],
        out_specs=(blk_cur, blk_cur),
        scratch_shapes=[pltpu.VMEM((n_res, tq + blk, aw), BF16),
                        pltpu.VMEM((n_res, tq + blk, aw), BF16),
                        pltpu.VMEM((2, N_HEADS, 2 * blk, blk), F32)],
        compiler_params=_cparams("arbitrary", "arbitrary", "arbitrary"),
        name=f"attn_dil{dil}",
    )(rel_bias.astype(F32), q, k, k, v, v, bucket)


def _ssm_kernel(u_ref, bmat_ref, cmat_ref, ar_ref, ai_ref, dskip_ref, wglu_ref, bglu_ref,
                o_ref, hbuf, hstate, *, n_steps):
    @pl.when(pl.program_id(0) == 0)
    def _():
        hstate[...] = jnp.zeros_like(hstate)

    n_slab = u_ref.shape[0]
    n_state = hbuf.shape[1] // 2
    per = n_state // n_slab
    us = [u_ref[s] for s in range(n_slab)]
    ys = [None] * n_slab

    def drive(s):
        bu = _dot(us[s].astype(BF16), bmat_ref[s])
        hbuf[:, s * per:(s + 1) * per] = bu[:, :per]
        hbuf[:, n_state + s * per:n_state + (s + 1) * per] = bu[:, per:]

    def scan(s):
        re_cols = slice(s * per, (s + 1) * per)
        im_cols = slice(n_state + s * per, n_state + (s + 1) * per)
        ar = ar_ref[:, re_cols]
        ai = ai_ref[:, re_cols]
        hr = hstate[:, re_cols]
        hi = hstate[:, im_cols]
        for t in range(n_steps):
            trow = slice(t * SUBLANES, (t + 1) * SUBLANES)
            nr = ar * hr - ai * hi + hbuf[trow, re_cols]
            ni = ar * hi + ai * hr + hbuf[trow, im_cols]
            hbuf[trow, re_cols] = nr
            hbuf[trow, im_cols] = ni
            hr, hi = nr, ni
        hstate[:, re_cols] = hr
        hstate[:, im_cols] = hi

    def read_out(s):
        h_s = _lane_concat([hbuf[:, s * per:(s + 1) * per],
                            hbuf[:, n_state + s * per:n_state + (s + 1) * per]])
        ys[s] = (_dot(h_s.astype(BF16), cmat_ref[s])
                 + dskip_ref[:, s * LANES:(s + 1) * LANES] * us[s])

    for tick in range(n_slab + 2):
        if tick < n_slab:
            drive(tick)
        if 0 <= tick - 1 < n_slab:
            scan(tick - 1)
        if 0 <= tick - 2 < n_slab:
            read_out(tick - 2)
    y = _lane_concat(ys)
    y = 0.5 * y * (1.0 + jnp.tanh(math.sqrt(2.0 / math.pi) * (y + 0.044715 * (y * y * y))))
    z = _dot(y.astype(BF16), wglu_ref[...]) + bglu_ref[...]
    out = y * _sigmoid(z)
    for s in range(n_slab):
        o_ref[s] = out[:, s * LANES:(s + 1) * LANES]


def _ssm_params(a_re, a_im, log_dt, b_re, b_im, c_re, c_im, bsz):
    g, p = a_re.shape
    hg = b_re.shape[-1]
    dt = jnp.exp(log_dt.astype(F32))[:, None]
    a_re, a_im = a_re.astype(F32), a_im.astype(F32)
    mag = jnp.exp(a_re * dt)
    abar_re = mag * jnp.cos(a_im * dt)
    abar_im = mag * jnp.sin(a_im * dt)
    den = a_re * a_re + a_im * a_im
    q_re = ((abar_re - 1.0) * a_re + abar_im * a_im) / den
    q_im = (abar_im * a_re - (abar_re - 1.0) * a_im) / den
    b_re, b_im = b_re.astype(F32), b_im.astype(F32)
    bb_re = q_re[..., None] * b_re - q_im[..., None] * b_im
    bb_im = q_re[..., None] * b_im + q_im[..., None] * b_re
    gs = LANES // hg
    n_slab = g // gs
    eye = jnp.eye(gs, dtype=F32)

    def in_mat(t):
        t = t.reshape(n_slab, gs, p, hg)
        return jnp.einsum('sgph,gk->sghkp', t, eye).reshape(n_slab, gs * hg, gs * p)

    def out_mat(t):
        t = t.reshape(n_slab, gs, hg, p)
        return jnp.einsum('sghp,gk->sgpkh', t, eye).reshape(n_slab, gs * p, gs * hg)

    bmat = jnp.concatenate([in_mat(bb_re), in_mat(bb_im)], axis=2).astype(BF16)
    cmat = jnp.concatenate([out_mat(c_re.astype(F32)), -out_mat(c_im.astype(F32))],
                           axis=1).astype(BF16)
    ar = jnp.broadcast_to(abar_re.reshape(1, g * p), (bsz, g * p))
    ai = jnp.broadcast_to(abar_im.reshape(1, g * p), (bsz, g * p))
    return bmat, cmat, ar, ai


def _ssm_branch(u_slabs, bsz, a_re, a_im, log_dt, b_re, b_im, c_re, c_im, d_skip, w_glu, b_glu):
    n_slab, n_rows, _ = u_slabs.shape
    width = n_slab * LANES
    assert bsz == SUBLANES
    bmat, cmat, ar, ai = _ssm_params(a_re, a_im, log_dt, b_re, b_im, c_re, c_im, bsz)
    n_state2 = n_slab * bmat.shape[2]
    rows = SSM_STEPS * bsz
    const = lambda c: (0, 0)
    const3 = lambda c: (0, 0, 0)
    slab_spec = pl.BlockSpec((n_slab, rows, LANES), lambda c: (0, c, 0))
    return pl.pallas_call(
        functools.partial(_ssm_kernel, n_steps=SSM_STEPS),
        out_shape=jax.ShapeDtypeStruct(u_slabs.shape, F32),
        grid=(n_rows // rows,),
        in_specs=[slab_spec,
                  pl.BlockSpec(bmat.shape, const3), pl.BlockSpec(cmat.shape, const3),
                  pl.BlockSpec(ar.shape, const), pl.BlockSpec(ai.shape, const),
                  pl.BlockSpec((1, width), const), pl.BlockSpec((width, width), const),
                  pl.BlockSpec((1, width), const)],
        out_specs=slab_spec,
        scratch_shapes=[pltpu.VMEM((rows, n_state2), F32), pltpu.VMEM((bsz, n_state2), F32)],
        compiler_params=_cparams("arbitrary"),
        name="ssm",
    )(u_slabs, bmat, cmat, ar, ai, d_skip.reshape(1, width).astype(F32),
      w_glu.astype(BF16), b_glu.reshape(1, width).astype(F32))


ROUTE_LANES = {"id0": 0, "id1": 1, "rank0": 2, "rank1": 3, "w0": 4, "w1": 5}
GROUP_LANE0 = N_EXPERTS


def _merge_kernel(*refs, bsz_total):
    n_pat = len(DILATION_PATTERNS)
    x_ref, gate_ref = refs[0:2]
    attn_refs = refs[2:2 + 2 * n_pat]
    (ssm_ref, mod_ref, gpost_ref, gpre_ref, wba_ref, wbs_ref, wout_ref,
     wr_cat_ref, wr_hi_ref, br_ref) = refs[2 + 2 * n_pat:12 + 2 * n_pat]
    x1_ref, h2_ref, route_ref, count_ref = refs[12 + 2 * n_pat:16 + 2 * n_pat]
    scratch = refs[16 + 2 * n_pat:]
    carry = scratch[-1]

    @pl.when(jnp.logical_and(pl.program_id(0) == 0, pl.program_id(1) == 0))
    def _():
        carry[...] = jnp.zeros_like(carry)

    nb, tt, d = x_ref.shape
    n_slab = ssm_ref.shape[0]
    b0 = pl.program_id(0) * nb
    cb = MERGE_CHUNK_BATCH
    rows = cb * tt
    lane = lax.broadcasted_iota(jnp.int32, (rows, LANES), 1).astype(F32)
    row = lax.broadcasted_iota(jnp.int32, (rows, rows), 0)
    col = lax.broadcasted_iota(jnp.int32, (rows, rows), 1)
    strict_lower = (col < row).astype(BF16)
    running = [carry[...]]

    def chunk_phases(c0):
        bbs = range(c0, c0 + cb)
        o_tok, lse_tok, ssm_tok = scratch[3 * (c0 // cb):3 * (c0 // cb) + 3]

        for bb in bbs:
            for s in range(n_slab):
                ssm_tok[(bb - c0) * tt:(bb - c0 + 1) * tt, s * LANES:(s + 1) * LANES] = (
                    ssm_ref[s, pl.ds(b0 + bb, tt, stride=bsz_total), :])

        slot = 0
        sources = []
        for p, (_, dil) in enumerate(DILATION_PATTERNS):
            o_ref, lse_ref = attn_refs[2 * p], attn_refs[2 * p + 1]
            if dil == 1:
                sources.append((o_ref, lse_ref, None))
                continue
            sub = tt // dil
            for bb in bbs:
                for r in range(dil):
                    o_blk = o_ref[bb, r].astype(F32)
                    l_blk = lse_ref[bb, r]
                    for s in range(n_slab):
                        dst = pl.ds((bb - c0) * tt + r, sub, stride=dil)
                        o_tok[slot, s, dst, :] = o_blk[:, s * LANES:(s + 1) * LANES]
                        lse_tok[slot, s, dst, :] = l_blk[:, s * LANES:(s + 1) * LANES]
            sources.append((o_ref, lse_ref, slot))
            slot += 1
        attn_slabs = []
        for s in range(n_slab):
            cols = slice(s * LANES, (s + 1) * LANES)
            o_ps, lse_ps = [], []
            for o_ref, lse_ref, src_slot in sources:
                if src_slot is None:
                    o_ps.append(o_ref[c0:c0 + cb, 0, :, cols].astype(F32).reshape(rows, LANES))
                    lse_ps.append(lse_ref[c0:c0 + cb, 0, :, cols].reshape(rows, LANES))
                else:
                    o_ps.append(o_tok[src_slot, s])
                    lse_ps.append(lse_tok[src_slot, s])
            m = functools.reduce(jnp.maximum, lse_ps)
            es = [jnp.exp(l - m) for l in lse_ps]
            num = functools.reduce(lambda a, b: a + b, [e * o for e, o in zip(es, o_ps)])
            den = functools.reduce(lambda a, b: a + b, es)
            attn_slabs.append(num / den)
        attn_b = _lane_concat(attn_slabs).astype(BF16)
        ssm_b = ssm_tok[...].astype(BF16)
        yield
        branch_attn = _dot(attn_b, wba_ref[...])
        branch_ssm = _dot(ssm_b, wbs_ref[...])
        yield
        g_attn = gate_ref[c0:c0 + cb, :, 0:d].astype(F32).reshape(rows, d)
        g_ssm = gate_ref[c0:c0 + cb, :, d:].astype(F32).reshape(rows, d)
        merged_b = (g_attn * branch_attn + g_ssm * branch_ssm).astype(BF16)
        yield
        y = _dot(merged_b, wout_ref[...])
        yield
        gate1 = mod_ref[c0:c0 + cb, 2, :][:, None, :]
        shift2 = mod_ref[c0:c0 + cb, 3, :][:, None, :]
        scale2 = mod_ref[c0:c0 + cb, 4, :][:, None, :]
        x1 = x_ref[c0:c0 + cb] + gate1 * _rms_norm(y, gpost_ref[...]).reshape(cb, tt, d)
        x1_ref[c0:c0 + cb] = x1
        h2 = _rms_norm(x1, gpre_ref[...]) * (1.0 + scale2) + shift2
        for j, bb in enumerate(bbs):
            _store_row_tiles(h2_ref.at[bb], _pack_rows(h2[j]))

        a_hi, a_lo = _split_bf16(h2.reshape(rows, d))
        yield
        hi_pass = _dot(a_hi, wr_cat_ref[...])
        lo_pass = _dot(a_lo, wr_hi_ref[...])
        yield
        logits = hi_pass[:, :LANES] + lo_pass + hi_pass[:, LANES:] + br_ref[...]
        big = float(LANES)
        is_group = jnp.logical_and(lane >= GROUP_LANE0, lane < GROUP_LANE0 + N_EXPERT_GROUPS)
        gl = jnp.where(is_group, logits, -jnp.inf)
        g_max = jnp.max(gl, axis=-1, keepdims=True)
        g_sel = jnp.min(jnp.where(gl == g_max, lane, big), axis=-1, keepdims=True) - GROUP_LANE0
        g_gate = 1.0 / jnp.sum(jnp.exp(gl - g_max), axis=-1, keepdims=True)
        lo = g_sel * EXPERTS_PER_GROUP
        in_group = jnp.logical_and(lane >= lo, lane < lo + EXPERTS_PER_GROUP)
        el = jnp.where(in_group, logits, -jnp.inf)
        t0 = jnp.max(el, axis=-1, keepdims=True)
        i0 = jnp.min(jnp.where(el == t0, lane, big), axis=-1, keepdims=True)
        el1 = jnp.where(lane == i0, -jnp.inf, el)
        t1 = jnp.max(el1, axis=-1, keepdims=True)
        i1 = jnp.min(jnp.where(el1 == t1, lane, big), axis=-1, keepdims=True)
        e = jnp.exp(t1 - t0)
        w0 = g_gate / (1.0 + e)
        w1 = g_gate * e / (1.0 + e)

        hit0 = lane == i0
        hit1 = lane == i1
        onehot = jnp.logical_or(hit0, hit1).astype(F32)
        yield
        before = _dot(strict_lower, onehot.astype(BF16)) + running[0]
        rank0 = jnp.sum(jnp.where(hit0, before, 0.0), axis=-1, keepdims=True)
        rank1 = jnp.sum(jnp.where(hit1, before, 0.0), axis=-1, keepdims=True)
        running[0] = running[0] + jnp.sum(onehot, axis=0, keepdims=True)

        route = jnp.zeros((rows, LANES), F32)
        for name, val in (("id0", i0), ("id1", i1), ("rank0", rank0), ("rank1", rank1),
                          ("w0", w0), ("w1", w1)):
            route = jnp.where(lane == ROUTE_LANES[name], val, route)
        route_ref[c0:c0 + cb] = route.reshape(cb, tt, LANES)

    pending = [chunk_phases(c0) for c0 in range(0, nb, cb)]
    active = []
    while pending or active:
        if pending:
            active.append(pending.pop(0))
        for gen in list(active):
            if next(gen, StopIteration) is StopIteration:
                active.remove(gen)

    carry[...] = running[0]
    count_ref[...] = jnp.broadcast_to(running[0], count_ref.shape)


def _merge_and_route(x, gates, attn_outs, ssm_slabs, mod3, g_post, g_pre, wba, wbs, wout,
                     w_rg, b_rg, w_re, b_re):
    bsz, seq, d = x.shape
    tt, nb = TIME_TILE, MERGE_BATCH
    aw = ATTN_WIDTH
    n_slab = ssm_slabs.shape[0]
    assert aw == n_slab * LANES
    wr = jnp.zeros((d, LANES), F32).at[:, :N_EXPERTS].set(w_re.astype(F32))
    wr = wr.at[:, GROUP_LANE0:GROUP_LANE0 + N_EXPERT_GROUPS].set(w_rg.astype(F32))
    br = jnp.zeros((1, LANES), F32).at[0, :N_EXPERTS].set(b_re.astype(F32))
    br = br.at[0, GROUP_LANE0:GROUP_LANE0 + N_EXPERT_GROUPS].set(b_rg.astype(F32))
    wr_hi, wr_lo = _split_bf16(wr)
    wr_cat = jnp.concatenate([wr_hi, wr_lo], axis=1)
    n_strided = sum(1 for _, dil in DILATION_PATTERNS if dil > 1)
    chunk_rows = MERGE_CHUNK_BATCH * tt
    tok = lambda h, i: (h, i, 0)
    const = lambda h, i: (0, 0)
    attn_args, attn_specs = [], []
    for (o_p, lse_p), (_, dil) in zip(attn_outs, DILATION_PATTERNS):
        spec = pl.BlockSpec((nb, dil, tt // dil, aw), lambda h, i: (h, 0, i, 0))
        attn_args += [o_p, lse_p]
        attn_specs += [spec, spec]
    return pl.pallas_call(
        functools.partial(_merge_kernel, bsz_total=bsz),
        out_shape=(jax.ShapeDtypeStruct((bsz, seq, d), F32),
                   jax.ShapeDtypeStruct((bsz, seq * ROW_TILE, LANES), U32),
                   jax.ShapeDtypeStruct((bsz, seq, LANES), F32),
                   jax.ShapeDtypeStruct((SUBLANES, LANES), F32)),
        grid=(bsz // nb, seq // tt),
        in_specs=[pl.BlockSpec((nb, tt, d), tok),
                  pl.BlockSpec((nb, tt, gates.shape[-1]), tok)]
                 + attn_specs
                 + [pl.BlockSpec((n_slab, tt * bsz, LANES), lambda h, i: (0, i, 0)),
                    pl.BlockSpec((nb, 6, d), lambda h, i: (h, 0, 0)),
                    pl.BlockSpec((1, d), const), pl.BlockSpec((1, d), const),
                    pl.BlockSpec(wba.shape, const), pl.BlockSpec(wbs.shape, const),
                    pl.BlockSpec(wout.shape, const),
                    pl.BlockSpec((d, 2 * LANES), const), pl.BlockSpec((d, LANES), const),
                    pl.BlockSpec((1, LANES), const)],
        out_specs=(pl.BlockSpec((nb, tt, d), tok), pl.BlockSpec((nb, tt * ROW_TILE, LANES), tok),
                   pl.BlockSpec((nb, tt, LANES), tok),
                   pl.BlockSpec((SUBLANES, LANES), const)),
        scratch_shapes=[pltpu.VMEM((n_strided, n_slab, chunk_rows, LANES), F32),
                        pltpu.VMEM((n_strided, n_slab, chunk_rows, LANES), F32),
                        pltpu.VMEM((chunk_rows, n_slab * LANES), F32)] * (nb // MERGE_CHUNK_BATCH)
                       + [pltpu.VMEM((1, LANES), F32)],
        compiler_params=_cparams("arbitrary", "arbitrary"),
        name="merge",
    )(x, gates, *attn_args, ssm_slabs, mod3, g_post, g_pre, wba, wbs, wout, wr_cat, wr_hi, br)


ISSUE_UNROLL = 8


def _dispatch_kernel(dest_ref, seg_ref, h_ref, xs_ref, zero_buf, sem, zsem):
    rows = h_ref.shape[0] // ROW_TILE
    blk = zero_buf.shape[0]
    n_blocks = xs_ref.shape[0] // blk
    base = pl.program_id(0) * rows

    @pl.when(pl.program_id(0) == 0)
    def _():
        zero_buf[...] = jnp.zeros_like(zero_buf)

        def zero_copy(row0):
            return pltpu.make_async_copy(zero_buf, xs_ref.at[pl.ds(pl.multiple_of(row0, blk), blk)],
                                         zsem)

        def fill_tail(e, carry):
            @pl.when(seg_ref[N_EXPERTS + e] > 0)
            def _():
                zero_copy(seg_ref[e] * ROW_TILE - blk).start()
            return carry

        def fill_unused(j, carry):
            zero_copy(j * blk).start()
            return carry

        def wait_tail(e, carry):
            @pl.when(seg_ref[N_EXPERTS + e] > 0)
            def _():
                zero_copy(0).wait()
            return carry

        def wait_unused(j, carry):
            zero_copy(0).wait()
            return carry

        n_used = seg_ref[2 * N_EXPERTS]
        lax.fori_loop(0, N_EXPERTS, fill_tail, 0)
        lax.fori_loop(n_used, n_blocks, fill_unused, 0)
        lax.fori_loop(0, N_EXPERTS, wait_tail, 0)
        lax.fori_loop(n_used, n_blocks, wait_unused, 0)

    group = ISSUE_UNROLL * ROW_TILE

    def issue(g, carry):
        g0 = pl.multiple_of(g * group, group)
        for rr in range(ISSUE_UNROLL):
            src = h_ref.at[pl.ds(g0 + rr * ROW_TILE, ROW_TILE)]
            for k in range(TOP_K):
                d = dest_ref[TOP_K * (base + g * ISSUE_UNROLL + rr) + k]
                dst = xs_ref.at[pl.ds(pl.multiple_of(d * ROW_TILE, ROW_TILE), ROW_TILE)]
                pltpu.make_async_copy(src, dst, sem).start(priority=k)
        return carry

    lax.fori_loop(0, rows // ISSUE_UNROLL, issue, 0)
    for _ in range(TOP_K):
        pltpu.make_async_copy(h_ref, xs_ref.at[pl.ds(0, rows * ROW_TILE)], sem).wait()


def _dispatch(h2_tiles, dest, seg_info, cap):
    n_tok = h2_tiles.shape[0] // ROW_TILE
    rows = DISPATCH_ROWS
    return pl.pallas_call(
        _dispatch_kernel,
        out_shape=jax.ShapeDtypeStruct((cap * ROW_TILE, LANES), h2_tiles.dtype),
        grid_spec=pltpu.PrefetchScalarGridSpec(
            num_scalar_prefetch=2,
            grid=(n_tok // rows,),
            in_specs=[pl.BlockSpec((rows * ROW_TILE, LANES), lambda i, dest, seg: (i, 0))],
            out_specs=pl.BlockSpec(memory_space=pl.ANY),
            scratch_shapes=[pltpu.VMEM((MOE_ROWS * ROW_TILE, LANES), h2_tiles.dtype),
                            pltpu.SemaphoreType.DMA, pltpu.SemaphoreType.DMA]),
        compiler_params=_cparams("arbitrary"),
        name="dispatch",
    )(dest, seg_info, h2_tiles)


def _expert_kernel(blk_expert_ref, seg_slot_ref, next_expert_ref, n_used_ref, x_ref,
                   w1_hbm, w3_hbm, w2_hbm, y_ref, w1f, w3f, w2f, w1b, w3b, w2b, sems):
    i = pl.program_id(0)
    used = i < n_used_ref[0]
    rows = x_ref.shape[0] // ROW_TILE
    expert = blk_expert_ref[i]
    new_expert = jnp.logical_or(i == 0, expert != blk_expert_ref[jnp.maximum(i - 1, 0)])
    slot = seg_slot_ref[i]

    def weight_copies(e, s):
        return [pltpu.make_async_copy(hbm.at[e], stage.at[s], sems.at[s, n])
                for n, (hbm, stage) in enumerate(((w1_hbm, w1f), (w3_hbm, w3f), (w2_hbm, w2f)))]

    @pl.when(jnp.logical_and(used, new_expert))
    def _():
        @pl.when(i == 0)
        def _():
            for cp in weight_copies(expert, slot):
                cp.start()

        for cp in weight_copies(expert, slot):
            cp.wait()
        w1b[...] = w1f[slot].astype(BF16)
        w3b[...] = w3f[slot].astype(BF16)
        w2b[...] = w2f[slot].astype(BF16)
        nxt = next_expert_ref[i]

        @pl.when(nxt >= 0)
        def _():
            for cp in weight_copies(nxt, 1 - slot):
                cp.start()

    @pl.when(used)
    def _():
        xb = _unpack_rows(_load_row_tiles(x_ref, rows)).astype(BF16)
        h1 = _dot(xb, w1b[...])
        h3 = _dot(xb, w3b[...])
        act = (h1 * _sigmoid(h1)) * h3
        y = _dot(act.astype(BF16), w2b[...])
        _store_row_tiles(y_ref, _pack_rows(y))

    @pl.when(jnp.logical_not(used))
    def _():
        y_ref[...] = jnp.zeros_like(y_ref)


def _experts(xs, blk_expert, n_used, w1, w3, w2):
    d, de = w1.shape[-2:]
    assert d == 2 * ROW_TILE * LANES
    rows = MOE_ROWS
    tile_rows = rows * ROW_TILE
    n_blocks = xs.shape[0] // tile_rows
    blk = jnp.arange(n_blocks, dtype=jnp.int32)
    changed = jnp.concatenate([jnp.zeros((1,), jnp.int32),
                               (blk_expert[1:] != blk_expert[:-1]).astype(jnp.int32)])
    seg_slot = (jnp.cumsum(changed) % 2).astype(jnp.int32)
    later = jnp.logical_and(blk[None, :] > blk[:, None],
                            jnp.logical_and(blk_expert[None, :] != blk_expert[:, None],
                                            blk[None, :] < n_used[0]))
    first_later = jnp.argmax(later, axis=1)
    next_expert = jnp.where(jnp.any(later, axis=1), blk_expert[first_later], -1).astype(jnp.int32)
    xblk = lambda i, be, ss, ne, nu: (jnp.minimum(i, nu[0] - 1), 0)
    return pl.pallas_call(
        _expert_kernel,
        out_shape=jax.ShapeDtypeStruct(xs.shape, U32),
        grid_spec=pltpu.PrefetchScalarGridSpec(
            num_scalar_prefetch=4,
            grid=(n_blocks,),
            in_specs=[pl.BlockSpec((tile_rows, LANES), xblk),
                      pl.BlockSpec(memory_space=pl.ANY), pl.BlockSpec(memory_space=pl.ANY),
                      pl.BlockSpec(memory_space=pl.ANY)],
            out_specs=pl.BlockSpec((tile_rows, LANES), lambda i, be, ss, ne, nu: (i, 0)),
            scratch_shapes=[pltpu.VMEM((2, d, de), w1.dtype), pltpu.VMEM((2, d, de), w3.dtype),
                            pltpu.VMEM((2, de, d), w2.dtype),
                            pltpu.VMEM((d, de), BF16), pltpu.VMEM((d, de), BF16),
                            pltpu.VMEM((de, d), BF16),
                            pltpu.SemaphoreType.DMA((2, 3))]),
        compiler_params=_cparams("arbitrary"),
        name="experts",
    )(blk_expert, seg_slot, next_expert, n_used, xs, w1, w3, w2)


def _combine_kernel(dest_ref, ys_ref, x1_ref, route_ref, mod_ref, g_ref, o_ref, ybuf, sem):
    rows = x1_ref.shape[0]
    step = pl.program_id(0)
    n_step = pl.num_programs(0)

    group = ISSUE_UNROLL * ROW_TILE

    def gather(tile, slot):
        base = tile * rows

        def issue(g, carry):
            g0 = pl.multiple_of(g * group, group)
            for rr in range(ISSUE_UNROLL):
                for k in range(TOP_K):
                    d = dest_ref[TOP_K * (base + g * ISSUE_UNROLL + rr) + k]
                    src = ys_ref.at[pl.ds(pl.multiple_of(d * ROW_TILE, ROW_TILE), ROW_TILE)]
                    dst = ybuf.at[slot, k, pl.ds(g0 + rr * ROW_TILE, ROW_TILE)]
                    pltpu.make_async_copy(src, dst, sem.at[slot]).start(priority=k)
            return carry

        lax.fori_loop(0, rows // ISSUE_UNROLL, issue, 0)

    slot = lax.rem(step, 2)

    @pl.when(step == 0)
    def _():
        gather(step, 0)

    @pl.when(step + 1 < n_step)
    def _():
        gather(step + 1, 1 - slot)

    for k in range(TOP_K):
        pltpu.make_async_copy(ys_ref.at[pl.ds(0, rows * ROW_TILE)], ybuf.at[slot, k],
                              sem.at[slot]).wait()
    route = route_ref[...]
    w0 = route[:, ROUTE_LANES["w0"]:ROUTE_LANES["w0"] + 1]
    w1 = route[:, ROUTE_LANES["w1"]:ROUTE_LANES["w1"] + 1]

    def rows_of(k):
        return _unpack_rows(_load_row_tiles(ybuf.at[slot, k], rows))

    y = rows_of(0) * w0 + rows_of(1) * w1
    o_ref[...] = x1_ref[...] + mod_ref[0, 5:6, :] * _rms_norm(y, g_ref[...])


def _combine(ys, dest, x1_flat, route_flat, mod3, g_post, seq):
    n_tok, d = x1_flat.shape
    rows = COMBINE_ROWS
    assert seq % rows == 0
    return pl.pallas_call(
        _combine_kernel,
        out_shape=jax.ShapeDtypeStruct((n_tok, d), F32),
        grid_spec=pltpu.PrefetchScalarGridSpec(
            num_scalar_prefetch=1,
            grid=(n_tok // rows,),
            in_specs=[pl.BlockSpec(memory_space=pl.ANY),
                      pl.BlockSpec((rows, d), lambda i, dest: (i, 0)),
                      pl.BlockSpec((rows, LANES), lambda i, dest: (i, 0)),
                      pl.BlockSpec((1, 6, d), lambda i, dest: (i * rows // seq, 0, 0)),
                      pl.BlockSpec((1, d), lambda i, dest: (0, 0))],
            out_specs=pl.BlockSpec((rows, d), lambda i, dest: (i, 0)),
            scratch_shapes=[pltpu.VMEM((2, TOP_K, rows * ROW_TILE, LANES), U32),
                            pltpu.SemaphoreType.DMA((2,))]),
        compiler_params=_cparams("arbitrary"),
        name="combine",
    )(dest, ys, x1_flat, route_flat, mod3, g_post)


def _moe_layout(route_flat, counts):
    rows = MOE_ROWS
    n_tok = route_flat.shape[0]
    ids = route_flat[:, 0:TOP_K].astype(jnp.int32)
    rank = route_flat[:, TOP_K:2 * TOP_K].astype(jnp.int32)
    counts = counts.astype(jnp.int32)
    padded = (counts + rows - 1) // rows * rows
    pad_ends = jnp.cumsum(padded)
    pad_starts = pad_ends - padded
    expert = jnp.arange(N_EXPERTS, dtype=jnp.int32)
    start_of = jnp.sum(jnp.where(ids[..., None] == expert, pad_starts, 0), axis=-1)
    dest = (start_of + rank).reshape(n_tok * TOP_K)
    n_blocks = (n_tok * TOP_K + N_EXPERTS * (rows - 1) + rows - 1) // rows
    blk_row0 = jnp.arange(n_blocks, dtype=jnp.int32) * rows
    blk_expert = jnp.minimum(jnp.sum(pad_ends[None, :] <= blk_row0[:, None], axis=-1),
                             N_EXPERTS - 1).astype(jnp.int32)
    n_used = (pad_ends[-1] // rows).reshape(1).astype(jnp.int32)
    seg_info = jnp.concatenate([pad_ends, padded, n_used]).astype(jnp.int32)
    return dest, blk_expert, n_used, seg_info, n_blocks * rows


def kernel(x, c, w_mod, b_mod, g_pre_mix, g_post_mix, g_pre_ffn, g_post_ffn, w_in, rel_bias, a_re, a_im, log_dt, ssm_b_re, ssm_b_im, ssm_c_re, ssm_c_im, d_skip, w_glu, b_glu, w_branch_attn, w_branch_ssm, w_out, w_router_group, b_router_group, w_router_expert, b_router_expert, w1, w3, w2):
    bsz, seq, d = x.shape
    depth = w_mod.shape[0]
    ssm_width = w_glu.shape[-1]
    n_pat = len(DILATION_PATTERNS)
    for l in range(depth):
        mod3 = _modulation(c, w_mod[l], b_mod[l]).reshape(bsz, 6, d)
        w_in_l = w_in[l].astype(BF16)
        g_pre = g_pre_mix[l].reshape(1, d)
        qkv = _qkv_projection(x, mod3, g_pre, w_in_l[:, :3 * ATTN_WIDTH])
        u_slabs, gates = _ugate_projection(x, mod3, g_pre, w_in_l[:, 3 * ATTN_WIDTH:], ssm_width)
        attn_outs = [_attention_pattern(*qkv[3 * p:3 * p + 3], rel_bias, DILATION_PATTERNS[p][1])
                     for p in range(n_pat)]
        ssm_slabs = _ssm_branch(u_slabs, bsz, a_re[l], a_im[l], log_dt[l], ssm_b_re[l],
                                ssm_b_im[l], ssm_c_re[l], ssm_c_im[l], d_skip[l], w_glu[l], b_glu[l])
        x1, h2, route, counts = _merge_and_route(
            x, gates, attn_outs, ssm_slabs, mod3, g_post_mix[l].reshape(1, d),
            g_pre_ffn[l].reshape(1, d), w_branch_attn[l].astype(BF16),
            w_branch_ssm[l].astype(BF16), w_out[l].astype(BF16),
            w_router_group[l], b_router_group[l], w_router_expert[l], b_router_expert[l])
        route_flat = route.reshape(bsz * seq, LANES)
        dest, blk_expert, n_used, seg_info, cap = _moe_layout(route_flat, counts[0, :N_EXPERTS])
        xs = _dispatch(h2.reshape(bsz * seq * ROW_TILE, LANES), dest, seg_info, cap)
        ys = _experts(xs, blk_expert, n_used, w1[l], w3[l], w2[l])
        x = _combine(ys, dest, x1.reshape(bsz * seq, d), route_flat, mod3,
                     g_post_ffn[l].reshape(1, d), seq).reshape(bsz, seq, d)
    return x
```

```python
import functools
import math

import numpy as np
import jax
import jax.numpy as jnp
from jax import lax
from jax.experimental import pallas as pl
from jax.experimental.pallas import tpu as pltpu

F32 = jnp.float32
BF16 = jnp.bfloat16

N_HEADS = 8
HEAD_DIM = 64
ATTN_WIDTH = N_HEADS * HEAD_DIM
DILATION_PATTERNS = ((128, 1), (512, 4), (2048, 16))
NUM_BUCKETS = 32
MAX_DISTANCE = 2048
N_EXPERT_GROUPS = 4
EXPERTS_PER_GROUP = 8
N_EXPERTS = N_EXPERT_GROUPS * EXPERTS_PER_GROUP
TOP_K = 2
RMS_EPS = 1e-6
NEG_INF = -1e30
LOG2_E = math.log2(math.e)
LN_2 = math.log(2.0)

LANES = 128
SUBLANES = 8
VMEM_LIMIT_BYTES = 56 * 1024 * 1024

ATTN_BLK = 128
ATTN_STEP_ROWS = 2048
QKV_ROWS = 1024
QKV_MID_DIL = 4
TIME_TILE = 128
MERGE_BATCH = 4
MERGE_CHUNK_BATCH = 2
SSM_STEPS = 128
MOE_ROWS = 512
DISPATCH_ROWS = 2048
COMBINE_ROWS = 1024
ROW_TILE = 4
U32 = jnp.uint32
HI_HALF = 0xFFFF0000


def _pack_rows(x):
    w = x.shape[1] // 2
    lo = lax.bitcast_convert_type(x[:, :w].astype(BF16).astype(F32), U32) >> 16
    hi = lax.bitcast_convert_type(x[:, w:].astype(BF16).astype(F32), U32) & U32(HI_HALF)
    return hi | lo


def _unpack_rows(p):
    lo = lax.bitcast_convert_type(p << 16, F32)
    hi = lax.bitcast_convert_type(p & U32(HI_HALF), F32)
    return jnp.concatenate([lo, hi], axis=1)


def _store_row_tiles(ref, packed):
    rows = packed.shape[0]
    for s in range(ROW_TILE):
        ref[pl.ds(s, rows, stride=ROW_TILE), :] = packed[:, s * LANES:(s + 1) * LANES]


def _load_row_tiles(ref, rows):
    return _lane_concat([ref[pl.ds(s, rows, stride=ROW_TILE), :] for s in range(ROW_TILE)])


def _cparams(*sem):
    return pltpu.CompilerParams(dimension_semantics=sem, vmem_limit_bytes=VMEM_LIMIT_BYTES)


def _sigmoid(x):
    return 1.0 / (1.0 + jnp.exp(-x))


def _dot(a, b):
    return jnp.dot(a, b, preferred_element_type=F32)


def _split_bf16(a):
    hi = a.astype(BF16)
    lo = (a - hi.astype(F32)).astype(BF16)
    return hi, lo


def _dot_split(a, w_hi, w_lo):
    a_hi, a_lo = _split_bf16(a)
    return _dot(a_hi, w_hi) + _dot(a_lo, w_hi) + _dot(a_hi, w_lo)


def _rms_norm(x, gain):
    ms = jnp.mean(x * x, axis=-1, keepdims=True)
    return x * lax.rsqrt(ms + RMS_EPS) * gain


def _lane_concat(ref_slabs):
    return jnp.concatenate(ref_slabs, axis=-1)


def _mod_kernel(c_ref, w_ref, b_ref, o_ref):
    c = c_ref[...]
    a = c * _sigmoid(c)
    w_hi, w_lo = _split_bf16(w_ref[...])
    o_ref[...] = _dot_split(a, w_hi, w_lo) + b_ref[...]


def _modulation(c, w_mod, b_mod):
    bsz, d = c.shape
    n = w_mod.shape[1]
    tn = 1024
    return pl.pallas_call(
        _mod_kernel,
        out_shape=jax.ShapeDtypeStruct((bsz, n), F32),
        grid=(n // tn,),
        in_specs=[pl.BlockSpec((bsz, d), lambda j: (0, 0)),
                  pl.BlockSpec((d, tn), lambda j: (0, j)),
                  pl.BlockSpec((1, tn), lambda j: (0, j))],
        out_specs=pl.BlockSpec((bsz, tn), lambda j: (0, j)),
        compiler_params=_cparams("arbitrary"),
        name="mod",
    )(c, w_mod, b_mod.reshape(1, n))


def _qkv_kernel(x_ref, mod_ref, g_ref, w_ref, *rest):
    n_pat = len(DILATION_PATTERNS)
    out_refs, slab, mid = rest[:3 * n_pat], rest[3 * n_pat], rest[3 * n_pat + 1]
    h = _rms_norm(x_ref[0], g_ref[...]) * (1.0 + mod_ref[0, 1:2, :]) + mod_ref[0, 0:1, :]
    hb = h.astype(BF16)
    rows = hb.shape[0]
    per_tensor = ATTN_WIDTH // LANES
    mid_dil = QKV_MID_DIL

    def project(t):
        res = _dot(hb, w_ref[:, t * ATTN_WIDTH:(t + 1) * ATTN_WIDTH])
        if t == 0:
            res = res * (HEAD_DIM ** -0.5 * LOG2_E)
        for s in range(per_tensor):
            slab[t * per_tensor + s] = res[:, s * LANES:(s + 1) * LANES]

    def split(t):
        for p, (_, dil) in enumerate(DILATION_PATTERNS):
            sub = rows // dil
            out = out_refs[3 * p + t]
            for r in range(dil):
                pieces = []
                for s in range(per_tensor):
                    ts = t * per_tensor + s
                    if dil == 1:
                        piece = slab[ts]
                    elif dil == mid_dil:
                        piece = slab[ts, pl.ds(r, sub, stride=dil), :]
                        mid[ts, r] = piece
                    else:
                        ratio = dil // mid_dil
                        piece = mid[ts, r % mid_dil, pl.ds(r // mid_dil, sub, stride=ratio), :]
                    pieces.append(piece)
                out[0, r] = _lane_concat(pieces).astype(out.dtype)

    project(0)
    for t in range(3):
        if t + 1 < 3:
            project(t + 1)
        split(t)


def _qkv_projection(x, mod3, g_pre, w_qkv):
    bsz, seq, d = x.shape
    tm = QKV_ROWS
    out_shape, out_specs = [], []
    for _, dil in DILATION_PATTERNS:
        assert tm % (dil * 2 * SUBLANES) == 0
        for _ in range(3):
            out_shape.append(jax.ShapeDtypeStruct((bsz, dil, seq // dil, ATTN_WIDTH), BF16))
            out_specs.append(pl.BlockSpec((1, dil, tm // dil, ATTN_WIDTH),
                                          lambda b, i: (b, 0, i, 0)))
    return pl.pallas_call(
        _qkv_kernel,
        out_shape=out_shape,
        grid=(bsz, seq // tm),
        in_specs=[pl.BlockSpec((1, tm, d), lambda b, i: (b, i, 0)),
                  pl.BlockSpec((1, 6, d), lambda b, i: (b, 0, 0)),
                  pl.BlockSpec((1, d), lambda b, i: (0, 0)),
                  pl.BlockSpec(w_qkv.shape, lambda b, i: (0, 0))],
        out_specs=out_specs,
        scratch_shapes=[pltpu.VMEM((w_qkv.shape[1] // LANES, tm, LANES), F32),
                        pltpu.VMEM((w_qkv.shape[1] // LANES, QKV_MID_DIL, tm // QKV_MID_DIL, LANES),
                                   F32)],
        compiler_params=_cparams("arbitrary", "arbitrary"),
        name="qkv",
    )(x, mod3, g_pre, w_qkv)


def _ugate_kernel(x_ref, mod_ref, g_ref, w_ref, u_ref, gate_ref):
    bsz, tt, d = x_ref.shape
    shift = mod_ref[:, 0, :][:, None, :]
    scale = mod_ref[:, 1, :][:, None, :]
    h = _rms_norm(x_ref[...], g_ref[...]) * (1.0 + scale) + shift
    hb = h.reshape(bsz * tt, d).astype(BF16)
    n_slab = u_ref.shape[0]
    sw = n_slab * LANES
    u = _dot(hb, w_ref[:, 0:sw])
    for b in range(bsz):
        for s in range(n_slab):
            u_ref[s, pl.ds(b, tt, stride=bsz), :] = u[b * tt:(b + 1) * tt, s * LANES:(s + 1) * LANES]
    gw = gate_ref.shape[-1]
    chunk = 512
    for c0 in range(0, gw, chunk):
        g = _sigmoid(_dot(hb, w_ref[:, sw + c0:sw + c0 + chunk]))
        gate_ref[:, :, c0:c0 + chunk] = g.reshape(bsz, tt, chunk).astype(BF16)


def _ugate_projection(x, mod3, g_pre, w_ug, ssm_width):
    bsz, seq, d = x.shape
    tt = TIME_TILE
    gw = w_ug.shape[1] - ssm_width
    n_slab = ssm_width // LANES
    return pl.pallas_call(
        _ugate_kernel,
        out_shape=(jax.ShapeDtypeStruct((n_slab, seq * bsz, LANES), F32),
                   jax.ShapeDtypeStruct((bsz, seq, gw), BF16)),
        grid=(seq // tt,),
        in_specs=[pl.BlockSpec((bsz, tt, d), lambda i: (0, i, 0)),
                  pl.BlockSpec((bsz, 6, d), lambda i: (0, 0, 0)),
                  pl.BlockSpec((1, d), lambda i: (0, 0)),
                  pl.BlockSpec(w_ug.shape, lambda i: (0, 0))],
        out_specs=(pl.BlockSpec((n_slab, tt * bsz, LANES), lambda i: (0, i, 0)),
                   pl.BlockSpec((bsz, tt, gw), lambda i: (0, i, 0))),
        compiler_params=_cparams("arbitrary"),
        name="ugate",
    )(x, mod3, g_pre, w_ug)


def _t5_bucket_np(dist):
    exact = NUM_BUCKETS // 2
    d_f = np.maximum(dist, exact).astype(np.float32)
    large = exact + (np.log(d_f / np.float32(exact)) / np.float32(math.log(MAX_DISTANCE / exact))
                     * np.float32(NUM_BUCKETS - exact)).astype(np.int32)
    return np.where(dist < exact, dist, np.minimum(large, NUM_BUCKETS - 1))


def _bucket_map_t(dil):
    blk = ATTN_BLK
    ki = np.arange(2 * blk)[:, None]
    qi = np.arange(blk)[None, :]
    return _t5_bucket_np(np.maximum(blk + qi - ki, 0) * dil).astype(np.int32)


def _attn_kernel(relb_ref, q_ref, kc_ref, kp_ref, vc_ref, vp_ref, bucket_ref,
                 o_ref, lse_ref, kbuf, vbuf, bias_t, *, n_sub):
    blk = ATTN_BLK
    first_call = jnp.logical_and(pl.program_id(0) == 0,
                                 jnp.logical_and(pl.program_id(1) == 0, pl.program_id(2) == 0))

    @pl.when(first_call)
    def _():
        bucket = bucket_ref[...]
        ki = lax.broadcasted_iota(jnp.int32, bucket.shape, 0)
        qi = lax.broadcasted_iota(jnp.int32, bucket.shape, 1)
        dist = blk + qi - ki
        band = jnp.logical_and(dist >= 0, dist <= blk)
        band_first = jnp.logical_and(band, ki >= blk)

        def per_head(h, carry):
            acc = jnp.zeros(bucket.shape, F32)
            for b in range(NUM_BUCKETS):
                acc = jnp.where(bucket == b, relb_ref[b, h] * LOG2_E, acc)
            bias_t[0, h] = jnp.where(band_first, acc, NEG_INF)
            bias_t[1, h] = jnp.where(band, acc, NEG_INF)
            return carry

        lax.fori_loop(0, N_HEADS, per_head, 0)

    first_variant = jnp.where(pl.program_id(2) == 0, 0, 1)
    n_res = q_ref.shape[1]
    for g in range(n_res):
        kbuf[g, 0:blk, :] = kp_ref[0, g]
        kbuf[g, blk:, :] = kc_ref[0, g]
        vbuf[g, 0:blk, :] = vp_ref[0, g]
        vbuf[g, blk:, :] = vc_ref[0, g]

    lane = lax.broadcasted_iota(jnp.int32, (1, LANES), 1)
    lo_half = lane < HEAD_DIM
    bd_row = lax.broadcasted_iota(jnp.int32, (4 * blk, LANES), 0)
    bd_col = lax.broadcasted_iota(jnp.int32, (4 * blk, LANES), 1)
    ones_bd = ((bd_row < 2 * blk) == (bd_col < HEAD_DIM)).astype(F32).astype(BF16)
    contract_last = (((1,), (1,)), ((), ()))
    contract_first = (((0,), (0,)), ((), ()))

    def sub_block(g, i):
        r0 = i * blk
        q = q_ref[0, g, r0:r0 + blk, :]
        kk = kbuf[g, r0:r0 + 2 * blk, :]
        vv = vbuf[g, r0:r0 + 2 * blk, :]
        variant = first_variant if i == 0 else 1
        for j in range(N_HEADS // 2):
            cols = slice(j * LANES, (j + 1) * LANES)
            qj, kj, vj = q[:, cols], kk[:, cols], vv[:, cols]
            probs_t, maxes = [], []
            for hh in range(2):
                sel = lo_half if hh == 0 else jnp.logical_not(lo_half)
                qm = jnp.where(sel, qj, jnp.zeros_like(qj))
                s_t = lax.dot_general(kj, qm, contract_last, preferred_element_type=F32)
                s_t = s_t + bias_t[variant, 2 * j + hh]
                m = jnp.max(s_t, axis=0, keepdims=True)
                probs_t.append(jnp.exp2(s_t - m).astype(BF16))
                maxes.append(m)
            p2_t = jnp.concatenate(probs_t, axis=0)
            v_bd = jnp.concatenate([jnp.where(lo_half, vj, jnp.zeros_like(vj)),
                                    jnp.where(lo_half, jnp.zeros_like(vj), vj)], axis=0)
            rhs = jnp.concatenate([v_bd, ones_bd], axis=1)
            ol = lax.dot_general(p2_t, rhs, contract_first, preferred_element_type=F32)
            o2, l2 = ol[:, :LANES], ol[:, LANES:]
            m_t = jnp.concatenate([jnp.broadcast_to(maxes[0], (HEAD_DIM, blk)),
                                   jnp.broadcast_to(maxes[1], (HEAD_DIM, blk))], axis=0)
            o_ref[0, g, r0:r0 + blk, cols] = (o2 / l2).astype(o_ref.dtype)
            lse_ref[0, g, r0:r0 + blk, cols] = (m_t.T + jnp.log2(l2)) * LN_2

    for g in range(n_res):
        for i in range(n_sub):
            sub_block(g, i)


def _attention_pattern(q, k, v, rel_bias, dil):
    bsz, _, sub_len, aw = q.shape
    blk = ATTN_BLK
    assert sub_len % blk == 0
    tq = min(ATTN_STEP_ROWS, sub_len)
    n_sub = tq // blk
    ratio = tq // blk
    n_res = min(dil, ATTN_STEP_ROWS // tq)
    cur = lambda b, r, n: (b, r, n, 0)
    prev = lambda b, r, n: (b, r, jnp.maximum(n * ratio - 1, 0), 0)
    blk_cur = pl.BlockSpec((1, n_res, tq, aw), cur)
    blk_prev = pl.BlockSpec((1, n_res, blk, aw), prev)
    bucket = jnp.asarray(_bucket_map_t(dil))
    o_dtype = BF16 if TIME_TILE // dil >= 2 * SUBLANES else F32
    return pl.pallas_call(
        functools.partial(_attn_kernel, n_sub=n_sub),
        out_shape=(jax.ShapeDtypeStruct(q.shape, o_dtype), jax.ShapeDtypeStruct(q.shape, F32)),
        grid=(bsz, dil // n_res, sub_len // tq),
        in_specs=[pl.BlockSpec(memory_space=pltpu.SMEM),
                  blk_cur, blk_cur, blk_prev, blk_cur, blk_prev,
                  pl.BlockSpec(bucket.shape, lambda b, r, n: (0, 0))],
        out_specs=(blk_cur, blk_cur),
        scratch_shapes=[pltpu.VMEM((n_res, tq + blk, aw), BF16),
                        pltpu.VMEM((n_res, tq + blk, aw), BF16),
                        pltpu.VMEM((2, N_HEADS, 2 * blk, blk), F32)],
        compiler_params=_cparams("arbitrary", "arbitrary", "arbitrary"),
        name=f"attn_dil{dil}",
    )(rel_bias.astype(F32), q, k, k, v, v, bucket)


def _ssm_kernel(u_ref, bmat_ref, cmat_ref, ar_ref, ai_ref, dskip_ref, wglu_ref, bglu_ref,
                o_ref, hbuf, hstate, *, n_steps):
    @pl.when(pl.program_id(0) == 0)
    def _():
        hstate[...] = jnp.zeros_like(hstate)

    n_slab = u_ref.shape[0]
    n_state = hbuf.shape[1] // 2
    per = n_state // n_slab
    us = [u_ref[s] for s in range(n_slab)]
    ys = [None] * n_slab

    def drive(s):
        bu = _dot(us[s].astype(BF16), bmat_ref[s])
        hbuf[:, s * per:(s + 1) * per] = bu[:, :per]
        hbuf[:, n_state + s * per:n_state + (s + 1) * per] = bu[:, per:]

    def scan(s):
        re_cols = slice(s * per, (s + 1) * per)
        im_cols = slice(n_state + s * per, n_state + (s + 1) * per)
        ar = ar_ref[:, re_cols]
        ai = ai_ref[:, re_cols]
        hr = hstate[:, re_cols]
        hi = hstate[:, im_cols]
        for t in range(n_steps):
            trow = slice(t * SUBLANES, (t + 1) * SUBLANES)
            nr = ar * hr - ai * hi + hbuf[trow, re_cols]
            ni = ar * hi + ai * hr + hbuf[trow, im_cols]
            hbuf[trow, re_cols] = nr
            hbuf[trow, im_cols] = ni
            hr, hi = nr, ni
        hstate[:, re_cols] = hr
        hstate[:, im_cols] = hi

    def read_out(s):
        h_s = _lane_concat([hbuf[:, s * per:(s + 1) * per],
                            hbuf[:, n_state + s * per:n_state + (s + 1) * per]])
        ys[s] = (_dot(h_s.astype(BF16), cmat_ref[s])
                 + dskip_ref[:, s * LANES:(s + 1) * LANES] * us[s])

    for tick in range(n_slab + 2):
        if tick < n_slab:
            drive(tick)
        if 0 <= tick - 1 < n_slab:
            scan(tick - 1)
        if 0 <= tick - 2 < n_slab:
            read_out(tick - 2)
    y = _lane_concat(ys)
    y = 0.5 * y * (1.0 + jnp.tanh(math.sqrt(2.0 / math.pi) * (y + 0.044715 * (y * y * y))))
    z = _dot(y.astype(BF16), wglu_ref[...]) + bglu_ref[...]
    out = y * _sigmoid(z)
    for s in range(n_slab):
        o_ref[s] = out[:, s * LANES:(s + 1) * LANES]


def _ssm_params(a_re, a_im, log_dt, b_re, b_im, c_re, c_im, bsz):
    g, p = a_re.shape
    hg = b_re.shape[-1]
    dt = jnp.exp(log_dt.astype(F32))[:, None]
    a_re, a_im = a_re.astype(F32), a_im.astype(F32)
    mag = jnp.exp(a_re * dt)
    abar_re = mag * jnp.cos(a_im * dt)
    abar_im = mag * jnp.sin(a_im * dt)
    den = a_re * a_re + a_im * a_im
    q_re = ((abar_re - 1.0) * a_re + abar_im * a_im) / den
    q_im = (abar_im * a_re - (abar_re - 1.0) * a_im) / den
    b_re, b_im = b_re.astype(F32), b_im.astype(F32)
    bb_re = q_re[..., None] * b_re - q_im[..., None] * b_im
    bb_im = q_re[..., None] * b_im + q_im[..., None] * b_re
    gs = LANES // hg
    n_slab = g // gs
    eye = jnp.eye(gs, dtype=F32)

    def in_mat(t):
        t = t.reshape(n_slab, gs, p, hg)
        return jnp.einsum('sgph,gk->sghkp', t, eye).reshape(n_slab, gs * hg, gs * p)

    def out_mat(t):
        t = t.reshape(n_slab, gs, hg, p)
        return jnp.einsum('sghp,gk->sgpkh', t, eye).reshape(n_slab, gs * p, gs * hg)

    bmat = jnp.concatenate([in_mat(bb_re), in_mat(bb_im)], axis=2).astype(BF16)
    cmat = jnp.concatenate([out_mat(c_re.astype(F32)), -out_mat(c_im.astype(F32))],
                           axis=1).astype(BF16)
    ar = jnp.broadcast_to(abar_re.reshape(1, g * p), (bsz, g * p))
    ai = jnp.broadcast_to(abar_im.reshape(1, g * p), (bsz, g * p))
    return bmat, cmat, ar, ai


def _ssm_branch(u_slabs, bsz, a_re, a_im, log_dt, b_re, b_im, c_re, c_im, d_skip, w_glu, b_glu):
    n_slab, n_rows, _ = u_slabs.shape
    width = n_slab * LANES
    assert bsz == SUBLANES
    bmat, cmat, ar, ai = _ssm_params(a_re, a_im, log_dt, b_re, b_im, c_re, c_im, bsz)
    n_state2 = n_slab * bmat.shape[2]
    rows = SSM_STEPS * bsz
    const = lambda c: (0, 0)
    const3 = lambda c: (0, 0, 0)
    slab_spec = pl.BlockSpec((n_slab, rows, LANES), lambda c: (0, c, 0))
    return pl.pallas_call(
        functools.partial(_ssm_kernel, n_steps=SSM_STEPS),
        out_shape=jax.ShapeDtypeStruct(u_slabs.shape, F32),
        grid=(n_rows // rows,),
        in_specs=[slab_spec,
                  pl.BlockSpec(bmat.shape, const3), pl.BlockSpec(cmat.shape, const3),
                  pl.BlockSpec(ar.shape, const), pl.BlockSpec(ai.shape, const),
                  pl.BlockSpec((1, width), const), pl.BlockSpec((width, width), const),
                  pl.BlockSpec((1, width), const)],
        out_specs=slab_spec,
        scratch_shapes=[pltpu.VMEM((rows, n_state2), F32), pltpu.VMEM((bsz, n_state2), F32)],
        compiler_params=_cparams("arbitrary"),
        name="ssm",
    )(u_slabs, bmat, cmat, ar, ai, d_skip.reshape(1, width).astype(F32),
      w_glu.astype(BF16), b_glu.reshape(1, width).astype(F32))


ROUTE_LANES = {"id0": 0, "id1": 1, "rank0": 2, "rank1": 3, "w0": 4, "w1": 5}
GROUP_LANE0 = N_EXPERTS


def _merge_kernel(*refs, bsz_total):
    n_pat = len(DILATION_PATTERNS)
    x_ref, gate_ref = refs[0:2]
    attn_refs = refs[2:2 + 2 * n_pat]
    (ssm_ref, mod_ref, gpost_ref, gpre_ref, wba_ref, wbs_ref, wout_ref,
     wr_cat_ref, wr_hi_ref, br_ref) = refs[2 + 2 * n_pat:12 + 2 * n_pat]
    x1_ref, h2_ref, route_ref, count_ref = refs[12 + 2 * n_pat:16 + 2 * n_pat]
    scratch = refs[16 + 2 * n_pat:]
    carry = scratch[-1]

    @pl.when(jnp.logical_and(pl.program_id(0) == 0, pl.program_id(1) == 0))
    def _():
        carry[...] = jnp.zeros_like(carry)

    nb, tt, d = x_ref.shape
    n_slab = ssm_ref.shape[0]
    b0 = pl.program_id(0) * nb
    cb = MERGE_CHUNK_BATCH
    rows = cb * tt
    lane = lax.broadcasted_iota(jnp.int32, (rows, LANES), 1).astype(F32)
    row = lax.broadcasted_iota(jnp.int32, (rows, rows), 0)
    col = lax.broadcasted_iota(jnp.int32, (rows, rows), 1)
    strict_lower = (col < row).astype(BF16)
    running = [carry[...]]

    def chunk_phases(c0):
        bbs = range(c0, c0 + cb)
        o_tok, lse_tok, ssm_tok = scratch[3 * (c0 // cb):3 * (c0 // cb) + 3]

        for bb in bbs:
            for s in range(n_slab):
                ssm_tok[(bb - c0) * tt:(bb - c0 + 1) * tt, s * LANES:(s + 1) * LANES] = (
                    ssm_ref[s, pl.ds(b0 + bb, tt, stride=bsz_total), :])

        slot = 0
        sources = []
        for p, (_, dil) in enumerate(DILATION_PATTERNS):
            o_ref, lse_ref = attn_refs[2 * p], attn_refs[2 * p + 1]
            if dil == 1:
                sources.append((o_ref, lse_ref, None))
                continue
            sub = tt // dil
            for bb in bbs:
                for r in range(dil):
                    o_blk = o_ref[bb, r].astype(F32)
                    l_blk = lse_ref[bb, r]
                    for s in range(n_slab):
                        dst = pl.ds((bb - c0) * tt + r, sub, stride=dil)
                        o_tok[slot, s, dst, :] = o_blk[:, s * LANES:(s + 1) * LANES]
                        lse_tok[slot, s, dst, :] = l_blk[:, s * LANES:(s + 1) * LANES]
            sources.append((o_ref, lse_ref, slot))
            slot += 1
        attn_slabs = []
        for s in range(n_slab):
            cols = slice(s * LANES, (s + 1) * LANES)
            o_ps, lse_ps = [], []
            for o_ref, lse_ref, src_slot in sources:
                if src_slot is None:
                    o_ps.append(o_ref[c0:c0 + cb, 0, :, cols].astype(F32).reshape(rows, LANES))
                    lse_ps.append(lse_ref[c0:c0 + cb, 0, :, cols].reshape(rows, LANES))
                else:
                    o_ps.append(o_tok[src_slot, s])
                    lse_ps.append(lse_tok[src_slot, s])
            m = functools.reduce(jnp.maximum, lse_ps)
            es = [jnp.exp(l - m) for l in lse_ps]
            num = functools.reduce(lambda a, b: a + b, [e * o for e, o in zip(es, o_ps)])
            den = functools.reduce(lambda a, b: a + b, es)
            attn_slabs.append(num / den)
        attn_b = _lane_concat(attn_slabs).astype(BF16)
        ssm_b = ssm_tok[...].astype(BF16)
        yield
        branch_attn = _dot(attn_b, wba_ref[...])
        branch_ssm = _dot(ssm_b, wbs_ref[...])
        yield
        g_attn = gate_ref[c0:c0 + cb, :, 0:d].astype(F32).reshape(rows, d)
        g_ssm = gate_ref[c0:c0 + cb, :, d:].astype(F32).reshape(rows, d)
        merged_b = (g_attn * branch_attn + g_ssm * branch_ssm).astype(BF16)
        yield
        y = _dot(merged_b, wout_ref[...])
        yield
        gate1 = mod_ref[c0:c0 + cb, 2, :][:, None, :]
        shift2 = mod_ref[c0:c0 + cb, 3, :][:, None, :]
        scale2 = mod_ref[c0:c0 + cb, 4, :][:, None, :]
        x1 = x_ref[c0:c0 + cb] + gate1 * _rms_norm(y, gpost_ref[...]).reshape(cb, tt, d)
        x1_ref[c0:c0 + cb] = x1
        h2 = _rms_norm(x1, gpre_ref[...]) * (1.0 + scale2) + shift2
        for j, bb in enumerate(bbs):
            _store_row_tiles(h2_ref.at[bb], _pack_rows(h2[j]))

        a_hi, a_lo = _split_bf16(h2.reshape(rows, d))
        yield
        hi_pass = _dot(a_hi, wr_cat_ref[...])
        lo_pass = _dot(a_lo, wr_hi_ref[...])
        yield
        logits = hi_pass[:, :LANES] + lo_pass + hi_pass[:, LANES:] + br_ref[...]
        big = float(LANES)
        is_group = jnp.logical_and(lane >= GROUP_LANE0, lane < GROUP_LANE0 + N_EXPERT_GROUPS)
        gl = jnp.where(is_group, logits, -jnp.inf)
        g_max = jnp.max(gl, axis=-1, keepdims=True)
        g_sel = jnp.min(jnp.where(gl == g_max, lane, big), axis=-1, keepdims=True) - GROUP_LANE0
        g_gate = 1.0 / jnp.sum(jnp.exp(gl - g_max), axis=-1, keepdims=True)
        lo = g_sel * EXPERTS_PER_GROUP
        in_group = jnp.logical_and(lane >= lo, lane < lo + EXPERTS_PER_GROUP)
        el = jnp.where(in_group, logits, -jnp.inf)
        t0 = jnp.max(el, axis=-1, keepdims=True)
        i0 = jnp.min(jnp.where(el == t0, lane, big), axis=-1, keepdims=True)
        el1 = jnp.where(lane == i0, -jnp.inf, el)
        t1 = jnp.max(el1, axis=-1, keepdims=True)
        i1 = jnp.min(jnp.where(el1 == t1, lane, big), axis=-1, keepdims=True)
        e = jnp.exp(t1 - t0)
        w0 = g_gate / (1.0 + e)
        w1 = g_gate * e / (1.0 + e)

        hit0 = lane == i0
        hit1 = lane == i1
        onehot = jnp.logical_or(hit0, hit1).astype(F32)
        yield
        before = _dot(strict_lower, onehot.astype(BF16)) + running[0]
        rank0 = jnp.sum(jnp.where(hit0, before, 0.0), axis=-1, keepdims=True)
        rank1 = jnp.sum(jnp.where(hit1, before, 0.0), axis=-1, keepdims=True)
        running[0] = running[0] + jnp.sum(onehot, axis=0, keepdims=True)

        route = jnp.zeros((rows, LANES), F32)
        for name, val in (("id0", i0), ("id1", i1), ("rank0", rank0), ("rank1", rank1),
                          ("w0", w0), ("w1", w1)):
            route = jnp.where(lane == ROUTE_LANES[name], val, route)
        route_ref[c0:c0 + cb] = route.reshape(cb, tt, LANES)

    pending = [chunk_phases(c0) for c0 in range(0, nb, cb)]
    active = []
    while pending or active:
        if pending:
            active.append(pending.pop(0))
        for gen in list(active):
            if next(gen, StopIteration) is StopIteration:
                active.remove(gen)

    carry[...] = running[0]
    count_ref[...] = jnp.broadcast_to(running[0], count_ref.shape)


def _merge_and_route(x, gates, attn_outs, ssm_slabs, mod3, g_post, g_pre, wba, wbs, wout,
                     w_rg, b_rg, w_re, b_re):
    bsz, seq, d = x.shape
    tt, nb = TIME_TILE, MERGE_BATCH
    aw = ATTN_WIDTH
    n_slab = ssm_slabs.shape[0]
    assert aw == n_slab * LANES
    wr = jnp.zeros((d, LANES), F32).at[:, :N_EXPERTS].set(w_re.astype(F32))
    wr = wr.at[:, GROUP_LANE0:GROUP_LANE0 + N_EXPERT_GROUPS].set(w_rg.astype(F32))
    br = jnp.zeros((1, LANES), F32).at[0, :N_EXPERTS].set(b_re.astype(F32))
    br = br.at[0, GROUP_LANE0:GROUP_LANE0 + N_EXPERT_GROUPS].set(b_rg.astype(F32))
    wr_hi, wr_lo = _split_bf16(wr)
    wr_cat = jnp.concatenate([wr_hi, wr_lo], axis=1)
    n_strided = sum(1 for _, dil in DILATION_PATTERNS if dil > 1)
    chunk_rows = MERGE_CHUNK_BATCH * tt
    tok = lambda h, i: (h, i, 0)
    const = lambda h, i: (0, 0)
    attn_args, attn_specs = [], []
    for (o_p, lse_p), (_, dil) in zip(attn_outs, DILATION_PATTERNS):
        spec = pl.BlockSpec((nb, dil, tt // dil, aw), lambda h, i: (h, 0, i, 0))
        attn_args += [o_p, lse_p]
        attn_specs += [spec, spec]
    return pl.pallas_call(
        functools.partial(_merge_kernel, bsz_total=bsz),
        out_shape=(jax.ShapeDtypeStruct((bsz, seq, d), F32),
                   jax.ShapeDtypeStruct((bsz, seq * ROW_TILE, LANES), U32),
                   jax.ShapeDtypeStruct((bsz, seq, LANES), F32),
                   jax.ShapeDtypeStruct((SUBLANES, LANES), F32)),
        grid=(bsz // nb, seq // tt),
        in_specs=[pl.BlockSpec((nb, tt, d), tok),
                  pl.BlockSpec((nb, tt, gates.shape[-1]), tok)]
                 + attn_specs
                 + [pl.BlockSpec((n_slab, tt * bsz, LANES), lambda h, i: (0, i, 0)),
                    pl.BlockSpec((nb, 6, d), lambda h, i: (h, 0, 0)),
                    pl.BlockSpec((1, d), const), pl.BlockSpec((1, d), const),
                    pl.BlockSpec(wba.shape, const), pl.BlockSpec(wbs.shape, const),
                    pl.BlockSpec(wout.shape, const),
                    pl.BlockSpec((d, 2 * LANES), const), pl.BlockSpec((d, LANES), const),
                    pl.BlockSpec((1, LANES), const)],
        out_specs=(pl.BlockSpec((nb, tt, d), tok), pl.BlockSpec((nb, tt * ROW_TILE, LANES), tok),
                   pl.BlockSpec((nb, tt, LANES), tok),
                   pl.BlockSpec((SUBLANES, LANES), const)),
        scratch_shapes=[pltpu.VMEM((n_strided, n_slab, chunk_rows, LANES), F32),
                        pltpu.VMEM((n_strided, n_slab, chunk_rows, LANES), F32),
                        pltpu.VMEM((chunk_rows, n_slab * LANES), F32)] * (nb // MERGE_CHUNK_BATCH)
                       + [pltpu.VMEM((1, LANES), F32)],
        compiler_params=_cparams("arbitrary", "arbitrary"),
        name="merge",
    )(x, gates, *attn_args, ssm_slabs, mod3, g_post, g_pre, wba, wbs, wout, wr_cat, wr_hi, br)


ISSUE_UNROLL = 8


def _dispatch_kernel(dest_ref, seg_ref, h_ref, xs_ref, zero_buf, sem, zsem):
    rows = h_ref.shape[0] // ROW_TILE
    blk = zero_buf.shape[0]
    n_blocks = xs_ref.shape[0] // blk
    base = pl.program_id(0) * rows

    @pl.when(pl.program_id(0) == 0)
    def _():
        zero_buf[...] = jnp.zeros_like(zero_buf)

        def zero_copy(row0):
            return pltpu.make_async_copy(zero_buf, xs_ref.at[pl.ds(pl.multiple_of(row0, blk), blk)],
                                         zsem)

        def fill_tail(e, carry):
            @pl.when(seg_ref[N_EXPERTS + e] > 0)
            def _():
                zero_copy(seg_ref[e] * ROW_TILE - blk).start()
            return carry

        def fill_unused(j, carry):
            zero_copy(j * blk).start()
            return carry

        def wait_tail(e, carry):
            @pl.when(seg_ref[N_EXPERTS + e] > 0)
            def _():
                zero_copy(0).wait()
            return carry

        def wait_unused(j, carry):
            zero_copy(0).wait()
            return carry

        n_used = seg_ref[2 * N_EXPERTS]
        lax.fori_loop(0, N_EXPERTS, fill_tail, 0)
        lax.fori_loop(n_used, n_blocks, fill_unused, 0)
        lax.fori_loop(0, N_EXPERTS, wait_tail, 0)
        lax.fori_loop(n_used, n_blocks, wait_unused, 0)

    group = ISSUE_UNROLL * ROW_TILE

    def issue(g, carry):
        g0 = pl.multiple_of(g * group, group)
        for rr in range(ISSUE_UNROLL):
            src = h_ref.at[pl.ds(g0 + rr * ROW_TILE, ROW_TILE)]
            for k in range(TOP_K):
                d = dest_ref[TOP_K * (base + g * ISSUE_UNROLL + rr) + k]
                dst = xs_ref.at[pl.ds(pl.multiple_of(d * ROW_TILE, ROW_TILE), ROW_TILE)]
                pltpu.make_async_copy(src, dst, sem).start(priority=k)
        return carry

    lax.fori_loop(0, rows // ISSUE_UNROLL, issue, 0)
    for _ in range(TOP_K):
        pltpu.make_async_copy(h_ref, xs_ref.at[pl.ds(0, rows * ROW_TILE)], sem).wait()


def _dispatch(h2_tiles, dest, seg_info, cap):
    n_tok = h2_tiles.shape[0] // ROW_TILE
    rows = DISPATCH_ROWS
    return pl.pallas_call(
        _dispatch_kernel,
        out_shape=jax.ShapeDtypeStruct((cap * ROW_TILE, LANES), h2_tiles.dtype),
        grid_spec=pltpu.PrefetchScalarGridSpec(
            num_scalar_prefetch=2,
            grid=(n_tok // rows,),
            in_specs=[pl.BlockSpec((rows * ROW_TILE, LANES), lambda i, dest, seg: (i, 0))],
            out_specs=pl.BlockSpec(memory_space=pl.ANY),
            scratch_shapes=[pltpu.VMEM((MOE_ROWS * ROW_TILE, LANES), h2_tiles.dtype),
                            pltpu.SemaphoreType.DMA, pltpu.SemaphoreType.DMA]),
        compiler_params=_cparams("arbitrary"),
        name="dispatch",
    )(dest, seg_info, h2_tiles)


def _expert_kernel(blk_expert_ref, seg_slot_ref, next_expert_ref, n_used_ref, x_ref,
                   w1_hbm, w3_hbm, w2_hbm, y_ref, w1f, w3f, w2f, w1b, w3b, w2b, sems):
    i = pl.program_id(0)
    used = i < n_used_ref[0]
    rows = x_ref.shape[0] // ROW_TILE
    expert = blk_expert_ref[i]
    new_expert = jnp.logical_or(i == 0, expert != blk_expert_ref[jnp.maximum(i - 1, 0)])
    slot = seg_slot_ref[i]

    def weight_copies(e, s):
        return [pltpu.make_async_copy(hbm.at[e], stage.at[s], sems.at[s, n])
                for n, (hbm, stage) in enumerate(((w1_hbm, w1f), (w3_hbm, w3f), (w2_hbm, w2f)))]

    @pl.when(jnp.logical_and(used, new_expert))
    def _():
        @pl.when(i == 0)
        def _():
            for cp in weight_copies(expert, slot):
                cp.start()

        for cp in weight_copies(expert, slot):
            cp.wait()
        w1b[...] = w1f[slot].astype(BF16)
        w3b[...] = w3f[slot].astype(BF16)
        w2b[...] = w2f[slot].astype(BF16)
        nxt = next_expert_ref[i]

        @pl.when(nxt >= 0)
        def _():
            for cp in weight_copies(nxt, 1 - slot):
                cp.start()

    @pl.when(used)
    def _():
        xb = _unpack_rows(_load_row_tiles(x_ref, rows)).astype(BF16)
        h1 = _dot(xb, w1b[...])
        h3 = _dot(xb, w3b[...])
        act = (h1 * _sigmoid(h1)) * h3
        y = _dot(act.astype(BF16), w2b[...])
        _store_row_tiles(y_ref, _pack_rows(y))

    @pl.when(jnp.logical_not(used))
    def _():
        y_ref[...] = jnp.zeros_like(y_ref)


def _experts(xs, blk_expert, n_used, w1, w3, w2):
    d, de = w1.shape[-2:]
    assert d == 2 * ROW_TILE * LANES
    rows = MOE_ROWS
    tile_rows = rows * ROW_TILE
    n_blocks = xs.shape[0] // tile_rows
    blk = jnp.arange(n_blocks, dtype=jnp.int32)
    changed = jnp.concatenate([jnp.zeros((1,), jnp.int32),
                               (blk_expert[1:] != blk_expert[:-1]).astype(jnp.int32)])
    seg_slot = (jnp.cumsum(changed) % 2).astype(jnp.int32)
    later = jnp.logical_and(blk[None, :] > blk[:, None],
                            jnp.logical_and(blk_expert[None, :] != blk_expert[:, None],
                                            blk[None, :] < n_used[0]))
    first_later = jnp.argmax(later, axis=1)
    next_expert = jnp.where(jnp.any(later, axis=1), blk_expert[first_later], -1).astype(jnp.int32)
    xblk = lambda i, be, ss, ne, nu: (jnp.minimum(i, nu[0] - 1), 0)
    return pl.pallas_call(
        _expert_kernel,
        out_shape=jax.ShapeDtypeStruct(xs.shape, U32),
        grid_spec=pltpu.PrefetchScalarGridSpec(
            num_scalar_prefetch=4,
            grid=(n_blocks,),
            in_specs=[pl.BlockSpec((tile_rows, LANES), xblk),
                      pl.BlockSpec(memory_space=pl.ANY), pl.BlockSpec(memory_space=pl.ANY),
                      pl.BlockSpec(memory_space=pl.ANY)],
            out_specs=pl.BlockSpec((tile_rows, LANES), lambda i, be, ss, ne, nu: (i, 0)),
            scratch_shapes=[pltpu.VMEM((2, d, de), w1.dtype), pltpu.VMEM((2, d, de), w3.dtype),
                            pltpu.VMEM((2, de, d), w2.dtype),
                            pltpu.VMEM((d, de), BF16), pltpu.VMEM((d, de), BF16),
                            pltpu.VMEM((de, d), BF16),
                            pltpu.SemaphoreType.DMA((2, 3))]),
        compiler_params=_cparams("arbitrary"),
        name="experts",
    )(blk_expert, seg_slot, next_expert, n_used, xs, w1, w3, w2)


def _combine_kernel(dest_ref, ys_ref, x1_ref, route_ref, mod_ref, g_ref, o_ref, ybuf, sem):
    rows = x1_ref.shape[0]
    step = pl.program_id(0)
    n_step = pl.num_programs(0)

    group = ISSUE_UNROLL * ROW_TILE

    def gather(tile, slot):
        base = tile * rows

        def issue(g, carry):
            g0 = pl.multiple_of(g * group, group)
            for rr in range(ISSUE_UNROLL):
                for k in range(TOP_K):
                    d = dest_ref[TOP_K * (base + g * ISSUE_UNROLL + rr) + k]
                    src = ys_ref.at[pl.ds(pl.multiple_of(d * ROW_TILE, ROW_TILE), ROW_TILE)]
                    dst = ybuf.at[slot, k, pl.ds(g0 + rr * ROW_TILE, ROW_TILE)]
                    pltpu.make_async_copy(src, dst, sem.at[slot]).start(priority=k)
            return carry

        lax.fori_loop(0, rows // ISSUE_UNROLL, issue, 0)

    slot = lax.rem(step, 2)

    @pl.when(step == 0)
    def _():
        gather(step, 0)

    @pl.when(step + 1 < n_step)
    def _():
        gather(step + 1, 1 - slot)

    for k in range(TOP_K):
        pltpu.make_async_copy(ys_ref.at[pl.ds(0, rows * ROW_TILE)], ybuf.at[slot, k],
                              sem.at[slot]).wait()
    route = route_ref[...]
    w0 = route[:, ROUTE_LANES["w0"]:ROUTE_LANES["w0"] + 1]
    w1 = route[:, ROUTE_LANES["w1"]:ROUTE_LANES["w1"] + 1]

    def rows_of(k):
        return _unpack_rows(_load_row_tiles(ybuf.at[slot, k], rows))

    y = rows_of(0) * w0 + rows_of(1) * w1
    o_ref[...] = x1_ref[...] + mod_ref[0, 5:6, :] * _rms_norm(y, g_ref[...])


def _combine(ys, dest, x1_flat, route_flat, mod3, g_post, seq):
    n_tok, d = x1_flat.shape
    rows = COMBINE_ROWS
    assert seq % rows == 0
    return pl.pallas_call(
        _combine_kernel,
        out_shape=jax.ShapeDtypeStruct((n_tok, d), F32),
        grid_spec=pltpu.PrefetchScalarGridSpec(
            num_scalar_prefetch=1,
            grid=(n_tok // rows,),
            in_specs=[pl.BlockSpec(memory_space=pl.ANY),
                      pl.BlockSpec((rows, d), lambda i, dest: (i, 0)),
                      pl.BlockSpec((rows, LANES), lambda i, dest: (i, 0)),
                      pl.BlockSpec((1, 6, d), lambda i, dest: (i * rows // seq, 0, 0)),
                      pl.BlockSpec((1, d), lambda i, dest: (0, 0))],
            out_specs=pl.BlockSpec((rows, d), lambda i, dest: (i, 0)),
            scratch_shapes=[pltpu.VMEM((2, TOP_K, rows * ROW_TILE, LANES), U32),
                            pltpu.SemaphoreType.DMA((2,))]),
        compiler_params=_cparams("arbitrary"),
        name="combine",
    )(dest, ys, x1_flat, route_flat, mod3, g_post)


def _moe_layout(route_flat, counts):
    rows = MOE_ROWS
    n_tok = route_flat.shape[0]
    ids = route_flat[:, 0:TOP_K].astype(jnp.int32)
    rank = route_flat[:, TOP_K:2 * TOP_K].astype(jnp.int32)
    counts = counts.astype(jnp.int32)
    padded = (counts + rows - 1) // rows * rows
    pad_ends = jnp.cumsum(padded)
    pad_starts = pad_ends - padded
    expert = jnp.arange(N_EXPERTS, dtype=jnp.int32)
    start_of = jnp.sum(jnp.where(ids[..., None] == expert, pad_starts, 0), axis=-1)
    dest = (start_of + rank).reshape(n_tok * TOP_K)
    n_blocks = (n_tok * TOP_K + N_EXPERTS * (rows - 1) + rows - 1) // rows
    blk_row0 = jnp.arange(n_blocks, dtype=jnp.int32) * rows
    blk_expert = jnp.minimum(jnp.sum(pad_ends[None, :] <= blk_row0[:, None], axis=-1),
                             N_EXPERTS - 1).astype(jnp.int32)
    n_used = (pad_ends[-1] // rows).reshape(1).astype(jnp.int32)
    seg_info = jnp.concatenate([pad_ends, padded, n_used]).astype(jnp.int32)
    return dest, blk_expert, n_used, seg_info, n_blocks * rows


def kernel(x, c, w_mod, b_mod, g_pre_mix, g_post_mix, g_pre_ffn, g_post_ffn, w_in, rel_bias, a_re, a_im, log_dt, ssm_b_re, ssm_b_im, ssm_c_re, ssm_c_im, d_skip, w_glu, b_glu, w_branch_attn, w_branch_ssm, w_out, w_router_group, b_router_group, w_router_expert, b_router_expert, w1, w3, w2):
    bsz, seq, d = x.shape
    depth = w_mod.shape[0]
    ssm_width = w_glu.shape[-1]
    n_pat = len(DILATION_PATTERNS)
    for l in range(depth):
        mod3 = _modulation(c, w_mod[l], b_mod[l]).reshape(bsz, 6, d)
        w_in_l = w_in[l].astype(BF16)
        g_pre = g_pre_mix[l].reshape(1, d)
        qkv = _qkv_projection(x, mod3, g_pre, w_in_l[:, :3 * ATTN_WIDTH])
        u_slabs, gates = _ugate_projection(x, mod3, g_pre, w_in_l[:, 3 * ATTN_WIDTH:], ssm_width)
        attn_outs = [_attention_pattern(*qkv[3 * p:3 * p + 3], rel_bias, DILATION_PATTERNS[p][1])
                     for p in range(n_pat)]
        ssm_slabs = _ssm_branch(u_slabs, bsz, a_re[l], a_im[l], log_dt[l], ssm_b_re[l],
                                ssm_b_im[l], ssm_c_re[l], ssm_c_im[l], d_skip[l], w_glu[l], b_glu[l])
        x1, h2, route, counts = _merge_and_route(
            x, gates, attn_outs, ssm_slabs, mod3, g_post_mix[l].reshape(1, d),
            g_pre_ffn[l].reshape(1, d), w_branch_attn[l].astype(BF16),
            w_branch_ssm[l].astype(BF16), w_out[l].astype(BF16),
            w_router_group[l], b_router_group[l], w_router_expert[l], b_router_expert[l])
        route_flat = route.reshape(bsz * seq, LANES)
        dest, blk_expert, n_used, seg_info, cap = _moe_layout(route_flat, counts[0, :N_EXPERTS])
        xs = _dispatch(h2.reshape(bsz * seq * ROW_TILE, LANES), dest, seg_info, cap)
        ys = _experts(xs, blk_expert, n_used, w1[l], w3[l], w2[l])
        x = _combine(ys, dest, x1.reshape(bsz * seq, d), route_flat, mod3,
                     g_post_ffn[l].reshape(1, d), seq).reshape(bsz, seq, d)
    return x
```

```python
import functools
import math

import numpy as np
import jax
import jax.numpy as jnp
from jax import lax
from jax.experimental import pallas as pl
from jax.experimental.pallas import tpu as pltpu

F32 = jnp.float32
BF16 = jnp.bfloat16

N_HEADS = 8
HEAD_DIM = 64
ATTN_WIDTH = N_HEADS * HEAD_DIM
DILATION_PATTERNS = ((128, 1), (512, 4), (2048, 16))
NUM_BUCKETS = 32
MAX_DISTANCE = 2048
N_EXPERT_GROUPS = 4
EXPERTS_PER_GROUP = 8
N_EXPERTS = N_EXPERT_GROUPS * EXPERTS_PER_GROUP
TOP_K = 2
RMS_EPS = 1e-6
NEG_INF = -1e30
LOG2_E = math.log2(math.e)

LANES = 128
SUBLANES = 8
VMEM_LIMIT_BYTES = 56 * 1024 * 1024

ATTN_BLK = 128
ATTN_STEP_ROWS = 2048
QKV_ROWS = 1024
QKV_MID_DIL = 4
TIME_TILE = 128
MERGE_BATCH = 4
MERGE_CHUNK_BATCH = 2
SSM_STEPS = 128
MOE_ROWS = 512
DISPATCH_ROWS = 2048
COMBINE_ROWS = 512
ROW_TILE = 4
U32 = jnp.uint32
HI_HALF = 0xFFFF0000


def _pack_rows(x):
    w = x.shape[1] // 2
    lo = lax.bitcast_convert_type(x[:, :w].astype(BF16).astype(F32), U32) >> 16
    hi = lax.bitcast_convert_type(x[:, w:].astype(BF16).astype(F32), U32) & U32(HI_HALF)
    return hi | lo


def _unpack_rows(p):
    lo = lax.bitcast_convert_type(p << 16, F32)
    hi = lax.bitcast_convert_type(p & U32(HI_HALF), F32)
    return jnp.concatenate([lo, hi], axis=1)


def _store_row_tiles(ref, packed):
    rows = packed.shape[0]
    for s in range(ROW_TILE):
        ref[pl.ds(s, rows, stride=ROW_TILE), :] = packed[:, s * LANES:(s + 1) * LANES]


def _load_row_tiles(ref, rows):
    return _lane_concat([ref[pl.ds(s, rows, stride=ROW_TILE), :] for s in range(ROW_TILE)])


def _cparams(*sem):
    return pltpu.CompilerParams(dimension_semantics=sem, vmem_limit_bytes=VMEM_LIMIT_BYTES)


def _sigmoid(x):
    return 1.0 / (1.0 + jnp.exp(-x))


def _dot(a, b):
    return jnp.dot(a, b, preferred_element_type=F32)


def _split_bf16(a):
    hi = a.astype(BF16)
    lo = (a - hi.astype(F32)).astype(BF16)
    return hi, lo


def _dot_split(a, w_hi, w_lo):
    a_hi, a_lo = _split_bf16(a)
    return _dot(a_hi, w_hi) + _dot(a_lo, w_hi) + _dot(a_hi, w_lo)


def _rms_norm(x, gain):
    ms = jnp.mean(x * x, axis=-1, keepdims=True)
    return x * lax.rsqrt(ms + RMS_EPS) * gain


def _lane_concat(ref_slabs):
    return jnp.concatenate(ref_slabs, axis=-1)


def _mod_kernel(c_ref, w_ref, b_ref, o_ref):
    c = c_ref[...]
    a = c * _sigmoid(c)
    w_hi, w_lo = _split_bf16(w_ref[...])
    o_ref[...] = _dot_split(a, w_hi, w_lo) + b_ref[...]


def _modulation(c, w_mod, b_mod):
    bsz, d = c.shape
    n = w_mod.shape[1]
    tn = 1024
    return pl.pallas_call(
        _mod_kernel,
        out_shape=jax.ShapeDtypeStruct((bsz, n), F32),
        grid=(n // tn,),
        in_specs=[pl.BlockSpec((bsz, d), lambda j: (0, 0)),
                  pl.BlockSpec((d, tn), lambda j: (0, j)),
                  pl.BlockSpec((1, tn), lambda j: (0, j))],
        out_specs=pl.BlockSpec((bsz, tn), lambda j: (0, j)),
        compiler_params=_cparams("arbitrary"),
        name="mod",
    )(c, w_mod, b_mod.reshape(1, n))


def _qkv_kernel(x_ref, mod_ref, g_ref, w_ref, *rest):
    n_pat = len(DILATION_PATTERNS)
    out_refs, slab, mid = rest[:3 * n_pat], rest[3 * n_pat], rest[3 * n_pat + 1]
    h = _rms_norm(x_ref[0], g_ref[...]) * (1.0 + mod_ref[0, 1:2, :]) + mod_ref[0, 0:1, :]
    hb = h.astype(BF16)
    rows = hb.shape[0]
    per_tensor = ATTN_WIDTH // LANES
    mid_dil = QKV_MID_DIL

    def project(t):
        res = _dot(hb, w_ref[:, t * ATTN_WIDTH:(t + 1) * ATTN_WIDTH])
        if t == 0:
            res = res * (HEAD_DIM ** -0.5 * LOG2_E)
        for s in range(per_tensor):
            slab[t * per_tensor + s] = res[:, s * LANES:(s + 1) * LANES]

    def split(t):
        for p, (_, dil) in enumerate(DILATION_PATTERNS):
            sub = rows // dil
            out = out_refs[3 * p + t]
            for r in range(dil):
                pieces = []
                for s in range(per_tensor):
                    ts = t * per_tensor + s
                    if dil == 1:
                        piece = slab[ts]
                    elif dil == mid_dil:
                        piece = slab[ts, pl.ds(r, sub, stride=dil), :]
                        mid[ts, r] = piece
                    else:
                        ratio = dil // mid_dil
                        piece = mid[ts, r % mid_dil, pl.ds(r // mid_dil, sub, stride=ratio), :]
                    pieces.append(piece)
                out[0, r] = _lane_concat(pieces).astype(out.dtype)

    project(0)
    for t in range(3):
        if t + 1 < 3:
            project(t + 1)
        split(t)


def _qkv_projection(x, mod3, g_pre, w_qkv):
    bsz, seq, d = x.shape
    tm = QKV_ROWS
    out_shape, out_specs = [], []
    for _, dil in DILATION_PATTERNS:
        assert tm % (dil * 2 * SUBLANES) == 0
        for _ in range(3):
            out_shape.append(jax.ShapeDtypeStruct((bsz, dil, seq // dil, ATTN_WIDTH), BF16))
            out_specs.append(pl.BlockSpec((1, dil, tm // dil, ATTN_WIDTH),
                                          lambda b, i: (b, 0, i, 0)))
    return pl.pallas_call(
        _qkv_kernel,
        out_shape=out_shape,
        grid=(bsz, seq // tm),
        in_specs=[pl.BlockSpec((1, tm, d), lambda b, i: (b, i, 0)),
                  pl.BlockSpec((1, 6, d), lambda b, i: (b, 0, 0)),
                  pl.BlockSpec((1, d), lambda b, i: (0, 0)),
                  pl.BlockSpec(w_qkv.shape, lambda b, i: (0, 0))],
        out_specs=out_specs,
        scratch_shapes=[pltpu.VMEM((w_qkv.shape[1] // LANES, tm, LANES), F32),
                        pltpu.VMEM((w_qkv.shape[1] // LANES, QKV_MID_DIL, tm // QKV_MID_DIL, LANES),
                                   F32)],
        compiler_params=_cparams("arbitrary", "arbitrary"),
        name="qkv",
    )(x, mod3, g_pre, w_qkv)


def _ugate_kernel(x_ref, mod_ref, g_ref, w_ref, u_ref, gate_ref):
    bsz, tt, d = x_ref.shape
    shift = mod_ref[:, 0, :][:, None, :]
    scale = mod_ref[:, 1, :][:, None, :]
    h = _rms_norm(x_ref[...], g_ref[...]) * (1.0 + scale) + shift
    hb = h.reshape(bsz * tt, d).astype(BF16)
    n_slab = u_ref.shape[0]
    sw = n_slab * LANES
    u = _dot(hb, w_ref[:, 0:sw])
    for b in range(bsz):
        for s in range(n_slab):
            u_ref[s, pl.ds(b, tt, stride=bsz), :] = u[b * tt:(b + 1) * tt, s * LANES:(s + 1) * LANES]
    gw = gate_ref.shape[-1]
    chunk = 512
    for c0 in range(0, gw, chunk):
        g = _sigmoid(_dot(hb, w_ref[:, sw + c0:sw + c0 + chunk]))
        gate_ref[:, :, c0:c0 + chunk] = g.reshape(bsz, tt, chunk).astype(BF16)


def _ugate_projection(x, mod3, g_pre, w_ug, ssm_width):
    bsz, seq, d = x.shape
    tt = TIME_TILE
    gw = w_ug.shape[1] - ssm_width
    n_slab = ssm_width // LANES
    return pl.pallas_call(
        _ugate_kernel,
        out_shape=(jax.ShapeDtypeStruct((n_slab, seq * bsz, LANES), F32),
                   jax.ShapeDtypeStruct((bsz, seq, gw), BF16)),
        grid=(seq // tt,),
        in_specs=[pl.BlockSpec((bsz, tt, d), lambda i: (0, i, 0)),
                  pl.BlockSpec((bsz, 6, d), lambda i: (0, 0, 0)),
                  pl.BlockSpec((1, d), lambda i: (0, 0)),
                  pl.BlockSpec(w_ug.shape, lambda i: (0, 0))],
        out_specs=(pl.BlockSpec((n_slab, tt * bsz, LANES), lambda i: (0, i, 0)),
                   pl.BlockSpec((bsz, tt, gw), lambda i: (0, i, 0))),
        compiler_params=_cparams("arbitrary"),
        name="ugate",
    )(x, mod3, g_pre, w_ug)


def _t5_bucket_np(dist):
    exact = NUM_BUCKETS // 2
    d_f = np.maximum(dist, exact).astype(np.float32)
    large = exact + (np.log(d_f / np.float32(exact)) / np.float32(math.log(MAX_DISTANCE / exact))
                     * np.float32(NUM_BUCKETS - exact)).astype(np.int32)
    return np.where(dist < exact, dist, np.minimum(large, NUM_BUCKETS - 1))


def _bucket_map_t(dil):
    blk = ATTN_BLK
    ki = np.arange(2 * blk)[:, None]
    qi = np.arange(blk)[None, :]
    return _t5_bucket_np(np.maximum(blk + qi - ki, 0) * dil).astype(np.int32)


def _attn_kernel(relb_ref, q_ref, kc_ref, kp_ref, vc_ref, vp_ref, bucket_ref,
                 o_ref, lse_ref, kbuf, vbuf, bias_t, *, n_sub):
    blk = ATTN_BLK
    first_call = jnp.logical_and(pl.program_id(0) == 0,
                                 jnp.logical_and(pl.program_id(1) == 0, pl.program_id(2) == 0))

    @pl.when(first_call)
    def _():
        bucket = bucket_ref[...]
        ki = lax.broadcasted_iota(jnp.int32, bucket.shape, 0)
        qi = lax.broadcasted_iota(jnp.int32, bucket.shape, 1)
        dist = blk + qi - ki
        band = jnp.logical_and(dist >= 0, dist <= blk)
        band_first = jnp.logical_and(band, ki >= blk)

        def per_head(h, carry):
            acc = jnp.zeros(bucket.shape, F32)
            for b in range(NUM_BUCKETS):
                acc = jnp.where(bucket == b, relb_ref[b, h] * LOG2_E, acc)
            bias_t[0, h] = jnp.where(band_first, acc, NEG_INF)
            bias_t[1, h] = jnp.where(band, acc, NEG_INF)
            return carry

        lax.fori_loop(0, N_HEADS, per_head, 0)

    first_variant = jnp.where(pl.program_id(2) == 0, 0, 1)
    n_res = q_ref.shape[1]
    for g in range(n_res):
        kbuf[g, 0:blk, :] = kp_ref[0, g]
        kbuf[g, blk:, :] = kc_ref[0, g]
        vbuf[g, 0:blk, :] = vp_ref[0, g]
        vbuf[g, blk:, :] = vc_ref[0, g]

    lane = lax.broadcasted_iota(jnp.int32, (1, LANES), 1)
    lo_half = lane < HEAD_DIM
    bd_row = lax.broadcasted_iota(jnp.int32, (4 * blk, LANES), 0)
    bd_col = lax.broadcasted_iota(jnp.int32, (4 * blk, LANES), 1)
    ones_bd = ((bd_row < 2 * blk) == (bd_col < HEAD_DIM)).astype(F32).astype(BF16)
    contract_last = (((1,), (1,)), ((), ()))
    contract_first = (((0,), (0,)), ((), ()))

    def sub_block(g, i):
        r0 = i * blk
        q = q_ref[0, g, r0:r0 + blk, :]
        kk = kbuf[g, r0:r0 + 2 * blk, :]
        vv = vbuf[g, r0:r0 + 2 * blk, :]
        variant = first_variant if i == 0 else 1
        for j in range(N_HEADS // 2):
            cols = slice(j * LANES, (j + 1) * LANES)
            qj, kj, vj = q[:, cols], kk[:, cols], vv[:, cols]
            probs_t, maxes = [], []
            for hh in range(2):
                sel = lo_half if hh == 0 else jnp.logical_not(lo_half)
                qm = jnp.where(sel, qj, jnp.zeros_like(qj))
                s_t = lax.dot_general(kj, qm, contract_last, preferred_element_type=F32)
                s_t = s_t + bias_t[variant, 2 * j + hh]
                m = jnp.max(s_t, axis=0, keepdims=True)
                probs_t.append(jnp.exp2(s_t - m).astype(BF16))
                maxes.append(m)
            p2_t = jnp.concatenate(probs_t, axis=0)
            v_bd = jnp.concatenate([jnp.where(lo_half, vj, jnp.zeros_like(vj)),
                                    jnp.where(lo_half, jnp.zeros_like(vj), vj)], axis=0)
            rhs = jnp.concatenate([v_bd, ones_bd], axis=1)
            ol = lax.dot_general(p2_t, rhs, contract_first, preferred_element_type=F32)
            o2, l2 = ol[:, :LANES], ol[:, LANES:]
            m_t = jnp.concatenate([jnp.broadcast_to(maxes[0], (HEAD_DIM, blk)),
                                   jnp.broadcast_to(maxes[1], (HEAD_DIM, blk))], axis=0)
            o_ref[0, g, r0:r0 + blk, cols] = (o2 / l2).astype(o_ref.dtype)
            lse_ref[0, g, r0:r0 + blk, cols] = m_t.T + jnp.log2(l2)

    for g in range(n_res):
        for i in range(n_sub):
            sub_block(g, i)


def _attention_pattern(q, k, v, rel_bias, dil):
    bsz, _, sub_len, aw = q.shape
    blk = ATTN_BLK
    assert sub_len % blk == 0
    tq = min(ATTN_STEP_ROWS, sub_len)
    n_sub = tq // blk
    ratio = tq // blk
    n_res = min(dil, ATTN_STEP_ROWS // tq)
    cur = lambda b, r, n: (b, r, n, 0)
    prev = lambda b, r, n: (b, r, jnp.maximum(n * ratio - 1, 0), 0)
    blk_cur = pl.BlockSpec((1, n_res, tq, aw), cur)
    blk_prev = pl.BlockSpec((1, n_res, blk, aw), prev)
    bucket = jnp.asarray(_bucket_map_t(dil))
    o_dtype = BF16 if TIME_TILE // dil >= 2 * SUBLANES else F32
    return pl.pallas_call(
        functools.partial(_attn_kernel, n_sub=n_sub),
        out_shape=(jax.ShapeDtypeStruct(q.shape, o_dtype), jax.ShapeDtypeStruct(q.shape, F32)),
        grid=(bsz, dil // n_res, sub_len // tq),
        in_specs=[pl.BlockSpec(memory_space=pltpu.SMEM),
                  blk_cur, blk_cur, blk_prev, blk_cur, blk_prev,
                  pl.BlockSpec(bucket.shape, lambda b, r, n: (0, 0))],
        out_specs=(blk_cur, blk_cur),
        scratch_shapes=[pltpu.VMEM((n_res, tq + blk, aw), BF16),
                        pltpu.VMEM((n_res, tq + blk, aw), BF16),
                        pltpu.VMEM((2, N_HEADS, 2 * blk, blk), F32)],
        compiler_params=_cparams("arbitrary", "arbitrary", "arbitrary"),
        name=f"attn_dil{dil}",
    )(rel_bias.astype(F32), q, k, k, v, v, bucket)


def _ssm_kernel(u_ref, bmat_ref, cmat_ref, ar_ref, ai_ref, dskip_ref, wglu_ref, bglu_ref,
                o_ref, hbuf, hstate, *, n_steps):
    @pl.when(pl.program_id(0) == 0)
    def _():
        hstate[...] = jnp.zeros_like(hstate)

    n_slab = u_ref.shape[0]
    n_state = hbuf.shape[1] // 2
    per = n_state // n_slab
    us = [u_ref[s] for s in range(n_slab)]
    ys = [None] * n_slab

    def drive(s):
        bu = _dot(us[s].astype(BF16), bmat_ref[s])
        hbuf[:, s * per:(s + 1) * per] = bu[:, :per]
        hbuf[:, n_state + s * per:n_state + (s + 1) * per] = bu[:, per:]

    def scan(s):
        re_cols = slice(s * per, (s + 1) * per)
        im_cols = slice(n_state + s * per, n_state + (s + 1) * per)
        ar = ar_ref[:, re_cols]
        ai = ai_ref[:, re_cols]
        hr = hstate[:, re_cols]
        hi = hstate[:, im_cols]
        for t in range(n_steps):
            trow = slice(t * SUBLANES, (t + 1) * SUBLANES)
            nr = ar * hr - ai * hi + hbuf[trow, re_cols]
            ni = ar * hi + ai * hr + hbuf[trow, im_cols]
            hbuf[trow, re_cols] = nr
            hbuf[trow, im_cols] = ni
            hr, hi = nr, ni
        hstate[:, re_cols] = hr
        hstate[:, im_cols] = hi

    def read_out(s):
        h_s = _lane_concat([hbuf[:, s * per:(s + 1) * per],
                            hbuf[:, n_state + s * per:n_state + (s + 1) * per]])
        ys[s] = (_dot(h_s.astype(BF16), cmat_ref[s])
                 + dskip_ref[:, s * LANES:(s + 1) * LANES] * us[s])

    for tick in range(n_slab + 2):
        if tick < n_slab:
            drive(tick)
        if 0 <= tick - 1 < n_slab:
            scan(tick - 1)
        if 0 <= tick - 2 < n_slab:
            read_out(tick - 2)
    y = _lane_concat(ys)
    y = 0.5 * y * (1.0 + jnp.tanh(math.sqrt(2.0 / math.pi) * (y + 0.044715 * (y * y * y))))
    z = _dot(y.astype(BF16), wglu_ref[...]) + bglu_ref[...]
    out = y * _sigmoid(z)
    for s in range(n_slab):
        o_ref[s] = out[:, s * LANES:(s + 1) * LANES]


def _ssm_params(a_re, a_im, log_dt, b_re, b_im, c_re, c_im, bsz):
    g, p = a_re.shape
    hg = b_re.shape[-1]
    dt = jnp.exp(log_dt.astype(F32))[:, None]
    a_re, a_im = a_re.astype(F32), a_im.astype(F32)
    mag = jnp.exp(a_re * dt)
    abar_re = mag * jnp.cos(a_im * dt)
    abar_im = mag * jnp.sin(a_im * dt)
    den = a_re * a_re + a_im * a_im
    q_re = ((abar_re - 1.0) * a_re + abar_im * a_im) / den
    q_im = (abar_im * a_re - (abar_re - 1.0) * a_im) / den
    b_re, b_im = b_re.astype(F32), b_im.astype(F32)
    bb_re = q_re[..., None] * b_re - q_im[..., None] * b_im
    bb_im = q_re[..., None] * b_im + q_im[..., None] * b_re
    gs = LANES // hg
    n_slab = g // gs
    eye = jnp.eye(gs, dtype=F32)

    def in_mat(t):
        t = t.reshape(n_slab, gs, p, hg)
        return jnp.einsum('sgph,gk->sghkp', t, eye).reshape(n_slab, gs * hg, gs * p)

    def out_mat(t):
        t = t.reshape(n_slab, gs, hg, p)
        return jnp.einsum('sghp,gk->sgpkh', t, eye).reshape(n_slab, gs * p, gs * hg)

    bmat = jnp.concatenate([in_mat(bb_re), in_mat(bb_im)], axis=2).astype(BF16)
    cmat = jnp.concatenate([out_mat(c_re.astype(F32)), -out_mat(c_im.astype(F32))],
                           axis=1).astype(BF16)
    ar = jnp.broadcast_to(abar_re.reshape(1, g * p), (bsz, g * p))
    ai = jnp.broadcast_to(abar_im.reshape(1, g * p), (bsz, g * p))
    return bmat, cmat, ar, ai


def _ssm_branch(u_slabs, bsz, a_re, a_im, log_dt, b_re, b_im, c_re, c_im, d_skip, w_glu, b_glu):
    n_slab, n_rows, _ = u_slabs.shape
    width = n_slab * LANES
    assert bsz == SUBLANES
    bmat, cmat, ar, ai = _ssm_params(a_re, a_im, log_dt, b_re, b_im, c_re, c_im, bsz)
    n_state2 = n_slab * bmat.shape[2]
    rows = SSM_STEPS * bsz
    const = lambda c: (0, 0)
    const3 = lambda c: (0, 0, 0)
    slab_spec = pl.BlockSpec((n_slab, rows, LANES), lambda c: (0, c, 0))
    return pl.pallas_call(
        functools.partial(_ssm_kernel, n_steps=SSM_STEPS),
        out_shape=jax.ShapeDtypeStruct(u_slabs.shape, F32),
        grid=(n_rows // rows,),
        in_specs=[slab_spec,
                  pl.BlockSpec(bmat.shape, const3), pl.BlockSpec(cmat.shape, const3),
                  pl.BlockSpec(ar.shape, const), pl.BlockSpec(ai.shape, const),
                  pl.BlockSpec((1, width), const), pl.BlockSpec((width, width), const),
                  pl.BlockSpec((1, width), const)],
        out_specs=slab_spec,
        scratch_shapes=[pltpu.VMEM((rows, n_state2), F32), pltpu.VMEM((bsz, n_state2), F32)],
        compiler_params=_cparams("arbitrary"),
        name="ssm",
    )(u_slabs, bmat, cmat, ar, ai, d_skip.reshape(1, width).astype(F32),
      w_glu.astype(BF16), b_glu.reshape(1, width).astype(F32))


ROUTE_LANES = {"id0": 0, "id1": 1, "rank0": 2, "rank1": 3, "w0": 4, "w1": 5}
GROUP_LANE0 = N_EXPERTS


def _merge_kernel(*refs, bsz_total):
    n_pat = len(DILATION_PATTERNS)
    x_ref, gate_ref = refs[0:2]
    attn_refs = refs[2:2 + 2 * n_pat]
    (ssm_ref, mod_ref, gpost_ref, gpre_ref, wba_ref, wbs_ref, wout_ref,
     wr_cat_ref, wr_hi_ref, br_ref) = refs[2 + 2 * n_pat:12 + 2 * n_pat]
    x1_ref, h2_ref, route_ref, count_ref = refs[12 + 2 * n_pat:16 + 2 * n_pat]
    scratch = refs[16 + 2 * n_pat:]
    carry = scratch[-1]

    @pl.when(jnp.logical_and(pl.program_id(0) == 0, pl.program_id(1) == 0))
    def _():
        carry[...] = jnp.zeros_like(carry)

    nb, tt, d = x_ref.shape
    n_slab = ssm_ref.shape[0]
    b0 = pl.program_id(0) * nb
    cb = MERGE_CHUNK_BATCH
    rows = cb * tt
    lane = lax.broadcasted_iota(jnp.int32, (rows, LANES), 1).astype(F32)
    row = lax.broadcasted_iota(jnp.int32, (rows, rows), 0)
    col = lax.broadcasted_iota(jnp.int32, (rows, rows), 1)
    strict_lower = (col < row).astype(BF16)
    running = [carry[...]]

    def chunk_phases(c0):
        bbs = range(c0, c0 + cb)
        o_tok, lse_tok, ssm_tok = scratch[3 * (c0 // cb):3 * (c0 // cb) + 3]

        for bb in bbs:
            for s in range(n_slab):
                ssm_tok[(bb - c0) * tt:(bb - c0 + 1) * tt, s * LANES:(s + 1) * LANES] = (
                    ssm_ref[s, pl.ds(b0 + bb, tt, stride=bsz_total), :])

        slot = 0
        sources = []
        for p, (_, dil) in enumerate(DILATION_PATTERNS):
            o_ref, lse_ref = attn_refs[2 * p], attn_refs[2 * p + 1]
            if dil == 1:
                sources.append((o_ref, lse_ref, None))
                continue
            sub = tt // dil
            for bb in bbs:
                for r in range(dil):
                    o_blk = o_ref[bb, r].astype(F32)
                    l_blk = lse_ref[bb, r]
                    for s in range(n_slab):
                        dst = pl.ds((bb - c0) * tt + r, sub, stride=dil)
                        o_tok[slot, s, dst, :] = o_blk[:, s * LANES:(s + 1) * LANES]
                        lse_tok[slot, s, dst, :] = l_blk[:, s * LANES:(s + 1) * LANES]
            sources.append((o_ref, lse_ref, slot))
            slot += 1
        attn_slabs = []
        for s in range(n_slab):
            cols = slice(s * LANES, (s + 1) * LANES)
            o_ps, lse_ps = [], []
            for o_ref, lse_ref, src_slot in sources:
                if src_slot is None:
                    o_ps.append(o_ref[c0:c0 + cb, 0, :, cols].astype(F32).reshape(rows, LANES))
                    lse_ps.append(lse_ref[c0:c0 + cb, 0, :, cols].reshape(rows, LANES))
                else:
                    o_ps.append(o_tok[src_slot, s])
                    lse_ps.append(lse_tok[src_slot, s])
            m = functools.reduce(jnp.maximum, lse_ps)
            es = [jnp.exp2(l - m) for l in lse_ps]
            num = functools.reduce(lambda a, b: a + b, [e * o for e, o in zip(es, o_ps)])
            den = functools.reduce(lambda a, b: a + b, es)
            attn_slabs.append(num / den)
        attn_b = _lane_concat(attn_slabs).astype(BF16)
        ssm_b = ssm_tok[...].astype(BF16)
        yield
        branch_attn = _dot(attn_b, wba_ref[...])
        branch_ssm = _dot(ssm_b, wbs_ref[...])
        yield
        g_attn = gate_ref[c0:c0 + cb, :, 0:d].reshape(rows, d)
        g_ssm = gate_ref[c0:c0 + cb, :, d:].reshape(rows, d)
        merged_b = g_attn * branch_attn.astype(BF16) + g_ssm * branch_ssm.astype(BF16)
        yield
        y = _dot(merged_b, wout_ref[...])
        yield
        gate1 = mod_ref[c0:c0 + cb, 2, :][:, None, :]
        shift2 = mod_ref[c0:c0 + cb, 3, :][:, None, :]
        scale2 = mod_ref[c0:c0 + cb, 4, :][:, None, :]
        x1 = x_ref[c0:c0 + cb] + gate1 * _rms_norm(y, gpost_ref[...]).reshape(cb, tt, d)
        x1_ref[c0:c0 + cb] = x1
        h2 = _rms_norm(x1, gpre_ref[...]) * (1.0 + scale2) + shift2
        for j, bb in enumerate(bbs):
            _store_row_tiles(h2_ref.at[bb], _pack_rows(h2[j]))

        a_hi, a_lo = _split_bf16(h2.reshape(rows, d))
        yield
        hi_pass = _dot(a_hi, wr_cat_ref[...])
        lo_pass = _dot(a_lo, wr_hi_ref[...])
        yield
        logits = hi_pass[:, :LANES] + lo_pass + hi_pass[:, LANES:] + br_ref[...]
        big = float(LANES)
        is_group = jnp.logical_and(lane >= GROUP_LANE0, lane < GROUP_LANE0 + N_EXPERT_GROUPS)
        gl = jnp.where(is_group, logits, -jnp.inf)
        g_max = jnp.max(gl, axis=-1, keepdims=True)
        g_sel = jnp.min(jnp.where(gl == g_max, lane, big), axis=-1, keepdims=True) - GROUP_LANE0
        g_gate = 1.0 / jnp.sum(jnp.exp(gl - g_max), axis=-1, keepdims=True)
        lo = g_sel * EXPERTS_PER_GROUP
        in_group = jnp.logical_and(lane >= lo, lane < lo + EXPERTS_PER_GROUP)
        el = jnp.where(in_group, logits, -jnp.inf)
        t0 = jnp.max(el, axis=-1, keepdims=True)
        i0 = jnp.min(jnp.where(el == t0, lane, big), axis=-1, keepdims=True)
        el1 = jnp.where(lane == i0, -jnp.inf, el)
        t1 = jnp.max(el1, axis=-1, keepdims=True)
        i1 = jnp.min(jnp.where(el1 == t1, lane, big), axis=-1, keepdims=True)
        e = jnp.exp(t1 - t0)
        w0 = g_gate / (1.0 + e)
        w1 = g_gate * e / (1.0 + e)

        hit0 = lane == i0
        hit1 = lane == i1
        onehot = jnp.logical_or(hit0, hit1).astype(F32)
        yield
        before = _dot(strict_lower, onehot.astype(BF16)) + running[0]
        rank0 = jnp.sum(jnp.where(hit0, before, 0.0), axis=-1, keepdims=True)
        rank1 = jnp.sum(jnp.where(hit1, before, 0.0), axis=-1, keepdims=True)
        running[0] = running[0] + jnp.sum(onehot, axis=0, keepdims=True)

        route = jnp.zeros((rows, LANES), F32)
        for name, val in (("id0", i0), ("id1", i1), ("rank0", rank0), ("rank1", rank1),
                          ("w0", w0), ("w1", w1)):
            route = jnp.where(lane == ROUTE_LANES[name], val, route)
        route_ref[c0:c0 + cb] = route.reshape(cb, tt, LANES)

    pending = [chunk_phases(c0) for c0 in range(0, nb, cb)]
    active = []
    while pending or active:
        if pending:
            active.append(pending.pop(0))
        for gen in list(active):
            if next(gen, StopIteration) is StopIteration:
                active.remove(gen)

    carry[...] = running[0]
    count_ref[...] = jnp.broadcast_to(running[0], count_ref.shape)


def _merge_and_route(x, gates, attn_outs, ssm_slabs, mod3, g_post, g_pre, wba, wbs, wout,
                     w_rg, b_rg, w_re, b_re):
    bsz, seq, d = x.shape
    tt, nb = TIME_TILE, MERGE_BATCH
    aw = ATTN_WIDTH
    n_slab = ssm_slabs.shape[0]
    assert aw == n_slab * LANES
    wr = jnp.zeros((d, LANES), F32).at[:, :N_EXPERTS].set(w_re.astype(F32))
    wr = wr.at[:, GROUP_LANE0:GROUP_LANE0 + N_EXPERT_GROUPS].set(w_rg.astype(F32))
    br = jnp.zeros((1, LANES), F32).at[0, :N_EXPERTS].set(b_re.astype(F32))
    br = br.at[0, GROUP_LANE0:GROUP_LANE0 + N_EXPERT_GROUPS].set(b_rg.astype(F32))
    wr_hi, wr_lo = _split_bf16(wr)
    wr_cat = jnp.concatenate([wr_hi, wr_lo], axis=1)
    n_strided = sum(1 for _, dil in DILATION_PATTERNS if dil > 1)
    chunk_rows = MERGE_CHUNK_BATCH * tt
    tok = lambda h, i: (h, i, 0)
    const = lambda h, i: (0, 0)
    attn_args, attn_specs = [], []
    for (o_p, lse_p), (_, dil) in zip(attn_outs, DILATION_PATTERNS):
        spec = pl.BlockSpec((nb, dil, tt // dil, aw), lambda h, i: (h, 0, i, 0))
        attn_args += [o_p, lse_p]
        attn_specs += [spec, spec]
    return pl.pallas_call(
        functools.partial(_merge_kernel, bsz_total=bsz),
        out_shape=(jax.ShapeDtypeStruct((bsz, seq, d), F32),
                   jax.ShapeDtypeStruct((bsz, seq * ROW_TILE, LANES), U32),
                   jax.ShapeDtypeStruct((bsz, seq, LANES), F32),
                   jax.ShapeDtypeStruct((SUBLANES, LANES), F32)),
        grid=(bsz // nb, seq // tt),
        in_specs=[pl.BlockSpec((nb, tt, d), tok),
                  pl.BlockSpec((nb, tt, gates.shape[-1]), tok)]
                 + attn_specs
                 + [pl.BlockSpec((n_slab, tt * bsz, LANES), lambda h, i: (0, i, 0)),
                    pl.BlockSpec((nb, 6, d), lambda h, i: (h, 0, 0)),
                    pl.BlockSpec((1, d), const), pl.BlockSpec((1, d), const),
                    pl.BlockSpec(wba.shape, const), pl.BlockSpec(wbs.shape, const),
                    pl.BlockSpec(wout.shape, const),
                    pl.BlockSpec((d, 2 * LANES), const), pl.BlockSpec((d, LANES), const),
                    pl.BlockSpec((1, LANES), const)],
        out_specs=(pl.BlockSpec((nb, tt, d), tok), pl.BlockSpec((nb, tt * ROW_TILE, LANES), tok),
                   pl.BlockSpec((nb, tt, LANES), tok),
                   pl.BlockSpec((SUBLANES, LANES), const)),
        scratch_shapes=[pltpu.VMEM((n_strided, n_slab, chunk_rows, LANES), F32),
                        pltpu.VMEM((n_strided, n_slab, chunk_rows, LANES), F32),
                        pltpu.VMEM((chunk_rows, n_slab * LANES), F32)] * (nb // MERGE_CHUNK_BATCH)
                       + [pltpu.VMEM((1, LANES), F32)],
        compiler_params=_cparams("arbitrary", "arbitrary"),
        name="merge",
    )(x, gates, *attn_args, ssm_slabs, mod3, g_post, g_pre, wba, wbs, wout, wr_cat, wr_hi, br)


ISSUE_UNROLL = 8


def _dispatch_kernel(dest_ref, seg_ref, h_ref, xs_ref, zero_buf, sem, zsem):
    rows = h_ref.shape[0] // ROW_TILE
    blk = zero_buf.shape[0]
    n_blocks = xs_ref.shape[0] // blk
    base = pl.program_id(0) * rows

    @pl.when(pl.program_id(0) == 0)
    def _():
        zero_buf[...] = jnp.zeros_like(zero_buf)

        def zero_copy(row0):
            return pltpu.make_async_copy(zero_buf, xs_ref.at[pl.ds(pl.multiple_of(row0, blk), blk)],
                                         zsem)

        def fill_tail(e, carry):
            @pl.when(seg_ref[N_EXPERTS + e] > 0)
            def _():
                zero_copy(seg_ref[e] * ROW_TILE - blk).start()
            return carry

        def fill_unused(j, carry):
            zero_copy(j * blk).start()
            return carry

        def wait_tail(e, carry):
            @pl.when(seg_ref[N_EXPERTS + e] > 0)
            def _():
                zero_copy(0).wait()
            return carry

        def wait_unused(j, carry):
            zero_copy(0).wait()
            return carry

        n_used = seg_ref[2 * N_EXPERTS]
        lax.fori_loop(0, N_EXPERTS, fill_tail, 0)
        lax.fori_loop(n_used, n_blocks, fill_unused, 0)
        lax.fori_loop(0, N_EXPERTS, wait_tail, 0)
        lax.fori_loop(n_used, n_blocks, wait_unused, 0)

    group = ISSUE_UNROLL * ROW_TILE

    def issue(g, carry):
        g0 = pl.multiple_of(g * group, group)
        for rr in range(ISSUE_UNROLL):
            src = h_ref.at[pl.ds(g0 + rr * ROW_TILE, ROW_TILE)]
            for k in range(TOP_K):
                d = dest_ref[TOP_K * (base + g * ISSUE_UNROLL + rr) + k]
                dst = xs_ref.at[pl.ds(pl.multiple_of(d * ROW_TILE, ROW_TILE), ROW_TILE)]
                pltpu.make_async_copy(src, dst, sem).start(priority=k)
        return carry

    lax.fori_loop(0, rows // ISSUE_UNROLL, issue, 0)
    for _ in range(TOP_K):
        pltpu.make_async_copy(h_ref, xs_ref.at[pl.ds(0, rows * ROW_TILE)], sem).wait()


def _dispatch(h2_tiles, dest, seg_info, cap):
    n_tok = h2_tiles.shape[0] // ROW_TILE
    rows = DISPATCH_ROWS
    return pl.pallas_call(
        _dispatch_kernel,
        out_shape=jax.ShapeDtypeStruct((cap * ROW_TILE, LANES), h2_tiles.dtype),
        grid_spec=pltpu.PrefetchScalarGridSpec(
            num_scalar_prefetch=2,
            grid=(n_tok // rows,),
            in_specs=[pl.BlockSpec((rows * ROW_TILE, LANES), lambda i, dest, seg: (i, 0))],
            out_specs=pl.BlockSpec(memory_space=pl.ANY),
            scratch_shapes=[pltpu.VMEM((MOE_ROWS * ROW_TILE, LANES), h2_tiles.dtype),
                            pltpu.SemaphoreType.DMA, pltpu.SemaphoreType.DMA]),
        compiler_params=_cparams("arbitrary"),
        name="dispatch",
    )(dest, seg_info, h2_tiles)


def _expert_kernel(blk_expert_ref, seg_slot_ref, next_expert_ref, n_used_ref, x_ref,
                   w1_hbm, w3_hbm, w2_hbm, y_ref, w1f, w3f, w2f, w1b, w3b, w2b, sems):
    i = pl.program_id(0)
    used = i < n_used_ref[0]
    rows = x_ref.shape[0] // ROW_TILE
    expert = blk_expert_ref[i]
    new_expert = jnp.logical_or(i == 0, expert != blk_expert_ref[jnp.maximum(i - 1, 0)])
    slot = seg_slot_ref[i]

    def weight_copies(e, s):
        return [pltpu.make_async_copy(hbm.at[e], stage.at[s], sems.at[s, n])
                for n, (hbm, stage) in enumerate(((w1_hbm, w1f), (w3_hbm, w3f), (w2_hbm, w2f)))]

    @pl.when(jnp.logical_and(used, new_expert))
    def _():
        @pl.when(i == 0)
        def _():
            for cp in weight_copies(expert, slot):
                cp.start()

        for cp in weight_copies(expert, slot):
            cp.wait()
        w1b[...] = w1f[slot].astype(BF16)
        w3b[...] = w3f[slot].astype(BF16)
        w2b[...] = w2f[slot].astype(BF16)
        nxt = next_expert_ref[i]

        @pl.when(nxt >= 0)
        def _():
            for cp in weight_copies(nxt, 1 - slot):
                cp.start()

    @pl.when(used)
    def _():
        xb = _unpack_rows(_load_row_tiles(x_ref, rows)).astype(BF16)
        h1 = _dot(xb, w1b[...])
        h3 = _dot(xb, w3b[...])
        act = (h1 * _sigmoid(h1)) * h3
        y = _dot(act.astype(BF16), w2b[...])
        _store_row_tiles(y_ref, _pack_rows(y))

    @pl.when(jnp.logical_not(used))
    def _():
        y_ref[...] = jnp.zeros_like(y_ref)


def _experts(xs, blk_expert, n_used, w1, w3, w2):
    d, de = w1.shape[-2:]
    assert d == 2 * ROW_TILE * LANES
    rows = MOE_ROWS
    tile_rows = rows * ROW_TILE
    n_blocks = xs.shape[0] // tile_rows
    blk = jnp.arange(n_blocks, dtype=jnp.int32)
    changed = jnp.concatenate([jnp.zeros((1,), jnp.int32),
                               (blk_expert[1:] != blk_expert[:-1]).astype(jnp.int32)])
    seg_slot = (jnp.cumsum(changed) % 2).astype(jnp.int32)
    later = jnp.logical_and(blk[None, :] > blk[:, None],
                            jnp.logical_and(blk_expert[None, :] != blk_expert[:, None],
                                            blk[None, :] < n_used[0]))
    first_later = jnp.argmax(later, axis=1)
    next_expert = jnp.where(jnp.any(later, axis=1), blk_expert[first_later], -1).astype(jnp.int32)
    xblk = lambda i, be, ss, ne, nu: (jnp.minimum(i, nu[0] - 1), 0)
    return pl.pallas_call(
        _expert_kernel,
        out_shape=jax.ShapeDtypeStruct(xs.shape, U32),
        grid_spec=pltpu.PrefetchScalarGridSpec(
            num_scalar_prefetch=4,
            grid=(n_blocks,),
            in_specs=[pl.BlockSpec((tile_rows, LANES), xblk),
                      pl.BlockSpec(memory_space=pl.ANY), pl.BlockSpec(memory_space=pl.ANY),
                      pl.BlockSpec(memory_space=pl.ANY)],
            out_specs=pl.BlockSpec((tile_rows, LANES), lambda i, be, ss, ne, nu: (i, 0)),
            scratch_shapes=[pltpu.VMEM((2, d, de), w1.dtype), pltpu.VMEM((2, d, de), w3.dtype),
                            pltpu.VMEM((2, de, d), w2.dtype),
                            pltpu.VMEM((d, de), BF16), pltpu.VMEM((d, de), BF16),
                            pltpu.VMEM((de, d), BF16),
                            pltpu.SemaphoreType.DMA((2, 3))]),
        compiler_params=_cparams("arbitrary"),
        name="experts",
    )(blk_expert, seg_slot, next_expert, n_used, xs, w1, w3, w2)


def _combine_kernel(dest_ref, ys_ref, x1_ref, route_ref, mod_ref, g_ref, o_ref, ybuf, sem):
    rows = x1_ref.shape[0]
    step = pl.program_id(0)
    n_step = pl.num_programs(0)

    group = ISSUE_UNROLL * ROW_TILE

    def gather(tile, slot):
        base = tile * rows

        def issue(g, carry):
            g0 = pl.multiple_of(g * group, group)
            for rr in range(ISSUE_UNROLL):
                for k in range(TOP_K):
                    d = dest_ref[TOP_K * (base + g * ISSUE_UNROLL + rr) + k]
                    src = ys_ref.at[pl.ds(pl.multiple_of(d * ROW_TILE, ROW_TILE), ROW_TILE)]
                    dst = ybuf.at[slot, k, pl.ds(g0 + rr * ROW_TILE, ROW_TILE)]
                    pltpu.make_async_copy(src, dst, sem.at[slot]).start(priority=k)
            return carry

        lax.fori_loop(0, rows // ISSUE_UNROLL, issue, 0)

    slot = lax.rem(step, 2)

    @pl.when(step == 0)
    def _():
        gather(step, 0)

    @pl.when(step + 1 < n_step)
    def _():
        gather(step + 1, 1 - slot)

    for k in range(TOP_K):
        pltpu.make_async_copy(ys_ref.at[pl.ds(0, rows * ROW_TILE)], ybuf.at[slot, k],
                              sem.at[slot]).wait()
    route = route_ref[...]
    w0 = route[:, ROUTE_LANES["w0"]:ROUTE_LANES["w0"] + 1]
    w1 = route[:, ROUTE_LANES["w1"]:ROUTE_LANES["w1"] + 1]

    def rows_of(k):
        return _unpack_rows(_load_row_tiles(ybuf.at[slot, k], rows))

    y = rows_of(0) * w0 + rows_of(1) * w1
    o_ref[...] = x1_ref[...] + mod_ref[0, 5:6, :] * _rms_norm(y, g_ref[...])


def _combine(ys, dest, x1_flat, route_flat, mod3, g_post, seq):
    n_tok, d = x1_flat.shape
    rows = COMBINE_ROWS
    assert seq % rows == 0
    return pl.pallas_call(
        _combine_kernel,
        out_shape=jax.ShapeDtypeStruct((n_tok, d), F32),
        grid_spec=pltpu.PrefetchScalarGridSpec(
            num_scalar_prefetch=1,
            grid=(n_tok // rows,),
            in_specs=[pl.BlockSpec(memory_space=pl.ANY),
                      pl.BlockSpec((rows, d), lambda i, dest: (i, 0)),
                      pl.BlockSpec((rows, LANES), lambda i, dest: (i, 0)),
                      pl.BlockSpec((1, 6, d), lambda i, dest: (i * rows // seq, 0, 0)),
                      pl.BlockSpec((1, d), lambda i, dest: (0, 0))],
            out_specs=pl.BlockSpec((rows, d), lambda i, dest: (i, 0)),
            scratch_shapes=[pltpu.VMEM((2, TOP_K, rows * ROW_TILE, LANES), U32),
                            pltpu.SemaphoreType.DMA((2,))]),
        compiler_params=_cparams("arbitrary"),
        name="combine",
    )(dest, ys, x1_flat, route_flat, mod3, g_post)


def _moe_layout(route_flat, counts):
    rows = MOE_ROWS
    n_tok = route_flat.shape[0]
    ids = route_flat[:, 0:TOP_K].astype(jnp.int32)
    rank = route_flat[:, TOP_K:2 * TOP_K].astype(jnp.int32)
    counts = counts.astype(jnp.int32)
    padded = (counts + rows - 1) // rows * rows
    pad_ends = jnp.cumsum(padded)
    pad_starts = pad_ends - padded
    expert = jnp.arange(N_EXPERTS, dtype=jnp.int32)
    start_of = jnp.sum(jnp.where(ids[..., None] == expert, pad_starts, 0), axis=-1)
    dest = (start_of + rank).reshape(n_tok * TOP_K)
    n_blocks = (n_tok * TOP_K + N_EXPERTS * (rows - 1) + rows - 1) // rows
    blk_row0 = jnp.arange(n_blocks, dtype=jnp.int32) * rows
    blk_expert = jnp.minimum(jnp.sum(pad_ends[None, :] <= blk_row0[:, None], axis=-1),
                             N_EXPERTS - 1).astype(jnp.int32)
    n_used = (pad_ends[-1] // rows).reshape(1).astype(jnp.int32)
    seg_info = jnp.concatenate([pad_ends, padded, n_used]).astype(jnp.int32)
    return dest, blk_expert, n_used, seg_info, n_blocks * rows


def kernel(x, c, w_mod, b_mod, g_pre_mix, g_post_mix, g_pre_ffn, g_post_ffn, w_in, rel_bias, a_re, a_im, log_dt, ssm_b_re, ssm_b_im, ssm_c_re, ssm_c_im, d_skip, w_glu, b_glu, w_branch_attn, w_branch_ssm, w_out, w_router_group, b_router_group, w_router_expert, b_router_expert, w1, w3, w2):
    bsz, seq, d = x.shape
    depth = w_mod.shape[0]
    ssm_width = w_glu.shape[-1]
    n_pat = len(DILATION_PATTERNS)
    for l in range(depth):
        mod3 = _modulation(c, w_mod[l], b_mod[l]).reshape(bsz, 6, d)
        w_in_l = w_in[l].astype(BF16)
        g_pre = g_pre_mix[l].reshape(1, d)
        qkv = _qkv_projection(x, mod3, g_pre, w_in_l[:, :3 * ATTN_WIDTH])
        u_slabs, gates = _ugate_projection(x, mod3, g_pre, w_in_l[:, 3 * ATTN_WIDTH:], ssm_width)
        attn_outs = [_attention_pattern(*qkv[3 * p:3 * p + 3], rel_bias, DILATION_PATTERNS[p][1])
                     for p in range(n_pat)]
        ssm_slabs = _ssm_branch(u_slabs, bsz, a_re[l], a_im[l], log_dt[l], ssm_b_re[l],
                                ssm_b_im[l], ssm_c_re[l], ssm_c_im[l], d_skip[l], w_glu[l], b_glu[l])
        x1, h2, route, counts = _merge_and_route(
            x, gates, attn_outs, ssm_slabs, mod3, g_post_mix[l].reshape(1, d),
            g_pre_ffn[l].reshape(1, d), w_branch_attn[l].astype(BF16),
            w_branch_ssm[l].astype(BF16), w_out[l].astype(BF16),
            w_router_group[l], b_router_group[l], w_router_expert[l], b_router_expert[l])
        route_flat = route.reshape(bsz * seq, LANES)
        dest, blk_expert, n_used, seg_info, cap = _moe_layout(route_flat, counts[0, :N_EXPERTS])
        xs = _dispatch(h2.reshape(bsz * seq * ROW_TILE, LANES), dest, seg_info, cap)
        ys = _experts(xs, blk_expert, n_used, w1[l], w3[l], w2[l])
        x = _combine(ys, dest, x1.reshape(bsz * seq, d), route_flat, mod3,
                     g_post_ffn[l].reshape(1, d), seq).reshape(bsz, seq, d)
    return x
```

```python
import functools
import math

import numpy as np
import jax
import jax.numpy as jnp
from jax import lax
from jax.experimental import pallas as pl
from jax.experimental.pallas import tpu as pltpu

F32 = jnp.float32
BF16 = jnp.bfloat16

N_HEADS = 8
HEAD_DIM = 64
ATTN_WIDTH = N_HEADS * HEAD_DIM
DILATION_PATTERNS = ((128, 1), (512, 4), (2048, 16))
NUM_BUCKETS = 32
MAX_DISTANCE = 2048
N_EXPERT_GROUPS = 4
EXPERTS_PER_GROUP = 8
N_EXPERTS = N_EXPERT_GROUPS * EXPERTS_PER_GROUP
TOP_K = 2
RMS_EPS = 1e-6
NEG_INF = -1e30
LOG2_E = math.log2(math.e)

LANES = 128
SUBLANES = 8
VMEM_LIMIT_BYTES = 56 * 1024 * 1024

ATTN_BLK = 128
ATTN_STEP_ROWS = 2048
QKV_ROWS = 1024
QKV_MID_DIL = 4
TIME_TILE = 128
UGATE_CHUNK_BATCH = 2
MERGE_BATCH = 4
MERGE_CHUNK_BATCH = 2
SSM_STEPS = 128
MOE_ROWS = 512
DISPATCH_ROWS = 2048
COMBINE_ROWS = 512
ROW_TILE = 4
U32 = jnp.uint32
HI_HALF = 0xFFFF0000


def _pack_rows(x):
    w = x.shape[1] // 2
    lo = lax.bitcast_convert_type(x[:, :w].astype(BF16).astype(F32), U32) >> 16
    hi = lax.bitcast_convert_type(x[:, w:].astype(BF16).astype(F32), U32) & U32(HI_HALF)
    return hi | lo


def _unpack_rows(p):
    lo = lax.bitcast_convert_type(p << 16, F32)
    hi = lax.bitcast_convert_type(p & U32(HI_HALF), F32)
    return jnp.concatenate([lo, hi], axis=1)


def _store_row_tiles(ref, packed):
    rows = packed.shape[0]
    for s in range(ROW_TILE):
        ref[pl.ds(s, rows, stride=ROW_TILE), :] = packed[:, s * LANES:(s + 1) * LANES]


def _load_row_tiles(ref, rows):
    return _lane_concat([ref[pl.ds(s, rows, stride=ROW_TILE), :] for s in range(ROW_TILE)])


def _cparams(*sem):
    return pltpu.CompilerParams(dimension_semantics=sem, vmem_limit_bytes=VMEM_LIMIT_BYTES)


def _sigmoid(x):
    return 1.0 / (1.0 + jnp.exp(-x))


def _dot(a, b):
    return jnp.dot(a, b, preferred_element_type=F32)


def _split_bf16(a):
    hi = a.astype(BF16)
    lo = (a - hi.astype(F32)).astype(BF16)
    return hi, lo


def _dot_split(a, w_hi, w_lo):
    a_hi, a_lo = _split_bf16(a)
    return _dot(a_hi, w_hi) + _dot(a_lo, w_hi) + _dot(a_hi, w_lo)


def _rms_norm(x, gain):
    ms = jnp.mean(x * x, axis=-1, keepdims=True)
    return x * lax.rsqrt(ms + RMS_EPS) * gain


def _lane_concat(ref_slabs):
    return jnp.concatenate(ref_slabs, axis=-1)


def _trace_staggered(chunks):
    pending, active = list(chunks), []
    while pending or active:
        if pending:
            active.append(pending.pop(0))
        for gen in list(active):
            if next(gen, StopIteration) is StopIteration:
                active.remove(gen)


def _mod_kernel(c_ref, w_ref, b_ref, o_ref):
    c = c_ref[...]
    a = c * _sigmoid(c)
    w_hi, w_lo = _split_bf16(w_ref[...])
    o_ref[...] = _dot_split(a, w_hi, w_lo) + b_ref[...]


def _modulation(c, w_mod, b_mod):
    bsz, d = c.shape
    n = w_mod.shape[1]
    tn = 1024
    return pl.pallas_call(
        _mod_kernel,
        out_shape=jax.ShapeDtypeStruct((bsz, n), F32),
        grid=(n // tn,),
        in_specs=[pl.BlockSpec((bsz, d), lambda j: (0, 0)),
                  pl.BlockSpec((d, tn), lambda j: (0, j)),
                  pl.BlockSpec((1, tn), lambda j: (0, j))],
        out_specs=pl.BlockSpec((bsz, tn), lambda j: (0, j)),
        compiler_params=_cparams("arbitrary"),
        name="mod",
    )(c, w_mod, b_mod.reshape(1, n))


def _qkv_kernel(x_ref, mod_ref, g_ref, w_ref, *rest):
    n_pat = len(DILATION_PATTERNS)
    out_refs, slab, mid = rest[:3 * n_pat], rest[3 * n_pat], rest[3 * n_pat + 1]
    h = _rms_norm(x_ref[0], g_ref[...]) * (1.0 + mod_ref[0, 1:2, :]) + mod_ref[0, 0:1, :]
    hb = h.astype(BF16)
    rows = hb.shape[0]
    per_tensor = ATTN_WIDTH // LANES
    mid_dil = QKV_MID_DIL

    def project(t):
        res = _dot(hb, w_ref[:, t * ATTN_WIDTH:(t + 1) * ATTN_WIDTH])
        if t == 0:
            res = res * (HEAD_DIM ** -0.5 * LOG2_E)
        for s in range(per_tensor):
            slab[t * per_tensor + s] = res[:, s * LANES:(s + 1) * LANES]

    def split(t):
        for p, (_, dil) in enumerate(DILATION_PATTERNS):
            sub = rows // dil
            out = out_refs[3 * p + t]
            for r in range(dil):
                pieces = []
                for s in range(per_tensor):
                    ts = t * per_tensor + s
                    if dil == 1:
                        piece = slab[ts]
                    elif dil == mid_dil:
                        piece = slab[ts, pl.ds(r, sub, stride=dil), :]
                        mid[ts, r] = piece
                    else:
                        ratio = dil // mid_dil
                        piece = mid[ts, r % mid_dil, pl.ds(r // mid_dil, sub, stride=ratio), :]
                    pieces.append(piece)
                out[0, r] = _lane_concat(pieces).astype(out.dtype)

    project(0)
    for t in range(3):
        if t + 1 < 3:
            project(t + 1)
        split(t)


def _qkv_projection(x, mod3, g_pre, w_qkv):
    bsz, seq, d = x.shape
    tm = QKV_ROWS
    out_shape, out_specs = [], []
    for _, dil in DILATION_PATTERNS:
        assert tm % (dil * 2 * SUBLANES) == 0
        for _ in range(3):
            out_shape.append(jax.ShapeDtypeStruct((bsz, dil, seq // dil, ATTN_WIDTH), BF16))
            out_specs.append(pl.BlockSpec((1, dil, tm // dil, ATTN_WIDTH),
                                          lambda b, i: (b, 0, i, 0)))
    return pl.pallas_call(
        _qkv_kernel,
        out_shape=out_shape,
        grid=(bsz, seq // tm),
        in_specs=[pl.BlockSpec((1, tm, d), lambda b, i: (b, i, 0)),
                  pl.BlockSpec((1, 6, d), lambda b, i: (b, 0, 0)),
                  pl.BlockSpec((1, d), lambda b, i: (0, 0)),
                  pl.BlockSpec(w_qkv.shape, lambda b, i: (0, 0))],
        out_specs=out_specs,
        scratch_shapes=[pltpu.VMEM((w_qkv.shape[1] // LANES, tm, LANES), F32),
                        pltpu.VMEM((w_qkv.shape[1] // LANES, QKV_MID_DIL, tm // QKV_MID_DIL, LANES),
                                   F32)],
        compiler_params=_cparams("arbitrary", "arbitrary"),
        name="qkv",
    )(x, mod3, g_pre, w_qkv)


def _ugate_kernel(x_ref, mod_ref, g_ref, w_ref, u_ref, gate_ref):
    bsz, tt, d = x_ref.shape
    n_slab = u_ref.shape[0]
    sw = n_slab * LANES
    gw = gate_ref.shape[-1]
    cb = UGATE_CHUNK_BATCH
    rows = cb * tt
    col_chunk = 512

    def normalise(b0):
        shift = mod_ref[b0:b0 + cb, 0, :][:, None, :]
        scale = mod_ref[b0:b0 + cb, 1, :][:, None, :]
        h = _rms_norm(x_ref[b0:b0 + cb], g_ref[...]) * (1.0 + scale) + shift
        return h.reshape(rows, d).astype(BF16)

    def project(b0, hb):
        u = _dot(hb, w_ref[:, 0:sw])
        for j in range(cb):
            for s in range(n_slab):
                u_ref[s, pl.ds(b0 + j, tt, stride=bsz), :] = (
                    u[j * tt:(j + 1) * tt, s * LANES:(s + 1) * LANES])
        for c0 in range(0, gw, col_chunk):
            g = _sigmoid(_dot(hb, w_ref[:, sw + c0:sw + c0 + col_chunk]))
            gate_ref[b0:b0 + cb, :, c0:c0 + col_chunk] = g.reshape(cb, tt, col_chunk).astype(BF16)

    starts = list(range(0, bsz, cb))
    hb = normalise(starts[0])
    for n, b0 in enumerate(starts):
        hb_next = normalise(starts[n + 1]) if n + 1 < len(starts) else None
        project(b0, hb)
        hb = hb_next


def _ugate_projection(x, mod3, g_pre, w_ug, ssm_width):
    bsz, seq, d = x.shape
    tt = TIME_TILE
    gw = w_ug.shape[1] - ssm_width
    n_slab = ssm_width // LANES
    return pl.pallas_call(
        _ugate_kernel,
        out_shape=(jax.ShapeDtypeStruct((n_slab, seq * bsz, LANES), F32),
                   jax.ShapeDtypeStruct((bsz, seq, gw), BF16)),
        grid=(seq // tt,),
        in_specs=[pl.BlockSpec((bsz, tt, d), lambda i: (0, i, 0)),
                  pl.BlockSpec((bsz, 6, d), lambda i: (0, 0, 0)),
                  pl.BlockSpec((1, d), lambda i: (0, 0)),
                  pl.BlockSpec(w_ug.shape, lambda i: (0, 0))],
        out_specs=(pl.BlockSpec((n_slab, tt * bsz, LANES), lambda i: (0, i, 0)),
                   pl.BlockSpec((bsz, tt, gw), lambda i: (0, i, 0))),
        compiler_params=_cparams("arbitrary"),
        name="ugate",
    )(x, mod3, g_pre, w_ug)


def _t5_bucket_np(dist):
    exact = NUM_BUCKETS // 2
    d_f = np.maximum(dist, exact).astype(np.float32)
    large = exact + (np.log(d_f / np.float32(exact)) / np.float32(math.log(MAX_DISTANCE / exact))
                     * np.float32(NUM_BUCKETS - exact)).astype(np.int32)
    return np.where(dist < exact, dist, np.minimum(large, NUM_BUCKETS - 1))


def _bucket_map_t(dil):
    blk = ATTN_BLK
    ki = np.arange(2 * blk)[:, None]
    qi = np.arange(blk)[None, :]
    return _t5_bucket_np(np.maximum(blk + qi - ki, 0) * dil).astype(np.int32)


def _attn_kernel(relb_ref, q_ref, kc_ref, kp_ref, vc_ref, vp_ref, bucket_ref,
                 o_ref, lse_ref, kbuf, vbuf, bias_t, *, n_sub):
    blk = ATTN_BLK
    first_call = jnp.logical_and(pl.program_id(0) == 0,
                                 jnp.logical_and(pl.program_id(1) == 0, pl.program_id(2) == 0))

    @pl.when(first_call)
    def _():
        bucket = bucket_ref[...]
        ki = lax.broadcasted_iota(jnp.int32, bucket.shape, 0)
        qi = lax.broadcasted_iota(jnp.int32, bucket.shape, 1)
        dist = blk + qi - ki
        band = jnp.logical_and(dist >= 0, dist <= blk)
        band_first = jnp.logical_and(band, ki >= blk)

        def per_head(h, carry):
            acc = jnp.zeros(bucket.shape, F32)
            for b in range(NUM_BUCKETS):
                acc = jnp.where(bucket == b, relb_ref[b, h] * LOG2_E, acc)
            bias_t[0, h] = jnp.where(band_first, acc, NEG_INF)
            bias_t[1, h] = jnp.where(band, acc, NEG_INF)
            return carry

        lax.fori_loop(0, N_HEADS, per_head, 0)

    first_variant = jnp.where(pl.program_id(2) == 0, 0, 1)
    n_res = q_ref.shape[1]
    for g in range(n_res):
        kbuf[g, 0:blk, :] = kp_ref[0, g]
        kbuf[g, blk:, :] = kc_ref[0, g]
        vbuf[g, 0:blk, :] = vp_ref[0, g]
        vbuf[g, blk:, :] = vc_ref[0, g]

    lane = lax.broadcasted_iota(jnp.int32, (1, LANES), 1)
    lo_half = lane < HEAD_DIM
    bd_row = lax.broadcasted_iota(jnp.int32, (4 * blk, LANES), 0)
    bd_col = lax.broadcasted_iota(jnp.int32, (4 * blk, LANES), 1)
    ones_bd = ((bd_row < 2 * blk) == (bd_col < HEAD_DIM)).astype(F32).astype(BF16)
    contract_last = (((1,), (1,)), ((), ()))
    contract_first = (((0,), (0,)), ((), ()))

    def sub_block(g, i):
        r0 = i * blk
        q = q_ref[0, g, r0:r0 + blk, :]
        kk = kbuf[g, r0:r0 + 2 * blk, :]
        vv = vbuf[g, r0:r0 + 2 * blk, :]
        variant = first_variant if i == 0 else 1
        for j in range(N_HEADS // 2):
            cols = slice(j * LANES, (j + 1) * LANES)
            qj, kj, vj = q[:, cols], kk[:, cols], vv[:, cols]
            probs_t, maxes = [], []
            for hh in range(2):
                sel = lo_half if hh == 0 else jnp.logical_not(lo_half)
                qm = jnp.where(sel, qj, jnp.zeros_like(qj))
                s_t = lax.dot_general(kj, qm, contract_last, preferred_element_type=F32)
                s_t = s_t + bias_t[variant, 2 * j + hh]
                m = jnp.max(s_t, axis=0, keepdims=True)
                probs_t.append(jnp.exp2(s_t - m).astype(BF16))
                maxes.append(m)
            p2_t = jnp.concatenate(probs_t, axis=0)
            v_bd = jnp.concatenate([jnp.where(lo_half, vj, jnp.zeros_like(vj)),
                                    jnp.where(lo_half, jnp.zeros_like(vj), vj)], axis=0)
            rhs = jnp.concatenate([v_bd, ones_bd], axis=1)
            ol = lax.dot_general(p2_t, rhs, contract_first, preferred_element_type=F32)
            o2, l2 = ol[:, :LANES], ol[:, LANES:]
            m_t = jnp.concatenate([jnp.broadcast_to(maxes[0], (HEAD_DIM, blk)),
                                   jnp.broadcast_to(maxes[1], (HEAD_DIM, blk))], axis=0)
            o_ref[0, g, r0:r0 + blk, cols] = (o2 / l2).astype(o_ref.dtype)
            lse_ref[0, g, r0:r0 + blk, cols] = m_t.T + jnp.log2(l2)

    for g in range(n_res):
        for i in range(n_sub):
            sub_block(g, i)


def _attention_pattern(q, k, v, rel_bias, dil):
    bsz, _, sub_len, aw = q.shape
    blk = ATTN_BLK
    assert sub_len % blk == 0
    tq = min(ATTN_STEP_ROWS, sub_len)
    n_sub = tq // blk
    ratio = tq // blk
    n_res = min(dil, ATTN_STEP_ROWS // tq)
    cur = lambda b, r, n: (b, r, n, 0)
    prev = lambda b, r, n: (b, r, jnp.maximum(n * ratio - 1, 0), 0)
    blk_cur = pl.BlockSpec((1, n_res, tq, aw), cur)
    blk_prev = pl.BlockSpec((1, n_res, blk, aw), prev)
    bucket = jnp.asarray(_bucket_map_t(dil))
    o_dtype = BF16 if TIME_TILE // dil >= 2 * SUBLANES else F32
    return pl.pallas_call(
        functools.partial(_attn_kernel, n_sub=n_sub),
        out_shape=(jax.ShapeDtypeStruct(q.shape, o_dtype), jax.ShapeDtypeStruct(q.shape, F32)),
        grid=(bsz, dil // n_res, sub_len // tq),
        in_specs=[pl.BlockSpec(memory_space=pltpu.SMEM),
                  blk_cur, blk_cur, blk_prev, blk_cur, blk_prev,
                  pl.BlockSpec(bucket.shape, lambda b, r, n: (0, 0))],
        out_specs=(blk_cur, blk_cur),
        scratch_shapes=[pltpu.VMEM((n_res, tq + blk, aw), BF16),
                        pltpu.VMEM((n_res, tq + blk, aw), BF16),
                        pltpu.VMEM((2, N_HEADS, 2 * blk, blk), F32)],
        compiler_params=_cparams("arbitrary", "arbitrary", "arbitrary"),
        name=f"attn_dil{dil}",
    )(rel_bias.astype(F32), q, k, k, v, v, bucket)


def _ssm_kernel(u_ref, bmat_ref, cmat_ref, ar_ref, ai_ref, dskip_ref, wglu_ref, bglu_ref,
                o_ref, hbuf, hstate, *, n_steps):
    @pl.when(pl.program_id(0) == 0)
    def _():
        hstate[...] = jnp.zeros_like(hstate)

    n_slab = u_ref.shape[0]
    n_state = hbuf.shape[1] // 2
    per = n_state // n_slab
    us = [u_ref[s] for s in range(n_slab)]
    ys = [None] * n_slab

    def drive(s):
        bu = _dot(us[s].astype(BF16), bmat_ref[s])
        hbuf[:, s * per:(s + 1) * per] = bu[:, :per]
        hbuf[:, n_state + s * per:n_state + (s + 1) * per] = bu[:, per:]

    def scan(s):
        re_cols = slice(s * per, (s + 1) * per)
        im_cols = slice(n_state + s * per, n_state + (s + 1) * per)
        ar = ar_ref[:, re_cols]
        ai = ai_ref[:, re_cols]
        hr = hstate[:, re_cols]
        hi = hstate[:, im_cols]
        for t in range(n_steps):
            trow = slice(t * SUBLANES, (t + 1) * SUBLANES)
            nr = ar * hr - ai * hi + hbuf[trow, re_cols]
            ni = ar * hi + ai * hr + hbuf[trow, im_cols]
            hbuf[trow, re_cols] = nr
            hbuf[trow, im_cols] = ni
            hr, hi = nr, ni
        hstate[:, re_cols] = hr
        hstate[:, im_cols] = hi

    def read_out(s):
        h_s = _lane_concat([hbuf[:, s * per:(s + 1) * per],
                            hbuf[:, n_state + s * per:n_state + (s + 1) * per]])
        ys[s] = (_dot(h_s.astype(BF16), cmat_ref[s])
                 + dskip_ref[:, s * LANES:(s + 1) * LANES] * us[s])

    for tick in range(n_slab + 2):
        if tick < n_slab:
            drive(tick)
        if 0 <= tick - 1 < n_slab:
            scan(tick - 1)
        if 0 <= tick - 2 < n_slab:
            read_out(tick - 2)
    y = _lane_concat(ys)
    y = 0.5 * y * (1.0 + jnp.tanh(math.sqrt(2.0 / math.pi) * (y + 0.044715 * (y * y * y))))
    z = _dot(y.astype(BF16), wglu_ref[...]) + bglu_ref[...]
    out = y * _sigmoid(z)
    for s in range(n_slab):
        o_ref[s] = out[:, s * LANES:(s + 1) * LANES]


def _ssm_params(a_re, a_im, log_dt, b_re, b_im, c_re, c_im, bsz):
    g, p = a_re.shape
    hg = b_re.shape[-1]
    dt = jnp.exp(log_dt.astype(F32))[:, None]
    a_re, a_im = a_re.astype(F32), a_im.astype(F32)
    mag = jnp.exp(a_re * dt)
    abar_re = mag * jnp.cos(a_im * dt)
    abar_im = mag * jnp.sin(a_im * dt)
    den = a_re * a_re + a_im * a_im
    q_re = ((abar_re - 1.0) * a_re + abar_im * a_im) / den
    q_im = (abar_im * a_re - (abar_re - 1.0) * a_im) / den
    b_re, b_im = b_re.astype(F32), b_im.astype(F32)
    bb_re = q_re[..., None] * b_re - q_im[..., None] * b_im
    bb_im = q_re[..., None] * b_im + q_im[..., None] * b_re
    gs = LANES // hg
    n_slab = g // gs
    eye = jnp.eye(gs, dtype=F32)

    def in_mat(t):
        t = t.reshape(n_slab, gs, p, hg)
        return jnp.einsum('sgph,gk->sghkp', t, eye).reshape(n_slab, gs * hg, gs * p)

    def out_mat(t):
        t = t.reshape(n_slab, gs, hg, p)
        return jnp.einsum('sghp,gk->sgpkh', t, eye).reshape(n_slab, gs * p, gs * hg)

    bmat = jnp.concatenate([in_mat(bb_re), in_mat(bb_im)], axis=2).astype(BF16)
    cmat = jnp.concatenate([out_mat(c_re.astype(F32)), -out_mat(c_im.astype(F32))],
                           axis=1).astype(BF16)
    ar = jnp.broadcast_to(abar_re.reshape(1, g * p), (bsz, g * p))
    ai = jnp.broadcast_to(abar_im.reshape(1, g * p), (bsz, g * p))
    return bmat, cmat, ar, ai


def _ssm_branch(u_slabs, bsz, a_re, a_im, log_dt, b_re, b_im, c_re, c_im, d_skip, w_glu, b_glu):
    n_slab, n_rows, _ = u_slabs.shape
    width = n_slab * LANES
    assert bsz == SUBLANES
    bmat, cmat, ar, ai = _ssm_params(a_re, a_im, log_dt, b_re, b_im, c_re, c_im, bsz)
    n_state2 = n_slab * bmat.shape[2]
    rows = SSM_STEPS * bsz
    const = lambda c: (0, 0)
    const3 = lambda c: (0, 0, 0)
    slab_spec = pl.BlockSpec((n_slab, rows, LANES), lambda c: (0, c, 0))
    return pl.pallas_call(
        functools.partial(_ssm_kernel, n_steps=SSM_STEPS),
        out_shape=jax.ShapeDtypeStruct(u_slabs.shape, F32),
        grid=(n_rows // rows,),
        in_specs=[slab_spec,
                  pl.BlockSpec(bmat.shape, const3), pl.BlockSpec(cmat.shape, const3),
                  pl.BlockSpec(ar.shape, const), pl.BlockSpec(ai.shape, const),
                  pl.BlockSpec((1, width), const), pl.BlockSpec((width, width), const),
                  pl.BlockSpec((1, width), const)],
        out_specs=slab_spec,
        scratch_shapes=[pltpu.VMEM((rows, n_state2), F32), pltpu.VMEM((bsz, n_state2), F32)],
        compiler_params=_cparams("arbitrary"),
        name="ssm",
    )(u_slabs, bmat, cmat, ar, ai, d_skip.reshape(1, width).astype(F32),
      w_glu.astype(BF16), b_glu.reshape(1, width).astype(F32))


ROUTE_LANES = {"id0": 0, "id1": 1, "rank0": 2, "rank1": 3, "w0": 4, "w1": 5}
GROUP_LANE0 = N_EXPERTS


def _merge_kernel(*refs, bsz_total):
    n_pat = len(DILATION_PATTERNS)
    x_ref, gate_ref = refs[0:2]
    attn_refs = refs[2:2 + 2 * n_pat]
    (ssm_ref, mod_ref, gpost_ref, gpre_ref, wba_ref, wbs_ref, wout_ref,
     wr_cat_ref, wr_hi_ref, br_ref) = refs[2 + 2 * n_pat:12 + 2 * n_pat]
    x1_ref, h2_ref, route_ref, count_ref = refs[12 + 2 * n_pat:16 + 2 * n_pat]
    scratch = refs[16 + 2 * n_pat:]
    carry = scratch[-1]

    @pl.when(jnp.logical_and(pl.program_id(0) == 0, pl.program_id(1) == 0))
    def _():
        carry[...] = jnp.zeros_like(carry)

    nb, tt, d = x_ref.shape
    n_slab = ssm_ref.shape[0]
    b0 = pl.program_id(0) * nb
    cb = MERGE_CHUNK_BATCH
    rows = cb * tt
    lane = lax.broadcasted_iota(jnp.int32, (rows, LANES), 1).astype(F32)
    row = lax.broadcasted_iota(jnp.int32, (rows, rows), 0)
    col = lax.broadcasted_iota(jnp.int32, (rows, rows), 1)
    strict_lower = (col < row).astype(BF16)
    running = [carry[...]]

    def chunk_phases(c0):
        bbs = range(c0, c0 + cb)
        o_tok, lse_tok, ssm_tok = scratch[3 * (c0 // cb):3 * (c0 // cb) + 3]

        for bb in bbs:
            for s in range(n_slab):
                ssm_tok[(bb - c0) * tt:(bb - c0 + 1) * tt, s * LANES:(s + 1) * LANES] = (
                    ssm_ref[s, pl.ds(b0 + bb, tt, stride=bsz_total), :])

        slot = 0
        sources = []
        for p, (_, dil) in enumerate(DILATION_PATTERNS):
            o_ref, lse_ref = attn_refs[2 * p], attn_refs[2 * p + 1]
            if dil == 1:
                sources.append((o_ref, lse_ref, None))
                continue
            sub = tt // dil
            for bb in bbs:
                for r in range(dil):
                    o_blk = o_ref[bb, r].astype(F32)
                    l_blk = lse_ref[bb, r]
                    for s in range(n_slab):
                        dst = pl.ds((bb - c0) * tt + r, sub, stride=dil)
                        o_tok[slot, s, dst, :] = o_blk[:, s * LANES:(s + 1) * LANES]
                        lse_tok[slot, s, dst, :] = l_blk[:, s * LANES:(s + 1) * LANES]
            sources.append((o_ref, lse_ref, slot))
            slot += 1
        attn_slabs = []
        for s in range(n_slab):
            cols = slice(s * LANES, (s + 1) * LANES)
            o_ps, lse_ps = [], []
            for o_ref, lse_ref, src_slot in sources:
                if src_slot is None:
                    o_ps.append(o_ref[c0:c0 + cb, 0, :, cols].astype(F32).reshape(rows, LANES))
                    lse_ps.append(lse_ref[c0:c0 + cb, 0, :, cols].reshape(rows, LANES))
                else:
                    o_ps.append(o_tok[src_slot, s])
                    lse_ps.append(lse_tok[src_slot, s])
            m = functools.reduce(jnp.maximum, lse_ps)
            es = [jnp.exp2(l - m) for l in lse_ps]
            num = functools.reduce(lambda a, b: a + b, [e * o for e, o in zip(es, o_ps)])
            den = functools.reduce(lambda a, b: a + b, es)
            attn_slabs.append(num / den)
        attn_b = _lane_concat(attn_slabs).astype(BF16)
        ssm_b = ssm_tok[...].astype(BF16)
        yield
        branch_attn = _dot(attn_b, wba_ref[...])
        branch_ssm = _dot(ssm_b, wbs_ref[...])
        yield
        g_attn = gate_ref[c0:c0 + cb, :, 0:d].reshape(rows, d)
        g_ssm = gate_ref[c0:c0 + cb, :, d:].reshape(rows, d)
        merged_b = g_attn * branch_attn.astype(BF16) + g_ssm * branch_ssm.astype(BF16)
        yield
        y = _dot(merged_b, wout_ref[...])
        yield
        gate1 = mod_ref[c0:c0 + cb, 2, :][:, None, :]
        shift2 = mod_ref[c0:c0 + cb, 3, :][:, None, :]
        scale2 = mod_ref[c0:c0 + cb, 4, :][:, None, :]
        x1 = x_ref[c0:c0 + cb] + gate1 * _rms_norm(y, gpost_ref[...]).reshape(cb, tt, d)
        x1_ref[c0:c0 + cb] = x1
        h2 = _rms_norm(x1, gpre_ref[...]) * (1.0 + scale2) + shift2
        for j, bb in enumerate(bbs):
            _store_row_tiles(h2_ref.at[bb], _pack_rows(h2[j]))

        a_hi, a_lo = _split_bf16(h2.reshape(rows, d))
        yield
        hi_pass = _dot(a_hi, wr_cat_ref[...])
        lo_pass = _dot(a_lo, wr_hi_ref[...])
        yield
        logits = hi_pass[:, :LANES] + lo_pass + hi_pass[:, LANES:] + br_ref[...]
        big = float(LANES)
        is_group = jnp.logical_and(lane >= GROUP_LANE0, lane < GROUP_LANE0 + N_EXPERT_GROUPS)
        gl = jnp.where(is_group, logits, -jnp.inf)
        g_max = jnp.max(gl, axis=-1, keepdims=True)
        g_sel = jnp.min(jnp.where(gl == g_max, lane, big), axis=-1, keepdims=True) - GROUP_LANE0
        g_gate = 1.0 / jnp.sum(jnp.exp(gl - g_max), axis=-1, keepdims=True)
        lo = g_sel * EXPERTS_PER_GROUP
        in_group = jnp.logical_and(lane >= lo, lane < lo + EXPERTS_PER_GROUP)
        el = jnp.where(in_group, logits, -jnp.inf)
        t0 = jnp.max(el, axis=-1, keepdims=True)
        i0 = jnp.min(jnp.where(el == t0, lane, big), axis=-1, keepdims=True)
        el1 = jnp.where(lane == i0, -jnp.inf, el)
        t1 = jnp.max(el1, axis=-1, keepdims=True)
        i1 = jnp.min(jnp.where(el1 == t1, lane, big), axis=-1, keepdims=True)
        e = jnp.exp(t1 - t0)
        w0 = g_gate / (1.0 + e)
        w1 = g_gate * e / (1.0 + e)

        hit0 = lane == i0
        hit1 = lane == i1
        onehot = jnp.logical_or(hit0, hit1).astype(F32)
        yield
        before = _dot(strict_lower, onehot.astype(BF16)) + running[0]
        rank0 = jnp.sum(jnp.where(hit0, before, 0.0), axis=-1, keepdims=True)
        rank1 = jnp.sum(jnp.where(hit1, before, 0.0), axis=-1, keepdims=True)
        running[0] = running[0] + jnp.sum(onehot, axis=0, keepdims=True)

        route = jnp.zeros((rows, LANES), F32)
        for name, val in (("id0", i0), ("id1", i1), ("rank0", rank0), ("rank1", rank1),
                          ("w0", w0), ("w1", w1)):
            route = jnp.where(lane == ROUTE_LANES[name], val, route)
        route_ref[c0:c0 + cb] = route.reshape(cb, tt, LANES)

    _trace_staggered([chunk_phases(c0) for c0 in range(0, nb, cb)])

    carry[...] = running[0]
    count_ref[...] = jnp.broadcast_to(running[0], count_ref.shape)


def _merge_and_route(x, gates, attn_outs, ssm_slabs, mod3, g_post, g_pre, wba, wbs, wout,
                     w_rg, b_rg, w_re, b_re):
    bsz, seq, d = x.shape
    tt, nb = TIME_TILE, MERGE_BATCH
    aw = ATTN_WIDTH
    n_slab = ssm_slabs.shape[0]
    assert aw == n_slab * LANES
    wr = jnp.zeros((d, LANES), F32).at[:, :N_EXPERTS].set(w_re.astype(F32))
    wr = wr.at[:, GROUP_LANE0:GROUP_LANE0 + N_EXPERT_GROUPS].set(w_rg.astype(F32))
    br = jnp.zeros((1, LANES), F32).at[0, :N_EXPERTS].set(b_re.astype(F32))
    br = br.at[0, GROUP_LANE0:GROUP_LANE0 + N_EXPERT_GROUPS].set(b_rg.astype(F32))
    wr_hi, wr_lo = _split_bf16(wr)
    wr_cat = jnp.concatenate([wr_hi, wr_lo], axis=1)
    n_strided = sum(1 for _, dil in DILATION_PATTERNS if dil > 1)
    chunk_rows = MERGE_CHUNK_BATCH * tt
    tok = lambda h, i: (h, i, 0)
    const = lambda h, i: (0, 0)
    attn_args, attn_specs = [], []
    for (o_p, lse_p), (_, dil) in zip(attn_outs, DILATION_PATTERNS):
        spec = pl.BlockSpec((nb, dil, tt // dil, aw), lambda h, i: (h, 0, i, 0))
        attn_args += [o_p, lse_p]
        attn_specs += [spec, spec]
    return pl.pallas_call(
        functools.partial(_merge_kernel, bsz_total=bsz),
        out_shape=(jax.ShapeDtypeStruct((bsz, seq, d), F32),
                   jax.ShapeDtypeStruct((bsz, seq * ROW_TILE, LANES), U32),
                   jax.ShapeDtypeStruct((bsz, seq, LANES), F32),
                   jax.ShapeDtypeStruct((SUBLANES, LANES), F32)),
        grid=(bsz // nb, seq // tt),
        in_specs=[pl.BlockSpec((nb, tt, d), tok),
                  pl.BlockSpec((nb, tt, gates.shape[-1]), tok)]
                 + attn_specs
                 + [pl.BlockSpec((n_slab, tt * bsz, LANES), lambda h, i: (0, i, 0)),
                    pl.BlockSpec((nb, 6, d), lambda h, i: (h, 0, 0)),
                    pl.BlockSpec((1, d), const), pl.BlockSpec((1, d), const),
                    pl.BlockSpec(wba.shape, const), pl.BlockSpec(wbs.shape, const),
                    pl.BlockSpec(wout.shape, const),
                    pl.BlockSpec((d, 2 * LANES), const), pl.BlockSpec((d, LANES), const),
                    pl.BlockSpec((1, LANES), const)],
        out_specs=(pl.BlockSpec((nb, tt, d), tok), pl.BlockSpec((nb, tt * ROW_TILE, LANES), tok),
                   pl.BlockSpec((nb, tt, LANES), tok),
                   pl.BlockSpec((SUBLANES, LANES), const)),
        scratch_shapes=[pltpu.VMEM((n_strided, n_slab, chunk_rows, LANES), F32),
                        pltpu.VMEM((n_strided, n_slab, chunk_rows, LANES), F32),
                        pltpu.VMEM((chunk_rows, n_slab * LANES), F32)] * (nb // MERGE_CHUNK_BATCH)
                       + [pltpu.VMEM((1, LANES), F32)],
        compiler_params=_cparams("arbitrary", "arbitrary"),
        name="merge",
    )(x, gates, *attn_args, ssm_slabs, mod3, g_post, g_pre, wba, wbs, wout, wr_cat, wr_hi, br)


ISSUE_UNROLL = 8


def _dispatch_kernel(dest_ref, seg_ref, h_ref, xs_ref, zero_buf, sem, zsem):
    rows = h_ref.shape[0] // ROW_TILE
    blk = zero_buf.shape[0]
    n_blocks = xs_ref.shape[0] // blk
    base = pl.program_id(0) * rows

    @pl.when(pl.program_id(0) == 0)
    def _():
        zero_buf[...] = jnp.zeros_like(zero_buf)

        def zero_copy(row0):
            return pltpu.make_async_copy(zero_buf, xs_ref.at[pl.ds(pl.multiple_of(row0, blk), blk)],
                                         zsem)

        def fill_tail(e, carry):
            @pl.when(seg_ref[N_EXPERTS + e] > 0)
            def _():
                zero_copy(seg_ref[e] * ROW_TILE - blk).start()
            return carry

        def fill_unused(j, carry):
            zero_copy(j * blk).start()
            return carry

        def wait_tail(e, carry):
            @pl.when(seg_ref[N_EXPERTS + e] > 0)
            def _():
                zero_copy(0).wait()
            return carry

        def wait_unused(j, carry):
            zero_copy(0).wait()
            return carry

        n_used = seg_ref[2 * N_EXPERTS]
        lax.fori_loop(0, N_EXPERTS, fill_tail, 0)
        lax.fori_loop(n_used, n_blocks, fill_unused, 0)
        lax.fori_loop(0, N_EXPERTS, wait_tail, 0)
        lax.fori_loop(n_used, n_blocks, wait_unused, 0)

    group = ISSUE_UNROLL * ROW_TILE

    def issue(g, carry):
        g0 = pl.multiple_of(g * group, group)
        for rr in range(ISSUE_UNROLL):
            src = h_ref.at[pl.ds(g0 + rr * ROW_TILE, ROW_TILE)]
            for k in range(TOP_K):
                d = dest_ref[TOP_K * (base + g * ISSUE_UNROLL + rr) + k]
                dst = xs_ref.at[pl.ds(pl.multiple_of(d * ROW_TILE, ROW_TILE), ROW_TILE)]
                pltpu.make_async_copy(src, dst, sem).start(priority=k)
        return carry

    lax.fori_loop(0, rows // ISSUE_UNROLL, issue, 0)
    for _ in range(TOP_K):
        pltpu.make_async_copy(h_ref, xs_ref.at[pl.ds(0, rows * ROW_TILE)], sem).wait()


def _dispatch(h2_tiles, dest, seg_info, cap):
    n_tok = h2_tiles.shape[0] // ROW_TILE
    rows = DISPATCH_ROWS
    return pl.pallas_call(
        _dispatch_kernel,
        out_shape=jax.ShapeDtypeStruct((cap * ROW_TILE, LANES), h2_tiles.dtype),
        grid_spec=pltpu.PrefetchScalarGridSpec(
            num_scalar_prefetch=2,
            grid=(n_tok // rows,),
            in_specs=[pl.BlockSpec((rows * ROW_TILE, LANES), lambda i, dest, seg: (i, 0))],
            out_specs=pl.BlockSpec(memory_space=pl.ANY),
            scratch_shapes=[pltpu.VMEM((MOE_ROWS * ROW_TILE, LANES), h2_tiles.dtype),
                            pltpu.SemaphoreType.DMA, pltpu.SemaphoreType.DMA]),
        compiler_params=_cparams("arbitrary"),
        name="dispatch",
    )(dest, seg_info, h2_tiles)


def _expert_kernel(blk_expert_ref, seg_slot_ref, next_expert_ref, n_used_ref, x_ref,
                   w1_hbm, w3_hbm, w2_hbm, y_ref, w1f, w3f, w2f, w1b, w3b, w2b, sems):
    i = pl.program_id(0)
    used = i < n_used_ref[0]
    rows = x_ref.shape[0] // ROW_TILE
    expert = blk_expert_ref[i]
    new_expert = jnp.logical_or(i == 0, expert != blk_expert_ref[jnp.maximum(i - 1, 0)])
    slot = seg_slot_ref[i]

    def weight_copies(e, s):
        return [pltpu.make_async_copy(hbm.at[e], stage.at[s], sems.at[s, n])
                for n, (hbm, stage) in enumerate(((w1_hbm, w1f), (w3_hbm, w3f), (w2_hbm, w2f)))]

    @pl.when(jnp.logical_and(used, new_expert))
    def _():
        @pl.when(i == 0)
        def _():
            for cp in weight_copies(expert, slot):
                cp.start()

        for cp in weight_copies(expert, slot):
            cp.wait()
        w1b[...] = w1f[slot].astype(BF16)
        w3b[...] = w3f[slot].astype(BF16)
        w2b[...] = w2f[slot].astype(BF16)
        nxt = next_expert_ref[i]

        @pl.when(nxt >= 0)
        def _():
            for cp in weight_copies(nxt, 1 - slot):
                cp.start()

    @pl.when(used)
    def _():
        xb = _unpack_rows(_load_row_tiles(x_ref, rows)).astype(BF16)
        h1 = _dot(xb, w1b[...])
        h3 = _dot(xb, w3b[...])
        act = (h1 * _sigmoid(h1)) * h3
        y = _dot(act.astype(BF16), w2b[...])
        _store_row_tiles(y_ref, _pack_rows(y))

    @pl.when(jnp.logical_not(used))
    def _():
        y_ref[...] = jnp.zeros_like(y_ref)


def _experts(xs, blk_expert, n_used, w1, w3, w2):
    d, de = w1.shape[-2:]
    assert d == 2 * ROW_TILE * LANES
    rows = MOE_ROWS
    tile_rows = rows * ROW_TILE
    n_blocks = xs.shape[0] // tile_rows
    blk = jnp.arange(n_blocks, dtype=jnp.int32)
    changed = jnp.concatenate([jnp.zeros((1,), jnp.int32),
                               (blk_expert[1:] != blk_expert[:-1]).astype(jnp.int32)])
    seg_slot = (jnp.cumsum(changed) % 2).astype(jnp.int32)
    later = jnp.logical_and(blk[None, :] > blk[:, None],
                            jnp.logical_and(blk_expert[None, :] != blk_expert[:, None],
                                            blk[None, :] < n_used[0]))
    first_later = jnp.argmax(later, axis=1)
    next_expert = jnp.where(jnp.any(later, axis=1), blk_expert[first_later], -1).astype(jnp.int32)
    xblk = lambda i, be, ss, ne, nu: (jnp.minimum(i, nu[0] - 1), 0)
    return pl.pallas_call(
        _expert_kernel,
        out_shape=jax.ShapeDtypeStruct(xs.shape, U32),
        grid_spec=pltpu.PrefetchScalarGridSpec(
            num_scalar_prefetch=4,
            grid=(n_blocks,),
            in_specs=[pl.BlockSpec((tile_rows, LANES), xblk),
                      pl.BlockSpec(memory_space=pl.ANY), pl.BlockSpec(memory_space=pl.ANY),
                      pl.BlockSpec(memory_space=pl.ANY)],
            out_specs=pl.BlockSpec((tile_rows, LANES), lambda i, be, ss, ne, nu: (i, 0)),
            scratch_shapes=[pltpu.VMEM((2, d, de), w1.dtype), pltpu.VMEM((2, d, de), w3.dtype),
                            pltpu.VMEM((2, de, d), w2.dtype),
                            pltpu.VMEM((d, de), BF16), pltpu.VMEM((d, de), BF16),
                            pltpu.VMEM((de, d), BF16),
                            pltpu.SemaphoreType.DMA((2, 3))]),
        compiler_params=_cparams("arbitrary"),
        name="experts",
    )(blk_expert, seg_slot, next_expert, n_used, xs, w1, w3, w2)


def _combine_kernel(dest_ref, ys_ref, x1_ref, route_ref, mod_ref, g_ref, o_ref, ybuf, sem):
    rows = x1_ref.shape[0]
    step = pl.program_id(0)
    n_step = pl.num_programs(0)

    group = ISSUE_UNROLL * ROW_TILE

    def gather(tile, slot):
        base = tile * rows

        def issue(g, carry):
            g0 = pl.multiple_of(g * group, group)
            for rr in range(ISSUE_UNROLL):
                for k in range(TOP_K):
                    d = dest_ref[TOP_K * (base + g * ISSUE_UNROLL + rr) + k]
                    src = ys_ref.at[pl.ds(pl.multiple_of(d * ROW_TILE, ROW_TILE), ROW_TILE)]
                    dst = ybuf.at[slot, k, pl.ds(g0 + rr * ROW_TILE, ROW_TILE)]
                    pltpu.make_async_copy(src, dst, sem.at[slot]).start(priority=k)
            return carry

        lax.fori_loop(0, rows // ISSUE_UNROLL, issue, 0)

    slot = lax.rem(step, 2)

    @pl.when(step == 0)
    def _():
        gather(step, 0)

    @pl.when(step + 1 < n_step)
    def _():
        gather(step + 1, 1 - slot)

    for k in range(TOP_K):
        pltpu.make_async_copy(ys_ref.at[pl.ds(0, rows * ROW_TILE)], ybuf.at[slot, k],
                              sem.at[slot]).wait()
    route = route_ref[...]
    w0 = route[:, ROUTE_LANES["w0"]:ROUTE_LANES["w0"] + 1]
    w1 = route[:, ROUTE_LANES["w1"]:ROUTE_LANES["w1"] + 1]

    def rows_of(k):
        return _unpack_rows(_load_row_tiles(ybuf.at[slot, k], rows))

    y = rows_of(0) * w0 + rows_of(1) * w1
    o_ref[...] = x1_ref[...] + mod_ref[0, 5:6, :] * _rms_norm(y, g_ref[...])


def _combine(ys, dest, x1_flat, route_flat, mod3, g_post, seq):
    n_tok, d = x1_flat.shape
    rows = COMBINE_ROWS
    assert seq % rows == 0
    return pl.pallas_call(
        _combine_kernel,
        out_shape=jax.ShapeDtypeStruct((n_tok, d), F32),
        grid_spec=pltpu.PrefetchScalarGridSpec(
            num_scalar_prefetch=1,
            grid=(n_tok // rows,),
            in_specs=[pl.BlockSpec(memory_space=pl.ANY),
                      pl.BlockSpec((rows, d), lambda i, dest: (i, 0)),
                      pl.BlockSpec((rows, LANES), lambda i, dest: (i, 0)),
                      pl.BlockSpec((1, 6, d), lambda i, dest: (i * rows // seq, 0, 0)),
                      pl.BlockSpec((1, d), lambda i, dest: (0, 0))],
            out_specs=pl.BlockSpec((rows, d), lambda i, dest: (i, 0)),
            scratch_shapes=[pltpu.VMEM((2, TOP_K, rows * ROW_TILE, LANES), U32),
                            pltpu.SemaphoreType.DMA((2,))]),
        compiler_params=_cparams("arbitrary"),
        name="combine",
    )(dest, ys, x1_flat, route_flat, mod3, g_post)


def _moe_layout(route_flat, counts):
    rows = MOE_ROWS
    n_tok = route_flat.shape[0]
    ids = route_flat[:, 0:TOP_K].astype(jnp.int32)
    rank = route_flat[:, TOP_K:2 * TOP_K].astype(jnp.int32)
    counts = counts.astype(jnp.int32)
    padded = (counts + rows - 1) // rows * rows
    pad_ends = jnp.cumsum(padded)
    pad_starts = pad_ends - padded
    expert = jnp.arange(N_EXPERTS, dtype=jnp.int32)
    start_of = jnp.sum(jnp.where(ids[..., None] == expert, pad_starts, 0), axis=-1)
    dest = (start_of + rank).reshape(n_tok * TOP_K)
    n_blocks = (n_tok * TOP_K + N_EXPERTS * (rows - 1) + rows - 1) // rows
    blk_row0 = jnp.arange(n_blocks, dtype=jnp.int32) * rows
    blk_expert = jnp.minimum(jnp.sum(pad_ends[None, :] <= blk_row0[:, None], axis=-1),
                             N_EXPERTS - 1).astype(jnp.int32)
    n_used = (pad_ends[-1] // rows).reshape(1).astype(jnp.int32)
    seg_info = jnp.concatenate([pad_ends, padded, n_used]).astype(jnp.int32)
    return dest, blk_expert, n_used, seg_info, n_blocks * rows


def kernel(x, c, w_mod, b_mod, g_pre_mix, g_post_mix, g_pre_ffn, g_post_ffn, w_in, rel_bias, a_re, a_im, log_dt, ssm_b_re, ssm_b_im, ssm_c_re, ssm_c_im, d_skip, w_glu, b_glu, w_branch_attn, w_branch_ssm, w_out, w_router_group, b_router_group, w_router_expert, b_router_expert, w1, w3, w2):
    bsz, seq, d = x.shape
    depth = w_mod.shape[0]
    ssm_width = w_glu.shape[-1]
    n_pat = len(DILATION_PATTERNS)
    for l in range(depth):
        mod3 = _modulation(c, w_mod[l], b_mod[l]).reshape(bsz, 6, d)
        w_in_l = w_in[l].astype(BF16)
        g_pre = g_pre_mix[l].reshape(1, d)
        qkv = _qkv_projection(x, mod3, g_pre, w_in_l[:, :3 * ATTN_WIDTH])
        u_slabs, gates = _ugate_projection(x, mod3, g_pre, w_in_l[:, 3 * ATTN_WIDTH:], ssm_width)
        attn_outs = [_attention_pattern(*qkv[3 * p:3 * p + 3], rel_bias, DILATION_PATTERNS[p][1])
                     for p in range(n_pat)]
        ssm_slabs = _ssm_branch(u_slabs, bsz, a_re[l], a_im[l], log_dt[l], ssm_b_re[l],
                                ssm_b_im[l], ssm_c_re[l], ssm_c_im[l], d_skip[l], w_glu[l], b_glu[l])
        x1, h2, route, counts = _merge_and_route(
            x, gates, attn_outs, ssm_slabs, mod3, g_post_mix[l].reshape(1, d),
            g_pre_ffn[l].reshape(1, d), w_branch_attn[l].astype(BF16),
            w_branch_ssm[l].astype(BF16), w_out[l].astype(BF16),
            w_router_group[l], b_router_group[l], w_router_expert[l], b_router_expert[l])
        route_flat = route.reshape(bsz * seq, LANES)
        dest, blk_expert, n_used, seg_info, cap = _moe_layout(route_flat, counts[0, :N_EXPERTS])
        xs = _dispatch(h2.reshape(bsz * seq * ROW_TILE, LANES), dest, seg_info, cap)
        ys = _experts(xs, blk_expert, n_used, w1[l], w3[l], w2[l])
        x = _combine(ys, dest, x1.reshape(bsz * seq, d), route_flat, mod3,
                     g_post_ffn[l].reshape(1, d), seq).reshape(bsz, seq, d)
    return x
```

```python
import functools
import math

import numpy as np
import jax
import jax.numpy as jnp
from jax import lax
from jax.experimental import pallas as pl
from jax.experimental.pallas import tpu as pltpu

F32 = jnp.float32
BF16 = jnp.bfloat16

N_HEADS = 8
HEAD_DIM = 64
ATTN_WIDTH = N_HEADS * HEAD_DIM
DILATION_PATTERNS = ((128, 1), (512, 4), (2048, 16))
NUM_BUCKETS = 32
MAX_DISTANCE = 2048
N_EXPERT_GROUPS = 4
EXPERTS_PER_GROUP = 8
N_EXPERTS = N_EXPERT_GROUPS * EXPERTS_PER_GROUP
TOP_K = 2
RMS_EPS = 1e-6
NEG_INF = -1e30
LOG2_E = math.log2(math.e)

LANES = 128
SUBLANES = 8
VMEM_LIMIT_BYTES = 56 * 1024 * 1024

ATTN_BLK = 128
ATTN_STEP_ROWS = 2048
QKV_ROWS = 1024
QKV_MID_DIL = 4
TIME_TILE = 128
UGATE_CHUNK_BATCH = 2
MERGE_BATCH = 4
MERGE_CHUNK_BATCH = 2
SSM_STEPS = 128
MOE_ROWS = 1024
DISPATCH_ROWS = 2048
COMBINE_ROWS = 512
ROW_TILE = 4
U32 = jnp.uint32
HI_HALF = 0xFFFF0000


def _pack_rows(x):
    w = x.shape[1] // 2
    lo = lax.bitcast_convert_type(x[:, :w].astype(BF16).astype(F32), U32) >> 16
    hi = lax.bitcast_convert_type(x[:, w:].astype(BF16).astype(F32), U32) & U32(HI_HALF)
    return hi | lo


def _unpack_rows(p):
    lo = lax.bitcast_convert_type(p << 16, F32)
    hi = lax.bitcast_convert_type(p & U32(HI_HALF), F32)
    return jnp.concatenate([lo, hi], axis=1)


def _store_row_tiles(ref, packed):
    rows = packed.shape[0]
    for s in range(ROW_TILE):
        ref[pl.ds(s, rows, stride=ROW_TILE), :] = packed[:, s * LANES:(s + 1) * LANES]


def _load_row_tiles(ref, rows):
    return _lane_concat([ref[pl.ds(s, rows, stride=ROW_TILE), :] for s in range(ROW_TILE)])


def _cparams(*sem):
    return pltpu.CompilerParams(dimension_semantics=sem, vmem_limit_bytes=VMEM_LIMIT_BYTES)


def _sigmoid(x):
    return 1.0 / (1.0 + jnp.exp(-x))


def _dot(a, b):
    return jnp.dot(a, b, preferred_element_type=F32)


def _split_bf16(a):
    hi = a.astype(BF16)
    lo = (a - hi.astype(F32)).astype(BF16)
    return hi, lo


def _dot_split(a, w_hi, w_lo):
    a_hi, a_lo = _split_bf16(a)
    return _dot(a_hi, w_hi) + _dot(a_lo, w_hi) + _dot(a_hi, w_lo)


def _rms_norm(x, gain):
    ms = jnp.mean(x * x, axis=-1, keepdims=True)
    return x * lax.rsqrt(ms + RMS_EPS) * gain


def _lane_concat(ref_slabs):
    return jnp.concatenate(ref_slabs, axis=-1)


def _trace_staggered(chunks):
    pending, active = list(chunks), []
    while pending or active:
        if pending:
            active.append(pending.pop(0))
        for gen in list(active):
            if next(gen, StopIteration) is StopIteration:
                active.remove(gen)


def _mod_kernel(c_ref, w_ref, b_ref, o_ref):
    c = c_ref[...]
    a = c * _sigmoid(c)
    w_hi, w_lo = _split_bf16(w_ref[...])
    o_ref[...] = _dot_split(a, w_hi, w_lo) + b_ref[...]


def _modulation(c, w_mod, b_mod):
    bsz, d = c.shape
    n = w_mod.shape[1]
    tn = 1024
    return pl.pallas_call(
        _mod_kernel,
        out_shape=jax.ShapeDtypeStruct((bsz, n), F32),
        grid=(n // tn,),
        in_specs=[pl.BlockSpec((bsz, d), lambda j: (0, 0)),
                  pl.BlockSpec((d, tn), lambda j: (0, j)),
                  pl.BlockSpec((1, tn), lambda j: (0, j))],
        out_specs=pl.BlockSpec((bsz, tn), lambda j: (0, j)),
        compiler_params=_cparams("arbitrary"),
        name="mod",
    )(c, w_mod, b_mod.reshape(1, n))


def _qkv_kernel(x_ref, mod_ref, g_ref, w_ref, *rest):
    n_pat = len(DILATION_PATTERNS)
    out_refs, slab, mid = rest[:3 * n_pat], rest[3 * n_pat], rest[3 * n_pat + 1]
    h = _rms_norm(x_ref[0], g_ref[...]) * (1.0 + mod_ref[0, 1:2, :]) + mod_ref[0, 0:1, :]
    hb = h.astype(BF16)
    rows = hb.shape[0]
    per_tensor = ATTN_WIDTH // LANES
    mid_dil = QKV_MID_DIL

    def project(t):
        res = _dot(hb, w_ref[:, t * ATTN_WIDTH:(t + 1) * ATTN_WIDTH])
        if t == 0:
            res = res * (HEAD_DIM ** -0.5 * LOG2_E)
        for s in range(per_tensor):
            slab[t * per_tensor + s] = res[:, s * LANES:(s + 1) * LANES]

    def split(t):
        for p, (_, dil) in enumerate(DILATION_PATTERNS):
            sub = rows // dil
            out = out_refs[3 * p + t]
            for r in range(dil):
                pieces = []
                for s in range(per_tensor):
                    ts = t * per_tensor + s
                    if dil == 1:
                        piece = slab[ts]
                    elif dil == mid_dil:
                        piece = slab[ts, pl.ds(r, sub, stride=dil), :]
                        mid[ts, r] = piece
                    else:
                        ratio = dil // mid_dil
                        piece = mid[ts, r % mid_dil, pl.ds(r // mid_dil, sub, stride=ratio), :]
                    pieces.append(piece)
                out[0, r] = _lane_concat(pieces).astype(out.dtype)

    project(0)
    for t in range(3):
        if t + 1 < 3:
            project(t + 1)
        split(t)


def _qkv_projection(x, mod3, g_pre, w_qkv):
    bsz, seq, d = x.shape
    tm = QKV_ROWS
    out_shape, out_specs = [], []
    for _, dil in DILATION_PATTERNS:
        assert tm % (dil * 2 * SUBLANES) == 0
        for _ in range(3):
            out_shape.append(jax.ShapeDtypeStruct((bsz, dil, seq // dil, ATTN_WIDTH), BF16))
            out_specs.append(pl.BlockSpec((1, dil, tm // dil, ATTN_WIDTH),
                                          lambda b, i: (b, 0, i, 0)))
    return pl.pallas_call(
        _qkv_kernel,
        out_shape=out_shape,
        grid=(bsz, seq // tm),
        in_specs=[pl.BlockSpec((1, tm, d), lambda b, i: (b, i, 0)),
                  pl.BlockSpec((1, 6, d), lambda b, i: (b, 0, 0)),
                  pl.BlockSpec((1, d), lambda b, i: (0, 0)),
                  pl.BlockSpec(w_qkv.shape, lambda b, i: (0, 0))],
        out_specs=out_specs,
        scratch_shapes=[pltpu.VMEM((w_qkv.shape[1] // LANES, tm, LANES), F32),
                        pltpu.VMEM((w_qkv.shape[1] // LANES, QKV_MID_DIL, tm // QKV_MID_DIL, LANES),
                                   F32)],
        compiler_params=_cparams("arbitrary", "arbitrary"),
        name="qkv",
    )(x, mod3, g_pre, w_qkv)


def _ugate_kernel(x_ref, mod_ref, g_ref, w_ref, u_ref, gate_ref):
    bsz, tt, d = x_ref.shape
    n_slab = u_ref.shape[0]
    sw = n_slab * LANES
    gw = gate_ref.shape[-1]
    cb = UGATE_CHUNK_BATCH
    rows = cb * tt
    col_chunk = 512

    def normalise(b0):
        shift = mod_ref[b0:b0 + cb, 0, :][:, None, :]
        scale = mod_ref[b0:b0 + cb, 1, :][:, None, :]
        h = _rms_norm(x_ref[b0:b0 + cb], g_ref[...]) * (1.0 + scale) + shift
        return h.reshape(rows, d).astype(BF16)

    def project(b0, hb):
        u = _dot(hb, w_ref[:, 0:sw])
        for j in range(cb):
            for s in range(n_slab):
                u_ref[s, pl.ds(b0 + j, tt, stride=bsz), :] = (
                    u[j * tt:(j + 1) * tt, s * LANES:(s + 1) * LANES])
        for c0 in range(0, gw, col_chunk):
            g = _sigmoid(_dot(hb, w_ref[:, sw + c0:sw + c0 + col_chunk]))
            gate_ref[b0:b0 + cb, :, c0:c0 + col_chunk] = g.reshape(cb, tt, col_chunk).astype(BF16)

    starts = list(range(0, bsz, cb))
    hb = normalise(starts[0])
    for n, b0 in enumerate(starts):
        hb_next = normalise(starts[n + 1]) if n + 1 < len(starts) else None
        project(b0, hb)
        hb = hb_next


def _ugate_projection(x, mod3, g_pre, w_ug, ssm_width):
    bsz, seq, d = x.shape
    tt = TIME_TILE
    gw = w_ug.shape[1] - ssm_width
    n_slab = ssm_width // LANES
    return pl.pallas_call(
        _ugate_kernel,
        out_shape=(jax.ShapeDtypeStruct((n_slab, seq * bsz, LANES), F32),
                   jax.ShapeDtypeStruct((bsz, seq, gw), BF16)),
        grid=(seq // tt,),
        in_specs=[pl.BlockSpec((bsz, tt, d), lambda i: (0, i, 0)),
                  pl.BlockSpec((bsz, 6, d), lambda i: (0, 0, 0)),
                  pl.BlockSpec((1, d), lambda i: (0, 0)),
                  pl.BlockSpec(w_ug.shape, lambda i: (0, 0))],
        out_specs=(pl.BlockSpec((n_slab, tt * bsz, LANES), lambda i: (0, i, 0)),
                   pl.BlockSpec((bsz, tt, gw), lambda i: (0, i, 0))),
        compiler_params=_cparams("arbitrary"),
        name="ugate",
    )(x, mod3, g_pre, w_ug)


def _t5_bucket_np(dist):
    exact = NUM_BUCKETS // 2
    d_f = np.maximum(dist, exact).astype(np.float32)
    large = exact + (np.log(d_f / np.float32(exact)) / np.float32(math.log(MAX_DISTANCE / exact))
                     * np.float32(NUM_BUCKETS - exact)).astype(np.int32)
    return np.where(dist < exact, dist, np.minimum(large, NUM_BUCKETS - 1))


def _bucket_map_t(dil):
    blk = ATTN_BLK
    ki = np.arange(2 * blk)[:, None]
    qi = np.arange(blk)[None, :]
    return _t5_bucket_np(np.maximum(blk + qi - ki, 0) * dil).astype(np.int32)


def _attn_kernel(relb_ref, q_ref, kc_ref, kp_ref, vc_ref, vp_ref, bucket_ref,
                 o_ref, lse_ref, kbuf, vbuf, bias_t, *, n_sub):
    blk = ATTN_BLK
    first_call = jnp.logical_and(pl.program_id(0) == 0,
                                 jnp.logical_and(pl.program_id(1) == 0, pl.program_id(2) == 0))

    @pl.when(first_call)
    def _():
        bucket = bucket_ref[...]
        ki = lax.broadcasted_iota(jnp.int32, bucket.shape, 0)
        qi = lax.broadcasted_iota(jnp.int32, bucket.shape, 1)
        dist = blk + qi - ki
        band = jnp.logical_and(dist >= 0, dist <= blk)
        band_first = jnp.logical_and(band, ki >= blk)

        def per_head(h, carry):
            acc = jnp.zeros(bucket.shape, F32)
            for b in range(NUM_BUCKETS):
                acc = jnp.where(bucket == b, relb_ref[b, h] * LOG2_E, acc)
            bias_t[0, h] = jnp.where(band_first, acc, NEG_INF)
            bias_t[1, h] = jnp.where(band, acc, NEG_INF)
            return carry

        lax.fori_loop(0, N_HEADS, per_head, 0)

    first_variant = jnp.where(pl.program_id(2) == 0, 0, 1)
    n_res = q_ref.shape[1]
    for g in range(n_res):
        kbuf[g, 0:blk, :] = kp_ref[0, g]
        kbuf[g, blk:, :] = kc_ref[0, g]
        vbuf[g, 0:blk, :] = vp_ref[0, g]
        vbuf[g, blk:, :] = vc_ref[0, g]

    lane = lax.broadcasted_iota(jnp.int32, (1, LANES), 1)
    lo_half = lane < HEAD_DIM
    bd_row = lax.broadcasted_iota(jnp.int32, (4 * blk, LANES), 0)
    bd_col = lax.broadcasted_iota(jnp.int32, (4 * blk, LANES), 1)
    ones_bd = ((bd_row < 2 * blk) == (bd_col < HEAD_DIM)).astype(F32).astype(BF16)
    contract_last = (((1,), (1,)), ((), ()))
    contract_first = (((0,), (0,)), ((), ()))

    def sub_block(g, i):
        r0 = i * blk
        q = q_ref[0, g, r0:r0 + blk, :]
        kk = kbuf[g, r0:r0 + 2 * blk, :]
        vv = vbuf[g, r0:r0 + 2 * blk, :]
        variant = first_variant if i == 0 else 1
        for j in range(N_HEADS // 2):
            cols = slice(j * LANES, (j + 1) * LANES)
            qj, kj, vj = q[:, cols], kk[:, cols], vv[:, cols]
            probs_t, maxes = [], []
            for hh in range(2):
                sel = lo_half if hh == 0 else jnp.logical_not(lo_half)
                qm = jnp.where(sel, qj, jnp.zeros_like(qj))
                s_t = lax.dot_general(kj, qm, contract_last, preferred_element_type=F32)
                s_t = s_t + bias_t[variant, 2 * j + hh]
                m = jnp.max(s_t, axis=0, keepdims=True)
                probs_t.append(jnp.exp2(s_t - m).astype(BF16))
                maxes.append(m)
            p2_t = jnp.concatenate(probs_t, axis=0)
            v_bd = jnp.concatenate([jnp.where(lo_half, vj, jnp.zeros_like(vj)),
                                    jnp.where(lo_half, jnp.zeros_like(vj), vj)], axis=0)
            rhs = jnp.concatenate([v_bd, ones_bd], axis=1)
            ol = lax.dot_general(p2_t, rhs, contract_first, preferred_element_type=F32)
            o2, l2 = ol[:, :LANES], ol[:, LANES:]
            m_t = jnp.concatenate([jnp.broadcast_to(maxes[0], (HEAD_DIM, blk)),
                                   jnp.broadcast_to(maxes[1], (HEAD_DIM, blk))], axis=0)
            o_ref[0, g, r0:r0 + blk, cols] = (o2 / l2).astype(o_ref.dtype)
            lse_ref[0, g, r0:r0 + blk, cols] = m_t.T + jnp.log2(l2)

    for g in range(n_res):
        for i in range(n_sub):
            sub_block(g, i)


def _attention_pattern(q, k, v, rel_bias, dil):
    bsz, _, sub_len, aw = q.shape
    blk = ATTN_BLK
    assert sub_len % blk == 0
    tq = min(ATTN_STEP_ROWS, sub_len)
    n_sub = tq // blk
    ratio = tq // blk
    n_res = min(dil, ATTN_STEP_ROWS // tq)
    cur = lambda b, r, n: (b, r, n, 0)
    prev = lambda b, r, n: (b, r, jnp.maximum(n * ratio - 1, 0), 0)
    blk_cur = pl.BlockSpec((1, n_res, tq, aw), cur)
    blk_prev = pl.BlockSpec((1, n_res, blk, aw), prev)
    bucket = jnp.asarray(_bucket_map_t(dil))
    o_dtype = BF16 if TIME_TILE // dil >= 2 * SUBLANES else F32
    return pl.pallas_call(
        functools.partial(_attn_kernel, n_sub=n_sub),
        out_shape=(jax.ShapeDtypeStruct(q.shape, o_dtype), jax.ShapeDtypeStruct(q.shape, F32)),
        grid=(bsz, dil // n_res, sub_len // tq),
        in_specs=[pl.BlockSpec(memory_space=pltpu.SMEM),
                  blk_cur, blk_cur, blk_prev, blk_cur, blk_prev,
                  pl.BlockSpec(bucket.shape, lambda b, r, n: (0, 0))],
        out_specs=(blk_cur, blk_cur),
        scratch_shapes=[pltpu.VMEM((n_res, tq + blk, aw), BF16),
                        pltpu.VMEM((n_res, tq + blk, aw), BF16),
                        pltpu.VMEM((2, N_HEADS, 2 * blk, blk), F32)],
        compiler_params=_cparams("arbitrary", "arbitrary", "arbitrary"),
        name=f"attn_dil{dil}",
    )(rel_bias.astype(F32), q, k, k, v, v, bucket)


def _ssm_kernel(u_ref, bmat_ref, cmat_ref, ar_ref, ai_ref, dskip_ref, wglu_ref, bglu_ref,
                o_ref, hbuf, hstate, *, n_steps):
    @pl.when(pl.program_id(0) == 0)
    def _():
        hstate[...] = jnp.zeros_like(hstate)

    n_slab = u_ref.shape[0]
    n_state = hbuf.shape[1] // 2
    per = n_state // n_slab
    us = [u_ref[s] for s in range(n_slab)]
    ys = [None] * n_slab

    def drive(s):
        bu = _dot(us[s].astype(BF16), bmat_ref[s])
        hbuf[:, s * per:(s + 1) * per] = bu[:, :per]
        hbuf[:, n_state + s * per:n_state + (s + 1) * per] = bu[:, per:]

    def scan(s):
        re_cols = slice(s * per, (s + 1) * per)
        im_cols = slice(n_state + s * per, n_state + (s + 1) * per)
        ar = ar_ref[:, re_cols]
        ai = ai_ref[:, re_cols]
        hr = hstate[:, re_cols]
        hi = hstate[:, im_cols]
        for t in range(n_steps):
            trow = slice(t * SUBLANES, (t + 1) * SUBLANES)
            nr = ar * hr - ai * hi + hbuf[trow, re_cols]
            ni = ar * hi + ai * hr + hbuf[trow, im_cols]
            hbuf[trow, re_cols] = nr
            hbuf[trow, im_cols] = ni
            hr, hi = nr, ni
        hstate[:, re_cols] = hr
        hstate[:, im_cols] = hi

    def read_out(s):
        h_s = _lane_concat([hbuf[:, s * per:(s + 1) * per],
                            hbuf[:, n_state + s * per:n_state + (s + 1) * per]])
        ys[s] = (_dot(h_s.astype(BF16), cmat_ref[s])
                 + dskip_ref[:, s * LANES:(s + 1) * LANES] * us[s])

    for tick in range(n_slab + 2):
        if tick < n_slab:
            drive(tick)
        if 0 <= tick - 1 < n_slab:
            scan(tick - 1)
        if 0 <= tick - 2 < n_slab:
            read_out(tick - 2)
    y = _lane_concat(ys)
    y = 0.5 * y * (1.0 + jnp.tanh(math.sqrt(2.0 / math.pi) * (y + 0.044715 * (y * y * y))))
    z = _dot(y.astype(BF16), wglu_ref[...]) + bglu_ref[...]
    out = y * _sigmoid(z)
    for s in range(n_slab):
        o_ref[s] = out[:, s * LANES:(s + 1) * LANES]


def _ssm_params(a_re, a_im, log_dt, b_re, b_im, c_re, c_im, bsz):
    g, p = a_re.shape
    hg = b_re.shape[-1]
    dt = jnp.exp(log_dt.astype(F32))[:, None]
    a_re, a_im = a_re.astype(F32), a_im.astype(F32)
    mag = jnp.exp(a_re * dt)
    abar_re = mag * jnp.cos(a_im * dt)
    abar_im = mag * jnp.sin(a_im * dt)
    den = a_re * a_re + a_im * a_im
    q_re = ((abar_re - 1.0) * a_re + abar_im * a_im) / den
    q_im = (abar_im * a_re - (abar_re - 1.0) * a_im) / den
    b_re, b_im = b_re.astype(F32), b_im.astype(F32)
    bb_re = q_re[..., None] * b_re - q_im[..., None] * b_im
    bb_im = q_re[..., None] * b_im + q_im[..., None] * b_re
    gs = LANES // hg
    n_slab = g // gs
    eye = jnp.eye(gs, dtype=F32)

    def in_mat(t):
        t = t.reshape(n_slab, gs, p, hg)
        return jnp.einsum('sgph,gk->sghkp', t, eye).reshape(n_slab, gs * hg, gs * p)

    def out_mat(t):
        t = t.reshape(n_slab, gs, hg, p)
        return jnp.einsum('sghp,gk->sgpkh', t, eye).reshape(n_slab, gs * p, gs * hg)

    bmat = jnp.concatenate([in_mat(bb_re), in_mat(bb_im)], axis=2).astype(BF16)
    cmat = jnp.concatenate([out_mat(c_re.astype(F32)), -out_mat(c_im.astype(F32))],
                           axis=1).astype(BF16)
    ar = jnp.broadcast_to(abar_re.reshape(1, g * p), (bsz, g * p))
    ai = jnp.broadcast_to(abar_im.reshape(1, g * p), (bsz, g * p))
    return bmat, cmat, ar, ai


def _ssm_branch(u_slabs, bsz, a_re, a_im, log_dt, b_re, b_im, c_re, c_im, d_skip, w_glu, b_glu):
    n_slab, n_rows, _ = u_slabs.shape
    width = n_slab * LANES
    assert bsz == SUBLANES
    bmat, cmat, ar, ai = _ssm_params(a_re, a_im, log_dt, b_re, b_im, c_re, c_im, bsz)
    n_state2 = n_slab * bmat.shape[2]
    rows = SSM_STEPS * bsz
    const = lambda c: (0, 0)
    const3 = lambda c: (0, 0, 0)
    slab_spec = pl.BlockSpec((n_slab, rows, LANES), lambda c: (0, c, 0))
    return pl.pallas_call(
        functools.partial(_ssm_kernel, n_steps=SSM_STEPS),
        out_shape=jax.ShapeDtypeStruct(u_slabs.shape, F32),
        grid=(n_rows // rows,),
        in_specs=[slab_spec,
                  pl.BlockSpec(bmat.shape, const3), pl.BlockSpec(cmat.shape, const3),
                  pl.BlockSpec(ar.shape, const), pl.BlockSpec(ai.shape, const),
                  pl.BlockSpec((1, width), const), pl.BlockSpec((width, width), const),
                  pl.BlockSpec((1, width), const)],
        out_specs=slab_spec,
        scratch_shapes=[pltpu.VMEM((rows, n_state2), F32), pltpu.VMEM((bsz, n_state2), F32)],
        compiler_params=_cparams("arbitrary"),
        name="ssm",
    )(u_slabs, bmat, cmat, ar, ai, d_skip.reshape(1, width).astype(F32),
      w_glu.astype(BF16), b_glu.reshape(1, width).astype(F32))


ROUTE_LANES = {"id0": 0, "id1": 1, "rank0": 2, "rank1": 3, "w0": 4, "w1": 5}
GROUP_LANE0 = N_EXPERTS


def _merge_kernel(*refs, bsz_total):
    n_pat = len(DILATION_PATTERNS)
    x_ref, gate_ref = refs[0:2]
    attn_refs = refs[2:2 + 2 * n_pat]
    (ssm_ref, mod_ref, gpost_ref, gpre_ref, wba_ref, wbs_ref, wout_ref,
     wr_cat_ref, wr_hi_ref, br_ref) = refs[2 + 2 * n_pat:12 + 2 * n_pat]
    x1_ref, h2_ref, route_ref, count_ref = refs[12 + 2 * n_pat:16 + 2 * n_pat]
    scratch = refs[16 + 2 * n_pat:]
    carry = scratch[-1]

    @pl.when(jnp.logical_and(pl.program_id(0) == 0, pl.program_id(1) == 0))
    def _():
        carry[...] = jnp.zeros_like(carry)

    nb, tt, d = x_ref.shape
    n_slab = ssm_ref.shape[0]
    b0 = pl.program_id(0) * nb
    cb = MERGE_CHUNK_BATCH
    rows = cb * tt
    lane = lax.broadcasted_iota(jnp.int32, (rows, LANES), 1).astype(F32)
    row = lax.broadcasted_iota(jnp.int32, (rows, rows), 0)
    col = lax.broadcasted_iota(jnp.int32, (rows, rows), 1)
    strict_lower = (col < row).astype(BF16)
    running = [carry[...]]

    def chunk_phases(c0):
        bbs = range(c0, c0 + cb)
        o_tok, lse_tok, ssm_tok = scratch[3 * (c0 // cb):3 * (c0 // cb) + 3]

        for bb in bbs:
            for s in range(n_slab):
                ssm_tok[(bb - c0) * tt:(bb - c0 + 1) * tt, s * LANES:(s + 1) * LANES] = (
                    ssm_ref[s, pl.ds(b0 + bb, tt, stride=bsz_total), :])

        slot = 0
        sources = []
        for p, (_, dil) in enumerate(DILATION_PATTERNS):
            o_ref, lse_ref = attn_refs[2 * p], attn_refs[2 * p + 1]
            if dil == 1:
                sources.append((o_ref, lse_ref, None))
                continue
            sub = tt // dil
            for bb in bbs:
                for r in range(dil):
                    o_blk = o_ref[bb, r].astype(F32)
                    l_blk = lse_ref[bb, r]
                    for s in range(n_slab):
                        dst = pl.ds((bb - c0) * tt + r, sub, stride=dil)
                        o_tok[slot, s, dst, :] = o_blk[:, s * LANES:(s + 1) * LANES]
                        lse_tok[slot, s, dst, :] = l_blk[:, s * LANES:(s + 1) * LANES]
            sources.append((o_ref, lse_ref, slot))
            slot += 1
        attn_slabs = []
        for s in range(n_slab):
            cols = slice(s * LANES, (s + 1) * LANES)
            o_ps, lse_ps = [], []
            for o_ref, lse_ref, src_slot in sources:
                if src_slot is None:
                    o_ps.append(o_ref[c0:c0 + cb, 0, :, cols].astype(F32).reshape(rows, LANES))
                    lse_ps.append(lse_ref[c0:c0 + cb, 0, :, cols].reshape(rows, LANES))
                else:
                    o_ps.append(o_tok[src_slot, s])
                    lse_ps.append(lse_tok[src_slot, s])
            m = functools.reduce(jnp.maximum, lse_ps)
            es = [jnp.exp2(l - m) for l in lse_ps]
            num = functools.reduce(lambda a, b: a + b, [e * o for e, o in zip(es, o_ps)])
            den = functools.reduce(lambda a, b: a + b, es)
            attn_slabs.append(num / den)
        attn_b = _lane_concat(attn_slabs).astype(BF16)
        ssm_b = ssm_tok[...].astype(BF16)
        yield
        branch_attn = _dot(attn_b, wba_ref[...])
        branch_ssm = _dot(ssm_b, wbs_ref[...])
        yield
        g_attn = gate_ref[c0:c0 + cb, :, 0:d].reshape(rows, d)
        g_ssm = gate_ref[c0:c0 + cb, :, d:].reshape(rows, d)
        merged_b = g_attn * branch_attn.astype(BF16) + g_ssm * branch_ssm.astype(BF16)
        yield
        y = _dot(merged_b, wout_ref[...])
        yield
        gate1 = mod_ref[c0:c0 + cb, 2, :][:, None, :]
        shift2 = mod_ref[c0:c0 + cb, 3, :][:, None, :]
        scale2 = mod_ref[c0:c0 + cb, 4, :][:, None, :]
        x1 = x_ref[c0:c0 + cb] + gate1 * _rms_norm(y, gpost_ref[...]).reshape(cb, tt, d)
        x1_ref[c0:c0 + cb] = x1
        h2 = _rms_norm(x1, gpre_ref[...]) * (1.0 + scale2) + shift2
        for j, bb in enumerate(bbs):
            _store_row_tiles(h2_ref.at[bb], _pack_rows(h2[j]))

        a_hi, a_lo = _split_bf16(h2.reshape(rows, d))
        yield
        hi_pass = _dot(a_hi, wr_cat_ref[...])
        lo_pass = _dot(a_lo, wr_hi_ref[...])
        yield
        logits = hi_pass[:, :LANES] + lo_pass + hi_pass[:, LANES:] + br_ref[...]
        big = float(LANES)
        is_group = jnp.logical_and(lane >= GROUP_LANE0, lane < GROUP_LANE0 + N_EXPERT_GROUPS)
        gl = jnp.where(is_group, logits, -jnp.inf)
        g_max = jnp.max(gl, axis=-1, keepdims=True)
        g_sel = jnp.min(jnp.where(gl == g_max, lane, big), axis=-1, keepdims=True) - GROUP_LANE0
        g_gate = 1.0 / jnp.sum(jnp.exp(gl - g_max), axis=-1, keepdims=True)
        lo = g_sel * EXPERTS_PER_GROUP
        in_group = jnp.logical_and(lane >= lo, lane < lo + EXPERTS_PER_GROUP)
        el = jnp.where(in_group, logits, -jnp.inf)
        t0 = jnp.max(el, axis=-1, keepdims=True)
        i0 = jnp.min(jnp.where(el == t0, lane, big), axis=-1, keepdims=True)
        el1 = jnp.where(lane == i0, -jnp.inf, el)
        t1 = jnp.max(el1, axis=-1, keepdims=True)
        i1 = jnp.min(jnp.where(el1 == t1, lane, big), axis=-1, keepdims=True)
        e = jnp.exp(t1 - t0)
        w0 = g_gate / (1.0 + e)
        w1 = g_gate * e / (1.0 + e)

        hit0 = lane == i0
        hit1 = lane == i1
        onehot = jnp.logical_or(hit0, hit1).astype(F32)
        yield
        before = _dot(strict_lower, onehot.astype(BF16)) + running[0]
        rank0 = jnp.sum(jnp.where(hit0, before, 0.0), axis=-1, keepdims=True)
        rank1 = jnp.sum(jnp.where(hit1, before, 0.0), axis=-1, keepdims=True)
        running[0] = running[0] + jnp.sum(onehot, axis=0, keepdims=True)

        route = jnp.zeros((rows, LANES), F32)
        for name, val in (("id0", i0), ("id1", i1), ("rank0", rank0), ("rank1", rank1),
                          ("w0", w0), ("w1", w1)):
            route = jnp.where(lane == ROUTE_LANES[name], val, route)
        route_ref[c0:c0 + cb] = route.reshape(cb, tt, LANES)

    _trace_staggered([chunk_phases(c0) for c0 in range(0, nb, cb)])

    carry[...] = running[0]
    count_ref[...] = jnp.broadcast_to(running[0], count_ref.shape)


def _merge_and_route(x, gates, attn_outs, ssm_slabs, mod3, g_post, g_pre, wba, wbs, wout,
                     w_rg, b_rg, w_re, b_re):
    bsz, seq, d = x.shape
    tt, nb = TIME_TILE, MERGE_BATCH
    aw = ATTN_WIDTH
    n_slab = ssm_slabs.shape[0]
    assert aw == n_slab * LANES
    wr = jnp.zeros((d, LANES), F32).at[:, :N_EXPERTS].set(w_re.astype(F32))
    wr = wr.at[:, GROUP_LANE0:GROUP_LANE0 + N_EXPERT_GROUPS].set(w_rg.astype(F32))
    br = jnp.zeros((1, LANES), F32).at[0, :N_EXPERTS].set(b_re.astype(F32))
    br = br.at[0, GROUP_LANE0:GROUP_LANE0 + N_EXPERT_GROUPS].set(b_rg.astype(F32))
    wr_hi, wr_lo = _split_bf16(wr)
    wr_cat = jnp.concatenate([wr_hi, wr_lo], axis=1)
    n_strided = sum(1 for _, dil in DILATION_PATTERNS if dil > 1)
    chunk_rows = MERGE_CHUNK_BATCH * tt
    tok = lambda h, i: (h, i, 0)
    const = lambda h, i: (0, 0)
    attn_args, attn_specs = [], []
    for (o_p, lse_p), (_, dil) in zip(attn_outs, DILATION_PATTERNS):
        spec = pl.BlockSpec((nb, dil, tt // dil, aw), lambda h, i: (h, 0, i, 0))
        attn_args += [o_p, lse_p]
        attn_specs += [spec, spec]
    return pl.pallas_call(
        functools.partial(_merge_kernel, bsz_total=bsz),
        out_shape=(jax.ShapeDtypeStruct((bsz, seq, d), F32),
                   jax.ShapeDtypeStruct((bsz, seq * ROW_TILE, LANES), U32),
                   jax.ShapeDtypeStruct((bsz, seq, LANES), F32),
                   jax.ShapeDtypeStruct((SUBLANES, LANES), F32)),
        grid=(bsz // nb, seq // tt),
        in_specs=[pl.BlockSpec((nb, tt, d), tok),
                  pl.BlockSpec((nb, tt, gates.shape[-1]), tok)]
                 + attn_specs
                 + [pl.BlockSpec((n_slab, tt * bsz, LANES), lambda h, i: (0, i, 0)),
                    pl.BlockSpec((nb, 6, d), lambda h, i: (h, 0, 0)),
                    pl.BlockSpec((1, d), const), pl.BlockSpec((1, d), const),
                    pl.BlockSpec(wba.shape, const), pl.BlockSpec(wbs.shape, const),
                    pl.BlockSpec(wout.shape, const),
                    pl.BlockSpec((d, 2 * LANES), const), pl.BlockSpec((d, LANES), const),
                    pl.BlockSpec((1, LANES), const)],
        out_specs=(pl.BlockSpec((nb, tt, d), tok), pl.BlockSpec((nb, tt * ROW_TILE, LANES), tok),
                   pl.BlockSpec((nb, tt, LANES), tok),
                   pl.BlockSpec((SUBLANES, LANES), const)),
        scratch_shapes=[pltpu.VMEM((n_strided, n_slab, chunk_rows, LANES), F32),
                        pltpu.VMEM((n_strided, n_slab, chunk_rows, LANES), F32),
                        pltpu.VMEM((chunk_rows, n_slab * LANES), F32)] * (nb // MERGE_CHUNK_BATCH)
                       + [pltpu.VMEM((1, LANES), F32)],
        compiler_params=_cparams("arbitrary", "arbitrary"),
        name="merge",
    )(x, gates, *attn_args, ssm_slabs, mod3, g_post, g_pre, wba, wbs, wout, wr_cat, wr_hi, br)


ISSUE_UNROLL = 8


def _dispatch_kernel(dest_ref, seg_ref, h_ref, xs_ref, zero_buf, sem, zsem):
    rows = h_ref.shape[0] // ROW_TILE
    blk = zero_buf.shape[0]
    n_blocks = xs_ref.shape[0] // blk
    base = pl.program_id(0) * rows

    @pl.when(pl.program_id(0) == 0)
    def _():
        zero_buf[...] = jnp.zeros_like(zero_buf)

        def zero_copy(row0):
            return pltpu.make_async_copy(zero_buf, xs_ref.at[pl.ds(pl.multiple_of(row0, blk), blk)],
                                         zsem)

        def fill_tail(e, carry):
            @pl.when(seg_ref[N_EXPERTS + e] > 0)
            def _():
                zero_copy(seg_ref[e] * ROW_TILE - blk).start()
            return carry

        def fill_unused(j, carry):
            zero_copy(j * blk).start()
            return carry

        def wait_tail(e, carry):
            @pl.when(seg_ref[N_EXPERTS + e] > 0)
            def _():
                zero_copy(0).wait()
            return carry

        def wait_unused(j, carry):
            zero_copy(0).wait()
            return carry

        n_used = seg_ref[2 * N_EXPERTS]
        lax.fori_loop(0, N_EXPERTS, fill_tail, 0)
        lax.fori_loop(n_used, n_blocks, fill_unused, 0)
        lax.fori_loop(0, N_EXPERTS, wait_tail, 0)
        lax.fori_loop(n_used, n_blocks, wait_unused, 0)

    group = ISSUE_UNROLL * ROW_TILE

    def issue(g, carry):
        g0 = pl.multiple_of(g * group, group)
        for rr in range(ISSUE_UNROLL):
            src = h_ref.at[pl.ds(g0 + rr * ROW_TILE, ROW_TILE)]
            for k in range(TOP_K):
                d = dest_ref[TOP_K * (base + g * ISSUE_UNROLL + rr) + k]
                dst = xs_ref.at[pl.ds(pl.multiple_of(d * ROW_TILE, ROW_TILE), ROW_TILE)]
                pltpu.make_async_copy(src, dst, sem).start(priority=k)
        return carry

    lax.fori_loop(0, rows // ISSUE_UNROLL, issue, 0)
    for _ in range(TOP_K):
        pltpu.make_async_copy(h_ref, xs_ref.at[pl.ds(0, rows * ROW_TILE)], sem).wait()


def _dispatch(h2_tiles, dest, seg_info, cap):
    n_tok = h2_tiles.shape[0] // ROW_TILE
    rows = DISPATCH_ROWS
    return pl.pallas_call(
        _dispatch_kernel,
        out_shape=jax.ShapeDtypeStruct((cap * ROW_TILE, LANES), h2_tiles.dtype),
        grid_spec=pltpu.PrefetchScalarGridSpec(
            num_scalar_prefetch=2,
            grid=(n_tok // rows,),
            in_specs=[pl.BlockSpec((rows * ROW_TILE, LANES), lambda i, dest, seg: (i, 0))],
            out_specs=pl.BlockSpec(memory_space=pl.ANY),
            scratch_shapes=[pltpu.VMEM((MOE_ROWS * ROW_TILE, LANES), h2_tiles.dtype),
                            pltpu.SemaphoreType.DMA, pltpu.SemaphoreType.DMA]),
        compiler_params=_cparams("arbitrary"),
        name="dispatch",
    )(dest, seg_info, h2_tiles)


def _expert_kernel(blk_expert_ref, seg_slot_ref, next_expert_ref, n_used_ref, x_ref,
                   w1_hbm, w3_hbm, w2_hbm, y_ref, w1f, w3f, w2f, w1b, w3b, w2b, sems):
    i = pl.program_id(0)
    used = i < n_used_ref[0]
    rows = x_ref.shape[0] // ROW_TILE
    expert = blk_expert_ref[i]
    new_expert = jnp.logical_or(i == 0, expert != blk_expert_ref[jnp.maximum(i - 1, 0)])
    slot = seg_slot_ref[i]

    def weight_copies(e, s):
        return [pltpu.make_async_copy(hbm.at[e], stage.at[s], sems.at[s, n])
                for n, (hbm, stage) in enumerate(((w1_hbm, w1f), (w3_hbm, w3f), (w2_hbm, w2f)))]

    @pl.when(jnp.logical_and(used, new_expert))
    def _():
        @pl.when(i == 0)
        def _():
            for cp in weight_copies(expert, slot):
                cp.start()

        for cp in weight_copies(expert, slot):
            cp.wait()
        w1b[...] = w1f[slot].astype(BF16)
        w3b[...] = w3f[slot].astype(BF16)
        w2b[...] = w2f[slot].astype(BF16)
        nxt = next_expert_ref[i]

        @pl.when(nxt >= 0)
        def _():
            for cp in weight_copies(nxt, 1 - slot):
                cp.start()

    @pl.when(used)
    def _():
        xb = _unpack_rows(_load_row_tiles(x_ref, rows)).astype(BF16)
        h1 = _dot(xb, w1b[...])
        h3 = _dot(xb, w3b[...])
        act = (h1 * _sigmoid(h1)) * h3
        y = _dot(act.astype(BF16), w2b[...])
        _store_row_tiles(y_ref, _pack_rows(y))

    @pl.when(jnp.logical_not(used))
    def _():
        y_ref[...] = jnp.zeros_like(y_ref)


def _experts(xs, blk_expert, n_used, w1, w3, w2):
    d, de = w1.shape[-2:]
    assert d == 2 * ROW_TILE * LANES
    rows = MOE_ROWS
    tile_rows = rows * ROW_TILE
    n_blocks = xs.shape[0] // tile_rows
    blk = jnp.arange(n_blocks, dtype=jnp.int32)
    changed = jnp.concatenate([jnp.zeros((1,), jnp.int32),
                               (blk_expert[1:] != blk_expert[:-1]).astype(jnp.int32)])
    seg_slot = (jnp.cumsum(changed) % 2).astype(jnp.int32)
    later = jnp.logical_and(blk[None, :] > blk[:, None],
                            jnp.logical_and(blk_expert[None, :] != blk_expert[:, None],
                                            blk[None, :] < n_used[0]))
    first_later = jnp.argmax(later, axis=1)
    next_expert = jnp.where(jnp.any(later, axis=1), blk_expert[first_later], -1).astype(jnp.int32)
    xblk = lambda i, be, ss, ne, nu: (jnp.minimum(i, nu[0] - 1), 0)
    return pl.pallas_call(
        _expert_kernel,
        out_shape=jax.ShapeDtypeStruct(xs.shape, U32),
        grid_spec=pltpu.PrefetchScalarGridSpec(
            num_scalar_prefetch=4,
            grid=(n_blocks,),
            in_specs=[pl.BlockSpec((tile_rows, LANES), xblk),
                      pl.BlockSpec(memory_space=pl.ANY), pl.BlockSpec(memory_space=pl.ANY),
                      pl.BlockSpec(memory_space=pl.ANY)],
            out_specs=pl.BlockSpec((tile_rows, LANES), lambda i, be, ss, ne, nu: (i, 0)),
            scratch_shapes=[pltpu.VMEM((2, d, de), w1.dtype), pltpu.VMEM((2, d, de), w3.dtype),
                            pltpu.VMEM((2, de, d), w2.dtype),
                            pltpu.VMEM((d, de), BF16), pltpu.VMEM((d, de), BF16),
                            pltpu.VMEM((de, d), BF16),
                            pltpu.SemaphoreType.DMA((2, 3))]),
        compiler_params=_cparams("arbitrary"),
        name="experts",
    )(blk_expert, seg_slot, next_expert, n_used, xs, w1, w3, w2)


def _combine_kernel(dest_ref, ys_ref, x1_ref, route_ref, mod_ref, g_ref, o_ref, ybuf, sem):
    rows = x1_ref.shape[0]
    step = pl.program_id(0)
    n_step = pl.num_programs(0)

    group = ISSUE_UNROLL * ROW_TILE

    def gather(tile, slot):
        base = tile * rows

        def issue(g, carry):
            g0 = pl.multiple_of(g * group, group)
            for rr in range(ISSUE_UNROLL):
                for k in range(TOP_K):
                    d = dest_ref[TOP_K * (base + g * ISSUE_UNROLL + rr) + k]
                    src = ys_ref.at[pl.ds(pl.multiple_of(d * ROW_TILE, ROW_TILE), ROW_TILE)]
                    dst = ybuf.at[slot, k, pl.ds(g0 + rr * ROW_TILE, ROW_TILE)]
                    pltpu.make_async_copy(src, dst, sem.at[slot]).start(priority=k)
            return carry

        lax.fori_loop(0, rows // ISSUE_UNROLL, issue, 0)

    slot = lax.rem(step, 2)

    @pl.when(step == 0)
    def _():
        gather(step, 0)

    @pl.when(step + 1 < n_step)
    def _():
        gather(step + 1, 1 - slot)

    for k in range(TOP_K):
        pltpu.make_async_copy(ys_ref.at[pl.ds(0, rows * ROW_TILE)], ybuf.at[slot, k],
                              sem.at[slot]).wait()
    route = route_ref[...]
    w0 = route[:, ROUTE_LANES["w0"]:ROUTE_LANES["w0"] + 1]
    w1 = route[:, ROUTE_LANES["w1"]:ROUTE_LANES["w1"] + 1]

    def rows_of(k):
        return _unpack_rows(_load_row_tiles(ybuf.at[slot, k], rows))

    y = rows_of(0) * w0 + rows_of(1) * w1
    o_ref[...] = x1_ref[...] + mod_ref[0, 5:6, :] * _rms_norm(y, g_ref[...])


def _combine(ys, dest, x1_flat, route_flat, mod3, g_post, seq):
    n_tok, d = x1_flat.shape
    rows = COMBINE_ROWS
    assert seq % rows == 0
    return pl.pallas_call(
        _combine_kernel,
        out_shape=jax.ShapeDtypeStruct((n_tok, d), F32),
        grid_spec=pltpu.PrefetchScalarGridSpec(
            num_scalar_prefetch=1,
            grid=(n_tok // rows,),
            in_specs=[pl.BlockSpec(memory_space=pl.ANY),
                      pl.BlockSpec((rows, d), lambda i, dest: (i, 0)),
                      pl.BlockSpec((rows, LANES), lambda i, dest: (i, 0)),
                      pl.BlockSpec((1, 6, d), lambda i, dest: (i * rows // seq, 0, 0)),
                      pl.BlockSpec((1, d), lambda i, dest: (0, 0))],
            out_specs=pl.BlockSpec((rows, d), lambda i, dest: (i, 0)),
            scratch_shapes=[pltpu.VMEM((2, TOP_K, rows * ROW_TILE, LANES), U32),
                            pltpu.SemaphoreType.DMA((2,))]),
        compiler_params=_cparams("arbitrary"),
        name="combine",
    )(dest, ys, x1_flat, route_flat, mod3, g_post)


def _moe_layout(route_flat, counts):
    rows = MOE_ROWS
    n_tok = route_flat.shape[0]
    ids = route_flat[:, 0:TOP_K].astype(jnp.int32)
    rank = route_flat[:, TOP_K:2 * TOP_K].astype(jnp.int32)
    counts = counts.astype(jnp.int32)
    padded = (counts + rows - 1) // rows * rows
    pad_ends = jnp.cumsum(padded)
    pad_starts = pad_ends - padded
    expert = jnp.arange(N_EXPERTS, dtype=jnp.int32)
    start_of = jnp.sum(jnp.where(ids[..., None] == expert, pad_starts, 0), axis=-1)
    dest = (start_of + rank).reshape(n_tok * TOP_K)
    n_blocks = (n_tok * TOP_K + N_EXPERTS * (rows - 1) + rows - 1) // rows
    blk_row0 = jnp.arange(n_blocks, dtype=jnp.int32) * rows
    blk_expert = jnp.minimum(jnp.sum(pad_ends[None, :] <= blk_row0[:, None], axis=-1),
                             N_EXPERTS - 1).astype(jnp.int32)
    n_used = (pad_ends[-1] // rows).reshape(1).astype(jnp.int32)
    seg_info = jnp.concatenate([pad_ends, padded, n_used]).astype(jnp.int32)
    return dest, blk_expert, n_used, seg_info, n_blocks * rows


def kernel(x, c, w_mod, b_mod, g_pre_mix, g_post_mix, g_pre_ffn, g_post_ffn, w_in, rel_bias, a_re, a_im, log_dt, ssm_b_re, ssm_b_im, ssm_c_re, ssm_c_im, d_skip, w_glu, b_glu, w_branch_attn, w_branch_ssm, w_out, w_router_group, b_router_group, w_router_expert, b_router_expert, w1, w3, w2):
    bsz, seq, d = x.shape
    depth = w_mod.shape[0]
    ssm_width = w_glu.shape[-1]
    n_pat = len(DILATION_PATTERNS)
    for l in range(depth):
        mod3 = _modulation(c, w_mod[l], b_mod[l]).reshape(bsz, 6, d)
        w_in_l = w_in[l].astype(BF16)
        g_pre = g_pre_mix[l].reshape(1, d)
        qkv = _qkv_projection(x, mod3, g_pre, w_in_l[:, :3 * ATTN_WIDTH])
        u_slabs, gates = _ugate_projection(x, mod3, g_pre, w_in_l[:, 3 * ATTN_WIDTH:], ssm_width)
        attn_outs = [_attention_pattern(*qkv[3 * p:3 * p + 3], rel_bias, DILATION_PATTERNS[p][1])
                     for p in range(n_pat)]
        ssm_slabs = _ssm_branch(u_slabs, bsz, a_re[l], a_im[l], log_dt[l], ssm_b_re[l],
                                ssm_b_im[l], ssm_c_re[l], ssm_c_im[l], d_skip[l], w_glu[l], b_glu[l])
        x1, h2, route, counts = _merge_and_route(
            x, gates, attn_outs, ssm_slabs, mod3, g_post_mix[l].reshape(1, d),
            g_pre_ffn[l].reshape(1, d), w_branch_attn[l].astype(BF16),
            w_branch_ssm[l].astype(BF16), w_out[l].astype(BF16),
            w_router_group[l], b_router_group[l], w_router_expert[l], b_router_expert[l])
        route_flat = route.reshape(bsz * seq, LANES)
        dest, blk_expert, n_used, seg_info, cap = _moe_layout(route_flat, counts[0, :N_EXPERTS])
        xs = _dispatch(h2.reshape(bsz * seq * ROW_TILE, LANES), dest, seg_info, cap)
        ys = _experts(xs, blk_expert, n_used, w1[l], w3[l], w2[l])
        x = _combine(ys, dest, x1.reshape(bsz * seq, d), route_flat, mod3,
                     g_post_ffn[l].reshape(1, d), seq).reshape(bsz, seq, d)
    return x
```

```python
import functools
import math

import numpy as np
import jax
import jax.numpy as jnp
from jax import lax
from jax.experimental import pallas as pl
from jax.experimental.pallas import tpu as pltpu

F32 = jnp.float32
BF16 = jnp.bfloat16

N_HEADS = 8
HEAD_DIM = 64
ATTN_WIDTH = N_HEADS * HEAD_DIM
DILATION_PATTERNS = ((128, 1), (512, 4), (2048, 16))
NUM_BUCKETS = 32
MAX_DISTANCE = 2048
N_EXPERT_GROUPS = 4
EXPERTS_PER_GROUP = 8
N_EXPERTS = N_EXPERT_GROUPS * EXPERTS_PER_GROUP
TOP_K = 2
RMS_EPS = 1e-6
NEG_INF = -1e30
LOG2_E = math.log2(math.e)

LANES = 128
SUBLANES = 8
VMEM_LIMIT_BYTES = 56 * 1024 * 1024

ATTN_BLK = 128
ATTN_STEP_ROWS = 2048
QKV_ROWS = 1024
QKV_MID_DIL = 4
TIME_TILE = 128
UGATE_CHUNK_BATCH = 2
MERGE_BATCH = 4
MERGE_CHUNK_BATCH = 2
SSM_STEPS = 128
MOE_ROWS = 512
DISPATCH_ROWS = 2048
COMBINE_ROWS = 512
ROW_TILE = 4
U32 = jnp.uint32
HI_HALF = 0xFFFF0000


def _pack_rows(x):
    w = x.shape[1] // 2
    lo = lax.bitcast_convert_type(x[:, :w].astype(BF16).astype(F32), U32) >> 16
    hi = lax.bitcast_convert_type(x[:, w:].astype(BF16).astype(F32), U32) & U32(HI_HALF)
    return hi | lo


def _unpack_rows(p):
    lo = lax.bitcast_convert_type(p << 16, F32)
    hi = lax.bitcast_convert_type(p & U32(HI_HALF), F32)
    return jnp.concatenate([lo, hi], axis=1)


def _store_row_tiles(ref, packed):
    rows = packed.shape[0]
    for s in range(ROW_TILE):
        ref[pl.ds(s, rows, stride=ROW_TILE), :] = packed[:, s * LANES:(s + 1) * LANES]


def _load_row_tiles(ref, rows):
    return _lane_concat([ref[pl.ds(s, rows, stride=ROW_TILE), :] for s in range(ROW_TILE)])


def _cparams(*sem):
    return pltpu.CompilerParams(dimension_semantics=sem, vmem_limit_bytes=VMEM_LIMIT_BYTES)


def _sigmoid(x):
    return 1.0 / (1.0 + jnp.exp(-x))


def _dot(a, b):
    return jnp.dot(a, b, preferred_element_type=F32)


def _split_bf16(a):
    hi = a.astype(BF16)
    lo = (a - hi.astype(F32)).astype(BF16)
    return hi, lo


def _dot_split(a, w_hi, w_lo):
    a_hi, a_lo = _split_bf16(a)
    return _dot(a_hi, w_hi) + _dot(a_lo, w_hi) + _dot(a_hi, w_lo)


def _rms_unit(x):
    ms = jnp.mean(x * x, axis=-1, keepdims=True)
    return x * lax.rsqrt(ms + RMS_EPS)


def _lane_concat(ref_slabs):
    return jnp.concatenate(ref_slabs, axis=-1)


def _trace_staggered(chunks):
    pending, active = list(chunks), []
    while pending or active:
        if pending:
            active.append(pending.pop(0))
        for gen in list(active):
            if next(gen, StopIteration) is StopIteration:
                active.remove(gen)


def _mod_kernel(c_ref, w_ref, b_ref, o_ref):
    c = c_ref[...]
    a = c * _sigmoid(c)
    w_hi, w_lo = _split_bf16(w_ref[...])
    o_ref[...] = _dot_split(a, w_hi, w_lo) + b_ref[...]


def _modulation(c, w_mod, b_mod):
    bsz, d = c.shape
    n = w_mod.shape[1]
    tn = 1024
    return pl.pallas_call(
        _mod_kernel,
        out_shape=jax.ShapeDtypeStruct((bsz, n), F32),
        grid=(n // tn,),
        in_specs=[pl.BlockSpec((bsz, d), lambda j: (0, 0)),
                  pl.BlockSpec((d, tn), lambda j: (0, j)),
                  pl.BlockSpec((1, tn), lambda j: (0, j))],
        out_specs=pl.BlockSpec((bsz, tn), lambda j: (0, j)),
        compiler_params=_cparams("arbitrary"),
        name="mod",
    )(c, w_mod, b_mod.reshape(1, n))


def _qkv_kernel(x_ref, mod_ref, g_ref, w_ref, *rest):
    n_pat = len(DILATION_PATTERNS)
    out_refs, slab, mid = rest[:3 * n_pat], rest[3 * n_pat], rest[3 * n_pat + 1]
    h = _rms_unit(x_ref[0]) * (g_ref[...] * (1.0 + mod_ref[0, 1:2, :])) + mod_ref[0, 0:1, :]
    hb = h.astype(BF16)
    rows = hb.shape[0]
    per_tensor = ATTN_WIDTH // LANES
    mid_dil = QKV_MID_DIL

    def project(t):
        res = _dot(hb, w_ref[:, t * ATTN_WIDTH:(t + 1) * ATTN_WIDTH])
        if t == 0:
            res = res * (HEAD_DIM ** -0.5 * LOG2_E)
        for s in range(per_tensor):
            slab[t * per_tensor + s] = res[:, s * LANES:(s + 1) * LANES]

    def split(t):
        for p, (_, dil) in enumerate(DILATION_PATTERNS):
            sub = rows // dil
            out = out_refs[3 * p + t]
            for r in range(dil):
                pieces = []
                for s in range(per_tensor):
                    ts = t * per_tensor + s
                    if dil == 1:
                        piece = slab[ts]
                    elif dil == mid_dil:
                        piece = slab[ts, pl.ds(r, sub, stride=dil), :]
                        mid[ts, r] = piece
                    else:
                        ratio = dil // mid_dil
                        piece = mid[ts, r % mid_dil, pl.ds(r // mid_dil, sub, stride=ratio), :]
                    pieces.append(piece)
                out[0, r] = _lane_concat(pieces).astype(out.dtype)

    project(0)
    for t in range(3):
        if t + 1 < 3:
            project(t + 1)
        split(t)


def _qkv_projection(x, mod3, g_pre, w_qkv):
    bsz, seq, d = x.shape
    tm = QKV_ROWS
    out_shape, out_specs = [], []
    for _, dil in DILATION_PATTERNS:
        assert tm % (dil * 2 * SUBLANES) == 0
        for _ in range(3):
            out_shape.append(jax.ShapeDtypeStruct((bsz, dil, seq // dil, ATTN_WIDTH), BF16))
            out_specs.append(pl.BlockSpec((1, dil, tm // dil, ATTN_WIDTH),
                                          lambda b, i: (b, 0, i, 0)))
    return pl.pallas_call(
        _qkv_kernel,
        out_shape=out_shape,
        grid=(bsz, seq // tm),
        in_specs=[pl.BlockSpec((1, tm, d), lambda b, i: (b, i, 0)),
                  pl.BlockSpec((1, 6, d), lambda b, i: (b, 0, 0)),
                  pl.BlockSpec((1, d), lambda b, i: (0, 0)),
                  pl.BlockSpec(w_qkv.shape, lambda b, i: (0, 0))],
        out_specs=out_specs,
        scratch_shapes=[pltpu.VMEM((w_qkv.shape[1] // LANES, tm, LANES), F32),
                        pltpu.VMEM((w_qkv.shape[1] // LANES, QKV_MID_DIL, tm // QKV_MID_DIL, LANES),
                                   F32)],
        compiler_params=_cparams("arbitrary", "arbitrary"),
        name="qkv",
    )(x, mod3, g_pre, w_qkv)


def _ugate_kernel(x_ref, mod_ref, g_ref, w_ref, u_ref, gate_ref):
    bsz, tt, d = x_ref.shape
    n_slab = u_ref.shape[0]
    sw = n_slab * LANES
    gw = gate_ref.shape[-1]
    cb = UGATE_CHUNK_BATCH
    rows = cb * tt
    col_chunk = 512

    def normalise(b0):
        shift = mod_ref[b0:b0 + cb, 0, :][:, None, :]
        scale = mod_ref[b0:b0 + cb, 1, :][:, None, :]
        h = _rms_unit(x_ref[b0:b0 + cb]) * (g_ref[...] * (1.0 + scale)) + shift
        return h.reshape(rows, d).astype(BF16)

    def project(b0, hb):
        u = _dot(hb, w_ref[:, 0:sw])
        for j in range(cb):
            for s in range(n_slab):
                u_ref[s, pl.ds(b0 + j, tt, stride=bsz), :] = (
                    u[j * tt:(j + 1) * tt, s * LANES:(s + 1) * LANES])
        for c0 in range(0, gw, col_chunk):
            g = _sigmoid(_dot(hb, w_ref[:, sw + c0:sw + c0 + col_chunk]))
            gate_ref[b0:b0 + cb, :, c0:c0 + col_chunk] = g.reshape(cb, tt, col_chunk).astype(BF16)

    starts = list(range(0, bsz, cb))
    hb = normalise(starts[0])
    for n, b0 in enumerate(starts):
        hb_next = normalise(starts[n + 1]) if n + 1 < len(starts) else None
        project(b0, hb)
        hb = hb_next


def _ugate_projection(x, mod3, g_pre, w_ug, ssm_width):
    bsz, seq, d = x.shape
    tt = TIME_TILE
    gw = w_ug.shape[1] - ssm_width
    n_slab = ssm_width // LANES
    return pl.pallas_call(
        _ugate_kernel,
        out_shape=(jax.ShapeDtypeStruct((n_slab, seq * bsz, LANES), F32),
                   jax.ShapeDtypeStruct((bsz, seq, gw), BF16)),
        grid=(seq // tt,),
        in_specs=[pl.BlockSpec((bsz, tt, d), lambda i: (0, i, 0)),
                  pl.BlockSpec((bsz, 6, d), lambda i: (0, 0, 0)),
                  pl.BlockSpec((1, d), lambda i: (0, 0)),
                  pl.BlockSpec(w_ug.shape, lambda i: (0, 0))],
        out_specs=(pl.BlockSpec((n_slab, tt * bsz, LANES), lambda i: (0, i, 0)),
                   pl.BlockSpec((bsz, tt, gw), lambda i: (0, i, 0))),
        compiler_params=_cparams("arbitrary"),
        name="ugate",
    )(x, mod3, g_pre, w_ug)


def _t5_bucket_np(dist):
    exact = NUM_BUCKETS // 2
    d_f = np.maximum(dist, exact).astype(np.float32)
    large = exact + (np.log(d_f / np.float32(exact)) / np.float32(math.log(MAX_DISTANCE / exact))
                     * np.float32(NUM_BUCKETS - exact)).astype(np.int32)
    return np.where(dist < exact, dist, np.minimum(large, NUM_BUCKETS - 1))


def _bucket_map_t(dil):
    blk = ATTN_BLK
    ki = np.arange(2 * blk)[:, None]
    qi = np.arange(blk)[None, :]
    return _t5_bucket_np(np.maximum(blk + qi - ki, 0) * dil).astype(np.int32)


def _attn_kernel(relb_ref, q_ref, kc_ref, kp_ref, vc_ref, vp_ref, bucket_ref,
                 o_ref, lse_ref, kbuf, vbuf, bias_t, *, n_sub):
    blk = ATTN_BLK
    first_call = jnp.logical_and(pl.program_id(0) == 0,
                                 jnp.logical_and(pl.program_id(1) == 0, pl.program_id(2) == 0))

    @pl.when(first_call)
    def _():
        bucket = bucket_ref[...]
        ki = lax.broadcasted_iota(jnp.int32, bucket.shape, 0)
        qi = lax.broadcasted_iota(jnp.int32, bucket.shape, 1)
        dist = blk + qi - ki
        band = jnp.logical_and(dist >= 0, dist <= blk)
        band_first = jnp.logical_and(band, ki >= blk)

        def per_head(h, carry):
            acc = jnp.zeros(bucket.shape, F32)
            for b in range(NUM_BUCKETS):
                acc = jnp.where(bucket == b, relb_ref[b, h] * LOG2_E, acc)
            bias_t[0, h] = jnp.where(band_first, acc, NEG_INF)
            bias_t[1, h] = jnp.where(band, acc, NEG_INF)
            return carry

        lax.fori_loop(0, N_HEADS, per_head, 0)

    first_variant = jnp.where(pl.program_id(2) == 0, 0, 1)
    n_res = q_ref.shape[1]
    for g in range(n_res):
        kbuf[g, 0:blk, :] = kp_ref[0, g]
        kbuf[g, blk:, :] = kc_ref[0, g]
        vbuf[g, 0:blk, :] = vp_ref[0, g]
        vbuf[g, blk:, :] = vc_ref[0, g]

    lane = lax.broadcasted_iota(jnp.int32, (1, LANES), 1)
    lo_half = lane < HEAD_DIM
    bd_row = lax.broadcasted_iota(jnp.int32, (4 * blk, LANES), 0)
    bd_col = lax.broadcasted_iota(jnp.int32, (4 * blk, LANES), 1)
    ones_bd = ((bd_row < 2 * blk) == (bd_col < HEAD_DIM)).astype(F32).astype(BF16)
    contract_last = (((1,), (1,)), ((), ()))
    contract_first = (((0,), (0,)), ((), ()))

    def sub_block(g, i):
        r0 = i * blk
        q = q_ref[0, g, r0:r0 + blk, :]
        kk = kbuf[g, r0:r0 + 2 * blk, :]
        vv = vbuf[g, r0:r0 + 2 * blk, :]
        variant = first_variant if i == 0 else 1
        for j in range(N_HEADS // 2):
            cols = slice(j * LANES, (j + 1) * LANES)
            qj, kj, vj = q[:, cols], kk[:, cols], vv[:, cols]
            probs_t, maxes = [], []
            for hh in range(2):
                sel = lo_half if hh == 0 else jnp.logical_not(lo_half)
                qm = jnp.where(sel, qj, jnp.zeros_like(qj))
                s_t = lax.dot_general(kj, qm, contract_last, preferred_element_type=F32)
                s_t = s_t + bias_t[variant, 2 * j + hh]
                m = jnp.max(s_t, axis=0, keepdims=True)
                probs_t.append(jnp.exp2(s_t - m).astype(BF16))
                maxes.append(m)
            p2_t = jnp.concatenate(probs_t, axis=0)
            v_bd = jnp.concatenate([jnp.where(lo_half, vj, jnp.zeros_like(vj)),
                                    jnp.where(lo_half, jnp.zeros_like(vj), vj)], axis=0)
            rhs = jnp.concatenate([v_bd, ones_bd], axis=1)
            ol = lax.dot_general(p2_t, rhs, contract_first, preferred_element_type=F32)
            o2, l2 = ol[:, :LANES], ol[:, LANES:]
            m_t = jnp.concatenate([jnp.broadcast_to(maxes[0], (HEAD_DIM, blk)),
                                   jnp.broadcast_to(maxes[1], (HEAD_DIM, blk))], axis=0)
            o_ref[0, g, r0:r0 + blk, cols] = (o2 / l2).astype(o_ref.dtype)
            lse_ref[0, g, r0:r0 + blk, cols] = m_t.T + jnp.log2(l2)

    for g in range(n_res):
        for i in range(n_sub):
            sub_block(g, i)


def _attention_pattern(q, k, v, rel_bias, dil):
    bsz, _, sub_len, aw = q.shape
    blk = ATTN_BLK
    assert sub_len % blk == 0
    tq = min(ATTN_STEP_ROWS, sub_len)
    n_sub = tq // blk
    ratio = tq // blk
    n_res = min(dil, ATTN_STEP_ROWS // tq)
    cur = lambda b, r, n: (b, r, n, 0)
    prev = lambda b, r, n: (b, r, jnp.maximum(n * ratio - 1, 0), 0)
    blk_cur = pl.BlockSpec((1, n_res, tq, aw), cur)
    blk_prev = pl.BlockSpec((1, n_res, blk, aw), prev)
    bucket = jnp.asarray(_bucket_map_t(dil))
    o_dtype = BF16 if TIME_TILE // dil >= 2 * SUBLANES else F32
    return pl.pallas_call(
        functools.partial(_attn_kernel, n_sub=n_sub),
        out_shape=(jax.ShapeDtypeStruct(q.shape, o_dtype), jax.ShapeDtypeStruct(q.shape, F32)),
        grid=(bsz, dil // n_res, sub_len // tq),
        in_specs=[pl.BlockSpec(memory_space=pltpu.SMEM),
                  blk_cur, blk_cur, blk_prev, blk_cur, blk_prev,
                  pl.BlockSpec(bucket.shape, lambda b, r, n: (0, 0))],
        out_specs=(blk_cur, blk_cur),
        scratch_shapes=[pltpu.VMEM((n_res, tq + blk, aw), BF16),
                        pltpu.VMEM((n_res, tq + blk, aw), BF16),
                        pltpu.VMEM((2, N_HEADS, 2 * blk, blk), F32)],
        compiler_params=_cparams("arbitrary", "arbitrary", "arbitrary"),
        name=f"attn_dil{dil}",
    )(rel_bias.astype(F32), q, k, k, v, v, bucket)


def _ssm_kernel(u_ref, bmat_ref, cmat_ref, ar_ref, ai_ref, dskip_ref, wglu_ref, bglu_ref,
                o_ref, hbuf, hstate, *, n_steps):
    @pl.when(pl.program_id(0) == 0)
    def _():
        hstate[...] = jnp.zeros_like(hstate)

    n_slab = u_ref.shape[0]
    n_state = hbuf.shape[1] // 2
    per = n_state // n_slab
    us = [u_ref[s] for s in range(n_slab)]
    ys = [None] * n_slab

    def drive(s):
        bu = _dot(us[s].astype(BF16), bmat_ref[s])
        hbuf[:, s * per:(s + 1) * per] = bu[:, :per]
        hbuf[:, n_state + s * per:n_state + (s + 1) * per] = bu[:, per:]

    def scan(s):
        re_cols = slice(s * per, (s + 1) * per)
        im_cols = slice(n_state + s * per, n_state + (s + 1) * per)
        ar = ar_ref[:, re_cols]
        ai = ai_ref[:, re_cols]
        hr = hstate[:, re_cols]
        hi = hstate[:, im_cols]
        for t in range(n_steps):
            trow = slice(t * SUBLANES, (t + 1) * SUBLANES)
            nr = ar * hr - ai * hi + hbuf[trow, re_cols]
            ni = ar * hi + ai * hr + hbuf[trow, im_cols]
            hbuf[trow, re_cols] = nr
            hbuf[trow, im_cols] = ni
            hr, hi = nr, ni
        hstate[:, re_cols] = hr
        hstate[:, im_cols] = hi

    def read_out(s):
        h_s = _lane_concat([hbuf[:, s * per:(s + 1) * per],
                            hbuf[:, n_state + s * per:n_state + (s + 1) * per]])
        ys[s] = (_dot(h_s.astype(BF16), cmat_ref[s])
                 + dskip_ref[:, s * LANES:(s + 1) * LANES] * us[s])

    for tick in range(n_slab + 2):
        if tick < n_slab:
            drive(tick)
        if 0 <= tick - 1 < n_slab:
            scan(tick - 1)
        if 0 <= tick - 2 < n_slab:
            read_out(tick - 2)
    y = _lane_concat(ys)
    y = 0.5 * y * (1.0 + jnp.tanh(math.sqrt(2.0 / math.pi) * (y + 0.044715 * (y * y * y))))
    z = _dot(y.astype(BF16), wglu_ref[...]) + bglu_ref[...]
    out = y * _sigmoid(z)
    for s in range(n_slab):
        o_ref[s] = out[:, s * LANES:(s + 1) * LANES]


def _ssm_params(a_re, a_im, log_dt, b_re, b_im, c_re, c_im, bsz):
    g, p = a_re.shape
    hg = b_re.shape[-1]
    dt = jnp.exp(log_dt.astype(F32))[:, None]
    a_re, a_im = a_re.astype(F32), a_im.astype(F32)
    mag = jnp.exp(a_re * dt)
    abar_re = mag * jnp.cos(a_im * dt)
    abar_im = mag * jnp.sin(a_im * dt)
    den = a_re * a_re + a_im * a_im
    q_re = ((abar_re - 1.0) * a_re + abar_im * a_im) / den
    q_im = (abar_im * a_re - (abar_re - 1.0) * a_im) / den
    b_re, b_im = b_re.astype(F32), b_im.astype(F32)
    bb_re = q_re[..., None] * b_re - q_im[..., None] * b_im
    bb_im = q_re[..., None] * b_im + q_im[..., None] * b_re
    gs = LANES // hg
    n_slab = g // gs
    eye = jnp.eye(gs, dtype=F32)

    def in_mat(t):
        t = t.reshape(n_slab, gs, p, hg)
        return jnp.einsum('sgph,gk->sghkp', t, eye).reshape(n_slab, gs * hg, gs * p)

    def out_mat(t):
        t = t.reshape(n_slab, gs, hg, p)
        return jnp.einsum('sghp,gk->sgpkh', t, eye).reshape(n_slab, gs * p, gs * hg)

    bmat = jnp.concatenate([in_mat(bb_re), in_mat(bb_im)], axis=2).astype(BF16)
    cmat = jnp.concatenate([out_mat(c_re.astype(F32)), -out_mat(c_im.astype(F32))],
                           axis=1).astype(BF16)
    ar = jnp.broadcast_to(abar_re.reshape(1, g * p), (bsz, g * p))
    ai = jnp.broadcast_to(abar_im.reshape(1, g * p), (bsz, g * p))
    return bmat, cmat, ar, ai


def _ssm_branch(u_slabs, bsz, a_re, a_im, log_dt, b_re, b_im, c_re, c_im, d_skip, w_glu, b_glu):
    n_slab, n_rows, _ = u_slabs.shape
    width = n_slab * LANES
    assert bsz == SUBLANES
    bmat, cmat, ar, ai = _ssm_params(a_re, a_im, log_dt, b_re, b_im, c_re, c_im, bsz)
    n_state2 = n_slab * bmat.shape[2]
    rows = SSM_STEPS * bsz
    const = lambda c: (0, 0)
    const3 = lambda c: (0, 0, 0)
    slab_spec = pl.BlockSpec((n_slab, rows, LANES), lambda c: (0, c, 0))
    return pl.pallas_call(
        functools.partial(_ssm_kernel, n_steps=SSM_STEPS),
        out_shape=jax.ShapeDtypeStruct(u_slabs.shape, F32),
        grid=(n_rows // rows,),
        in_specs=[slab_spec,
                  pl.BlockSpec(bmat.shape, const3), pl.BlockSpec(cmat.shape, const3),
                  pl.BlockSpec(ar.shape, const), pl.BlockSpec(ai.shape, const),
                  pl.BlockSpec((1, width), const), pl.BlockSpec((width, width), const),
                  pl.BlockSpec((1, width), const)],
        out_specs=slab_spec,
        scratch_shapes=[pltpu.VMEM((rows, n_state2), F32), pltpu.VMEM((bsz, n_state2), F32)],
        compiler_params=_cparams("arbitrary"),
        name="ssm",
    )(u_slabs, bmat, cmat, ar, ai, d_skip.reshape(1, width).astype(F32),
      w_glu.astype(BF16), b_glu.reshape(1, width).astype(F32))


ROUTE_LANES = {"id0": 0, "id1": 1, "rank0": 2, "rank1": 3, "w0": 4, "w1": 5}
GROUP_LANE0 = N_EXPERTS


def _merge_kernel(*refs, bsz_total):
    n_pat = len(DILATION_PATTERNS)
    x_ref, gate_ref = refs[0:2]
    attn_refs = refs[2:2 + 2 * n_pat]
    (ssm_ref, mod_ref, gpost_ref, gpre_ref, wba_ref, wbs_ref, wout_ref,
     wr_cat_ref, wr_hi_ref, br_ref) = refs[2 + 2 * n_pat:12 + 2 * n_pat]
    x1_ref, h2_ref, route_ref, count_ref = refs[12 + 2 * n_pat:16 + 2 * n_pat]
    scratch = refs[16 + 2 * n_pat:]
    carry = scratch[-1]

    @pl.when(jnp.logical_and(pl.program_id(0) == 0, pl.program_id(1) == 0))
    def _():
        carry[...] = jnp.zeros_like(carry)

    nb, tt, d = x_ref.shape
    n_slab = ssm_ref.shape[0]
    b0 = pl.program_id(0) * nb
    cb = MERGE_CHUNK_BATCH
    rows = cb * tt
    lane = lax.broadcasted_iota(jnp.int32, (rows, LANES), 1).astype(F32)
    row = lax.broadcasted_iota(jnp.int32, (rows, rows), 0)
    col = lax.broadcasted_iota(jnp.int32, (rows, rows), 1)
    strict_lower = (col < row).astype(BF16)
    running = [carry[...]]

    def chunk_phases(c0):
        bbs = range(c0, c0 + cb)
        o_tok, lse_tok, ssm_tok = scratch[3 * (c0 // cb):3 * (c0 // cb) + 3]

        for bb in bbs:
            for s in range(n_slab):
                ssm_tok[(bb - c0) * tt:(bb - c0 + 1) * tt, s * LANES:(s + 1) * LANES] = (
                    ssm_ref[s, pl.ds(b0 + bb, tt, stride=bsz_total), :])

        slot = 0
        sources = []
        for p, (_, dil) in enumerate(DILATION_PATTERNS):
            o_ref, lse_ref = attn_refs[2 * p], attn_refs[2 * p + 1]
            if dil == 1:
                sources.append((o_ref, lse_ref, None))
                continue
            sub = tt // dil
            for bb in bbs:
                for r in range(dil):
                    o_blk = o_ref[bb, r].astype(F32)
                    l_blk = lse_ref[bb, r]
                    for s in range(n_slab):
                        dst = pl.ds((bb - c0) * tt + r, sub, stride=dil)
                        o_tok[slot, s, dst, :] = o_blk[:, s * LANES:(s + 1) * LANES]
                        lse_tok[slot, s, dst, :] = l_blk[:, s * LANES:(s + 1) * LANES]
            sources.append((o_ref, lse_ref, slot))
            slot += 1
        attn_slabs = []
        for s in range(n_slab):
            cols = slice(s * LANES, (s + 1) * LANES)
            o_ps, lse_ps = [], []
            for o_ref, lse_ref, src_slot in sources:
                if src_slot is None:
                    o_ps.append(o_ref[c0:c0 + cb, 0, :, cols].astype(F32).reshape(rows, LANES))
                    lse_ps.append(lse_ref[c0:c0 + cb, 0, :, cols].reshape(rows, LANES))
                else:
                    o_ps.append(o_tok[src_slot, s])
                    lse_ps.append(lse_tok[src_slot, s])
            m = functools.reduce(jnp.maximum, lse_ps)
            es = [jnp.exp2(l - m) for l in lse_ps]
            num = functools.reduce(lambda a, b: a + b, [e * o for e, o in zip(es, o_ps)])
            den = functools.reduce(lambda a, b: a + b, es)
            attn_slabs.append(num / den)
        attn_b = _lane_concat(attn_slabs).astype(BF16)
        ssm_b = ssm_tok[...].astype(BF16)
        yield
        branch_attn = _dot(attn_b, wba_ref[...])
        branch_ssm = _dot(ssm_b, wbs_ref[...])
        yield
        g_attn = gate_ref[c0:c0 + cb, :, 0:d].reshape(rows, d)
        g_ssm = gate_ref[c0:c0 + cb, :, d:].reshape(rows, d)
        merged_b = g_attn * branch_attn.astype(BF16) + g_ssm * branch_ssm.astype(BF16)
        yield
        y = _dot(merged_b, wout_ref[...])
        yield
        gate1 = mod_ref[c0:c0 + cb, 2, :][:, None, :]
        shift2 = mod_ref[c0:c0 + cb, 3, :][:, None, :]
        scale2 = mod_ref[c0:c0 + cb, 4, :][:, None, :]
        x1 = x_ref[c0:c0 + cb] + (gate1 * gpost_ref[...]) * _rms_unit(y).reshape(cb, tt, d)
        x1_ref[c0:c0 + cb] = x1
        h2 = _rms_unit(x1) * (gpre_ref[...] * (1.0 + scale2)) + shift2
        for j, bb in enumerate(bbs):
            _store_row_tiles(h2_ref.at[bb], _pack_rows(h2[j]))

        a_hi, a_lo = _split_bf16(h2.reshape(rows, d))
        yield
        hi_pass = _dot(a_hi, wr_cat_ref[...])
        lo_pass = _dot(a_lo, wr_hi_ref[...])
        yield
        logits = hi_pass[:, :LANES] + lo_pass + hi_pass[:, LANES:] + br_ref[...]
        big = float(LANES)
        is_group = jnp.logical_and(lane >= GROUP_LANE0, lane < GROUP_LANE0 + N_EXPERT_GROUPS)
        gl = jnp.where(is_group, logits, -jnp.inf)
        g_max = jnp.max(gl, axis=-1, keepdims=True)
        g_sel = jnp.min(jnp.where(gl == g_max, lane, big), axis=-1, keepdims=True) - GROUP_LANE0
        g_gate = 1.0 / jnp.sum(jnp.exp(gl - g_max), axis=-1, keepdims=True)
        lo = g_sel * EXPERTS_PER_GROUP
        in_group = jnp.logical_and(lane >= lo, lane < lo + EXPERTS_PER_GROUP)
        el = jnp.where(in_group, logits, -jnp.inf)
        t0 = jnp.max(el, axis=-1, keepdims=True)
        i0 = jnp.min(jnp.where(el == t0, lane, big), axis=-1, keepdims=True)
        el1 = jnp.where(lane == i0, -jnp.inf, el)
        t1 = jnp.max(el1, axis=-1, keepdims=True)
        i1 = jnp.min(jnp.where(el1 == t1, lane, big), axis=-1, keepdims=True)
        e = jnp.exp(t1 - t0)
        w0 = g_gate / (1.0 + e)
        w1 = g_gate * e / (1.0 + e)

        hit0 = lane == i0
        hit1 = lane == i1
        onehot = jnp.logical_or(hit0, hit1).astype(F32)
        yield
        before = _dot(strict_lower, onehot.astype(BF16)) + running[0]
        rank0 = jnp.sum(jnp.where(hit0, before, 0.0), axis=-1, keepdims=True)
        rank1 = jnp.sum(jnp.where(hit1, before, 0.0), axis=-1, keepdims=True)
        running[0] = running[0] + jnp.sum(onehot, axis=0, keepdims=True)

        route = jnp.zeros((rows, LANES), F32)
        for name, val in (("id0", i0), ("id1", i1), ("rank0", rank0), ("rank1", rank1),
                          ("w0", w0), ("w1", w1)):
            route = jnp.where(lane == ROUTE_LANES[name], val, route)
        route_ref[c0:c0 + cb] = route.reshape(cb, tt, LANES)

    _trace_staggered([chunk_phases(c0) for c0 in range(0, nb, cb)])

    carry[...] = running[0]
    count_ref[...] = jnp.broadcast_to(running[0], count_ref.shape)


def _merge_and_route(x, gates, attn_outs, ssm_slabs, mod3, g_post, g_pre, wba, wbs, wout,
                     w_rg, b_rg, w_re, b_re):
    bsz, seq, d = x.shape
    tt, nb = TIME_TILE, MERGE_BATCH
    aw = ATTN_WIDTH
    n_slab = ssm_slabs.shape[0]
    assert aw == n_slab * LANES
    wr = jnp.zeros((d, LANES), F32).at[:, :N_EXPERTS].set(w_re.astype(F32))
    wr = wr.at[:, GROUP_LANE0:GROUP_LANE0 + N_EXPERT_GROUPS].set(w_rg.astype(F32))
    br = jnp.zeros((1, LANES), F32).at[0, :N_EXPERTS].set(b_re.astype(F32))
    br = br.at[0, GROUP_LANE0:GROUP_LANE0 + N_EXPERT_GROUPS].set(b_rg.astype(F32))
    wr_hi, wr_lo = _split_bf16(wr)
    wr_cat = jnp.concatenate([wr_hi, wr_lo], axis=1)
    n_strided = sum(1 for _, dil in DILATION_PATTERNS if dil > 1)
    chunk_rows = MERGE_CHUNK_BATCH * tt
    tok = lambda h, i: (h, i, 0)
    const = lambda h, i: (0, 0)
    attn_args, attn_specs = [], []
    for (o_p, lse_p), (_, dil) in zip(attn_outs, DILATION_PATTERNS):
        spec = pl.BlockSpec((nb, dil, tt // dil, aw), lambda h, i: (h, 0, i, 0))
        attn_args += [o_p, lse_p]
        attn_specs += [spec, spec]
    return pl.pallas_call(
        functools.partial(_merge_kernel, bsz_total=bsz),
        out_shape=(jax.ShapeDtypeStruct((bsz, seq, d), F32),
                   jax.ShapeDtypeStruct((bsz, seq * ROW_TILE, LANES), U32),
                   jax.ShapeDtypeStruct((bsz, seq, LANES), F32),
                   jax.ShapeDtypeStruct((SUBLANES, LANES), F32)),
        grid=(bsz // nb, seq // tt),
        in_specs=[pl.BlockSpec((nb, tt, d), tok),
                  pl.BlockSpec((nb, tt, gates.shape[-1]), tok)]
                 + attn_specs
                 + [pl.BlockSpec((n_slab, tt * bsz, LANES), lambda h, i: (0, i, 0)),
                    pl.BlockSpec((nb, 6, d), lambda h, i: (h, 0, 0)),
                    pl.BlockSpec((1, d), const), pl.BlockSpec((1, d), const),
                    pl.BlockSpec(wba.shape, const), pl.BlockSpec(wbs.shape, const),
                    pl.BlockSpec(wout.shape, const),
                    pl.BlockSpec((d, 2 * LANES), const), pl.BlockSpec((d, LANES), const),
                    pl.BlockSpec((1, LANES), const)],
        out_specs=(pl.BlockSpec((nb, tt, d), tok), pl.BlockSpec((nb, tt * ROW_TILE, LANES), tok),
                   pl.BlockSpec((nb, tt, LANES), tok),
                   pl.BlockSpec((SUBLANES, LANES), const)),
        scratch_shapes=[pltpu.VMEM((n_strided, n_slab, chunk_rows, LANES), F32),
                        pltpu.VMEM((n_strided, n_slab, chunk_rows, LANES), F32),
                        pltpu.VMEM((chunk_rows, n_slab * LANES), F32)] * (nb // MERGE_CHUNK_BATCH)
                       + [pltpu.VMEM((1, LANES), F32)],
        compiler_params=_cparams("arbitrary", "arbitrary"),
        name="merge",
    )(x, gates, *attn_args, ssm_slabs, mod3, g_post, g_pre, wba, wbs, wout, wr_cat, wr_hi, br)


ISSUE_UNROLL = 8


def _dispatch_kernel(dest_ref, seg_ref, h_ref, xs_ref, zero_buf, sem, zsem):
    rows = h_ref.shape[0] // ROW_TILE
    blk = zero_buf.shape[0]
    n_blocks = xs_ref.shape[0] // blk
    base = pl.program_id(0) * rows

    @pl.when(pl.program_id(0) == 0)
    def _():
        zero_buf[...] = jnp.zeros_like(zero_buf)

        def zero_copy(row0):
            return pltpu.make_async_copy(zero_buf, xs_ref.at[pl.ds(pl.multiple_of(row0, blk), blk)],
                                         zsem)

        def fill_tail(e, carry):
            @pl.when(seg_ref[N_EXPERTS + e] > 0)
            def _():
                zero_copy(seg_ref[e] * ROW_TILE - blk).start()
            return carry

        def fill_unused(j, carry):
            zero_copy(j * blk).start()
            return carry

        def wait_tail(e, carry):
            @pl.when(seg_ref[N_EXPERTS + e] > 0)
            def _():
                zero_copy(0).wait()
            return carry

        def wait_unused(j, carry):
            zero_copy(0).wait()
            return carry

        n_used = seg_ref[2 * N_EXPERTS]
        lax.fori_loop(0, N_EXPERTS, fill_tail, 0)
        lax.fori_loop(n_used, n_blocks, fill_unused, 0)
        lax.fori_loop(0, N_EXPERTS, wait_tail, 0)
        lax.fori_loop(n_used, n_blocks, wait_unused, 0)

    group = ISSUE_UNROLL * ROW_TILE

    def issue(g, carry):
        g0 = pl.multiple_of(g * group, group)
        for rr in range(ISSUE_UNROLL):
            src = h_ref.at[pl.ds(g0 + rr * ROW_TILE, ROW_TILE)]
            for k in range(TOP_K):
                d = dest_ref[TOP_K * (base + g * ISSUE_UNROLL + rr) + k]
                dst = xs_ref.at[pl.ds(pl.multiple_of(d * ROW_TILE, ROW_TILE), ROW_TILE)]
                pltpu.make_async_copy(src, dst, sem).start(priority=k)
        return carry

    lax.fori_loop(0, rows // ISSUE_UNROLL, issue, 0)
    for _ in range(TOP_K):
        pltpu.make_async_copy(h_ref, xs_ref.at[pl.ds(0, rows * ROW_TILE)], sem).wait()


def _dispatch(h2_tiles, dest, seg_info, cap):
    n_tok = h2_tiles.shape[0] // ROW_TILE
    rows = DISPATCH_ROWS
    return pl.pallas_call(
        _dispatch_kernel,
        out_shape=jax.ShapeDtypeStruct((cap * ROW_TILE, LANES), h2_tiles.dtype),
        grid_spec=pltpu.PrefetchScalarGridSpec(
            num_scalar_prefetch=2,
            grid=(n_tok // rows,),
            in_specs=[pl.BlockSpec((rows * ROW_TILE, LANES), lambda i, dest, seg: (i, 0))],
            out_specs=pl.BlockSpec(memory_space=pl.ANY),
            scratch_shapes=[pltpu.VMEM((MOE_ROWS * ROW_TILE, LANES), h2_tiles.dtype),
                            pltpu.SemaphoreType.DMA, pltpu.SemaphoreType.DMA]),
        compiler_params=_cparams("arbitrary"),
        name="dispatch",
    )(dest, seg_info, h2_tiles)


def _expert_kernel(blk_expert_ref, seg_slot_ref, next_expert_ref, n_used_ref, x_ref,
                   w1_hbm, w3_hbm, w2_hbm, y_ref, w1f, w3f, w2f, w1b, w3b, w2b, sems):
    i = pl.program_id(0)
    used = i < n_used_ref[0]
    rows = x_ref.shape[0] // ROW_TILE
    expert = blk_expert_ref[i]
    new_expert = jnp.logical_or(i == 0, expert != blk_expert_ref[jnp.maximum(i - 1, 0)])
    slot = seg_slot_ref[i]

    def weight_copies(e, s):
        return [pltpu.make_async_copy(hbm.at[e], stage.at[s], sems.at[s, n])
                for n, (hbm, stage) in enumerate(((w1_hbm, w1f), (w3_hbm, w3f), (w2_hbm, w2f)))]

    @pl.when(jnp.logical_and(used, new_expert))
    def _():
        @pl.when(i == 0)
        def _():
            for cp in weight_copies(expert, slot):
                cp.start()

        for cp in weight_copies(expert, slot):
            cp.wait()
        w1b[...] = w1f[slot].astype(BF16)
        w3b[...] = w3f[slot].astype(BF16)
        w2b[...] = w2f[slot].astype(BF16)
        nxt = next_expert_ref[i]

        @pl.when(nxt >= 0)
        def _():
            for cp in weight_copies(nxt, 1 - slot):
                cp.start()

    @pl.when(used)
    def _():
        xb = _unpack_rows(_load_row_tiles(x_ref, rows)).astype(BF16)
        h1 = _dot(xb, w1b[...])
        h3 = _dot(xb, w3b[...])
        act = (h1 * _sigmoid(h1)) * h3
        y = _dot(act.astype(BF16), w2b[...])
        _store_row_tiles(y_ref, _pack_rows(y))

    @pl.when(jnp.logical_not(used))
    def _():
        y_ref[...] = jnp.zeros_like(y_ref)


def _experts(xs, blk_expert, n_used, w1, w3, w2):
    d, de = w1.shape[-2:]
    assert d == 2 * ROW_TILE * LANES
    rows = MOE_ROWS
    tile_rows = rows * ROW_TILE
    n_blocks = xs.shape[0] // tile_rows
    blk = jnp.arange(n_blocks, dtype=jnp.int32)
    changed = jnp.concatenate([jnp.zeros((1,), jnp.int32),
                               (blk_expert[1:] != blk_expert[:-1]).astype(jnp.int32)])
    seg_slot = (jnp.cumsum(changed) % 2).astype(jnp.int32)
    later = jnp.logical_and(blk[None, :] > blk[:, None],
                            jnp.logical_and(blk_expert[None, :] != blk_expert[:, None],
                                            blk[None, :] < n_used[0]))
    first_later = jnp.argmax(later, axis=1)
    next_expert = jnp.where(jnp.any(later, axis=1), blk_expert[first_later], -1).astype(jnp.int32)
    xblk = lambda i, be, ss, ne, nu: (jnp.minimum(i, nu[0] - 1), 0)
    return pl.pallas_call(
        _expert_kernel,
        out_shape=jax.ShapeDtypeStruct(xs.shape, U32),
        grid_spec=pltpu.PrefetchScalarGridSpec(
            num_scalar_prefetch=4,
            grid=(n_blocks,),
            in_specs=[pl.BlockSpec((tile_rows, LANES), xblk),
                      pl.BlockSpec(memory_space=pl.ANY), pl.BlockSpec(memory_space=pl.ANY),
                      pl.BlockSpec(memory_space=pl.ANY)],
            out_specs=pl.BlockSpec((tile_rows, LANES), lambda i, be, ss, ne, nu: (i, 0)),
            scratch_shapes=[pltpu.VMEM((2, d, de), w1.dtype), pltpu.VMEM((2, d, de), w3.dtype),
                            pltpu.VMEM((2, de, d), w2.dtype),
                            pltpu.VMEM((d, de), BF16), pltpu.VMEM((d, de), BF16),
                            pltpu.VMEM((de, d), BF16),
                            pltpu.SemaphoreType.DMA((2, 3))]),
        compiler_params=_cparams("arbitrary"),
        name="experts",
    )(blk_expert, seg_slot, next_expert, n_used, xs, w1, w3, w2)


def _combine_kernel(dest_ref, ys_ref, x1_ref, route_ref, mod_ref, g_ref, o_ref, ybuf, sem):
    rows = x1_ref.shape[0]
    step = pl.program_id(0)
    n_step = pl.num_programs(0)

    group = ISSUE_UNROLL * ROW_TILE

    def gather(tile, slot):
        base = tile * rows

        def issue(g, carry):
            g0 = pl.multiple_of(g * group, group)
            for rr in range(ISSUE_UNROLL):
                for k in range(TOP_K):
                    d = dest_ref[TOP_K * (base + g * ISSUE_UNROLL + rr) + k]
                    src = ys_ref.at[pl.ds(pl.multiple_of(d * ROW_TILE, ROW_TILE), ROW_TILE)]
                    dst = ybuf.at[slot, k, pl.ds(g0 + rr * ROW_TILE, ROW_TILE)]
                    pltpu.make_async_copy(src, dst, sem.at[slot]).start(priority=k)
            return carry

        lax.fori_loop(0, rows // ISSUE_UNROLL, issue, 0)

    slot = lax.rem(step, 2)

    @pl.when(step == 0)
    def _():
        gather(step, 0)

    @pl.when(step + 1 < n_step)
    def _():
        gather(step + 1, 1 - slot)

    for k in range(TOP_K):
        pltpu.make_async_copy(ys_ref.at[pl.ds(0, rows * ROW_TILE)], ybuf.at[slot, k],
                              sem.at[slot]).wait()
    route = route_ref[...]
    w0 = route[:, ROUTE_LANES["w0"]:ROUTE_LANES["w0"] + 1]
    w1 = route[:, ROUTE_LANES["w1"]:ROUTE_LANES["w1"] + 1]

    def rows_of(k):
        return _unpack_rows(_load_row_tiles(ybuf.at[slot, k], rows))

    y = rows_of(0) * w0 + rows_of(1) * w1
    o_ref[...] = x1_ref[...] + (mod_ref[0, 5:6, :] * g_ref[...]) * _rms_unit(y)


def _combine(ys, dest, x1_flat, route_flat, mod3, g_post, seq):
    n_tok, d = x1_flat.shape
    rows = COMBINE_ROWS
    assert seq % rows == 0
    return pl.pallas_call(
        _combine_kernel,
        out_shape=jax.ShapeDtypeStruct((n_tok, d), F32),
        grid_spec=pltpu.PrefetchScalarGridSpec(
            num_scalar_prefetch=1,
            grid=(n_tok // rows,),
            in_specs=[pl.BlockSpec(memory_space=pl.ANY),
                      pl.BlockSpec((rows, d), lambda i, dest: (i, 0)),
                      pl.BlockSpec((rows, LANES), lambda i, dest: (i, 0)),
                      pl.BlockSpec((1, 6, d), lambda i, dest: (i * rows // seq, 0, 0)),
                      pl.BlockSpec((1, d), lambda i, dest: (0, 0))],
            out_specs=pl.BlockSpec((rows, d), lambda i, dest: (i, 0)),
            scratch_shapes=[pltpu.VMEM((2, TOP_K, rows * ROW_TILE, LANES), U32),
                            pltpu.SemaphoreType.DMA((2,))]),
        compiler_params=_cparams("arbitrary"),
        name="combine",
    )(dest, ys, x1_flat, route_flat, mod3, g_post)


def _moe_layout(route_flat, counts):
    rows = MOE_ROWS
    n_tok = route_flat.shape[0]
    ids = route_flat[:, 0:TOP_K].astype(jnp.int32)
    rank = route_flat[:, TOP_K:2 * TOP_K].astype(jnp.int32)
    counts = counts.astype(jnp.int32)
    padded = (counts + rows - 1) // rows * rows
    pad_ends = jnp.cumsum(padded)
    pad_starts = pad_ends - padded
    expert = jnp.arange(N_EXPERTS, dtype=jnp.int32)
    start_of = jnp.sum(jnp.where(ids[..., None] == expert, pad_starts, 0), axis=-1)
    dest = (start_of + rank).reshape(n_tok * TOP_K)
    n_blocks = (n_tok * TOP_K + N_EXPERTS * (rows - 1) + rows - 1) // rows
    blk_row0 = jnp.arange(n_blocks, dtype=jnp.int32) * rows
    blk_expert = jnp.minimum(jnp.sum(pad_ends[None, :] <= blk_row0[:, None], axis=-1),
                             N_EXPERTS - 1).astype(jnp.int32)
    n_used = (pad_ends[-1] // rows).reshape(1).astype(jnp.int32)
    seg_info = jnp.concatenate([pad_ends, padded, n_used]).astype(jnp.int32)
    return dest, blk_expert, n_used, seg_info, n_blocks * rows


def kernel(x, c, w_mod, b_mod, g_pre_mix, g_post_mix, g_pre_ffn, g_post_ffn, w_in, rel_bias, a_re, a_im, log_dt, ssm_b_re, ssm_b_im, ssm_c_re, ssm_c_im, d_skip, w_glu, b_glu, w_branch_attn, w_branch_ssm, w_out, w_router_group, b_router_group, w_router_expert, b_router_expert, w1, w3, w2):
    bsz, seq, d = x.shape
    depth = w_mod.shape[0]
    ssm_width = w_glu.shape[-1]
    n_pat = len(DILATION_PATTERNS)
    for l in range(depth):
        mod3 = _modulation(c, w_mod[l], b_mod[l]).reshape(bsz, 6, d)
        w_in_l = w_in[l].astype(BF16)
        g_pre = g_pre_mix[l].reshape(1, d)
        qkv = _qkv_projection(x, mod3, g_pre, w_in_l[:, :3 * ATTN_WIDTH])
        u_slabs, gates = _ugate_projection(x, mod3, g_pre, w_in_l[:, 3 * ATTN_WIDTH:], ssm_width)
        attn_outs = [_attention_pattern(*qkv[3 * p:3 * p + 3], rel_bias, DILATION_PATTERNS[p][1])
                     for p in range(n_pat)]
        ssm_slabs = _ssm_branch(u_slabs, bsz, a_re[l], a_im[l], log_dt[l], ssm_b_re[l],
                                ssm_b_im[l], ssm_c_re[l], ssm_c_im[l], d_skip[l], w_glu[l], b_glu[l])
        x1, h2, route, counts = _merge_and_route(
            x, gates, attn_outs, ssm_slabs, mod3, g_post_mix[l].reshape(1, d),
            g_pre_ffn[l].reshape(1, d), w_branch_attn[l].astype(BF16),
            w_branch_ssm[l].astype(BF16), w_out[l].astype(BF16),
            w_router_group[l], b_router_group[l], w_router_expert[l], b_router_expert[l])
        route_flat = route.reshape(bsz * seq, LANES)
        dest, blk_expert, n_used, seg_info, cap = _moe_layout(route_flat, counts[0, :N_EXPERTS])
        xs = _dispatch(h2.reshape(bsz * seq * ROW_TILE, LANES), dest, seg_info, cap)
        ys = _experts(xs, blk_expert, n_used, w1[l], w3[l], w2[l])
        x = _combine(ys, dest, x1.reshape(bsz * seq, d), route_flat, mod3,
                     g_post_ffn[l].reshape(1, d), seq).reshape(bsz, seq, d)
    return x
```

```python
import functools
import math

import numpy as np
import jax
import jax.numpy as jnp
from jax import lax
from jax.experimental import pallas as pl
from jax.experimental.pallas import tpu as pltpu

F32 = jnp.float32
BF16 = jnp.bfloat16

N_HEADS = 8
HEAD_DIM = 64
ATTN_WIDTH = N_HEADS * HEAD_DIM
DILATION_PATTERNS = ((128, 1), (512, 4), (2048, 16))
NUM_BUCKETS = 32
MAX_DISTANCE = 2048
N_EXPERT_GROUPS = 4
EXPERTS_PER_GROUP = 8
N_EXPERTS = N_EXPERT_GROUPS * EXPERTS_PER_GROUP
TOP_K = 2
RMS_EPS = 1e-6
NEG_INF = -1e30
LOG2_E = math.log2(math.e)

LANES = 128
SUBLANES = 8
VMEM_LIMIT_BYTES = 56 * 1024 * 1024

ATTN_BLK = 128
ATTN_STEP_ROWS = 2048
QKV_ROWS = 1024
QKV_MID_DIL = 4
TIME_TILE = 128
UGATE_TIME_TILE = 256
UGATE_CHUNK_BATCH = 1
MERGE_BATCH = 4
MERGE_CHUNK_BATCH = 2
SSM_STEPS = 128
MOE_ROWS = 512
DISPATCH_ROWS = 2048
COMBINE_ROWS = 512
ROW_TILE = 4
U32 = jnp.uint32
HI_HALF = 0xFFFF0000


def _pack_rows(x):
    w = x.shape[1] // 2
    lo = lax.bitcast_convert_type(x[:, :w].astype(BF16).astype(F32), U32) >> 16
    hi = lax.bitcast_convert_type(x[:, w:].astype(BF16).astype(F32), U32) & U32(HI_HALF)
    return hi | lo


def _unpack_rows(p):
    lo = lax.bitcast_convert_type(p << 16, F32)
    hi = lax.bitcast_convert_type(p & U32(HI_HALF), F32)
    return jnp.concatenate([lo, hi], axis=1)


def _store_row_tiles(ref, packed):
    rows = packed.shape[0]
    for s in range(ROW_TILE):
        ref[pl.ds(s, rows, stride=ROW_TILE), :] = packed[:, s * LANES:(s + 1) * LANES]


def _load_row_tiles(ref, rows):
    return _lane_concat([ref[pl.ds(s, rows, stride=ROW_TILE), :] for s in range(ROW_TILE)])


def _cparams(*sem):
    return pltpu.CompilerParams(dimension_semantics=sem, vmem_limit_bytes=VMEM_LIMIT_BYTES)


def _sigmoid(x):
    return 1.0 / (1.0 + jnp.exp(-x))


def _dot(a, b):
    return jnp.dot(a, b, preferred_element_type=F32)


def _split_bf16(a):
    hi = a.astype(BF16)
    lo = (a - hi.astype(F32)).astype(BF16)
    return hi, lo


def _dot_split(a, w_hi, w_lo):
    a_hi, a_lo = _split_bf16(a)
    return _dot(a_hi, w_hi) + _dot(a_lo, w_hi) + _dot(a_hi, w_lo)


def _rms_unit(x):
    ms = jnp.mean(x * x, axis=-1, keepdims=True)
    return x * lax.rsqrt(ms + RMS_EPS)


def _lane_concat(ref_slabs):
    return jnp.concatenate(ref_slabs, axis=-1)


def _trace_staggered(chunks):
    pending, active = list(chunks), []
    while pending or active:
        if pending:
            active.append(pending.pop(0))
        for gen in list(active):
            if next(gen, StopIteration) is StopIteration:
                active.remove(gen)


def _mod_kernel(c_ref, w_ref, b_ref, o_ref):
    c = c_ref[...]
    a = c * _sigmoid(c)
    w_hi, w_lo = _split_bf16(w_ref[...])
    o_ref[...] = _dot_split(a, w_hi, w_lo) + b_ref[...]


def _modulation(c, w_mod, b_mod):
    bsz, d = c.shape
    n = w_mod.shape[1]
    tn = 1024
    return pl.pallas_call(
        _mod_kernel,
        out_shape=jax.ShapeDtypeStruct((bsz, n), F32),
        grid=(n // tn,),
        in_specs=[pl.BlockSpec((bsz, d), lambda j: (0, 0)),
                  pl.BlockSpec((d, tn), lambda j: (0, j)),
                  pl.BlockSpec((1, tn), lambda j: (0, j))],
        out_specs=pl.BlockSpec((bsz, tn), lambda j: (0, j)),
        compiler_params=_cparams("arbitrary"),
        name="mod",
    )(c, w_mod, b_mod.reshape(1, n))


def _qkv_kernel(x_ref, mod_ref, g_ref, w_ref, *rest):
    n_pat = len(DILATION_PATTERNS)
    out_refs, slab, mid = rest[:3 * n_pat], rest[3 * n_pat], rest[3 * n_pat + 1]
    h = _rms_unit(x_ref[0]) * (g_ref[...] * (1.0 + mod_ref[0, 1:2, :])) + mod_ref[0, 0:1, :]
    hb = h.astype(BF16)
    rows = hb.shape[0]
    per_tensor = ATTN_WIDTH // LANES
    mid_dil = QKV_MID_DIL

    def project(t):
        res = _dot(hb, w_ref[:, t * ATTN_WIDTH:(t + 1) * ATTN_WIDTH])
        if t == 0:
            res = res * (HEAD_DIM ** -0.5 * LOG2_E)
        for s in range(per_tensor):
            slab[t * per_tensor + s] = res[:, s * LANES:(s + 1) * LANES]

    def split(t):
        for p, (_, dil) in enumerate(DILATION_PATTERNS):
            sub = rows // dil
            out = out_refs[3 * p + t]
            for r in range(dil):
                pieces = []
                for s in range(per_tensor):
                    ts = t * per_tensor + s
                    if dil == 1:
                        piece = slab[ts]
                    elif dil == mid_dil:
                        piece = slab[ts, pl.ds(r, sub, stride=dil), :]
                        mid[ts, r] = piece
                    else:
                        ratio = dil // mid_dil
                        piece = mid[ts, r % mid_dil, pl.ds(r // mid_dil, sub, stride=ratio), :]
                    pieces.append(piece)
                out[0, r] = _lane_concat(pieces).astype(out.dtype)

    project(0)
    for t in range(3):
        if t + 1 < 3:
            project(t + 1)
        split(t)


def _qkv_projection(x, mod3, g_pre, w_qkv):
    bsz, seq, d = x.shape
    tm = QKV_ROWS
    out_shape, out_specs = [], []
    for _, dil in DILATION_PATTERNS:
        assert tm % (dil * 2 * SUBLANES) == 0
        for _ in range(3):
            out_shape.append(jax.ShapeDtypeStruct((bsz, dil, seq // dil, ATTN_WIDTH), BF16))
            out_specs.append(pl.BlockSpec((1, dil, tm // dil, ATTN_WIDTH),
                                          lambda b, i: (b, 0, i, 0)))
    return pl.pallas_call(
        _qkv_kernel,
        out_shape=out_shape,
        grid=(bsz, seq // tm),
        in_specs=[pl.BlockSpec((1, tm, d), lambda b, i: (b, i, 0)),
                  pl.BlockSpec((1, 6, d), lambda b, i: (b, 0, 0)),
                  pl.BlockSpec((1, d), lambda b, i: (0, 0)),
                  pl.BlockSpec(w_qkv.shape, lambda b, i: (0, 0))],
        out_specs=out_specs,
        scratch_shapes=[pltpu.VMEM((w_qkv.shape[1] // LANES, tm, LANES), F32),
                        pltpu.VMEM((w_qkv.shape[1] // LANES, QKV_MID_DIL, tm // QKV_MID_DIL, LANES),
                                   F32)],
        compiler_params=_cparams("arbitrary", "arbitrary"),
        name="qkv",
    )(x, mod3, g_pre, w_qkv)


def _ugate_kernel(x_ref, mod_ref, g_ref, w_ref, u_ref, gate_ref):
    bsz, tt, d = x_ref.shape
    n_slab = u_ref.shape[0]
    sw = n_slab * LANES
    gw = gate_ref.shape[-1]
    cb = UGATE_CHUNK_BATCH
    rows = cb * tt
    col_chunk = 512

    def normalise(b0):
        shift = mod_ref[b0:b0 + cb, 0, :][:, None, :]
        scale = mod_ref[b0:b0 + cb, 1, :][:, None, :]
        h = _rms_unit(x_ref[b0:b0 + cb]) * (g_ref[...] * (1.0 + scale)) + shift
        return h.reshape(rows, d).astype(BF16)

    def project(b0, hb):
        u = _dot(hb, w_ref[:, 0:sw])
        for j in range(cb):
            for s in range(n_slab):
                u_ref[s, pl.ds(b0 + j, tt, stride=bsz), :] = (
                    u[j * tt:(j + 1) * tt, s * LANES:(s + 1) * LANES])
        for c0 in range(0, gw, col_chunk):
            g = _sigmoid(_dot(hb, w_ref[:, sw + c0:sw + c0 + col_chunk]))
            gate_ref[b0:b0 + cb, :, c0:c0 + col_chunk] = g.reshape(cb, tt, col_chunk).astype(BF16)

    starts = list(range(0, bsz, cb))
    hb = normalise(starts[0])
    for n, b0 in enumerate(starts):
        hb_next = normalise(starts[n + 1]) if n + 1 < len(starts) else None
        project(b0, hb)
        hb = hb_next


def _ugate_projection(x, mod3, g_pre, w_ug, ssm_width):
    bsz, seq, d = x.shape
    tt = UGATE_TIME_TILE
    gw = w_ug.shape[1] - ssm_width
    n_slab = ssm_width // LANES
    return pl.pallas_call(
        _ugate_kernel,
        out_shape=(jax.ShapeDtypeStruct((n_slab, seq * bsz, LANES), F32),
                   jax.ShapeDtypeStruct((bsz, seq, gw), BF16)),
        grid=(seq // tt,),
        in_specs=[pl.BlockSpec((bsz, tt, d), lambda i: (0, i, 0)),
                  pl.BlockSpec((bsz, 6, d), lambda i: (0, 0, 0)),
                  pl.BlockSpec((1, d), lambda i: (0, 0)),
                  pl.BlockSpec(w_ug.shape, lambda i: (0, 0))],
        out_specs=(pl.BlockSpec((n_slab, tt * bsz, LANES), lambda i: (0, i, 0)),
                   pl.BlockSpec((bsz, tt, gw), lambda i: (0, i, 0))),
        compiler_params=_cparams("arbitrary"),
        name="ugate",
    )(x, mod3, g_pre, w_ug)


def _t5_bucket_np(dist):
    exact = NUM_BUCKETS // 2
    d_f = np.maximum(dist, exact).astype(np.float32)
    large = exact + (np.log(d_f / np.float32(exact)) / np.float32(math.log(MAX_DISTANCE / exact))
                     * np.float32(NUM_BUCKETS - exact)).astype(np.int32)
    return np.where(dist < exact, dist, np.minimum(large, NUM_BUCKETS - 1))


def _bucket_map_t(dil):
    blk = ATTN_BLK
    ki = np.arange(2 * blk)[:, None]
    qi = np.arange(blk)[None, :]
    return _t5_bucket_np(np.maximum(blk + qi - ki, 0) * dil).astype(np.int32)


def _attn_kernel(relb_ref, q_ref, kc_ref, kp_ref, vc_ref, vp_ref, bucket_ref,
                 o_ref, lse_ref, kbuf, vbuf, bias_t, *, n_sub):
    blk = ATTN_BLK
    first_call = jnp.logical_and(pl.program_id(0) == 0,
                                 jnp.logical_and(pl.program_id(1) == 0, pl.program_id(2) == 0))

    @pl.when(first_call)
    def _():
        bucket = bucket_ref[...]
        ki = lax.broadcasted_iota(jnp.int32, bucket.shape, 0)
        qi = lax.broadcasted_iota(jnp.int32, bucket.shape, 1)
        dist = blk + qi - ki
        band = jnp.logical_and(dist >= 0, dist <= blk)
        band_first = jnp.logical_and(band, ki >= blk)

        def per_head(h, carry):
            acc = jnp.zeros(bucket.shape, F32)
            for b in range(NUM_BUCKETS):
                acc = jnp.where(bucket == b, relb_ref[b, h] * LOG2_E, acc)
            bias_t[0, h] = jnp.where(band_first, acc, NEG_INF)
            bias_t[1, h] = jnp.where(band, acc, NEG_INF)
            return carry

        lax.fori_loop(0, N_HEADS, per_head, 0)

    first_variant = jnp.where(pl.program_id(2) == 0, 0, 1)
    n_res = q_ref.shape[1]
    for g in range(n_res):
        kbuf[g, 0:blk, :] = kp_ref[0, g]
        kbuf[g, blk:, :] = kc_ref[0, g]
        vbuf[g, 0:blk, :] = vp_ref[0, g]
        vbuf[g, blk:, :] = vc_ref[0, g]

    lane = lax.broadcasted_iota(jnp.int32, (1, LANES), 1)
    lo_half = lane < HEAD_DIM
    bd_row = lax.broadcasted_iota(jnp.int32, (4 * blk, LANES), 0)
    bd_col = lax.broadcasted_iota(jnp.int32, (4 * blk, LANES), 1)
    ones_bd = ((bd_row < 2 * blk) == (bd_col < HEAD_DIM)).astype(F32).astype(BF16)
    contract_last = (((1,), (1,)), ((), ()))
    contract_first = (((0,), (0,)), ((), ()))

    def sub_block(g, i):
        r0 = i * blk
        q = q_ref[0, g, r0:r0 + blk, :]
        kk = kbuf[g, r0:r0 + 2 * blk, :]
        vv = vbuf[g, r0:r0 + 2 * blk, :]
        variant = first_variant if i == 0 else 1
        for j in range(N_HEADS // 2):
            cols = slice(j * LANES, (j + 1) * LANES)
            qj, kj, vj = q[:, cols], kk[:, cols], vv[:, cols]
            probs_t, maxes = [], []
            for hh in range(2):
                sel = lo_half if hh == 0 else jnp.logical_not(lo_half)
                qm = jnp.where(sel, qj, jnp.zeros_like(qj))
                s_t = lax.dot_general(kj, qm, contract_last, preferred_element_type=F32)
                s_t = s_t + bias_t[variant, 2 * j + hh]
                m = jnp.max(s_t, axis=0, keepdims=True)
                probs_t.append(jnp.exp2(s_t - m).astype(BF16))
                maxes.append(m)
            p2_t = jnp.concatenate(probs_t, axis=0)
            v_bd = jnp.concatenate([jnp.where(lo_half, vj, jnp.zeros_like(vj)),
                                    jnp.where(lo_half, jnp.zeros_like(vj), vj)], axis=0)
            rhs = jnp.concatenate([v_bd, ones_bd], axis=1)
            ol = lax.dot_general(p2_t, rhs, contract_first, preferred_element_type=F32)
            o2, l2 = ol[:, :LANES], ol[:, LANES:]
            m_t = jnp.concatenate([jnp.broadcast_to(maxes[0], (HEAD_DIM, blk)),
                                   jnp.broadcast_to(maxes[1], (HEAD_DIM, blk))], axis=0)
            o_ref[0, g, r0:r0 + blk, cols] = (o2 / l2).astype(o_ref.dtype)
            lse_ref[0, g, r0:r0 + blk, cols] = m_t.T + jnp.log2(l2)

    for g in range(n_res):
        for i in range(n_sub):
            sub_block(g, i)


def _attention_pattern(q, k, v, rel_bias, dil):
    bsz, _, sub_len, aw = q.shape
    blk = ATTN_BLK
    assert sub_len % blk == 0
    tq = min(ATTN_STEP_ROWS, sub_len)
    n_sub = tq // blk
    ratio = tq // blk
    n_res = min(dil, ATTN_STEP_ROWS // tq)
    cur = lambda b, r, n: (b, r, n, 0)
    prev = lambda b, r, n: (b, r, jnp.maximum(n * ratio - 1, 0), 0)
    blk_cur = pl.BlockSpec((1, n_res, tq, aw), cur)
    blk_prev = pl.BlockSpec((1, n_res, blk, aw), prev)
    bucket = jnp.asarray(_bucket_map_t(dil))
    o_dtype = BF16 if TIME_TILE // dil >= 2 * SUBLANES else F32
    return pl.pallas_call(
        functools.partial(_attn_kernel, n_sub=n_sub),
        out_shape=(jax.ShapeDtypeStruct(q.shape, o_dtype), jax.ShapeDtypeStruct(q.shape, F32)),
        grid=(bsz, dil // n_res, sub_len // tq),
        in_specs=[pl.BlockSpec(memory_space=pltpu.SMEM),
                  blk_cur, blk_cur, blk_prev, blk_cur, blk_prev,
                  pl.BlockSpec(bucket.shape, lambda b, r, n: (0, 0))],
        out_specs=(blk_cur, blk_cur),
        scratch_shapes=[pltpu.VMEM((n_res, tq + blk, aw), BF16),
                        pltpu.VMEM((n_res, tq + blk, aw), BF16),
                        pltpu.VMEM((2, N_HEADS, 2 * blk, blk), F32)],
        compiler_params=_cparams("arbitrary", "arbitrary", "arbitrary"),
        name=f"attn_dil{dil}",
    )(rel_bias.astype(F32), q, k, k, v, v, bucket)


def _ssm_kernel(u_ref, bmat_ref, cmat_ref, ar_ref, ai_ref, dskip_ref, wglu_ref, bglu_ref,
                o_ref, hbuf, hstate, *, n_steps):
    @pl.when(pl.program_id(0) == 0)
    def _():
        hstate[...] = jnp.zeros_like(hstate)

    n_slab = u_ref.shape[0]
    n_state = hbuf.shape[1] // 2
    per = n_state // n_slab
    us = [u_ref[s] for s in range(n_slab)]
    ys = [None] * n_slab

    def drive(s):
        bu = _dot(us[s].astype(BF16), bmat_ref[s])
        hbuf[:, s * per:(s + 1) * per] = bu[:, :per]
        hbuf[:, n_state + s * per:n_state + (s + 1) * per] = bu[:, per:]

    def scan(s):
        re_cols = slice(s * per, (s + 1) * per)
        im_cols = slice(n_state + s * per, n_state + (s + 1) * per)
        ar = ar_ref[:, re_cols]
        ai = ai_ref[:, re_cols]
        hr = hstate[:, re_cols]
        hi = hstate[:, im_cols]
        for t in range(n_steps):
            trow = slice(t * SUBLANES, (t + 1) * SUBLANES)
            nr = ar * hr - ai * hi + hbuf[trow, re_cols]
            ni = ar * hi + ai * hr + hbuf[trow, im_cols]
            hbuf[trow, re_cols] = nr
            hbuf[trow, im_cols] = ni
            hr, hi = nr, ni
        hstate[:, re_cols] = hr
        hstate[:, im_cols] = hi

    def read_out(s):
        h_s = _lane_concat([hbuf[:, s * per:(s + 1) * per],
                            hbuf[:, n_state + s * per:n_state + (s + 1) * per]])
        ys[s] = (_dot(h_s.astype(BF16), cmat_ref[s])
                 + dskip_ref[:, s * LANES:(s + 1) * LANES] * us[s])

    for tick in range(n_slab + 2):
        if tick < n_slab:
            drive(tick)
        if 0 <= tick - 1 < n_slab:
            scan(tick - 1)
        if 0 <= tick - 2 < n_slab:
            read_out(tick - 2)
    y = _lane_concat(ys)
    y = 0.5 * y * (1.0 + jnp.tanh(math.sqrt(2.0 / math.pi) * (y + 0.044715 * (y * y * y))))
    z = _dot(y.astype(BF16), wglu_ref[...]) + bglu_ref[...]
    out = y * _sigmoid(z)
    for s in range(n_slab):
        o_ref[s] = out[:, s * LANES:(s + 1) * LANES]


def _ssm_params(a_re, a_im, log_dt, b_re, b_im, c_re, c_im, bsz):
    g, p = a_re.shape
    hg = b_re.shape[-1]
    dt = jnp.exp(log_dt.astype(F32))[:, None]
    a_re, a_im = a_re.astype(F32), a_im.astype(F32)
    mag = jnp.exp(a_re * dt)
    abar_re = mag * jnp.cos(a_im * dt)
    abar_im = mag * jnp.sin(a_im * dt)
    den = a_re * a_re + a_im * a_im
    q_re = ((abar_re - 1.0) * a_re + abar_im * a_im) / den
    q_im = (abar_im * a_re - (abar_re - 1.0) * a_im) / den
    b_re, b_im = b_re.astype(F32), b_im.astype(F32)
    bb_re = q_re[..., None] * b_re - q_im[..., None] * b_im
    bb_im = q_re[..., None] * b_im + q_im[..., None] * b_re
    gs = LANES // hg
    n_slab = g // gs
    eye = jnp.eye(gs, dtype=F32)

    def in_mat(t):
        t = t.reshape(n_slab, gs, p, hg)
        return jnp.einsum('sgph,gk->sghkp', t, eye).reshape(n_slab, gs * hg, gs * p)

    def out_mat(t):
        t = t.reshape(n_slab, gs, hg, p)
        return jnp.einsum('sghp,gk->sgpkh', t, eye).reshape(n_slab, gs * p, gs * hg)

    bmat = jnp.concatenate([in_mat(bb_re), in_mat(bb_im)], axis=2).astype(BF16)
    cmat = jnp.concatenate([out_mat(c_re.astype(F32)), -out_mat(c_im.astype(F32))],
                           axis=1).astype(BF16)
    ar = jnp.broadcast_to(abar_re.reshape(1, g * p), (bsz, g * p))
    ai = jnp.broadcast_to(abar_im.reshape(1, g * p), (bsz, g * p))
    return bmat, cmat, ar, ai


def _ssm_branch(u_slabs, bsz, a_re, a_im, log_dt, b_re, b_im, c_re, c_im, d_skip, w_glu, b_glu):
    n_slab, n_rows, _ = u_slabs.shape
    width = n_slab * LANES
    assert bsz == SUBLANES
    bmat, cmat, ar, ai = _ssm_params(a_re, a_im, log_dt, b_re, b_im, c_re, c_im, bsz)
    n_state2 = n_slab * bmat.shape[2]
    rows = SSM_STEPS * bsz
    const = lambda c: (0, 0)
    const3 = lambda c: (0, 0, 0)
    slab_spec = pl.BlockSpec((n_slab, rows, LANES), lambda c: (0, c, 0))
    return pl.pallas_call(
        functools.partial(_ssm_kernel, n_steps=SSM_STEPS),
        out_shape=jax.ShapeDtypeStruct(u_slabs.shape, F32),
        grid=(n_rows // rows,),
        in_specs=[slab_spec,
                  pl.BlockSpec(bmat.shape, const3), pl.BlockSpec(cmat.shape, const3),
                  pl.BlockSpec(ar.shape, const), pl.BlockSpec(ai.shape, const),
                  pl.BlockSpec((1, width), const), pl.BlockSpec((width, width), const),
                  pl.BlockSpec((1, width), const)],
        out_specs=slab_spec,
        scratch_shapes=[pltpu.VMEM((rows, n_state2), F32), pltpu.VMEM((bsz, n_state2), F32)],
        compiler_params=_cparams("arbitrary"),
        name="ssm",
    )(u_slabs, bmat, cmat, ar, ai, d_skip.reshape(1, width).astype(F32),
      w_glu.astype(BF16), b_glu.reshape(1, width).astype(F32))


ROUTE_LANES = {"id0": 0, "id1": 1, "rank0": 2, "rank1": 3, "w0": 4, "w1": 5}
GROUP_LANE0 = N_EXPERTS


def _merge_kernel(*refs, bsz_total):
    n_pat = len(DILATION_PATTERNS)
    x_ref, gate_ref = refs[0:2]
    attn_refs = refs[2:2 + 2 * n_pat]
    (ssm_ref, mod_ref, gpost_ref, gpre_ref, wba_ref, wbs_ref, wout_ref,
     wr_cat_ref, wr_hi_ref, br_ref) = refs[2 + 2 * n_pat:12 + 2 * n_pat]
    x1_ref, h2_ref, route_ref, count_ref = refs[12 + 2 * n_pat:16 + 2 * n_pat]
    scratch = refs[16 + 2 * n_pat:]
    carry = scratch[-1]

    @pl.when(jnp.logical_and(pl.program_id(0) == 0, pl.program_id(1) == 0))
    def _():
        carry[...] = jnp.zeros_like(carry)

    nb, tt, d = x_ref.shape
    n_slab = ssm_ref.shape[0]
    b0 = pl.program_id(0) * nb
    cb = MERGE_CHUNK_BATCH
    rows = cb * tt
    lane = lax.broadcasted_iota(jnp.int32, (rows, LANES), 1).astype(F32)
    row = lax.broadcasted_iota(jnp.int32, (rows, rows), 0)
    col = lax.broadcasted_iota(jnp.int32, (rows, rows), 1)
    strict_lower = (col < row).astype(BF16)
    running = [carry[...]]

    def chunk_phases(c0):
        bbs = range(c0, c0 + cb)
        o_tok, lse_tok, ssm_tok = scratch[3 * (c0 // cb):3 * (c0 // cb) + 3]

        for bb in bbs:
            for s in range(n_slab):
                ssm_tok[(bb - c0) * tt:(bb - c0 + 1) * tt, s * LANES:(s + 1) * LANES] = (
                    ssm_ref[s, pl.ds(b0 + bb, tt, stride=bsz_total), :])

        slot = 0
        sources = []
        for p, (_, dil) in enumerate(DILATION_PATTERNS):
            o_ref, lse_ref = attn_refs[2 * p], attn_refs[2 * p + 1]
            if dil == 1:
                sources.append((o_ref, lse_ref, None))
                continue
            sub = tt // dil
            for bb in bbs:
                for r in range(dil):
                    o_blk = o_ref[bb, r].astype(F32)
                    l_blk = lse_ref[bb, r]
                    for s in range(n_slab):
                        dst = pl.ds((bb - c0) * tt + r, sub, stride=dil)
                        o_tok[slot, s, dst, :] = o_blk[:, s * LANES:(s + 1) * LANES]
                        lse_tok[slot, s, dst, :] = l_blk[:, s * LANES:(s + 1) * LANES]
            sources.append((o_ref, lse_ref, slot))
            slot += 1
        attn_slabs = []
        for s in range(n_slab):
            cols = slice(s * LANES, (s + 1) * LANES)
            o_ps, lse_ps = [], []
            for o_ref, lse_ref, src_slot in sources:
                if src_slot is None:
                    o_ps.append(o_ref[c0:c0 + cb, 0, :, cols].astype(F32).reshape(rows, LANES))
                    lse_ps.append(lse_ref[c0:c0 + cb, 0, :, cols].reshape(rows, LANES))
                else:
                    o_ps.append(o_tok[src_slot, s])
                    lse_ps.append(lse_tok[src_slot, s])
            m = functools.reduce(jnp.maximum, lse_ps)
            es = [jnp.exp2(l - m) for l in lse_ps]
            num = functools.reduce(lambda a, b: a + b, [e * o for e, o in zip(es, o_ps)])
            den = functools.reduce(lambda a, b: a + b, es)
            attn_slabs.append(num / den)
        attn_b = _lane_concat(attn_slabs).astype(BF16)
        ssm_b = ssm_tok[...].astype(BF16)
        yield
        branch_attn = _dot(attn_b, wba_ref[...])
        branch_ssm = _dot(ssm_b, wbs_ref[...])
        yield
        g_attn = gate_ref[c0:c0 + cb, :, 0:d].reshape(rows, d)
        g_ssm = gate_ref[c0:c0 + cb, :, d:].reshape(rows, d)
        merged_b = g_attn * branch_attn.astype(BF16) + g_ssm * branch_ssm.astype(BF16)
        yield
        y = _dot(merged_b, wout_ref[...])
        yield
        gate1 = mod_ref[c0:c0 + cb, 2, :][:, None, :]
        shift2 = mod_ref[c0:c0 + cb, 3, :][:, None, :]
        scale2 = mod_ref[c0:c0 + cb, 4, :][:, None, :]
        x1 = x_ref[c0:c0 + cb] + (gate1 * gpost_ref[...]) * _rms_unit(y).reshape(cb, tt, d)
        x1_ref[c0:c0 + cb] = x1
        h2 = _rms_unit(x1) * (gpre_ref[...] * (1.0 + scale2)) + shift2
        for j, bb in enumerate(bbs):
            _store_row_tiles(h2_ref.at[bb], _pack_rows(h2[j]))

        a_hi, a_lo = _split_bf16(h2.reshape(rows, d))
        yield
        hi_pass = _dot(a_hi, wr_cat_ref[...])
        lo_pass = _dot(a_lo, wr_hi_ref[...])
        yield
        logits = hi_pass[:, :LANES] + lo_pass + hi_pass[:, LANES:] + br_ref[...]
        big = float(LANES)
        is_group = jnp.logical_and(lane >= GROUP_LANE0, lane < GROUP_LANE0 + N_EXPERT_GROUPS)
        gl = jnp.where(is_group, logits, -jnp.inf)
        g_max = jnp.max(gl, axis=-1, keepdims=True)
        g_sel = jnp.min(jnp.where(gl == g_max, lane, big), axis=-1, keepdims=True) - GROUP_LANE0
        g_gate = 1.0 / jnp.sum(jnp.exp(gl - g_max), axis=-1, keepdims=True)
        lo = g_sel * EXPERTS_PER_GROUP
        in_group = jnp.logical_and(lane >= lo, lane < lo + EXPERTS_PER_GROUP)
        el = jnp.where(in_group, logits, -jnp.inf)
        t0 = jnp.max(el, axis=-1, keepdims=True)
        i0 = jnp.min(jnp.where(el == t0, lane, big), axis=-1, keepdims=True)
        el1 = jnp.where(lane == i0, -jnp.inf, el)
        t1 = jnp.max(el1, axis=-1, keepdims=True)
        i1 = jnp.min(jnp.where(el1 == t1, lane, big), axis=-1, keepdims=True)
        e = jnp.exp(t1 - t0)
        w0 = g_gate / (1.0 + e)
        w1 = g_gate * e / (1.0 + e)

        hit0 = lane == i0
        hit1 = lane == i1
        onehot = jnp.logical_or(hit0, hit1).astype(F32)
        yield
        before = _dot(strict_lower, onehot.astype(BF16)) + running[0]
        rank0 = jnp.sum(jnp.where(hit0, before, 0.0), axis=-1, keepdims=True)
        rank1 = jnp.sum(jnp.where(hit1, before, 0.0), axis=-1, keepdims=True)
        running[0] = running[0] + jnp.sum(onehot, axis=0, keepdims=True)

        route = jnp.zeros((rows, LANES), F32)
        for name, val in (("id0", i0), ("id1", i1), ("rank0", rank0), ("rank1", rank1),
                          ("w0", w0), ("w1", w1)):
            route = jnp.where(lane == ROUTE_LANES[name], val, route)
        route_ref[c0:c0 + cb] = route.reshape(cb, tt, LANES)

    _trace_staggered([chunk_phases(c0) for c0 in range(0, nb, cb)])

    carry[...] = running[0]
    count_ref[...] = jnp.broadcast_to(running[0], count_ref.shape)


def _merge_and_route(x, gates, attn_outs, ssm_slabs, mod3, g_post, g_pre, wba, wbs, wout,
                     w_rg, b_rg, w_re, b_re):
    bsz, seq, d = x.shape
    tt, nb = TIME_TILE, MERGE_BATCH
    aw = ATTN_WIDTH
    n_slab = ssm_slabs.shape[0]
    assert aw == n_slab * LANES
    wr = jnp.zeros((d, LANES), F32).at[:, :N_EXPERTS].set(w_re.astype(F32))
    wr = wr.at[:, GROUP_LANE0:GROUP_LANE0 + N_EXPERT_GROUPS].set(w_rg.astype(F32))
    br = jnp.zeros((1, LANES), F32).at[0, :N_EXPERTS].set(b_re.astype(F32))
    br = br.at[0, GROUP_LANE0:GROUP_LANE0 + N_EXPERT_GROUPS].set(b_rg.astype(F32))
    wr_hi, wr_lo = _split_bf16(wr)
    wr_cat = jnp.concatenate([wr_hi, wr_lo], axis=1)
    n_strided = sum(1 for _, dil in DILATION_PATTERNS if dil > 1)
    chunk_rows = MERGE_CHUNK_BATCH * tt
    tok = lambda h, i: (h, i, 0)
    const = lambda h, i: (0, 0)
    attn_args, attn_specs = [], []
    for (o_p, lse_p), (_, dil) in zip(attn_outs, DILATION_PATTERNS):
        spec = pl.BlockSpec((nb, dil, tt // dil, aw), lambda h, i: (h, 0, i, 0))
        attn_args += [o_p, lse_p]
        attn_specs += [spec, spec]
    return pl.pallas_call(
        functools.partial(_merge_kernel, bsz_total=bsz),
        out_shape=(jax.ShapeDtypeStruct((bsz, seq, d), F32),
                   jax.ShapeDtypeStruct((bsz, seq * ROW_TILE, LANES), U32),
                   jax.ShapeDtypeStruct((bsz, seq, LANES), F32),
                   jax.ShapeDtypeStruct((SUBLANES, LANES), F32)),
        grid=(bsz // nb, seq // tt),
        in_specs=[pl.BlockSpec((nb, tt, d), tok),
                  pl.BlockSpec((nb, tt, gates.shape[-1]), tok)]
                 + attn_specs
                 + [pl.BlockSpec((n_slab, tt * bsz, LANES), lambda h, i: (0, i, 0)),
                    pl.BlockSpec((nb, 6, d), lambda h, i: (h, 0, 0)),
                    pl.BlockSpec((1, d), const), pl.BlockSpec((1, d), const),
                    pl.BlockSpec(wba.shape, const), pl.BlockSpec(wbs.shape, const),
                    pl.BlockSpec(wout.shape, const),
                    pl.BlockSpec((d, 2 * LANES), const), pl.BlockSpec((d, LANES), const),
                    pl.BlockSpec((1, LANES), const)],
        out_specs=(pl.BlockSpec((nb, tt, d), tok), pl.BlockSpec((nb, tt * ROW_TILE, LANES), tok),
                   pl.BlockSpec((nb, tt, LANES), tok),
                   pl.BlockSpec((SUBLANES, LANES), const)),
        scratch_shapes=[pltpu.VMEM((n_strided, n_slab, chunk_rows, LANES), F32),
                        pltpu.VMEM((n_strided, n_slab, chunk_rows, LANES), F32),
                        pltpu.VMEM((chunk_rows, n_slab * LANES), F32)] * (nb // MERGE_CHUNK_BATCH)
                       + [pltpu.VMEM((1, LANES), F32)],
        compiler_params=_cparams("arbitrary", "arbitrary"),
        name="merge",
    )(x, gates, *attn_args, ssm_slabs, mod3, g_post, g_pre, wba, wbs, wout, wr_cat, wr_hi, br)


ISSUE_UNROLL = 8


def _dispatch_kernel(dest_ref, seg_ref, h_ref, xs_ref, zero_buf, sem, zsem):
    rows = h_ref.shape[0] // ROW_TILE
    blk = zero_buf.shape[0]
    n_blocks = xs_ref.shape[0] // blk
    base = pl.program_id(0) * rows

    @pl.when(pl.program_id(0) == 0)
    def _():
        zero_buf[...] = jnp.zeros_like(zero_buf)

        def zero_copy(row0):
            return pltpu.make_async_copy(zero_buf, xs_ref.at[pl.ds(pl.multiple_of(row0, blk), blk)],
                                         zsem)

        def fill_tail(e, carry):
            @pl.when(seg_ref[N_EXPERTS + e] > 0)
            def _():
                zero_copy(seg_ref[e] * ROW_TILE - blk).start()
            return carry

        def fill_unused(j, carry):
            zero_copy(j * blk).start()
            return carry

        def wait_tail(e, carry):
            @pl.when(seg_ref[N_EXPERTS + e] > 0)
            def _():
                zero_copy(0).wait()
            return carry

        def wait_unused(j, carry):
            zero_copy(0).wait()
            return carry

        n_used = seg_ref[2 * N_EXPERTS]
        lax.fori_loop(0, N_EXPERTS, fill_tail, 0)
        lax.fori_loop(n_used, n_blocks, fill_unused, 0)
        lax.fori_loop(0, N_EXPERTS, wait_tail, 0)
        lax.fori_loop(n_used, n_blocks, wait_unused, 0)

    group = ISSUE_UNROLL * ROW_TILE

    def issue(g, carry):
        g0 = pl.multiple_of(g * group, group)
        for rr in range(ISSUE_UNROLL):
            src = h_ref.at[pl.ds(g0 + rr * ROW_TILE, ROW_TILE)]
            for k in range(TOP_K):
                d = dest_ref[TOP_K * (base + g * ISSUE_UNROLL + rr) + k]
                dst = xs_ref.at[pl.ds(pl.multiple_of(d * ROW_TILE, ROW_TILE), ROW_TILE)]
                pltpu.make_async_copy(src, dst, sem).start(priority=k)
        return carry

    lax.fori_loop(0, rows // ISSUE_UNROLL, issue, 0)
    for _ in range(TOP_K):
        pltpu.make_async_copy(h_ref, xs_ref.at[pl.ds(0, rows * ROW_TILE)], sem).wait()


def _dispatch(h2_tiles, dest, seg_info, cap):
    n_tok = h2_tiles.shape[0] // ROW_TILE
    rows = DISPATCH_ROWS
    return pl.pallas_call(
        _dispatch_kernel,
        out_shape=jax.ShapeDtypeStruct((cap * ROW_TILE, LANES), h2_tiles.dtype),
        grid_spec=pltpu.PrefetchScalarGridSpec(
            num_scalar_prefetch=2,
            grid=(n_tok // rows,),
            in_specs=[pl.BlockSpec((rows * ROW_TILE, LANES), lambda i, dest, seg: (i, 0))],
            out_specs=pl.BlockSpec(memory_space=pl.ANY),
            scratch_shapes=[pltpu.VMEM((MOE_ROWS * ROW_TILE, LANES), h2_tiles.dtype),
                            pltpu.SemaphoreType.DMA, pltpu.SemaphoreType.DMA]),
        compiler_params=_cparams("arbitrary"),
        name="dispatch",
    )(dest, seg_info, h2_tiles)


def _expert_kernel(blk_expert_ref, seg_slot_ref, next_expert_ref, n_used_ref, x_ref,
                   w1_hbm, w3_hbm, w2_hbm, y_ref, w1f, w3f, w2f, w1b, w3b, w2b, sems):
    i = pl.program_id(0)
    used = i < n_used_ref[0]
    rows = x_ref.shape[0] // ROW_TILE
    expert = blk_expert_ref[i]
    new_expert = jnp.logical_or(i == 0, expert != blk_expert_ref[jnp.maximum(i - 1, 0)])
    slot = seg_slot_ref[i]

    def weight_copies(e, s):
        return [pltpu.make_async_copy(hbm.at[e], stage.at[s], sems.at[s, n])
                for n, (hbm, stage) in enumerate(((w1_hbm, w1f), (w3_hbm, w3f), (w2_hbm, w2f)))]

    @pl.when(jnp.logical_and(used, new_expert))
    def _():
        @pl.when(i == 0)
        def _():
            for cp in weight_copies(expert, slot):
                cp.start()

        for cp in weight_copies(expert, slot):
            cp.wait()
        w1b[...] = w1f[slot].astype(BF16)
        w3b[...] = w3f[slot].astype(BF16)
        w2b[...] = w2f[slot].astype(BF16)
        nxt = next_expert_ref[i]

        @pl.when(nxt >= 0)
        def _():
            for cp in weight_copies(nxt, 1 - slot):
                cp.start()

    @pl.when(used)
    def _():
        xb = _unpack_rows(_load_row_tiles(x_ref, rows)).astype(BF16)
        h1 = _dot(xb, w1b[...])
        h3 = _dot(xb, w3b[...])
        act = (h1 * _sigmoid(h1)) * h3
        y = _dot(act.astype(BF16), w2b[...])
        _store_row_tiles(y_ref, _pack_rows(y))

    @pl.when(jnp.logical_not(used))
    def _():
        y_ref[...] = jnp.zeros_like(y_ref)


def _experts(xs, blk_expert, n_used, w1, w3, w2):
    d, de = w1.shape[-2:]
    assert d == 2 * ROW_TILE * LANES
    rows = MOE_ROWS
    tile_rows = rows * ROW_TILE
    n_blocks = xs.shape[0] // tile_rows
    blk = jnp.arange(n_blocks, dtype=jnp.int32)
    changed = jnp.concatenate([jnp.zeros((1,), jnp.int32),
                               (blk_expert[1:] != blk_expert[:-1]).astype(jnp.int32)])
    seg_slot = (jnp.cumsum(changed) % 2).astype(jnp.int32)
    later = jnp.logical_and(blk[None, :] > blk[:, None],
                            jnp.logical_and(blk_expert[None, :] != blk_expert[:, None],
                                            blk[None, :] < n_used[0]))
    first_later = jnp.argmax(later, axis=1)
    next_expert = jnp.where(jnp.any(later, axis=1), blk_expert[first_later], -1).astype(jnp.int32)
    xblk = lambda i, be, ss, ne, nu: (jnp.minimum(i, nu[0] - 1), 0)
    return pl.pallas_call(
        _expert_kernel,
        out_shape=jax.ShapeDtypeStruct(xs.shape, U32),
        grid_spec=pltpu.PrefetchScalarGridSpec(
            num_scalar_prefetch=4,
            grid=(n_blocks,),
            in_specs=[pl.BlockSpec((tile_rows, LANES), xblk),
                      pl.BlockSpec(memory_space=pl.ANY), pl.BlockSpec(memory_space=pl.ANY),
                      pl.BlockSpec(memory_space=pl.ANY)],
            out_specs=pl.BlockSpec((tile_rows, LANES), lambda i, be, ss, ne, nu: (i, 0)),
            scratch_shapes=[pltpu.VMEM((2, d, de), w1.dtype), pltpu.VMEM((2, d, de), w3.dtype),
                            pltpu.VMEM((2, de, d), w2.dtype),
                            pltpu.VMEM((d, de), BF16), pltpu.VMEM((d, de), BF16),
                            pltpu.VMEM((de, d), BF16),
                            pltpu.SemaphoreType.DMA((2, 3))]),
        compiler_params=_cparams("arbitrary"),
        name="experts",
    )(blk_expert, seg_slot, next_expert, n_used, xs, w1, w3, w2)


def _combine_kernel(dest_ref, ys_ref, x1_ref, route_ref, mod_ref, g_ref, o_ref, ybuf, sem):
    rows = x1_ref.shape[0]
    step = pl.program_id(0)
    n_step = pl.num_programs(0)

    group = ISSUE_UNROLL * ROW_TILE

    def gather(tile, slot):
        base = tile * rows

        def issue(g, carry):
            g0 = pl.multiple_of(g * group, group)
            for rr in range(ISSUE_UNROLL):
                for k in range(TOP_K):
                    d = dest_ref[TOP_K * (base + g * ISSUE_UNROLL + rr) + k]
                    src = ys_ref.at[pl.ds(pl.multiple_of(d * ROW_TILE, ROW_TILE), ROW_TILE)]
                    dst = ybuf.at[slot, k, pl.ds(g0 + rr * ROW_TILE, ROW_TILE)]
                    pltpu.make_async_copy(src, dst, sem.at[slot]).start(priority=k)
            return carry

        lax.fori_loop(0, rows // ISSUE_UNROLL, issue, 0)

    slot = lax.rem(step, 2)

    @pl.when(step == 0)
    def _():
        gather(step, 0)

    @pl.when(step + 1 < n_step)
    def _():
        gather(step + 1, 1 - slot)

    for k in range(TOP_K):
        pltpu.make_async_copy(ys_ref.at[pl.ds(0, rows * ROW_TILE)], ybuf.at[slot, k],
                              sem.at[slot]).wait()
    route = route_ref[...]
    w0 = route[:, ROUTE_LANES["w0"]:ROUTE_LANES["w0"] + 1]
    w1 = route[:, ROUTE_LANES["w1"]:ROUTE_LANES["w1"] + 1]

    def rows_of(k):
        return _unpack_rows(_load_row_tiles(ybuf.at[slot, k], rows))

    y = rows_of(0) * w0 + rows_of(1) * w1
    o_ref[...] = x1_ref[...] + (mod_ref[0, 5:6, :] * g_ref[...]) * _rms_unit(y)


def _combine(ys, dest, x1_flat, route_flat, mod3, g_post, seq):
    n_tok, d = x1_flat.shape
    rows = COMBINE_ROWS
    assert seq % rows == 0
    return pl.pallas_call(
        _combine_kernel,
        out_shape=jax.ShapeDtypeStruct((n_tok, d), F32),
        grid_spec=pltpu.PrefetchScalarGridSpec(
            num_scalar_prefetch=1,
            grid=(n_tok // rows,),
            in_specs=[pl.BlockSpec(memory_space=pl.ANY),
                      pl.BlockSpec((rows, d), lambda i, dest: (i, 0)),
                      pl.BlockSpec((rows, LANES), lambda i, dest: (i, 0)),
                      pl.BlockSpec((1, 6, d), lambda i, dest: (i * rows // seq, 0, 0)),
                      pl.BlockSpec((1, d), lambda i, dest: (0, 0))],
            out_specs=pl.BlockSpec((rows, d), lambda i, dest: (i, 0)),
            scratch_shapes=[pltpu.VMEM((2, TOP_K, rows * ROW_TILE, LANES), U32),
                            pltpu.SemaphoreType.DMA((2,))]),
        compiler_params=_cparams("arbitrary"),
        name="combine",
    )(dest, ys, x1_flat, route_flat, mod3, g_post)


def _moe_layout(route_flat, counts):
    rows = MOE_ROWS
    n_tok = route_flat.shape[0]
    ids = route_flat[:, 0:TOP_K].astype(jnp.int32)
    rank = route_flat[:, TOP_K:2 * TOP_K].astype(jnp.int32)
    counts = counts.astype(jnp.int32)
    padded = (counts + rows - 1) // rows * rows
    pad_ends = jnp.cumsum(padded)
    pad_starts = pad_ends - padded
    expert = jnp.arange(N_EXPERTS, dtype=jnp.int32)
    start_of = jnp.sum(jnp.where(ids[..., None] == expert, pad_starts, 0), axis=-1)
    dest = (start_of + rank).reshape(n_tok * TOP_K)
    n_blocks = (n_tok * TOP_K + N_EXPERTS * (rows - 1) + rows - 1) // rows
    blk_row0 = jnp.arange(n_blocks, dtype=jnp.int32) * rows
    blk_expert = jnp.minimum(jnp.sum(pad_ends[None, :] <= blk_row0[:, None], axis=-1),
                             N_EXPERTS - 1).astype(jnp.int32)
    n_used = (pad_ends[-1] // rows).reshape(1).astype(jnp.int32)
    seg_info = jnp.concatenate([pad_ends, padded, n_used]).astype(jnp.int32)
    return dest, blk_expert, n_used, seg_info, n_blocks * rows


def kernel(x, c, w_mod, b_mod, g_pre_mix, g_post_mix, g_pre_ffn, g_post_ffn, w_in, rel_bias, a_re, a_im, log_dt, ssm_b_re, ssm_b_im, ssm_c_re, ssm_c_im, d_skip, w_glu, b_glu, w_branch_attn, w_branch_ssm, w_out, w_router_group, b_router_group, w_router_expert, b_router_expert, w1, w3, w2):
    bsz, seq, d = x.shape
    depth = w_mod.shape[0]
    ssm_width = w_glu.shape[-1]
    n_pat = len(DILATION_PATTERNS)
    for l in range(depth):
        mod3 = _modulation(c, w_mod[l], b_mod[l]).reshape(bsz, 6, d)
        w_in_l = w_in[l].astype(BF16)
        g_pre = g_pre_mix[l].reshape(1, d)
        qkv = _qkv_projection(x, mod3, g_pre, w_in_l[:, :3 * ATTN_WIDTH])
        u_slabs, gates = _ugate_projection(x, mod3, g_pre, w_in_l[:, 3 * ATTN_WIDTH:], ssm_width)
        attn_outs = [_attention_pattern(*qkv[3 * p:3 * p + 3], rel_bias, DILATION_PATTERNS[p][1])
                     for p in range(n_pat)]
        ssm_slabs = _ssm_branch(u_slabs, bsz, a_re[l], a_im[l], log_dt[l], ssm_b_re[l],
                                ssm_b_im[l], ssm_c_re[l], ssm_c_im[l], d_skip[l], w_glu[l], b_glu[l])
        x1, h2, route, counts = _merge_and_route(
            x, gates, attn_outs, ssm_slabs, mod3, g_post_mix[l].reshape(1, d),
            g_pre_ffn[l].reshape(1, d), w_branch_attn[l].astype(BF16),
            w_branch_ssm[l].astype(BF16), w_out[l].astype(BF16),
            w_router_group[l], b_router_group[l], w_router_expert[l], b_router_expert[l])
        route_flat = route.reshape(bsz * seq, LANES)
        dest, blk_expert, n_used, seg_info, cap = _moe_layout(route_flat, counts[0, :N_EXPERTS])
        xs = _dispatch(h2.reshape(bsz * seq * ROW_TILE, LANES), dest, seg_info, cap)
        ys = _experts(xs, blk_expert, n_used, w1[l], w3[l], w2[l])
        x = _combine(ys, dest, x1.reshape(bsz * seq, d), route_flat, mod3,
                     g_post_ffn[l].reshape(1, d), seq).reshape(bsz, seq, d)
    return x
```

```python
import functools
import math

import numpy as np
import jax
import jax.numpy as jnp
from jax import lax
from jax.experimental import pallas as pl
from jax.experimental.pallas import tpu as pltpu

F32 = jnp.float32
BF16 = jnp.bfloat16

N_HEADS = 8
HEAD_DIM = 64
ATTN_WIDTH = N_HEADS * HEAD_DIM
DILATION_PATTERNS = ((128, 1), (512, 4), (2048, 16))
NUM_BUCKETS = 32
MAX_DISTANCE = 2048
N_EXPERT_GROUPS = 4
EXPERTS_PER_GROUP = 8
N_EXPERTS = N_EXPERT_GROUPS * EXPERTS_PER_GROUP
TOP_K = 2
RMS_EPS = 1e-6
NEG_INF = -1e30
LOG2_E = math.log2(math.e)

LANES = 128
SUBLANES = 8
VMEM_LIMIT_BYTES = 56 * 1024 * 1024

ATTN_BLK = 128
ATTN_STEP_ROWS = 2048
QKV_ROWS = 1024
QKV_MID_DIL = 4
TIME_TILE = 128
UGATE_CHUNK_BATCH = 2
MERGE_BATCH = 4
MERGE_CHUNK_BATCH = 2
SSM_STEPS = 128
MOE_ROWS = 512
DISPATCH_ROWS = 4096
COMBINE_ROWS = 512
ROW_TILE = 4
U32 = jnp.uint32
HI_HALF = 0xFFFF0000


def _pack_rows(x):
    w = x.shape[1] // 2
    lo = lax.bitcast_convert_type(x[:, :w].astype(BF16).astype(F32), U32) >> 16
    hi = lax.bitcast_convert_type(x[:, w:].astype(BF16).astype(F32), U32) & U32(HI_HALF)
    return hi | lo


def _unpack_rows(p):
    lo = lax.bitcast_convert_type(p << 16, F32)
    hi = lax.bitcast_convert_type(p & U32(HI_HALF), F32)
    return jnp.concatenate([lo, hi], axis=1)


def _store_row_tiles(ref, packed):
    rows = packed.shape[0]
    for s in range(ROW_TILE):
        ref[pl.ds(s, rows, stride=ROW_TILE), :] = packed[:, s * LANES:(s + 1) * LANES]


def _load_row_tiles(ref, rows):
    return _lane_concat([ref[pl.ds(s, rows, stride=ROW_TILE), :] for s in range(ROW_TILE)])


def _cparams(*sem):
    return pltpu.CompilerParams(dimension_semantics=sem, vmem_limit_bytes=VMEM_LIMIT_BYTES)


def _sigmoid(x):
    return 1.0 / (1.0 + jnp.exp(-x))


def _dot(a, b):
    return jnp.dot(a, b, preferred_element_type=F32)


def _split_bf16(a):
    hi = a.astype(BF16)
    lo = (a - hi.astype(F32)).astype(BF16)
    return hi, lo


def _dot_split(a, w_hi, w_lo):
    a_hi, a_lo = _split_bf16(a)
    return _dot(a_hi, w_hi) + _dot(a_lo, w_hi) + _dot(a_hi, w_lo)


def _rms_unit(x):
    ms = jnp.mean(x * x, axis=-1, keepdims=True)
    return x * lax.rsqrt(ms + RMS_EPS)


def _lane_concat(ref_slabs):
    return jnp.concatenate(ref_slabs, axis=-1)


def _trace_staggered(chunks):
    pending, active = list(chunks), []
    while pending or active:
        if pending:
            active.append(pending.pop(0))
        for gen in list(active):
            if next(gen, StopIteration) is StopIteration:
                active.remove(gen)


def _mod_kernel(c_ref, w_ref, b_ref, o_ref):
    c = c_ref[...]
    a = c * _sigmoid(c)
    w_hi, w_lo = _split_bf16(w_ref[...])
    o_ref[...] = _dot_split(a, w_hi, w_lo) + b_ref[...]


def _modulation(c, w_mod, b_mod):
    bsz, d = c.shape
    n = w_mod.shape[1]
    tn = 1024
    return pl.pallas_call(
        _mod_kernel,
        out_shape=jax.ShapeDtypeStruct((bsz, n), F32),
        grid=(n // tn,),
        in_specs=[pl.BlockSpec((bsz, d), lambda j: (0, 0)),
                  pl.BlockSpec((d, tn), lambda j: (0, j)),
                  pl.BlockSpec((1, tn), lambda j: (0, j))],
        out_specs=pl.BlockSpec((bsz, tn), lambda j: (0, j)),
        compiler_params=_cparams("arbitrary"),
        name="mod",
    )(c, w_mod, b_mod.reshape(1, n))


def _qkv_kernel(x_ref, mod_ref, g_ref, w_ref, *rest):
    n_pat = len(DILATION_PATTERNS)
    out_refs, slab, mid = rest[:3 * n_pat], rest[3 * n_pat], rest[3 * n_pat + 1]
    h = _rms_unit(x_ref[0]) * (g_ref[...] * (1.0 + mod_ref[0, 1:2, :])) + mod_ref[0, 0:1, :]
    hb = h.astype(BF16)
    rows = hb.shape[0]
    per_tensor = ATTN_WIDTH // LANES
    mid_dil = QKV_MID_DIL

    def project(t):
        res = _dot(hb, w_ref[:, t * ATTN_WIDTH:(t + 1) * ATTN_WIDTH])
        if t == 0:
            res = res * (HEAD_DIM ** -0.5 * LOG2_E)
        for s in range(per_tensor):
            slab[t * per_tensor + s] = res[:, s * LANES:(s + 1) * LANES]

    def split(t):
        for p, (_, dil) in enumerate(DILATION_PATTERNS):
            sub = rows // dil
            out = out_refs[3 * p + t]
            for r in range(dil):
                pieces = []
                for s in range(per_tensor):
                    ts = t * per_tensor + s
                    if dil == 1:
                        piece = slab[ts]
                    elif dil == mid_dil:
                        piece = slab[ts, pl.ds(r, sub, stride=dil), :]
                        mid[ts, r] = piece
                    else:
                        ratio = dil // mid_dil
                        piece = mid[ts, r % mid_dil, pl.ds(r // mid_dil, sub, stride=ratio), :]
                    pieces.append(piece)
                out[0, r] = _lane_concat(pieces).astype(out.dtype)

    project(0)
    for t in range(3):
        if t + 1 < 3:
            project(t + 1)
        split(t)


def _qkv_projection(x, mod3, g_pre, w_qkv):
    bsz, seq, d = x.shape
    tm = QKV_ROWS
    out_shape, out_specs = [], []
    for _, dil in DILATION_PATTERNS:
        assert tm % (dil * 2 * SUBLANES) == 0
        for _ in range(3):
            out_shape.append(jax.ShapeDtypeStruct((bsz, dil, seq // dil, ATTN_WIDTH), BF16))
            out_specs.append(pl.BlockSpec((1, dil, tm // dil, ATTN_WIDTH),
                                          lambda b, i: (b, 0, i, 0)))
    return pl.pallas_call(
        _qkv_kernel,
        out_shape=out_shape,
        grid=(bsz, seq // tm),
        in_specs=[pl.BlockSpec((1, tm, d), lambda b, i: (b, i, 0)),
                  pl.BlockSpec((1, 6, d), lambda b, i: (b, 0, 0)),
                  pl.BlockSpec((1, d), lambda b, i: (0, 0)),
                  pl.BlockSpec(w_qkv.shape, lambda b, i: (0, 0))],
        out_specs=out_specs,
        scratch_shapes=[pltpu.VMEM((w_qkv.shape[1] // LANES, tm, LANES), F32),
                        pltpu.VMEM((w_qkv.shape[1] // LANES, QKV_MID_DIL, tm // QKV_MID_DIL, LANES),
                                   F32)],
        compiler_params=_cparams("arbitrary", "arbitrary"),
        name="qkv",
    )(x, mod3, g_pre, w_qkv)


def _ugate_kernel(x_ref, mod_ref, g_ref, w_ref, u_ref, gate_ref):
    bsz, tt, d = x_ref.shape
    n_slab = u_ref.shape[0]
    sw = n_slab * LANES
    gw = gate_ref.shape[-1]
    cb = UGATE_CHUNK_BATCH
    rows = cb * tt
    col_chunk = 512

    def normalise(b0):
        shift = mod_ref[b0:b0 + cb, 0, :][:, None, :]
        scale = mod_ref[b0:b0 + cb, 1, :][:, None, :]
        h = _rms_unit(x_ref[b0:b0 + cb]) * (g_ref[...] * (1.0 + scale)) + shift
        return h.reshape(rows, d).astype(BF16)

    def project(b0, hb):
        u = _dot(hb, w_ref[:, 0:sw])
        for j in range(cb):
            for s in range(n_slab):
                u_ref[s, pl.ds(b0 + j, tt, stride=bsz), :] = (
                    u[j * tt:(j + 1) * tt, s * LANES:(s + 1) * LANES])
        for c0 in range(0, gw, col_chunk):
            g = _sigmoid(_dot(hb, w_ref[:, sw + c0:sw + c0 + col_chunk]))
            gate_ref[b0:b0 + cb, :, c0:c0 + col_chunk] = g.reshape(cb, tt, col_chunk).astype(BF16)

    starts = list(range(0, bsz, cb))
    hb = normalise(starts[0])
    for n, b0 in enumerate(starts):
        hb_next = normalise(starts[n + 1]) if n + 1 < len(starts) else None
        project(b0, hb)
        hb = hb_next


def _ugate_projection(x, mod3, g_pre, w_ug, ssm_width):
    bsz, seq, d = x.shape
    tt = TIME_TILE
    gw = w_ug.shape[1] - ssm_width
    n_slab = ssm_width // LANES
    return pl.pallas_call(
        _ugate_kernel,
        out_shape=(jax.ShapeDtypeStruct((n_slab, seq * bsz, LANES), F32),
                   jax.ShapeDtypeStruct((bsz, seq, gw), BF16)),
        grid=(seq // tt,),
        in_specs=[pl.BlockSpec((bsz, tt, d), lambda i: (0, i, 0)),
                  pl.BlockSpec((bsz, 6, d), lambda i: (0, 0, 0)),
                  pl.BlockSpec((1, d), lambda i: (0, 0)),
                  pl.BlockSpec(w_ug.shape, lambda i: (0, 0))],
        out_specs=(pl.BlockSpec((n_slab, tt * bsz, LANES), lambda i: (0, i, 0)),
                   pl.BlockSpec((bsz, tt, gw), lambda i: (0, i, 0))),
        compiler_params=_cparams("arbitrary"),
        name="ugate",
    )(x, mod3, g_pre, w_ug)


def _t5_bucket_np(dist):
    exact = NUM_BUCKETS // 2
    d_f = np.maximum(dist, exact).astype(np.float32)
    large = exact + (np.log(d_f / np.float32(exact)) / np.float32(math.log(MAX_DISTANCE / exact))
                     * np.float32(NUM_BUCKETS - exact)).astype(np.int32)
    return np.where(dist < exact, dist, np.minimum(large, NUM_BUCKETS - 1))


def _bucket_map_t(dil):
    blk = ATTN_BLK
    ki = np.arange(2 * blk)[:, None]
    qi = np.arange(blk)[None, :]
    return _t5_bucket_np(np.maximum(blk + qi - ki, 0) * dil).astype(np.int32)


def _attn_kernel(relb_ref, q_ref, kc_ref, kp_ref, vc_ref, vp_ref, bucket_ref,
                 o_ref, lse_ref, kbuf, vbuf, bias_t, *, n_sub):
    blk = ATTN_BLK
    first_call = jnp.logical_and(pl.program_id(0) == 0,
                                 jnp.logical_and(pl.program_id(1) == 0, pl.program_id(2) == 0))

    @pl.when(first_call)
    def _():
        bucket = bucket_ref[...]
        ki = lax.broadcasted_iota(jnp.int32, bucket.shape, 0)
        qi = lax.broadcasted_iota(jnp.int32, bucket.shape, 1)
        dist = blk + qi - ki
        band = jnp.logical_and(dist >= 0, dist <= blk)
        band_first = jnp.logical_and(band, ki >= blk)

        def per_head(h, carry):
            acc = jnp.zeros(bucket.shape, F32)
            for b in range(NUM_BUCKETS):
                acc = jnp.where(bucket == b, relb_ref[b, h] * LOG2_E, acc)
            bias_t[0, h] = jnp.where(band_first, acc, NEG_INF)
            bias_t[1, h] = jnp.where(band, acc, NEG_INF)
            return carry

        lax.fori_loop(0, N_HEADS, per_head, 0)

    first_variant = jnp.where(pl.program_id(2) == 0, 0, 1)
    n_res = q_ref.shape[1]
    for g in range(n_res):
        kbuf[g, 0:blk, :] = kp_ref[0, g]
        kbuf[g, blk:, :] = kc_ref[0, g]
        vbuf[g, 0:blk, :] = vp_ref[0, g]
        vbuf[g, blk:, :] = vc_ref[0, g]

    lane = lax.broadcasted_iota(jnp.int32, (1, LANES), 1)
    lo_half = lane < HEAD_DIM
    bd_row = lax.broadcasted_iota(jnp.int32, (4 * blk, LANES), 0)
    bd_col = lax.broadcasted_iota(jnp.int32, (4 * blk, LANES), 1)
    ones_bd = ((bd_row < 2 * blk) == (bd_col < HEAD_DIM)).astype(F32).astype(BF16)
    contract_last = (((1,), (1,)), ((), ()))
    contract_first = (((0,), (0,)), ((), ()))

    def sub_block(g, i):
        r0 = i * blk
        q = q_ref[0, g, r0:r0 + blk, :]
        kk = kbuf[g, r0:r0 + 2 * blk, :]
        vv = vbuf[g, r0:r0 + 2 * blk, :]
        variant = first_variant if i == 0 else 1
        for j in range(N_HEADS // 2):
            cols = slice(j * LANES, (j + 1) * LANES)
            qj, kj, vj = q[:, cols], kk[:, cols], vv[:, cols]
            probs_t, maxes = [], []
            for hh in range(2):
                sel = lo_half if hh == 0 else jnp.logical_not(lo_half)
                qm = jnp.where(sel, qj, jnp.zeros_like(qj))
                s_t = lax.dot_general(kj, qm, contract_last, preferred_element_type=F32)
                s_t = s_t + bias_t[variant, 2 * j + hh]
                m = jnp.max(s_t, axis=0, keepdims=True)
                probs_t.append(jnp.exp2(s_t - m).astype(BF16))
                maxes.append(m)
            p2_t = jnp.concatenate(probs_t, axis=0)
            v_bd = jnp.concatenate([jnp.where(lo_half, vj, jnp.zeros_like(vj)),
                                    jnp.where(lo_half, jnp.zeros_like(vj), vj)], axis=0)
            rhs = jnp.concatenate([v_bd, ones_bd], axis=1)
            ol = lax.dot_general(p2_t, rhs, contract_first, preferred_element_type=F32)
            o2, l2 = ol[:, :LANES], ol[:, LANES:]
            m_t = jnp.concatenate([jnp.broadcast_to(maxes[0], (HEAD_DIM, blk)),
                                   jnp.broadcast_to(maxes[1], (HEAD_DIM, blk))], axis=0)
            o_ref[0, g, r0:r0 + blk, cols] = (o2 / l2).astype(o_ref.dtype)
            lse_ref[0, g, r0:r0 + blk, cols] = m_t.T + jnp.log2(l2)

    for g in range(n_res):
        for i in range(n_sub):
            sub_block(g, i)


def _attention_pattern(q, k, v, rel_bias, dil):
    bsz, _, sub_len, aw = q.shape
    blk = ATTN_BLK
    assert sub_len % blk == 0
    tq = min(ATTN_STEP_ROWS, sub_len)
    n_sub = tq // blk
    ratio = tq // blk
    n_res = min(dil, ATTN_STEP_ROWS // tq)
    cur = lambda b, r, n: (b, r, n, 0)
    prev = lambda b, r, n: (b, r, jnp.maximum(n * ratio - 1, 0), 0)
    blk_cur = pl.BlockSpec((1, n_res, tq, aw), cur)
    blk_prev = pl.BlockSpec((1, n_res, blk, aw), prev)
    bucket = jnp.asarray(_bucket_map_t(dil))
    o_dtype = BF16 if TIME_TILE // dil >= 2 * SUBLANES else F32
    return pl.pallas_call(
        functools.partial(_attn_kernel, n_sub=n_sub),
        out_shape=(jax.ShapeDtypeStruct(q.shape, o_dtype), jax.ShapeDtypeStruct(q.shape, F32)),
        grid=(bsz, dil // n_res, sub_len // tq),
        in_specs=[pl.BlockSpec(memory_space=pltpu.SMEM),
                  blk_cur, blk_cur, blk_prev, blk_cur, blk_prev,
                  pl.BlockSpec(bucket.shape, lambda b, r, n: (0, 0))],
        out_specs=(blk_cur, blk_cur),
        scratch_shapes=[pltpu.VMEM((n_res, tq + blk, aw), BF16),
                        pltpu.VMEM((n_res, tq + blk, aw), BF16),
                        pltpu.VMEM((2, N_HEADS, 2 * blk, blk), F32)],
        compiler_params=_cparams("arbitrary", "arbitrary", "arbitrary"),
        name=f"attn_dil{dil}",
    )(rel_bias.astype(F32), q, k, k, v, v, bucket)


def _ssm_kernel(u_ref, bmat_ref, cmat_ref, ar_ref, ai_ref, dskip_ref, wglu_ref, bglu_ref,
                o_ref, hbuf, hstate, *, n_steps):
    @pl.when(pl.program_id(0) == 0)
    def _():
        hstate[...] = jnp.zeros_like(hstate)

    n_slab = u_ref.shape[0]
    n_state = hbuf.shape[1] // 2
    per = n_state // n_slab
    us = [u_ref[s] for s in range(n_slab)]
    ys = [None] * n_slab

    def drive(s):
        bu = _dot(us[s].astype(BF16), bmat_ref[s])
        hbuf[:, s * per:(s + 1) * per] = bu[:, :per]
        hbuf[:, n_state + s * per:n_state + (s + 1) * per] = bu[:, per:]

    def scan(s):
        re_cols = slice(s * per, (s + 1) * per)
        im_cols = slice(n_state + s * per, n_state + (s + 1) * per)
        ar = ar_ref[:, re_cols]
        ai = ai_ref[:, re_cols]
        hr = hstate[:, re_cols]
        hi = hstate[:, im_cols]
        for t in range(n_steps):
            trow = slice(t * SUBLANES, (t + 1) * SUBLANES)
            nr = ar * hr - ai * hi + hbuf[trow, re_cols]
            ni = ar * hi + ai * hr + hbuf[trow, im_cols]
            hbuf[trow, re_cols] = nr
            hbuf[trow, im_cols] = ni
            hr, hi = nr, ni
        hstate[:, re_cols] = hr
        hstate[:, im_cols] = hi

    def read_out(s):
        h_s = _lane_concat([hbuf[:, s * per:(s + 1) * per],
                            hbuf[:, n_state + s * per:n_state + (s + 1) * per]])
        ys[s] = (_dot(h_s.astype(BF16), cmat_ref[s])
                 + dskip_ref[:, s * LANES:(s + 1) * LANES] * us[s])

    for tick in range(n_slab + 2):
        if tick < n_slab:
            drive(tick)
        if 0 <= tick - 1 < n_slab:
            scan(tick - 1)
        if 0 <= tick - 2 < n_slab:
            read_out(tick - 2)
    y = _lane_concat(ys)
    y = 0.5 * y * (1.0 + jnp.tanh(math.sqrt(2.0 / math.pi) * (y + 0.044715 * (y * y * y))))
    z = _dot(y.astype(BF16), wglu_ref[...]) + bglu_ref[...]
    out = y * _sigmoid(z)
    for s in range(n_slab):
        o_ref[s] = out[:, s * LANES:(s + 1) * LANES]


def _ssm_params(a_re, a_im, log_dt, b_re, b_im, c_re, c_im, bsz):
    g, p = a_re.shape
    hg = b_re.shape[-1]
    dt = jnp.exp(log_dt.astype(F32))[:, None]
    a_re, a_im = a_re.astype(F32), a_im.astype(F32)
    mag = jnp.exp(a_re * dt)
    abar_re = mag * jnp.cos(a_im * dt)
    abar_im = mag * jnp.sin(a_im * dt)
    den = a_re * a_re + a_im * a_im
    q_re = ((abar_re - 1.0) * a_re + abar_im * a_im) / den
    q_im = (abar_im * a_re - (abar_re - 1.0) * a_im) / den
    b_re, b_im = b_re.astype(F32), b_im.astype(F32)
    bb_re = q_re[..., None] * b_re - q_im[..., None] * b_im
    bb_im = q_re[..., None] * b_im + q_im[..., None] * b_re
    gs = LANES // hg
    n_slab = g // gs
    eye = jnp.eye(gs, dtype=F32)

    def in_mat(t):
        t = t.reshape(n_slab, gs, p, hg)
        return jnp.einsum('sgph,gk->sghkp', t, eye).reshape(n_slab, gs * hg, gs * p)

    def out_mat(t):
        t = t.reshape(n_slab, gs, hg, p)
        return jnp.einsum('sghp,gk->sgpkh', t, eye).reshape(n_slab, gs * p, gs * hg)

    bmat = jnp.concatenate([in_mat(bb_re), in_mat(bb_im)], axis=2).astype(BF16)
    cmat = jnp.concatenate([out_mat(c_re.astype(F32)), -out_mat(c_im.astype(F32))],
                           axis=1).astype(BF16)
    ar = jnp.broadcast_to(abar_re.reshape(1, g * p), (bsz, g * p))
    ai = jnp.broadcast_to(abar_im.reshape(1, g * p), (bsz, g * p))
    return bmat, cmat, ar, ai


def _ssm_branch(u_slabs, bsz, a_re, a_im, log_dt, b_re, b_im, c_re, c_im, d_skip, w_glu, b_glu):
    n_slab, n_rows, _ = u_slabs.shape
    width = n_slab * LANES
    assert bsz == SUBLANES
    bmat, cmat, ar, ai = _ssm_params(a_re, a_im, log_dt, b_re, b_im, c_re, c_im, bsz)
    n_state2 = n_slab * bmat.shape[2]
    rows = SSM_STEPS * bsz
    const = lambda c: (0, 0)
    const3 = lambda c: (0, 0, 0)
    slab_spec = pl.BlockSpec((n_slab, rows, LANES), lambda c: (0, c, 0))
    return pl.pallas_call(
        functools.partial(_ssm_kernel, n_steps=SSM_STEPS),
        out_shape=jax.ShapeDtypeStruct(u_slabs.shape, F32),
        grid=(n_rows // rows,),
        in_specs=[slab_spec,
                  pl.BlockSpec(bmat.shape, const3), pl.BlockSpec(cmat.shape, const3),
                  pl.BlockSpec(ar.shape, const), pl.BlockSpec(ai.shape, const),
                  pl.BlockSpec((1, width), const), pl.BlockSpec((width, width), const),
                  pl.BlockSpec((1, width), const)],
        out_specs=slab_spec,
        scratch_shapes=[pltpu.VMEM((rows, n_state2), F32), pltpu.VMEM((bsz, n_state2), F32)],
        compiler_params=_cparams("arbitrary"),
        name="ssm",
    )(u_slabs, bmat, cmat, ar, ai, d_skip.reshape(1, width).astype(F32),
      w_glu.astype(BF16), b_glu.reshape(1, width).astype(F32))


ROUTE_LANES = {"id0": 0, "id1": 1, "rank0": 2, "rank1": 3, "w0": 4, "w1": 5}
GROUP_LANE0 = N_EXPERTS


def _merge_kernel(*refs, bsz_total):
    n_pat = len(DILATION_PATTERNS)
    x_ref, gate_ref = refs[0:2]
    attn_refs = refs[2:2 + 2 * n_pat]
    (ssm_ref, mod_ref, gpost_ref, gpre_ref, wba_ref, wbs_ref, wout_ref,
     wr_cat_ref, wr_hi_ref, br_ref) = refs[2 + 2 * n_pat:12 + 2 * n_pat]
    x1_ref, h2_ref, route_ref, count_ref = refs[12 + 2 * n_pat:16 + 2 * n_pat]
    scratch = refs[16 + 2 * n_pat:]
    carry = scratch[-1]

    @pl.when(jnp.logical_and(pl.program_id(0) == 0, pl.program_id(1) == 0))
    def _():
        carry[...] = jnp.zeros_like(carry)

    nb, tt, d = x_ref.shape
    n_slab = ssm_ref.shape[0]
    b0 = pl.program_id(0) * nb
    cb = MERGE_CHUNK_BATCH
    rows = cb * tt
    lane = lax.broadcasted_iota(jnp.int32, (rows, LANES), 1).astype(F32)
    row = lax.broadcasted_iota(jnp.int32, (rows, rows), 0)
    col = lax.broadcasted_iota(jnp.int32, (rows, rows), 1)
    strict_lower = (col < row).astype(BF16)
    running = [carry[...]]

    def chunk_phases(c0):
        bbs = range(c0, c0 + cb)
        o_tok, lse_tok, ssm_tok = scratch[3 * (c0 // cb):3 * (c0 // cb) + 3]

        for bb in bbs:
            for s in range(n_slab):
                ssm_tok[(bb - c0) * tt:(bb - c0 + 1) * tt, s * LANES:(s + 1) * LANES] = (
                    ssm_ref[s, pl.ds(b0 + bb, tt, stride=bsz_total), :])

        slot = 0
        sources = []
        for p, (_, dil) in enumerate(DILATION_PATTERNS):
            o_ref, lse_ref = attn_refs[2 * p], attn_refs[2 * p + 1]
            if dil == 1:
                sources.append((o_ref, lse_ref, None))
                continue
            sub = tt // dil
            for bb in bbs:
                for r in range(dil):
                    o_blk = o_ref[bb, r].astype(F32)
                    l_blk = lse_ref[bb, r]
                    for s in range(n_slab):
                        dst = pl.ds((bb - c0) * tt + r, sub, stride=dil)
                        o_tok[slot, s, dst, :] = o_blk[:, s * LANES:(s + 1) * LANES]
                        lse_tok[slot, s, dst, :] = l_blk[:, s * LANES:(s + 1) * LANES]
            sources.append((o_ref, lse_ref, slot))
            slot += 1
        attn_slabs = []
        for s in range(n_slab):
            cols = slice(s * LANES, (s + 1) * LANES)
            o_ps, lse_ps = [], []
            for o_ref, lse_ref, src_slot in sources:
                if src_slot is None:
                    o_ps.append(o_ref[c0:c0 + cb, 0, :, cols].astype(F32).reshape(rows, LANES))
                    lse_ps.append(lse_ref[c0:c0 + cb, 0, :, cols].reshape(rows, LANES))
                else:
                    o_ps.append(o_tok[src_slot, s])
                    lse_ps.append(lse_tok[src_slot, s])
            m = functools.reduce(jnp.maximum, lse_ps)
            es = [jnp.exp2(l - m) for l in lse_ps]
            num = functools.reduce(lambda a, b: a + b, [e * o for e, o in zip(es, o_ps)])
            den = functools.reduce(lambda a, b: a + b, es)
            attn_slabs.append(num / den)
        attn_b = _lane_concat(attn_slabs).astype(BF16)
        ssm_b = ssm_tok[...].astype(BF16)
        yield
        branch_attn = _dot(attn_b, wba_ref[...])
        branch_ssm = _dot(ssm_b, wbs_ref[...])
        yield
        g_attn = gate_ref[c0:c0 + cb, :, 0:d].reshape(rows, d)
        g_ssm = gate_ref[c0:c0 + cb, :, d:].reshape(rows, d)
        merged_b = g_attn * branch_attn.astype(BF16) + g_ssm * branch_ssm.astype(BF16)
        yield
        y = _dot(merged_b, wout_ref[...])
        yield
        gate1 = mod_ref[c0:c0 + cb, 2, :][:, None, :]
        shift2 = mod_ref[c0:c0 + cb, 3, :][:, None, :]
        scale2 = mod_ref[c0:c0 + cb, 4, :][:, None, :]
        x1 = x_ref[c0:c0 + cb] + (gate1 * gpost_ref[...]) * _rms_unit(y).reshape(cb, tt, d)
        x1_ref[c0:c0 + cb] = x1
        h2 = _rms_unit(x1) * (gpre_ref[...] * (1.0 + scale2)) + shift2
        for j, bb in enumerate(bbs):
            _store_row_tiles(h2_ref.at[bb], _pack_rows(h2[j]))

        a_hi, a_lo = _split_bf16(h2.reshape(rows, d))
        yield
        hi_pass = _dot(a_hi, wr_cat_ref[...])
        lo_pass = _dot(a_lo, wr_hi_ref[...])
        yield
        logits = hi_pass[:, :LANES] + lo_pass + hi_pass[:, LANES:] + br_ref[...]
        big = float(LANES)
        is_group = jnp.logical_and(lane >= GROUP_LANE0, lane < GROUP_LANE0 + N_EXPERT_GROUPS)
        gl = jnp.where(is_group, logits, -jnp.inf)
        g_max = jnp.max(gl, axis=-1, keepdims=True)
        g_sel = jnp.min(jnp.where(gl == g_max, lane, big), axis=-1, keepdims=True) - GROUP_LANE0
        g_gate = 1.0 / jnp.sum(jnp.exp(gl - g_max), axis=-1, keepdims=True)
        lo = g_sel * EXPERTS_PER_GROUP
        in_group = jnp.logical_and(lane >= lo, lane < lo + EXPERTS_PER_GROUP)
        el = jnp.where(in_group, logits, -jnp.inf)
        t0 = jnp.max(el, axis=-1, keepdims=True)
        i0 = jnp.min(jnp.where(el == t0, lane, big), axis=-1, keepdims=True)
        el1 = jnp.where(lane == i0, -jnp.inf, el)
        t1 = jnp.max(el1, axis=-1, keepdims=True)
        i1 = jnp.min(jnp.where(el1 == t1, lane, big), axis=-1, keepdims=True)
        e = jnp.exp(t1 - t0)
        w0 = g_gate / (1.0 + e)
        w1 = g_gate * e / (1.0 + e)

        hit0 = lane == i0
        hit1 = lane == i1
        onehot = jnp.logical_or(hit0, hit1).astype(F32)
        yield
        before = _dot(strict_lower, onehot.astype(BF16)) + running[0]
        rank0 = jnp.sum(jnp.where(hit0, before, 0.0), axis=-1, keepdims=True)
        rank1 = jnp.sum(jnp.where(hit1, before, 0.0), axis=-1, keepdims=True)
        running[0] = running[0] + jnp.sum(onehot, axis=0, keepdims=True)

        route = jnp.zeros((rows, LANES), F32)
        for name, val in (("id0", i0), ("id1", i1), ("rank0", rank0), ("rank1", rank1),
                          ("w0", w0), ("w1", w1)):
            route = jnp.where(lane == ROUTE_LANES[name], val, route)
        route_ref[c0:c0 + cb] = route.reshape(cb, tt, LANES)

    _trace_staggered([chunk_phases(c0) for c0 in range(0, nb, cb)])

    carry[...] = running[0]
    count_ref[...] = jnp.broadcast_to(running[0], count_ref.shape)


def _merge_and_route(x, gates, attn_outs, ssm_slabs, mod3, g_post, g_pre, wba, wbs, wout,
                     w_rg, b_rg, w_re, b_re):
    bsz, seq, d = x.shape
    tt, nb = TIME_TILE, MERGE_BATCH
    aw = ATTN_WIDTH
    n_slab = ssm_slabs.shape[0]
    assert aw == n_slab * LANES
    wr = jnp.zeros((d, LANES), F32).at[:, :N_EXPERTS].set(w_re.astype(F32))
    wr = wr.at[:, GROUP_LANE0:GROUP_LANE0 + N_EXPERT_GROUPS].set(w_rg.astype(F32))
    br = jnp.zeros((1, LANES), F32).at[0, :N_EXPERTS].set(b_re.astype(F32))
    br = br.at[0, GROUP_LANE0:GROUP_LANE0 + N_EXPERT_GROUPS].set(b_rg.astype(F32))
    wr_hi, wr_lo = _split_bf16(wr)
    wr_cat = jnp.concatenate([wr_hi, wr_lo], axis=1)
    n_strided = sum(1 for _, dil in DILATION_PATTERNS if dil > 1)
    chunk_rows = MERGE_CHUNK_BATCH * tt
    tok = lambda h, i: (h, i, 0)
    const = lambda h, i: (0, 0)
    attn_args, attn_specs = [], []
    for (o_p, lse_p), (_, dil) in zip(attn_outs, DILATION_PATTERNS):
        spec = pl.BlockSpec((nb, dil, tt // dil, aw), lambda h, i: (h, 0, i, 0))
        attn_args += [o_p, lse_p]
        attn_specs += [spec, spec]
    return pl.pallas_call(
        functools.partial(_merge_kernel, bsz_total=bsz),
        out_shape=(jax.ShapeDtypeStruct((bsz, seq, d), F32),
                   jax.ShapeDtypeStruct((bsz, seq * ROW_TILE, LANES), U32),
                   jax.ShapeDtypeStruct((bsz, seq, LANES), F32),
                   jax.ShapeDtypeStruct((SUBLANES, LANES), F32)),
        grid=(bsz // nb, seq // tt),
        in_specs=[pl.BlockSpec((nb, tt, d), tok),
                  pl.BlockSpec((nb, tt, gates.shape[-1]), tok)]
                 + attn_specs
                 + [pl.BlockSpec((n_slab, tt * bsz, LANES), lambda h, i: (0, i, 0)),
                    pl.BlockSpec((nb, 6, d), lambda h, i: (h, 0, 0)),
                    pl.BlockSpec((1, d), const), pl.BlockSpec((1, d), const),
                    pl.BlockSpec(wba.shape, const), pl.BlockSpec(wbs.shape, const),
                    pl.BlockSpec(wout.shape, const),
                    pl.BlockSpec((d, 2 * LANES), const), pl.BlockSpec((d, LANES), const),
                    pl.BlockSpec((1, LANES), const)],
        out_specs=(pl.BlockSpec((nb, tt, d), tok), pl.BlockSpec((nb, tt * ROW_TILE, LANES), tok),
                   pl.BlockSpec((nb, tt, LANES), tok),
                   pl.BlockSpec((SUBLANES, LANES), const)),
        scratch_shapes=[pltpu.VMEM((n_strided, n_slab, chunk_rows, LANES), F32),
                        pltpu.VMEM((n_strided, n_slab, chunk_rows, LANES), F32),
                        pltpu.VMEM((chunk_rows, n_slab * LANES), F32)] * (nb // MERGE_CHUNK_BATCH)
                       + [pltpu.VMEM((1, LANES), F32)],
        compiler_params=_cparams("arbitrary", "arbitrary"),
        name="merge",
    )(x, gates, *attn_args, ssm_slabs, mod3, g_post, g_pre, wba, wbs, wout, wr_cat, wr_hi, br)


ISSUE_UNROLL = 16


def _dispatch_kernel(dest_ref, seg_ref, h_ref, xs_ref, zero_buf, sem, zsem):
    rows = h_ref.shape[0] // ROW_TILE
    blk = zero_buf.shape[0]
    n_blocks = xs_ref.shape[0] // blk
    base = pl.program_id(0) * rows

    @pl.when(pl.program_id(0) == 0)
    def _():
        zero_buf[...] = jnp.zeros_like(zero_buf)

        def zero_copy(row0):
            return pltpu.make_async_copy(zero_buf, xs_ref.at[pl.ds(pl.multiple_of(row0, blk), blk)],
                                         zsem)

        def fill_tail(e, carry):
            @pl.when(seg_ref[N_EXPERTS + e] > 0)
            def _():
                zero_copy(seg_ref[e] * ROW_TILE - blk).start()
            return carry

        def fill_unused(j, carry):
            zero_copy(j * blk).start()
            return carry

        def wait_tail(e, carry):
            @pl.when(seg_ref[N_EXPERTS + e] > 0)
            def _():
                zero_copy(0).wait()
            return carry

        def wait_unused(j, carry):
            zero_copy(0).wait()
            return carry

        n_used = seg_ref[2 * N_EXPERTS]
        lax.fori_loop(0, N_EXPERTS, fill_tail, 0)
        lax.fori_loop(n_used, n_blocks, fill_unused, 0)
        lax.fori_loop(0, N_EXPERTS, wait_tail, 0)
        lax.fori_loop(n_used, n_blocks, wait_unused, 0)

    group = ISSUE_UNROLL * ROW_TILE

    def issue(g, carry):
        g0 = pl.multiple_of(g * group, group)
        for rr in range(ISSUE_UNROLL):
            src = h_ref.at[pl.ds(g0 + rr * ROW_TILE, ROW_TILE)]
            for k in range(TOP_K):
                d = dest_ref[TOP_K * (base + g * ISSUE_UNROLL + rr) + k]
                dst = xs_ref.at[pl.ds(pl.multiple_of(d * ROW_TILE, ROW_TILE), ROW_TILE)]
                pltpu.make_async_copy(src, dst, sem).start(priority=k)
        return carry

    lax.fori_loop(0, rows // ISSUE_UNROLL, issue, 0)
    for _ in range(TOP_K):
        pltpu.make_async_copy(h_ref, xs_ref.at[pl.ds(0, rows * ROW_TILE)], sem).wait()


def _dispatch(h2_tiles, dest, seg_info, cap):
    n_tok = h2_tiles.shape[0] // ROW_TILE
    rows = DISPATCH_ROWS
    return pl.pallas_call(
        _dispatch_kernel,
        out_shape=jax.ShapeDtypeStruct((cap * ROW_TILE, LANES), h2_tiles.dtype),
        grid_spec=pltpu.PrefetchScalarGridSpec(
            num_scalar_prefetch=2,
            grid=(n_tok // rows,),
            in_specs=[pl.BlockSpec((rows * ROW_TILE, LANES), lambda i, dest, seg: (i, 0))],
            out_specs=pl.BlockSpec(memory_space=pl.ANY),
            scratch_shapes=[pltpu.VMEM((MOE_ROWS * ROW_TILE, LANES), h2_tiles.dtype),
                            pltpu.SemaphoreType.DMA, pltpu.SemaphoreType.DMA]),
        compiler_params=_cparams("arbitrary"),
        name="dispatch",
    )(dest, seg_info, h2_tiles)


def _expert_kernel(blk_expert_ref, seg_slot_ref, next_expert_ref, n_used_ref, x_ref,
                   w1_hbm, w3_hbm, w2_hbm, y_ref, w1f, w3f, w2f, w1b, w3b, w2b, sems):
    i = pl.program_id(0)
    used = i < n_used_ref[0]
    rows = x_ref.shape[0] // ROW_TILE
    expert = blk_expert_ref[i]
    new_expert = jnp.logical_or(i == 0, expert != blk_expert_ref[jnp.maximum(i - 1, 0)])
    slot = seg_slot_ref[i]

    def weight_copies(e, s):
        return [pltpu.make_async_copy(hbm.at[e], stage.at[s], sems.at[s, n])
                for n, (hbm, stage) in enumerate(((w1_hbm, w1f), (w3_hbm, w3f), (w2_hbm, w2f)))]

    @pl.when(jnp.logical_and(used, new_expert))
    def _():
        @pl.when(i == 0)
        def _():
            for cp in weight_copies(expert, slot):
                cp.start()

        for cp in weight_copies(expert, slot):
            cp.wait()
        w1b[...] = w1f[slot].astype(BF16)
        w3b[...] = w3f[slot].astype(BF16)
        w2b[...] = w2f[slot].astype(BF16)
        nxt = next_expert_ref[i]

        @pl.when(nxt >= 0)
        def _():
            for cp in weight_copies(nxt, 1 - slot):
                cp.start()

    @pl.when(used)
    def _():
        xb = _unpack_rows(_load_row_tiles(x_ref, rows)).astype(BF16)
        h1 = _dot(xb, w1b[...])
        h3 = _dot(xb, w3b[...])
        act = (h1 * _sigmoid(h1)) * h3
        y = _dot(act.astype(BF16), w2b[...])
        _store_row_tiles(y_ref, _pack_rows(y))

    @pl.when(jnp.logical_not(used))
    def _():
        y_ref[...] = jnp.zeros_like(y_ref)


def _experts(xs, blk_expert, n_used, w1, w3, w2):
    d, de = w1.shape[-2:]
    assert d == 2 * ROW_TILE * LANES
    rows = MOE_ROWS
    tile_rows = rows * ROW_TILE
    n_blocks = xs.shape[0] // tile_rows
    blk = jnp.arange(n_blocks, dtype=jnp.int32)
    changed = jnp.concatenate([jnp.zeros((1,), jnp.int32),
                               (blk_expert[1:] != blk_expert[:-1]).astype(jnp.int32)])
    seg_slot = (jnp.cumsum(changed) % 2).astype(jnp.int32)
    later = jnp.logical_and(blk[None, :] > blk[:, None],
                            jnp.logical_and(blk_expert[None, :] != blk_expert[:, None],
                                            blk[None, :] < n_used[0]))
    first_later = jnp.argmax(later, axis=1)
    next_expert = jnp.where(jnp.any(later, axis=1), blk_expert[first_later], -1).astype(jnp.int32)
    xblk = lambda i, be, ss, ne, nu: (jnp.minimum(i, nu[0] - 1), 0)
    return pl.pallas_call(
        _expert_kernel,
        out_shape=jax.ShapeDtypeStruct(xs.shape, U32),
        grid_spec=pltpu.PrefetchScalarGridSpec(
            num_scalar_prefetch=4,
            grid=(n_blocks,),
            in_specs=[pl.BlockSpec((tile_rows, LANES), xblk),
                      pl.BlockSpec(memory_space=pl.ANY), pl.BlockSpec(memory_space=pl.ANY),
                      pl.BlockSpec(memory_space=pl.ANY)],
            out_specs=pl.BlockSpec((tile_rows, LANES), lambda i, be, ss, ne, nu: (i, 0)),
            scratch_shapes=[pltpu.VMEM((2, d, de), w1.dtype), pltpu.VMEM((2, d, de), w3.dtype),
                            pltpu.VMEM((2, de, d), w2.dtype),
                            pltpu.VMEM((d, de), BF16), pltpu.VMEM((d, de), BF16),
                            pltpu.VMEM((de, d), BF16),
                            pltpu.SemaphoreType.DMA((2, 3))]),
        compiler_params=_cparams("arbitrary"),
        name="experts",
    )(blk_expert, seg_slot, next_expert, n_used, xs, w1, w3, w2)


def _combine_kernel(dest_ref, ys_ref, x1_ref, route_ref, mod_ref, g_ref, o_ref, ybuf, sem):
    rows = x1_ref.shape[0]
    step = pl.program_id(0)
    n_step = pl.num_programs(0)

    group = ISSUE_UNROLL * ROW_TILE

    def gather(tile, slot):
        base = tile * rows

        def issue(g, carry):
            g0 = pl.multiple_of(g * group, group)
            for rr in range(ISSUE_UNROLL):
                for k in range(TOP_K):
                    d = dest_ref[TOP_K * (base + g * ISSUE_UNROLL + rr) + k]
                    src = ys_ref.at[pl.ds(pl.multiple_of(d * ROW_TILE, ROW_TILE), ROW_TILE)]
                    dst = ybuf.at[slot, k, pl.ds(g0 + rr * ROW_TILE, ROW_TILE)]
                    pltpu.make_async_copy(src, dst, sem.at[slot]).start(priority=k)
            return carry

        lax.fori_loop(0, rows // ISSUE_UNROLL, issue, 0)

    slot = lax.rem(step, 2)

    @pl.when(step == 0)
    def _():
        gather(step, 0)

    @pl.when(step + 1 < n_step)
    def _():
        gather(step + 1, 1 - slot)

    for k in range(TOP_K):
        pltpu.make_async_copy(ys_ref.at[pl.ds(0, rows * ROW_TILE)], ybuf.at[slot, k],
                              sem.at[slot]).wait()
    route = route_ref[...]
    w0 = route[:, ROUTE_LANES["w0"]:ROUTE_LANES["w0"] + 1]
    w1 = route[:, ROUTE_LANES["w1"]:ROUTE_LANES["w1"] + 1]

    def rows_of(k):
        return _unpack_rows(_load_row_tiles(ybuf.at[slot, k], rows))

    y = rows_of(0) * w0 + rows_of(1) * w1
    o_ref[...] = x1_ref[...] + (mod_ref[0, 5:6, :] * g_ref[...]) * _rms_unit(y)


def _combine(ys, dest, x1_flat, route_flat, mod3, g_post, seq):
    n_tok, d = x1_flat.shape
    rows = COMBINE_ROWS
    assert seq % rows == 0
    return pl.pallas_call(
        _combine_kernel,
        out_shape=jax.ShapeDtypeStruct((n_tok, d), F32),
        grid_spec=pltpu.PrefetchScalarGridSpec(
            num_scalar_prefetch=1,
            grid=(n_tok // rows,),
            in_specs=[pl.BlockSpec(memory_space=pl.ANY),
                      pl.BlockSpec((rows, d), lambda i, dest: (i, 0)),
                      pl.BlockSpec((rows, LANES), lambda i, dest: (i, 0)),
                      pl.BlockSpec((1, 6, d), lambda i, dest: (i * rows // seq, 0, 0)),
                      pl.BlockSpec((1, d), lambda i, dest: (0, 0))],
            out_specs=pl.BlockSpec((rows, d), lambda i, dest: (i, 0)),
            scratch_shapes=[pltpu.VMEM((2, TOP_K, rows * ROW_TILE, LANES), U32),
                            pltpu.SemaphoreType.DMA((2,))]),
        compiler_params=_cparams("arbitrary"),
        name="combine",
    )(dest, ys, x1_flat, route_flat, mod3, g_post)


def _moe_layout(route_flat, counts):
    rows = MOE_ROWS
    n_tok = route_flat.shape[0]
    ids = route_flat[:, 0:TOP_K].astype(jnp.int32)
    rank = route_flat[:, TOP_K:2 * TOP_K].astype(jnp.int32)
    counts = counts.astype(jnp.int32)
    padded = (counts + rows - 1) // rows * rows
    pad_ends = jnp.cumsum(padded)
    pad_starts = pad_ends - padded
    expert = jnp.arange(N_EXPERTS, dtype=jnp.int32)
    start_of = jnp.sum(jnp.where(ids[..., None] == expert, pad_starts, 0), axis=-1)
    dest = (start_of + rank).reshape(n_tok * TOP_K)
    n_blocks = (n_tok * TOP_K + N_EXPERTS * (rows - 1) + rows - 1) // rows
    blk_row0 = jnp.arange(n_blocks, dtype=jnp.int32) * rows
    blk_expert = jnp.minimum(jnp.sum(pad_ends[None, :] <= blk_row0[:, None], axis=-1),
                             N_EXPERTS - 1).astype(jnp.int32)
    n_used = (pad_ends[-1] // rows).reshape(1).astype(jnp.int32)
    seg_info = jnp.concatenate([pad_ends, padded, n_used]).astype(jnp.int32)
    return dest, blk_expert, n_used, seg_info, n_blocks * rows


def kernel(x, c, w_mod, b_mod, g_pre_mix, g_post_mix, g_pre_ffn, g_post_ffn, w_in, rel_bias, a_re, a_im, log_dt, ssm_b_re, ssm_b_im, ssm_c_re, ssm_c_im, d_skip, w_glu, b_glu, w_branch_attn, w_branch_ssm, w_out, w_router_group, b_router_group, w_router_expert, b_router_expert, w1, w3, w2):
    bsz, seq, d = x.shape
    depth = w_mod.shape[0]
    ssm_width = w_glu.shape[-1]
    n_pat = len(DILATION_PATTERNS)
    for l in range(depth):
        mod3 = _modulation(c, w_mod[l], b_mod[l]).reshape(bsz, 6, d)
        w_in_l = w_in[l].astype(BF16)
        g_pre = g_pre_mix[l].reshape(1, d)
        qkv = _qkv_projection(x, mod3, g_pre, w_in_l[:, :3 * ATTN_WIDTH])
        u_slabs, gates = _ugate_projection(x, mod3, g_pre, w_in_l[:, 3 * ATTN_WIDTH:], ssm_width)
        attn_outs = [_attention_pattern(*qkv[3 * p:3 * p + 3], rel_bias, DILATION_PATTERNS[p][1])
                     for p in range(n_pat)]
        ssm_slabs = _ssm_branch(u_slabs, bsz, a_re[l], a_im[l], log_dt[l], ssm_b_re[l],
                                ssm_b_im[l], ssm_c_re[l], ssm_c_im[l], d_skip[l], w_glu[l], b_glu[l])
        x1, h2, route, counts = _merge_and_route(
            x, gates, attn_outs, ssm_slabs, mod3, g_post_mix[l].reshape(1, d),
            g_pre_ffn[l].reshape(1, d), w_branch_attn[l].astype(BF16),
            w_branch_ssm[l].astype(BF16), w_out[l].astype(BF16),
            w_router_group[l], b_router_group[l], w_router_expert[l], b_router_expert[l])
        route_flat = route.reshape(bsz * seq, LANES)
        dest, blk_expert, n_used, seg_info, cap = _moe_layout(route_flat, counts[0, :N_EXPERTS])
        xs = _dispatch(h2.reshape(bsz * seq * ROW_TILE, LANES), dest, seg_info, cap)
        ys = _experts(xs, blk_expert, n_used, w1[l], w3[l], w2[l])
        x = _combine(ys, dest, x1.reshape(bsz * seq, d), route_flat, mod3,
                     g_post_ffn[l].reshape(1, d), seq).reshape(bsz, seq, d)
    return x
```
